```python
import jax, jax.numpy as jnp
from jax import lax
import numpy as np

D_MODEL = 1024
BATCH = 8
SEQ = 4096
DEPTH = 4

N_EVEN = (DEPTH + 1) // 2
N_ODD = DEPTH // 2
ROPE_THETA = 10000.0
EPS = 1e-6
NEG = -1e30
Q_BLOCK = 128

A_HEADS = 8
A_NOPE = 64
A_ROPE = 32
A_VD = 64
A_QLAT = 256
A_KVLAT = 128
A_WIDTH = A_HEADS * A_VD

B_HEADS = 8
B_KV_HEADS = 2
B_HD = 64
B_WIDTH = B_HEADS * B_HD
B_KV = B_KV_HEADS * B_HD
CMP_LEN = 32
CMP_STRIDE = 16
SEL_LEN = 64
N_SEL = 16
WINDOW = 512
NSA_Q_BLOCK = 64

C_HEADS = 8
C_HD = 64
C_WIDTH = C_HEADS * C_HD
IDX_HEADS = 8
IDX_HD = 32
TOPK_MAX = 256

D_HEADS = 4
D_QK = 64
D_VD = 128
D_QK_W = D_HEADS * D_QK
D_WIDTH = D_HEADS * D_VD
CONV_W = 4
CHUNK = 64

M_SLOTS = 256
M_HEADS = 4
M_HD = 64
M_WIDTH = M_HEADS * M_HD

MIX_WIDTH = A_WIDTH + B_WIDTH + M_WIDTH

EVEN_SIZES = (A_QLAT, A_KVLAT, A_ROPE, A_WIDTH,
              B_WIDTH, B_KV, B_KV, B_KV, B_KV, B_KV, B_KV, 3 * B_HEADS, B_WIDTH,
              M_WIDTH, M_WIDTH)
EVEN_COLS = A_QLAT + A_KVLAT + A_ROPE + A_WIDTH + 2 * B_WIDTH + 6 * B_KV + 3 * B_HEADS + 2 * M_WIDTH
ODD_SIZES = (C_WIDTH, C_HD, C_HD, IDX_HEADS * IDX_HD, IDX_HD, IDX_HEADS, C_WIDTH,
             D_QK_W, D_QK_W, D_WIDTH, D_HEADS, D_HEADS, D_WIDTH, D_WIDTH,
             M_WIDTH, M_WIDTH)
ODD_COLS = 2 * C_WIDTH + 2 * C_HD + IDX_HEADS * IDX_HD + IDX_HD + IDX_HEADS + 2 * D_QK_W + 3 * D_WIDTH + 2 * D_HEADS + 2 * M_WIDTH

kernel_name = "hybrid_mla_nsa_dsa_mlstm_trunk"


def rms_norm(x, g):
    xf = x.astype(jnp.float32)
    y = xf * lax.rsqrt(jnp.mean(xf * xf, axis=-1, keepdims=True) + EPS)
    return (y * g.astype(jnp.float32)).astype(x.dtype)


def rope(x, pos):
    d2 = x.shape[-1] // 2
    inv = ROPE_THETA ** (-jnp.arange(d2, dtype=jnp.float32) / d2)
    ang = pos.astype(jnp.float32)[..., None, None] * inv
    c, s = jnp.cos(ang), jnp.sin(ang)
    xf = x.astype(jnp.float32)
    x1, x2 = xf[..., :d2], xf[..., d2:]
    return jnp.concatenate([x1 * c - x2 * s, x1 * s + x2 * c], axis=-1).astype(x.dtype)


def split_cols(u, sizes):
    out, start = [], 0
    for n in sizes:
        out.append(u[..., start:start + n])
        start += n
    return out


def to_blocks(t, qb):
    return t.reshape(t.shape[0], t.shape[1] // qb, qb, *t.shape[2:]).swapaxes(0, 1)


def from_blocks(o):
    o = o.swapaxes(0, 1)
    return o.reshape(o.shape[0], o.shape[1] * o.shape[2], -1)


def causal_attention_blocked(q, k, v, scale):
    S = q.shape[1]
    kpos = jnp.arange(S)

    def block(args):
        qi, qx = args
        t = qi * Q_BLOCK + jnp.arange(Q_BLOCK)
        s = jnp.einsum('bqhd,bkhd->bhqk', qx, k).astype(jnp.float32) * scale
        s = jnp.where(kpos[None, :] <= t[:, None], s, NEG)
        p = jax.nn.softmax(s, axis=-1).astype(v.dtype)
        return jnp.einsum('bhqk,bkhd->bqhd', p, v)

    o = lax.map(block, (jnp.arange(S // Q_BLOCK), to_blocks(q, Q_BLOCK)))
    return from_blocks(o)


def mla_mixer(q_lat, kv_lat, k_rope, pos, q_lat_g, kv_lat_g, w_uq, w_ukv, q_norm_g, k_norm_g):
    B, S, _ = q_lat.shape
    q = (rms_norm(q_lat, q_lat_g) @ w_uq).reshape(B, S, A_HEADS, A_NOPE + A_ROPE)
    kv = (rms_norm(kv_lat, kv_lat_g) @ w_ukv).reshape(B, S, A_HEADS, A_NOPE + A_VD)
    q_nope = rms_norm(q[..., :A_NOPE], q_norm_g[:A_NOPE])
    q_pe = rope(rms_norm(q[..., A_NOPE:], q_norm_g[A_NOPE:]), pos)
    k_nope = rms_norm(kv[..., :A_NOPE], k_norm_g[:A_NOPE])
    k_pe = rope(rms_norm(k_rope[:, :, None, :], k_norm_g[A_NOPE:]), pos)
    k_pe = jnp.broadcast_to(k_pe, (B, S, A_HEADS, A_ROPE))
    qf = jnp.concatenate([q_nope, q_pe], axis=-1)
    kf = jnp.concatenate([k_nope, k_pe], axis=-1)
    v = kv[..., A_NOPE:]
    o = causal_attention_blocked(qf, kf, v, (A_NOPE + A_ROPE) ** -0.5)
    return o.reshape(B, S, A_WIDTH)


def nsa_mixer(q, k_c, v_c, k_s, v_s, k_w, v_w, gates, pos, q_g, k_g, cmp_pos, cmp_w1, cmp_w2):
    B, S, _ = q.shape
    G, R, hd = B_KV_HEADS, B_HEADS // B_KV_HEADS, B_HD
    scale = hd ** -0.5
    q = rope(rms_norm(q.reshape(B, S, B_HEADS, hd), q_g), pos)
    k_c, v_c, k_s, v_s, k_w, v_w = [t.reshape(B, S, G, hd) for t in (k_c, v_c, k_s, v_s, k_w, v_w)]
    k_s = rope(rms_norm(k_s, k_g[1]), pos)
    k_w = rope(rms_norm(k_w, k_g[2]), pos)

    n_cmp = (S - CMP_LEN) // CMP_STRIDE + 1
    tok = jnp.arange(n_cmp)[:, None] * CMP_STRIDE + jnp.arange(CMP_LEN)[None, :]
    cmp_end = tok[:, -1]

    def compress(t, pe, w1, w2):
        blk = t[:, tok] + pe[:, None, :]
        flat = blk.transpose(0, 1, 3, 2, 4).reshape(B, n_cmp, G, CMP_LEN * hd)
        return jax.nn.silu(flat @ w1) @ w2

    kcmp = rope(rms_norm(compress(k_c, cmp_pos[0], cmp_w1[0], cmp_w2[0]), k_g[0]), pos[:, cmp_end])
    vcmp = compress(v_c, cmp_pos[1], cmp_w1[1], cmp_w2[1])

    n_blk = S // SEL_LEN
    n_sel = min(N_SEL, n_blk)
    blk_start = jnp.arange(n_blk) * SEL_LEN
    overlap = ((tok[:, :1] <= blk_start[None, :] + SEL_LEN - 1)
               & (cmp_end[:, None] >= blk_start[None, :])).astype(jnp.float32)
    ksb = k_s.reshape(B, n_blk, SEL_LEN, G, hd).transpose(0, 3, 1, 2, 4)
    vsb = v_s.reshape(B, n_blk, SEL_LEN, G, hd).transpose(0, 3, 1, 2, 4)
    bi = jnp.arange(B)[:, None, None, None]
    gi = jnp.arange(G)[None, :, None, None]
    jb = jnp.arange(n_blk)

    kwp = jnp.pad(k_w, ((0, 0), (WINDOW, 0), (0, 0), (0, 0)))
    vwp = jnp.pad(v_w, ((0, 0), (WINDOW, 0), (0, 0), (0, 0)))

    qb = to_blocks(q.reshape(B, S, G, R, hd), NSA_Q_BLOCK)
    gb = to_blocks(jax.nn.sigmoid(gates.astype(jnp.float32)).reshape(B, S, G, R, 3), NSA_Q_BLOCK)

    def block(args):
        qi, qx, gx = args
        t = qi * NSA_Q_BLOCK + jnp.arange(NSA_Q_BLOCK)
        s = jnp.einsum('bqgrd,bngd->bgrqn', qx, kcmp).astype(jnp.float32) * scale
        m_c = cmp_end[None, :] <= t[:, None]
        p_c = jax.nn.softmax(jnp.where(m_c, s, NEG), axis=-1) * m_c
        o_c = jnp.einsum('bgrqn,bngd->bqgrd', p_c.astype(vcmp.dtype), vcmp)
        imp = jnp.einsum('bgrqn,nj->bgqj', p_c, overlap)
        cur = (t // SEL_LEN)[:, None]
        forced = jnp.where(jb[None, :] == cur, 3e4,
                           jnp.where(jb[None, :] == cur - 1, 2e4,
                                     jnp.where(jb[None, :] == 0, 1e4, 0.0)))
        adm = blk_start[None, :] <= t[:, None]
        score = jnp.where(adm, imp + forced, NEG)
        _, sel = lax.top_k(score, n_sel)
        kg = ksb[bi, gi, sel]
        vg = vsb[bi, gi, sel]
        s = jnp.einsum('bqgrd,bgqnld->bgrqnl', qx, kg).astype(jnp.float32) * scale
        kpos = sel[..., None] * SEL_LEN + jnp.arange(SEL_LEN)
        m_s = kpos <= t[None, None, :, None, None]
        s = jnp.where(m_s[:, :, None], s, NEG)
        p_s = jax.nn.softmax(s.reshape(*s.shape[:4], -1), axis=-1).reshape(s.shape)
        o_s = jnp.einsum('bgrqnl,bgqnld->bqgrd', p_s.astype(vg.dtype), vg)
        kx = lax.dynamic_slice_in_dim(kwp, qi * NSA_Q_BLOCK, WINDOW + NSA_Q_BLOCK, axis=1)
        vx = lax.dynamic_slice_in_dim(vwp, qi * NSA_Q_BLOCK, WINDOW + NSA_Q_BLOCK, axis=1)
        kp = qi * NSA_Q_BLOCK - WINDOW + jnp.arange(WINDOW + NSA_Q_BLOCK)
        m_w = (kp[None, :] <= t[:, None]) & (kp[None, :] > t[:, None] - WINDOW) & (kp[None, :] >= 0)
        s = jnp.einsum('bqgrd,bkgd->bgrqk', qx, kx).astype(jnp.float32) * scale
        p_w = jax.nn.softmax(jnp.where(m_w, s, NEG), axis=-1)
        o_w = jnp.einsum('bgrqk,bkgd->bqgrd', p_w.astype(vx.dtype), vx)
        o = gx[..., 0:1] * o_c + gx[..., 1:2] * o_s + gx[..., 2:3] * o_w
        return o.astype(qx.dtype)

    o = lax.map(block, (jnp.arange(S // NSA_Q_BLOCK), qb, gb))
    return from_blocks(o)


def dsa_mixer(q, k, v, iq, ik, iw, pos, q_g, k_g):
    B, S, _ = q.shape
    topk = min(TOPK_MAX, S // 4)
    scale = C_HD ** -0.5
    q = rope(rms_norm(q.reshape(B, S, C_HEADS, C_HD), q_g), pos)
    k = rope(rms_norm(k.reshape(B, S, 1, C_HD), k_g), pos)[:, :, 0]
    iq = rope(iq.reshape(B, S, IDX_HEADS, IDX_HD), pos)
    ik = rope(ik.reshape(B, S, 1, IDX_HD), pos)[:, :, 0]
    iw = iw * IDX_HEADS ** -0.5
    kpos = jnp.arange(S)
    bi = jnp.arange(B)[:, None, None]

    def block(args):
        qi, qx, iqx, iwx = args
        t = qi * Q_BLOCK + jnp.arange(Q_BLOCK)
        logits = jax.nn.relu(jnp.einsum('bqhd,bsd->bqhs', iqx, ik).astype(jnp.float32))
        score = jnp.einsum('bqh,bqhs->bqs', iwx.astype(jnp.float32), logits)
        score = jnp.where(kpos[None, None, :] <= t[None, :, None], score, NEG)
        _, sel = lax.top_k(score, topk)
        kg = k[bi, sel]
        vg = v[bi, sel]
        s = jnp.einsum('bqhd,bqkd->bhqk', qx, kg).astype(jnp.float32) * scale
        s = jnp.where((sel <= t[None, :, None])[:, None], s, NEG)
        p = jax.nn.softmax(s, axis=-1).astype(vg.dtype)
        return jnp.einsum('bhqk,bqkd->bqhd', p, vg)

    o = lax.map(block, (jnp.arange(S // Q_BLOCK), to_blocks(q, Q_BLOCK), to_blocks(iq, Q_BLOCK), to_blocks(iw, Q_BLOCK)))
    return from_blocks(o)


def mlstm_mixer(q, k, v, i_pre, f_pre, o_pre, conv_w, conv_b, i_bias, f_bias, h_norm_g):
    B, S, _ = q.shape
    dt = q.dtype
    H, dk, dv = D_HEADS, D_QK, D_VD
    qk = jnp.concatenate([q, k], axis=-1)
    y = lax.conv_general_dilated(qk, conv_w[:, None, :], window_strides=(1,), padding=[(CONV_W - 1, 0)],
                                 dimension_numbers=('NWC', 'WIO', 'NWC'), feature_group_count=qk.shape[-1])
    qk = jax.nn.silu(y + conv_b)
    f32 = jnp.float32
    qh = qk[..., :D_QK_W].reshape(B, S, H, dk).astype(f32)
    kh = qk[..., D_QK_W:].reshape(B, S, H, dk).astype(f32) * dk ** -0.5
    vh = v.reshape(B, S, H, dv).astype(f32)
    ig = (i_pre + i_bias).astype(f32)
    lf = jax.nn.log_sigmoid((f_pre + f_bias).astype(f32))
    nc = S // CHUNK

    def chunks(t):
        t = t.reshape(B, nc, CHUNK, H, *t.shape[3:])
        return jnp.moveaxis(jnp.moveaxis(t, 1, 0), 3, 2)

    tril = jnp.tri(CHUNK, dtype=bool)

    def step(carry, inp):
        C, n, m = carry
        qc, kc, vc, ic, lfc = inp
        b = jnp.cumsum(lfc, axis=-1)
        D = jnp.where(tril, b[..., :, None] - b[..., None, :] + ic[..., None, :], NEG)
        inter = b + m[..., None]
        m_t = jnp.maximum(inter, D.max(-1))
        a = jnp.exp(inter - m_t)
        w = jnp.einsum('bhtd,bhsd->bhts', qc, kc) * jnp.exp(D - m_t[..., None])
        num = a[..., None] * jnp.einsum('bhtd,bhdv->bhtv', qc, C) + jnp.einsum('bhts,bhsv->bhtv', w, vc)
        den = a * jnp.einsum('bhtd,bhd->bht', qc, n) + w.sum(-1)
        h = num / jnp.maximum(jnp.abs(den), jnp.exp(-m_t))[..., None]
        bL = b[..., -1]
        g = bL[..., None] - b + ic
        m_new = jnp.maximum(bL + m, g.max(-1))
        ws = jnp.exp(g - m_new[..., None])
        decay = jnp.exp(bL + m - m_new)
        C_new = decay[..., None, None] * C + jnp.einsum('bhs,bhsd,bhsv->bhdv', ws, kc, vc)
        n_new = decay[..., None] * n + jnp.einsum('bhs,bhsd->bhd', ws, kc)
        return (C_new, n_new, m_new), h

    init = (jnp.zeros((B, H, dk, dv), f32), jnp.zeros((B, H, dk), f32), jnp.zeros((B, H), f32))
    _, hs = lax.scan(step, init, (chunks(qh), chunks(kh), chunks(vh), chunks(ig), chunks(lf)))
    h = jnp.moveaxis(jnp.moveaxis(hs, 3, 2), 0, 1).reshape(B, S, H, dv)
    h = rms_norm(h, h_norm_g).reshape(B, S, D_WIDTH)
    return (jax.nn.sigmoid(o_pre.astype(f32)) * h).astype(dt)


def mem_xattn(q_raw, mem, mem_norm_g, w_kv, q_g, k_g):
    B, S, _ = q_raw.shape
    q = rms_norm(q_raw.reshape(B, S, M_HEADS, M_HD), q_g)
    kv = (rms_norm(mem, mem_norm_g) @ w_kv).reshape(B, mem.shape[1], 2, M_HEADS, M_HD)
    k = rms_norm(kv[:, :, 0], k_g)
    v = kv[:, :, 1]
    s = jnp.einsum('bqhd,bmhd->bhqm', q, k).astype(jnp.float32) * M_HD ** -0.5
    p = jax.nn.softmax(s, axis=-1).astype(v.dtype)
    return jnp.einsum('bhqm,bmhd->bqhd', p, v).reshape(B, S, M_WIDTH)


def setup_inputs(seed: int = 0) -> dict:
    key = jax.random.key(seed)
    keys = iter(jax.random.split(key, 40))
    f32 = jnp.float32

    def nrm(shape, scale):
        return jax.random.normal(next(keys), shape, f32) * scale

    def gain(shape):
        return 1.0 + 0.02 * jax.random.normal(next(keys), shape, f32)

    x = nrm((BATCH, SEQ, D_MODEL), 1.0)
    mem = nrm((BATCH, M_SLOTS, D_MODEL), 1.0)
    start = jax.random.randint(next(keys), (BATCH, 1), 0, 4096, dtype=jnp.int32)
    positions = start + jnp.arange(SEQ, dtype=jnp.int32)[None, :]
    return {
        "x": x,
        "mem": mem,
        "positions": positions,
        "ln_g": gain((DEPTH, D_MODEL)),
        "mem_norm_g": gain((DEPTH, D_MODEL)),
        "mem_w_kv": nrm((DEPTH, D_MODEL, 2 * M_WIDTH), D_MODEL ** -0.5),
        "mem_q_norm_g": gain((DEPTH, M_HD)),
        "mem_k_norm_g": gain((DEPTH, M_HD)),
        "w_out": nrm((DEPTH, MIX_WIDTH, D_MODEL), MIX_WIDTH ** -0.5),
        "even_w_in": nrm((N_EVEN, D_MODEL, EVEN_COLS), D_MODEL ** -0.5),
        "mla_q_lat_g": gain((N_EVEN, A_QLAT)),
        "mla_kv_lat_g": gain((N_EVEN, A_KVLAT)),
        "mla_w_uq": nrm((N_EVEN, A_QLAT, A_HEADS * (A_NOPE + A_ROPE)), A_QLAT ** -0.5),
        "mla_w_ukv": nrm((N_EVEN, A_KVLAT, A_HEADS * (A_NOPE + A_VD)), A_KVLAT ** -0.5),
        "mla_q_norm_g": gain((N_EVEN, A_NOPE + A_ROPE)),
        "mla_k_norm_g": gain((N_EVEN, A_NOPE + A_ROPE)),
        "nsa_q_norm_g": gain((N_EVEN, B_HD)),
        "nsa_k_norm_g": gain((N_EVEN, 3, B_HD)),
        "nsa_cmp_pos": nrm((N_EVEN, 2, CMP_LEN, B_HD), 0.02),
        "nsa_cmp_w1": nrm((N_EVEN, 2, CMP_LEN * B_HD, B_HD), (CMP_LEN * B_HD) ** -0.5),
        "nsa_cmp_w2": nrm((N_EVEN, 2, B_HD, B_HD), B_HD ** -0.5),
        "odd_w_in": nrm((N_ODD, D_MODEL, ODD_COLS), D_MODEL ** -0.5),
        "dsa_q_norm_g": gain((N_ODD, C_HD)),
        "dsa_k_norm_g": gain((N_ODD, C_HD)),
        "mlstm_conv_w": nrm((N_ODD, CONV_W, 2 * D_QK_W), CONV_W ** -0.5),
        "mlstm_conv_b": nrm((N_ODD, 2 * D_QK_W), 0.02),
        "mlstm_i_bias": nrm((N_ODD, D_HEADS), 0.1),
        "mlstm_f_bias": 3.0 + nrm((N_ODD, D_HEADS), 0.5),
        "mlstm_h_norm_g": gain((N_ODD, D_VD)),
    }


def reference(x, mem, positions, ln_g, mem_norm_g, mem_w_kv, mem_q_norm_g, mem_k_norm_g, w_out,
              even_w_in, mla_q_lat_g, mla_kv_lat_g, mla_w_uq, mla_w_ukv, mla_q_norm_g, mla_k_norm_g,
              nsa_q_norm_g, nsa_k_norm_g, nsa_cmp_pos, nsa_cmp_w1, nsa_cmp_w2,
              odd_w_in, dsa_q_norm_g, dsa_k_norm_g,
              mlstm_conv_w, mlstm_conv_b, mlstm_i_bias, mlstm_f_bias, mlstm_h_norm_g):
    for layer in range(DEPTH):
        h = rms_norm(x, ln_g[layer])
        li = layer // 2
        if layer % 2 == 0:
            u = h @ even_w_in[li]
            (a_ql, a_kvl, a_kr, a_gate, b_q, b_kc, b_vc, b_ks, b_vs, b_kw, b_vw, b_g, b_gate,
             m_q, m_gate) = split_cols(u, EVEN_SIZES)
            y_a = mla_mixer(a_ql, a_kvl, a_kr, positions, mla_q_lat_g[li], mla_kv_lat_g[li],
                            mla_w_uq[li], mla_w_ukv[li], mla_q_norm_g[li], mla_k_norm_g[li])
            y_b = nsa_mixer(b_q, b_kc, b_vc, b_ks, b_vs, b_kw, b_vw, b_g, positions,
                            nsa_q_norm_g[li], nsa_k_norm_g[li], nsa_cmp_pos[li], nsa_cmp_w1[li], nsa_cmp_w2[li])
            y1 = y_a * jax.nn.silu(a_gate)
            y2 = y_b * jax.nn.silu(b_gate)
        else:
            u = h @ odd_w_in[li]
            (c_q, c_k, c_v, c_iq, c_ik, c_iw, c_gate, d_q, d_k, d_v, d_i, d_f, d_o, d_gate,
             m_q, m_gate) = split_cols(u, ODD_SIZES)
            y_c = dsa_mixer(c_q, c_k, c_v, c_iq, c_ik, c_iw, positions, dsa_q_norm_g[li], dsa_k_norm_g[li])
            y_d = mlstm_mixer(d_q, d_k, d_v, d_i, d_f, d_o, mlstm_conv_w[li], mlstm_conv_b[li],
                              mlstm_i_bias[li], mlstm_f_bias[li], mlstm_h_norm_g[li])
            y1 = y_c * jax.nn.silu(c_gate)
            y2 = y_d * jax.nn.silu(d_gate)
        y_m = mem_xattn(m_q, mem, mem_norm_g[layer], mem_w_kv[layer], mem_q_norm_g[layer], mem_k_norm_g[layer])
        mix = jnp.concatenate([y1, y2, y_m * jax.nn.silu(m_gate)], axis=-1)
        x = x + mix @ w_out[layer]
    return x
```

```python
import functools

import numpy as np
import jax
import jax.numpy as jnp
from jax import lax
from jax.experimental import pallas as pl
from jax.experimental.pallas import tpu as pltpu

F32, BF16, I32 = jnp.float32, jnp.bfloat16, jnp.int32
HI = lax.Precision.HIGHEST
NEG = -1e30
EPS = 1e-6
ROPE_THETA = 10000.0
LANES = 128
VMEM_LIMIT_BYTES = 48 * 1024 * 1024

D_MODEL = 1024
DEPTH = 4
A_HEADS, A_NOPE, A_ROPE, A_VD, A_QLAT, A_KVLAT = 8, 64, 32, 64, 256, 128
B_HEADS, B_KV_HEADS, B_HD = 8, 2, 64
CMP_LEN, CMP_STRIDE, SEL_LEN, N_SEL, WINDOW = 32, 16, 64, 16, 512
C_HEADS, C_HD, IDX_HEADS, IDX_HD, TOPK_MAX = 8, 64, 8, 32, 256
D_HEADS, D_QK, D_VD, CONV_W, CHUNK = 4, 64, 128, 4, 64
M_HEADS, M_HD = 4, 64

INT_MIN = np.int32(-2 ** 31)
NEG_KEY = int(np.float32(NEG).view(np.int32) ^ np.int32(0x7FFFFFFF))


def _cp(*sem):
    return pltpu.CompilerParams(dimension_semantics=sem, vmem_limit_bytes=VMEM_LIMIT_BYTES)


def _nt(a, b):
    return lax.dot_general(a, b, (((1,), (1,)), ((), ())), preferred_element_type=F32)


def _tn(a, b):
    return lax.dot_general(a, b, (((0,), (0,)), ((), ())), preferred_element_type=F32)


def _mm(a, b):
    return jnp.dot(a, b, preferred_element_type=F32)


def _sigmoid(x):
    return 1.0 / (1.0 + jnp.exp(-x))


def _log_sigmoid(x):
    return jnp.minimum(x, 0.0) - jnp.log1p(jnp.exp(-jnp.abs(x)))


def _lane(shape):
    return lax.broadcasted_iota(I32, shape, len(shape) - 1)


def _group_mat(gs):
    r = lax.broadcasted_iota(I32, (LANES, LANES), 0)
    c = lax.broadcasted_iota(I32, (LANES, LANES), 1)
    sh = gs.bit_length() - 1
    return jnp.where((r >> sh) == (c >> sh), 1.0, 0.0).astype(F32)


def _group_rms(x, gs):
    ss = jnp.dot(x * x, _group_mat(gs), preferred_element_type=F32, precision=HI)
    return x * lax.rsqrt(ss * (1.0 / gs) + EPS)


def _rope(x, cosp, sinp, half):
    lane = _lane(x.shape)
    rot = jnp.where((lane & (2 * half - 1)) < half,
                    pltpu.roll(x, LANES - half, 1), pltpu.roll(x, half, 1))
    return x * cosp + rot * sinp


def _row_rms(x):
    return x * lax.rsqrt(jnp.mean(x * x, axis=-1, keepdims=True) + EPS)


def _in_proj_kernel(x_ref, g_ref, w_ref, o_ref):
    h = (_row_rms(x_ref[0]) * g_ref[...]).astype(BF16)
    ncol = o_ref.shape[-1]
    for c0 in range(0, ncol, 512):
        c1 = min(ncol, c0 + 512)
        o_ref[0, :, c0:c1] = _mm(h, w_ref[:, c0:c1])


def _in_proj(x, g, w):
    b, s, d = x.shape
    c = w.shape[1]
    tm = min(256, s)
    return pl.pallas_call(
        _in_proj_kernel,
        out_shape=jax.ShapeDtypeStruct((b, s, c), F32),
        grid=(b, s // tm),
        in_specs=[pl.BlockSpec((1, tm, d), lambda bi, i: (bi, i, 0)),
                  pl.BlockSpec((1, d), lambda bi, i: (0, 0)),
                  pl.BlockSpec((d, c), lambda bi, i: (0, 0))],
        out_specs=pl.BlockSpec((1, tm, c), lambda bi, i: (bi, i, 0)),
        compiler_params=_cp("parallel", "parallel"),
        name="in_proj",
    )(x, g, w)


def _out_proj_kernel(x_ref, y1_ref, y2_ref, ym_ref, g1_ref, g2_ref, gm_ref, w_ref, o_ref):
    def gated(y_ref, g_ref):
        g = g_ref[0]
        return (y_ref[0].astype(F32) * (g * _sigmoid(g))).astype(BF16)

    n1 = y1_ref.shape[-1]
    n2 = y2_ref.shape[-1]
    acc = x_ref[0]
    acc = acc + _mm(gated(y1_ref, g1_ref), w_ref[0:n1, :])
    acc = acc + _mm(gated(y2_ref, g2_ref), w_ref[n1:n1 + n2, :])
    acc = acc + _mm(gated(ym_ref, gm_ref), w_ref[n1 + n2:, :])
    o_ref[0] = acc


def _out_proj(x, y1, y2, ym, u, gate_blocks, w):
    b, s, d = x.shape
    tm = min(512, s)
    i1, i2, im = gate_blocks
    n1, n2, nm = y1.shape[-1], y2.shape[-1], ym.shape[-1]
    row = lambda bi, i: (bi, i, 0)
    return pl.pallas_call(
        _out_proj_kernel,
        out_shape=jax.ShapeDtypeStruct((b, s, d), F32),
        grid=(b, s // tm),
        in_specs=[pl.BlockSpec((1, tm, d), row),
                  pl.BlockSpec((1, tm, n1), row),
                  pl.BlockSpec((1, tm, n2), row),
                  pl.BlockSpec((1, tm, nm), row),
                  pl.BlockSpec((1, tm, n1), lambda bi, i: (bi, i, i1)),
                  pl.BlockSpec((1, tm, n2), lambda bi, i: (bi, i, i2)),
                  pl.BlockSpec((1, tm, nm), lambda bi, i: (bi, i, im)),
                  pl.BlockSpec(w.shape, lambda bi, i: (0, 0))],
        out_specs=pl.BlockSpec((1, tm, d), row),
        compiler_params=_cp("parallel", "parallel"),
        name="out_proj",
    )(x, y1, y2, ym, u, u, u, w)


def _mem_kv_kernel(mem_ref, g_ref, w_ref, kg_ref, k_ref, v_ref):
    h = (_row_rms(mem_ref[0]) * g_ref[0]).astype(BF16)
    kv = _mm(h, w_ref[0])
    nk = k_ref.shape[-1]
    for c0 in range(0, nk, LANES):
        k_ref[0, 0, :, c0:c0 + LANES] = (_group_rms(kv[:, c0:c0 + LANES], M_HD) * kg_ref[0]).astype(BF16)
    v_ref[0, 0] = kv[:, nk:].astype(BF16)


def _mem_kv(mem, g, w, kg):
    b, m, d = mem.shape
    depth = w.shape[0]
    nk = M_HEADS * M_HD
    out = jax.ShapeDtypeStruct((depth, b, m, nk), BF16)
    return pl.pallas_call(
        _mem_kv_kernel,
        out_shape=(out, out),
        grid=(depth, b),
        in_specs=[pl.BlockSpec((1, m, d), lambda l, bi: (bi, 0, 0)),
                  pl.BlockSpec((1, 1, d), lambda l, bi: (l, 0, 0)),
                  pl.BlockSpec((1, d, 2 * nk), lambda l, bi: (l, 0, 0)),
                  pl.BlockSpec((1, 1, LANES), lambda l, bi: (l, 0, 0))],
        out_specs=(pl.BlockSpec((1, 1, m, nk), lambda l, bi: (l, bi, 0, 0)),
                   pl.BlockSpec((1, 1, m, nk), lambda l, bi: (l, bi, 0, 0))),
        compiler_params=_cp("parallel", "parallel"),
        name="mem_kv",
    )(mem, g, w, kg)


def _mem_attn_kernel(q_ref, k_ref, v_ref, qg_ref, o_ref):
    tq = q_ref.shape[1]
    lane = _lane((tq, LANES))
    for p in range(M_HEADS // 2):
        sl = slice(p * LANES, (p + 1) * LANES)
        q = (_group_rms(q_ref[0, :, sl], M_HD) * qg_ref[...] * (M_HD ** -0.5)).astype(BF16)
        k = k_ref[0, 0, :, sl]
        v = v_ref[0, 0, :, sl]
        halves = []
        for e in range(2):
            in_half = (lane >= 64 * e) & (lane < 64 * e + 64)
            s = _nt(jnp.where(in_half, q, jnp.zeros_like(q)), k)
            ex = jnp.exp(s - jnp.max(s, axis=-1, keepdims=True))
            o = _mm(ex.astype(BF16), v) / jnp.sum(ex, axis=-1, keepdims=True)
            halves.append(o)
        o_ref[0, :, sl] = jnp.where(lane < 64, halves[0], halves[1]).astype(BF16)


def _mem_attn(u, q_block, k, v, layer, qg):
    b, s, _ = u.shape
    m, nk = k.shape[2], k.shape[3]
    tq = min(512, s)
    return pl.pallas_call(
        _mem_attn_kernel,
        out_shape=jax.ShapeDtypeStruct((b, s, nk), BF16),
        grid=(b, s // tq),
        in_specs=[pl.BlockSpec((1, tq, nk), lambda bi, i: (bi, i, q_block)),
                  pl.BlockSpec((1, 1, m, nk), lambda bi, i: (layer, bi, 0, 0)),
                  pl.BlockSpec((1, 1, m, nk), lambda bi, i: (layer, bi, 0, 0)),
                  pl.BlockSpec((1, LANES), lambda bi, i: (0, 0))],
        out_specs=pl.BlockSpec((1, tq, nk), lambda bi, i: (bi, i, 0)),
        compiler_params=_cp("parallel", "parallel"),
        name="mem_attn",
    )(u, k, v, qg)


def _flash(qs, k_at, v_at, j0, j1, bias_at):
    rows = qs[0].shape[0]

    def body(j, carry):
        k = k_at(j)
        v = v_at(j)
        bias = bias_at(j)
        new = []
        for q, (m, l, acc) in zip(qs, carry):
            s = _nt(q, k) + bias
            m_new = jnp.maximum(m, jnp.max(s, axis=-1, keepdims=True))
            alpha = jnp.exp(m - m_new)
            p = jnp.exp(s - m_new)
            l = alpha * l + jnp.sum(p, axis=-1, keepdims=True)
            acc = alpha * acc + _mm(p.astype(BF16), v)
            new.append((m_new, l, acc))
        return tuple(new)

    init = tuple((jnp.full((rows, 1), NEG, F32), jnp.zeros((rows, 1), F32),
                  jnp.zeros((rows, LANES), F32)) for _ in qs)
    out = lax.fori_loop(j0, j1, body, init)
    return [acc / l for (_, l, acc) in out]


def _mla_prep_kernel(u_ref, cos_ref, sin_ref, qlg_ref, kvlg_ref, wuq_ref, wukv_ref,
                     qng_ref, qrg_ref, kng_ref, krg_ref, q_ref, k_ref, v_ref):
    tm = u_ref.shape[1]
    lane = _lane((tm, LANES))
    cosp, sinp = cos_ref[0], sin_ref[0]
    half = A_ROPE // 2
    scale = (A_NOPE + A_ROPE) ** -0.5
    ql = (_row_rms(u_ref[0, :, 0:A_QLAT]) * qlg_ref[...]).astype(BF16)
    kvl = (_row_rms(u_ref[0, :, A_QLAT:A_QLAT + A_KVLAT]) * kvlg_ref[...]).astype(BF16)
    q = _mm(ql, wuq_ref[...])
    kv = _mm(kvl, wukv_ref[...])
    n_nope = A_HEADS * A_NOPE
    v_ref[0] = kv[:, n_nope:].astype(BF16)
    kr = u_ref[0, :, A_QLAT + A_KVLAT:A_QLAT + A_KVLAT + LANES]
    kpe = _rope(_group_rms(kr, A_ROPE) * krg_ref[...], cosp, sinp, half)
    kpe = pltpu.roll(kpe, A_NOPE, 1)
    qn = [_group_rms(q[:, c:c + LANES], A_NOPE) * qng_ref[...] for c in range(0, n_nope, LANES)]
    kn = [_group_rms(kv[:, c:c + LANES], A_NOPE) * kng_ref[...] for c in range(0, n_nope, LANES)]
    qr = [_rope(_group_rms(q[:, n_nope + c:n_nope + c + LANES], A_ROPE) * qrg_ref[...], cosp, sinp, half)
          for c in range(0, A_HEADS * A_ROPE, LANES)]
    for h in range(A_HEADS):
        qn_h = qn[h // 2] if h % 2 == 0 else pltpu.roll(qn[h // 2], A_NOPE, 1)
        kn_h = kn[h // 2] if h % 2 == 0 else pltpu.roll(kn[h // 2], A_NOPE, 1)
        shift = (A_NOPE - (h % 4) * A_ROPE) % LANES
        qr_h = qr[h // 4] if shift == 0 else pltpu.roll(qr[h // 4], shift, 1)
        qf = jnp.where(lane < A_NOPE, qn_h, jnp.where(lane < A_NOPE + A_ROPE, qr_h, 0.0))
        kf = jnp.where(lane < A_NOPE, kn_h, jnp.where(lane < A_NOPE + A_ROPE, kpe, 0.0))
        q_ref[0, h] = (qf * scale).astype(BF16)
        k_ref[0, h] = kf.astype(BF16)


def _mla_prep(u, cos32, sin32, p):
    b, s, _ = u.shape
    tm = min(256, s)
    hd = jax.ShapeDtypeStruct((b, A_HEADS, s, LANES), BF16)
    const = lambda shape: pl.BlockSpec(shape, lambda bi, i: (0,) * len(shape))
    return pl.pallas_call(
        _mla_prep_kernel,
        out_shape=(hd, hd, jax.ShapeDtypeStruct((b, s, A_HEADS * A_VD), BF16)),
        grid=(b, s // tm),
        in_specs=[pl.BlockSpec((1, tm, 512), lambda bi, i: (bi, i, 0)),
                  pl.BlockSpec((1, tm, LANES), lambda bi, i: (bi, i, 0)),
                  pl.BlockSpec((1, tm, LANES), lambda bi, i: (bi, i, 0)),
                  const((1, A_QLAT)), const((1, A_KVLAT)),
                  const(p["w_uq"].shape), const(p["w_ukv"].shape),
                  const((1, LANES)), const((1, LANES)), const((1, LANES)), const((1, LANES))],
        out_specs=(pl.BlockSpec((1, A_HEADS, tm, LANES), lambda bi, i: (bi, 0, i, 0)),
                   pl.BlockSpec((1, A_HEADS, tm, LANES), lambda bi, i: (bi, 0, i, 0)),
                   pl.BlockSpec((1, tm, A_HEADS * A_VD), lambda bi, i: (bi, i, 0))),
        compiler_params=_cp("parallel", "parallel"),
        name="mla_prep",
    )(u, cos32, sin32, p["q_lat_g"], p["kv_lat_g"], p["w_uq"], p["w_ukv"],
      p["qn_g"], p["qr_g"], p["kn_g"], p["kr_g"])


def _mla_attn_kernel(q_ref, k_ref, v_ref, o_ref):
    tq = q_ref.shape[2]
    i = pl.program_id(2)
    lane = _lane((tq, LANES))
    r = lax.broadcasted_iota(I32, (tq, tq), 0)
    c = lax.broadcasted_iota(I32, (tq, tq), 1)
    diag_bias = jnp.where(c <= r, 0.0, NEG).astype(F32)
    zero_bias = jnp.zeros((1, tq), F32)
    res = []
    for e in range(2):
        k_at = lambda j, e=e: k_ref[0, e, pl.ds(pl.multiple_of(j * tq, tq), tq), :]
        v_at = lambda j: v_ref[0, pl.ds(pl.multiple_of(j * tq, tq), tq), :]
        bias_at = lambda j: jnp.where(j == i, diag_bias, zero_bias)
        res.append(_flash([q_ref[0, e]], k_at, v_at, 0, i + 1, bias_at)[0])
    o_ref[0] = jnp.where(lane < A_VD, res[0], res[1]).astype(BF16)


def _mla_attn(q, k, v):
    b, h, s, _ = q.shape
    tq = min(256, s)
    return pl.pallas_call(
        _mla_attn_kernel,
        out_shape=jax.ShapeDtypeStruct((b, s, h * A_VD), BF16),
        grid=(b, h // 2, s // tq),
        in_specs=[pl.BlockSpec((1, 2, tq, LANES), lambda bi, p, i: (bi, p, i, 0)),
                  pl.BlockSpec((1, 2, s, LANES), lambda bi, p, i: (bi, p, 0, 0)),
                  pl.BlockSpec((1, s, LANES), lambda bi, p, i: (bi, 0, p))],
        out_specs=pl.BlockSpec((1, tq, LANES), lambda bi, p, i: (bi, i, p)),
        compiler_params=_cp("parallel", "parallel", "parallel"),
        name="mla_attn",
    )(q, k, v)


def _nsa_prep_kernel(q_in, ks_in, vs_in, kw_in, vw_in, cos_ref, sin_ref, qg_ref, ksg_ref, kwg_ref,
                     q_ref, ks_ref, vs_ref, kw_ref, vw_ref):
    tm = q_in.shape[1]
    lane = _lane((tm, LANES))
    cosp, sinp = cos_ref[0], sin_ref[0]
    half = B_HD // 2
    rep = B_HEADS // B_KV_HEADS
    for p in range(B_HEADS // 2):
        y = _rope(_group_rms(q_in[0, :, p * LANES:(p + 1) * LANES], B_HD) * qg_ref[...], cosp, sinp, half)
        y = y * (B_HD ** -0.5)
        y_sw = pltpu.roll(y, B_HD, 1)
        for e in range(2):
            h = 2 * p + e
            g = h // rep
            src = y if e == g else y_sw
            in_grp = (lane >= B_HD * g) & (lane < B_HD * (g + 1))
            q_ref[0, h] = jnp.where(in_grp, src, 0.0).astype(BF16)
    ks_ref[0] = _rope(_group_rms(ks_in[0], B_HD) * ksg_ref[...], cosp, sinp, half).astype(BF16)
    kw_ref[0] = _rope(_group_rms(kw_in[0], B_HD) * kwg_ref[...], cosp, sinp, half).astype(BF16)
    vs_ref[0] = vs_in[0].astype(BF16)
    vw_ref[0] = vw_in[0].astype(BF16)


def _nsa_prep(u, cols, cos64, sin64, p):
    b, s, _ = u.shape
    tm = min(256, s)
    blk = lambda name: pl.BlockSpec((1, tm, LANES), lambda bi, i, c=cols[name] // LANES: (bi, i, c))
    row = pl.BlockSpec((1, tm, LANES), lambda bi, i: (bi, i, 0))
    const = pl.BlockSpec((1, LANES), lambda bi, i: (0, 0))
    kvs = jax.ShapeDtypeStruct((b, s, LANES), BF16)
    return pl.pallas_call(
        _nsa_prep_kernel,
        out_shape=(jax.ShapeDtypeStruct((b, B_HEADS, s, LANES), BF16), kvs, kvs, kvs, kvs),
        grid=(b, s // tm),
        in_specs=[pl.BlockSpec((1, tm, 512), lambda bi, i, c=cols["b_q"] // 512: (bi, i, c)),
                  blk("b_ks"), blk("b_vs"), blk("b_kw"), blk("b_vw"), row, row, const, const, const],
        out_specs=(pl.BlockSpec((1, B_HEADS, tm, LANES), lambda bi, i: (bi, 0, i, 0)), row, row, row, row),
        compiler_params=_cp("parallel", "parallel"),
        name="nsa_prep",
    )(u, u, u, u, u, cos64, sin64, p["q_g"], p["ks_g"], p["kw_g"])


def _nsa_cmp_kernel(kc_in, vc_in, pek_ref, pev_ref, w1k_ref, w1v_ref, w2k_ref, w2v_ref, kg_ref,
                    cos_ref, sin_ref, ko_ref, vo_ref, pad_ref):
    s = kc_in.shape[1]
    n_pad = ko_ref.shape[1]

    def compress(x_in, pe_ref, w1_ref, w2_ref):
        pad_ref[0:s, :] = x_in[0]
        pad_ref[s:s + CMP_STRIDE, :] = jnp.zeros((CMP_STRIDE, LANES), F32)
        acc = jnp.zeros((n_pad, LANES), F32)
        for l in range(CMP_LEN):
            xl = pad_ref[pl.ds(l, n_pad, stride=CMP_STRIDE), :] + pe_ref[l:l + 1, :]
            acc = acc + _mm(xl.astype(BF16), w1_ref[l])
        mid = acc * _sigmoid(acc)
        return _mm(mid.astype(BF16), w2_ref[...])

    kc = compress(kc_in, pek_ref, w1k_ref, w2k_ref)
    ko_ref[0] = _rope(_group_rms(kc, B_HD) * kg_ref[...], cos_ref[0], sin_ref[0], B_HD // 2).astype(BF16)
    vo_ref[0] = compress(vc_in, pev_ref, w1v_ref, w2v_ref).astype(BF16)


def _nsa_cmp(u, cols, cosc, sinc, p):
    b, s, _ = u.shape
    n_pad = s // CMP_STRIDE
    blk = lambda name: pl.BlockSpec((1, s, LANES), lambda bi, c=cols[name] // LANES: (bi, 0, c))
    const = lambda shape: pl.BlockSpec(shape, lambda bi: (0,) * len(shape))
    out = jax.ShapeDtypeStruct((b, n_pad, LANES), BF16)
    ospec = pl.BlockSpec((1, n_pad, LANES), lambda bi: (bi, 0, 0))
    return pl.pallas_call(
        _nsa_cmp_kernel,
        out_shape=(out, out),
        grid=(b,),
        in_specs=[blk("b_kc"), blk("b_vc"), const((CMP_LEN, LANES)), const((CMP_LEN, LANES)),
                  const((CMP_LEN, LANES, LANES)), const((CMP_LEN, LANES, LANES)),
                  const((LANES, LANES)), const((LANES, LANES)), const((1, LANES)), ospec, ospec],
        out_specs=(ospec, ospec),
        scratch_shapes=[pltpu.VMEM((s + CMP_STRIDE, LANES), F32)],
        compiler_params=_cp("parallel"),
        name="nsa_cmp",
    )(u, u, p["pe_k"], p["pe_v"], p["w1k"], p["w1v"], p["w2k"], p["w2v"], p["kc_g"], cosc, sinc)


def _nsa_attn_kernel(q_ref, kc_ref, vc_ref, ks_ref, vs_ref, kw_ref, vw_ref, g_ref, e_ref, ovl_ref, o_ref,
                     *, n_blk, n_sel, tks):
    tq = q_ref.shape[2]
    n_pad = kc_ref.shape[1]
    i = pl.program_id(1)
    t = i * tq + lax.broadcasted_iota(I32, (tq, 1), 0)
    lane = _lane((tq, LANES))
    gates = _sigmoid(g_ref[0])
    rep = B_HEADS // B_KV_HEADS
    ncol = lax.broadcasted_iota(I32, (1, n_pad), 1)
    valid_c = (ncol * CMP_STRIDE + (CMP_LEN - 1)) <= t
    kc = kc_ref[0]
    vc = vc_ref[0]
    cur = t >> (SEL_LEN.bit_length() - 1)
    forced = jnp.where(lane == cur, 3e4, jnp.where(lane == cur - 1, 2e4, jnp.where(lane == 0, 1e4, 0.0)))
    adm = (lane * SEL_LEN <= t) & (lane < n_blk)
    for g in range(B_KV_HEADS):
        qs = [q_ref[0, rep * g + r] for r in range(rep)]
        psum = jnp.zeros((tq, n_pad), F32)
        o_c = []
        for q in qs:
            s = jnp.where(valid_c, _nt(q, kc), NEG)
            ex = jnp.exp(s - jnp.max(s, axis=-1, keepdims=True))
            pc = jnp.where(valid_c, ex / jnp.sum(ex, axis=-1, keepdims=True), 0.0)
            psum = psum + pc
            o_c.append(_mm(pc.astype(BF16), vc))
        imp = jnp.dot(psum, ovl_ref[...], preferred_element_type=F32, precision=HI)
        score = jnp.where(adm, imp + forced, NEG)
        rank = jnp.zeros((tq, LANES), F32)
        for jp in range(n_blk):
            col = score[:, jp:jp + 1]
            beats = (col > score) | ((col == score) & (lane > jp))
            rank = rank + jnp.where(beats, 1.0, 0.0)
        sel = jnp.where((rank < n_sel) & (lane < n_blk), 1.0, 0.0).astype(BF16)

        def sel_bias(j):
            kp = j * tks + lax.broadcasted_iota(I32, (1, tks), 1)
            hit = _mm(sel, e_ref[:, pl.ds(pl.multiple_of(j * tks, tks), tks)])
            return jnp.where((hit > 0.5) & (kp <= t), 0.0, NEG)

        n_tile = ((i + 1) * tq + tks - 1) // tks
        o_s = _flash(qs,
                     lambda j: ks_ref[0, pl.ds(pl.multiple_of(j * tks, tks), tks), :],
                     lambda j: vs_ref[0, pl.ds(pl.multiple_of(j * tks, tks), tks), :],
                     0, n_tile, sel_bias)

        def win_bias(j):
            kp = j * tq + lax.broadcasted_iota(I32, (1, tq), 1)
            return jnp.where((kp <= t) & (kp > t - WINDOW), 0.0, NEG)

        o_w = _flash(qs,
                     lambda j: kw_ref[0, pl.ds(pl.multiple_of(j * tq, tq), tq), :],
                     lambda j: vw_ref[0, pl.ds(pl.multiple_of(j * tq, tq), tq), :],
                     jnp.maximum(i - WINDOW // tq, 0), i + 1, win_bias)
        outs = []
        for r in range(rep):
            c0 = (rep * g + r) * 3
            outs.append(gates[:, c0:c0 + 1] * o_c[r] + gates[:, c0 + 1:c0 + 2] * o_s[r]
                        + gates[:, c0 + 2:c0 + 3] * o_w[r])
        for pp in range(rep // 2):
            a, bb = outs[2 * pp], outs[2 * pp + 1]
            if g == 0:
                bb = pltpu.roll(bb, B_HD, 1)
            else:
                a = pltpu.roll(a, B_HD, 1)
            c0 = (rep // 2 * g + pp) * LANES
            o_ref[0, :, c0:c0 + LANES] = jnp.where(lane < B_HD, a, bb).astype(BF16)


def _nsa_attn(u, g_block, qx, kcmp, vcmp, ks, vs, kw, vw):
    b, h, s, _ = qx.shape
    tq = min(128, s)
    tks = min(512, s)
    n_pad = kcmp.shape[1]
    n_blk = s // SEL_LEN
    n_sel = min(N_SEL, n_blk)
    n_cmp = (s - CMP_LEN) // CMP_STRIDE + 1
    expand = np.zeros((LANES, s), np.float32)
    expand[np.arange(s) // SEL_LEN, np.arange(s)] = 1.0
    nn = np.arange(n_pad)[:, None]
    jj = np.arange(LANES)[None, :]
    ovl = ((nn * CMP_STRIDE <= jj * SEL_LEN + SEL_LEN - 1) & (nn * CMP_STRIDE + CMP_LEN - 1 >= jj * SEL_LEN)
           & (jj < n_blk) & (nn < n_cmp)).astype(np.float32)
    full = lambda shape: pl.BlockSpec(shape, lambda bi, i: (bi,) + (0,) * (len(shape) - 1))
    const = lambda shape: pl.BlockSpec(shape, lambda bi, i: (0,) * len(shape))
    return pl.pallas_call(
        functools.partial(_nsa_attn_kernel, n_blk=n_blk, n_sel=n_sel, tks=tks),
        out_shape=jax.ShapeDtypeStruct((b, s, h * B_HD), BF16),
        grid=(b, s // tq),
        in_specs=[pl.BlockSpec((1, h, tq, LANES), lambda bi, i: (bi, 0, i, 0)),
                  full((1, n_pad, LANES)), full((1, n_pad, LANES)),
                  full((1, s, LANES)), full((1, s, LANES)), full((1, s, LANES)), full((1, s, LANES)),
                  pl.BlockSpec((1, tq, LANES), lambda bi, i: (bi, i, g_block)),
                  const((LANES, s)), const((n_pad, LANES))],
        out_specs=pl.BlockSpec((1, tq, h * B_HD), lambda bi, i: (bi, i, 0)),
        compiler_params=_cp("parallel", "parallel"),
        name="nsa_attn",
    )(qx, kcmp, vcmp, ks, vs, kw, vw, u, jnp.asarray(expand, BF16), jnp.asarray(ovl, F32))


def _dsa_prep_kernel(q_in, iq_in, kv_in, sm_in, cos64_ref, sin64_ref, cos32_ref, sin32_ref, qg_ref, kg_ref,
                     q_ref, kv_ref, iq_ref, ik_ref):
    tm = q_in.shape[1]
    lane = _lane((tm, LANES))
    c64, s64, c32, s32 = cos64_ref[0], sin64_ref[0], cos32_ref[0], sin32_ref[0]
    for p in range(C_HEADS // 2):
        y = _rope(_group_rms(q_in[0, :, p * LANES:(p + 1) * LANES], C_HD) * qg_ref[...], c64, s64, C_HD // 2)
        y = y * (C_HD ** -0.5)
        q_ref[0, 2 * p] = jnp.where(lane < C_HD, y, 0.0).astype(BF16)
        q_ref[0, 2 * p + 1] = jnp.where(lane < C_HD, pltpu.roll(y, C_HD, 1), 0.0).astype(BF16)
    kv = kv_in[0]
    kn = _rope(_group_rms(kv, C_HD) * kg_ref[...], c64, s64, C_HD // 2)
    kv_ref[0] = jnp.where(lane < C_HD, kn, kv).astype(BF16)
    for c0 in range(0, IDX_HEADS * IDX_HD, LANES):
        iq_ref[0, :, c0:c0 + LANES] = _rope(iq_in[0, :, c0:c0 + LANES], c32, s32, IDX_HD // 2).astype(BF16)
    ik = jnp.where(lane < IDX_HD, _rope(sm_in[0], c32, s32, IDX_HD // 2), 0.0)
    ik = ik + pltpu.roll(ik, IDX_HD, 1)
    ik = ik + pltpu.roll(ik, 2 * IDX_HD, 1)
    ik_ref[0] = ik.astype(BF16)


def _dsa_prep(u, cols, tabs, p):
    b, s, _ = u.shape
    tm = min(256, s)
    row = pl.BlockSpec((1, tm, LANES), lambda bi, i: (bi, i, 0))
    const = pl.BlockSpec((1, LANES), lambda bi, i: (0, 0))
    ublk = lambda name, w: pl.BlockSpec((1, tm, w), lambda bi, i, c=cols[name] // w: (bi, i, c))
    dense = jax.ShapeDtypeStruct((b, s, LANES), BF16)
    return pl.pallas_call(
        _dsa_prep_kernel,
        out_shape=(jax.ShapeDtypeStruct((b, C_HEADS, s, LANES), BF16), dense,
                   jax.ShapeDtypeStruct((b, s, IDX_HEADS * IDX_HD), BF16), dense),
        grid=(b, s // tm),
        in_specs=[ublk("c_q", 512), ublk("c_iq", 256), ublk("c_kv", LANES), ublk("small", LANES),
                  row, row, row, row, const, const],
        out_specs=(pl.BlockSpec((1, C_HEADS, tm, LANES), lambda bi, i: (bi, 0, i, 0)), row,
                   pl.BlockSpec((1, tm, IDX_HEADS * IDX_HD), lambda bi, i: (bi, i, 0)), row),
        compiler_params=_cp("parallel", "parallel"),
        name="dsa_prep",
    )(u, u, u, u, tabs["cos64"], tabs["sin64"], tabs["cos32"], tabs["sin32"], p["q_g"], p["k_g"])


def _dsa_attn_kernel(q_ref, iq_ref, sm_ref, ik_ref, kv_ref, tri_ref, o_ref, key_scr, bias_scr, *, tk, topk):
    tq = q_ref.shape[2]
    i = pl.program_id(1)
    t = i * tq + lax.broadcasted_iota(I32, (tq, 1), 0)
    lane = _lane((tq, LANES))
    n_tile = ((i + 1) * tq + tk - 1) // tk
    tile = lambda j: pl.ds(pl.multiple_of(j * tk, tk), tk)
    kpos = lambda j: j * tk + lax.broadcasted_iota(I32, (1, tk), 1)

    iw = sm_ref[0] * (IDX_HEADS ** -0.5)
    parts = []
    for h in range(IDX_HEADS):
        blk = iq_ref[0, :, (h // 4) * LANES:(h // 4 + 1) * LANES]
        lo = (h % 4) * IDX_HD
        parts.append(jnp.where((lane >= lo) & (lane < lo + IDX_HD), blk, jnp.zeros_like(blk)))
    iqs = jnp.concatenate(parts, axis=0)

    def idx_body(j, carry):
        lg = _nt(iqs, ik_ref[0, tile(j), :])
        acc = jnp.zeros((tq, tk), F32)
        for h in range(IDX_HEADS):
            acc = acc + iw[:, IDX_HD + h:IDX_HD + h + 1] * jnp.maximum(lg[h * tq:(h + 1) * tq], 0.0)
        sc = jnp.where(kpos(j) <= t, acc, NEG)
        bits = pltpu.bitcast(sc, I32)
        key = jnp.where(bits < 0, bits ^ 0x7FFFFFFF, bits)
        key_scr[:, tile(j)] = jnp.where(sc == 0.0, 0, key)
        return carry

    lax.fori_loop(0, n_tile, idx_body, 0)

    def count_ge(thr_key):
        def body(j, c):
            ge = jnp.where(key_scr[:, tile(j)] >= thr_key, 1, 0)
            for c0 in range(0, tk, LANES):
                c = c + ge[:, c0:c0 + LANES]
            return c

        c = lax.fori_loop(0, n_tile, body, jnp.zeros((tq, LANES), I32))
        return jnp.sum(c, axis=-1, keepdims=True)

    def bit_body(bi, ucand):
        utrial = ucand | jnp.left_shift(jnp.int32(1), 31 - bi)
        return jnp.where(count_ge(utrial ^ INT_MIN) >= topk, utrial, ucand)

    thr = lax.fori_loop(0, 32, bit_body, jnp.zeros((tq, 1), I32)) ^ INT_MIN
    n_ge = count_ge(thr)
    n_gt = count_ge(thr + 1)
    need = topk - n_gt
    row_ok = (n_ge - n_gt == need) | (thr == NEG_KEY) | (n_ge < topk)
    simple = jnp.min(jnp.where(row_ok, 1.0, 0.0)) > 0.5

    def fast_bias():
        def body(j, carry):
            keep = (key_scr[:, tile(j)] >= thr) & (kpos(j) <= t)
            bias_scr[:, tile(j)] = jnp.where(keep, 0.0, NEG)
            return carry

        lax.fori_loop(0, n_tile, body, 0)

    def tie_bias():
        need_f = need.astype(F32)

        def body(j, run):
            key = key_scr[:, tile(j)]
            kp = kpos(j)
            for c0 in range(0, tk, LANES):
                kc = key[:, c0:c0 + LANES]
                eq = kc == thr
                eq_f = jnp.where(eq, 1.0, 0.0)
                pref = _mm(eq_f.astype(BF16), tri_ref[...]) + run
                keep = ((kc > thr) | (eq & (pref <= need_f))) & (kp[:, c0:c0 + LANES] <= t)
                bias_scr[:, pl.ds(pl.multiple_of(j * tk + c0, LANES), LANES)] = jnp.where(keep, 0.0, NEG)
                run = run + jnp.sum(eq_f, axis=-1, keepdims=True)
            return run

        lax.fori_loop(0, n_tile, body, jnp.zeros((tq, 1), F32))

    lax.cond(simple, fast_bias, tie_bias)

    qs = jnp.concatenate([q_ref[0, h] for h in range(C_HEADS)], axis=0)

    def att_body(j, carry):
        m, l, acc = carry
        kv = kv_ref[0, tile(j), :]
        s = _nt(qs, kv).reshape(C_HEADS, tq, tk) + bias_scr[:, tile(j)][None]
        s = s.reshape(C_HEADS * tq, tk)
        m_new = jnp.maximum(m, jnp.max(s, axis=-1, keepdims=True))
        alpha = jnp.exp(m - m_new)
        p = jnp.exp(s - m_new)
        l = alpha * l + jnp.sum(p, axis=-1, keepdims=True)
        acc = alpha * acc + _mm(p.astype(BF16), kv)
        return m_new, l, acc

    rows = C_HEADS * tq
    init = (jnp.full((rows, 1), NEG, F32), jnp.zeros((rows, 1), F32), jnp.zeros((rows, LANES), F32))
    _, l, acc = lax.fori_loop(0, n_tile, att_body, init)
    o = acc / l
    for p in range(C_HEADS // 2):
        a = pltpu.roll(o[(2 * p) * tq:(2 * p + 1) * tq], C_HD, 1)
        bb = o[(2 * p + 1) * tq:(2 * p + 2) * tq]
        o_ref[0, :, p * LANES:(p + 1) * LANES] = jnp.where(lane < C_HD, a, bb).astype(BF16)


def _dsa_attn(u, small_block, qc, iq, ik, kv):
    b, h, s, _ = qc.shape
    tq = min(128, s)
    tk = min(512, s)
    topk = min(TOPK_MAX, s // 4)
    tri = np.triu(np.ones((LANES, LANES), np.float32))
    full = lambda shape: pl.BlockSpec(shape, lambda bi, i: (bi,) + (0,) * (len(shape) - 1))
    return pl.pallas_call(
        functools.partial(_dsa_attn_kernel, tk=tk, topk=topk),
        out_shape=jax.ShapeDtypeStruct((b, s, h * C_HD), BF16),
        grid=(b, s // tq),
        in_specs=[pl.BlockSpec((1, h, tq, LANES), lambda bi, i: (bi, 0, i, 0)),
                  pl.BlockSpec((1, tq, IDX_HEADS * IDX_HD), lambda bi, i: (bi, i, 0)),
                  pl.BlockSpec((1, tq, LANES), lambda bi, i: (bi, i, small_block)),
                  full((1, s, LANES)), full((1, s, LANES)),
                  pl.BlockSpec((LANES, LANES), lambda bi, i: (0, 0))],
        out_specs=pl.BlockSpec((1, tq, h * C_HD), lambda bi, i: (bi, i, 0)),
        scratch_shapes=[pltpu.VMEM((tq, s), I32), pltpu.VMEM((tq, s), F32)],
        compiler_params=_cp("parallel", "parallel"),
        name="dsa_attn",
    )(qc, iq, u, ik, kv, jnp.asarray(tri, BF16))


def _mlstm_conv_kernel(x_ref, halo_ref, w_ref, b_ref, o_ref):
    tm = x_ref.shape[1]
    i = pl.program_id(1)
    lane = _lane((tm, x_ref.shape[2]))
    halo = jnp.where(i > 0, halo_ref[0], 0.0)
    xc = jnp.concatenate([halo, x_ref[0]], axis=0)
    off = halo.shape[0] - (CONV_W - 1)
    y = b_ref[...] + jnp.zeros_like(x_ref[0])
    for j in range(CONV_W):
        y = y + w_ref[j:j + 1, :] * xc[off + j:off + j + tm, :]
    y = y * _sigmoid(y)
    o_ref[0] = jnp.where(lane >= D_HEADS * D_QK, y * (D_QK ** -0.5), y)


def _mlstm_conv(u, qk_block, w, bias):
    b, s, _ = u.shape
    c = w.shape[1]
    tm = min(512, s)
    hb = 8
    return pl.pallas_call(
        _mlstm_conv_kernel,
        out_shape=jax.ShapeDtypeStruct((b, s, c), F32),
        grid=(b, s // tm),
        in_specs=[pl.BlockSpec((1, tm, c), lambda bi, i: (bi, i, qk_block)),
                  pl.BlockSpec((1, hb, c), lambda bi, i: (bi, jnp.maximum(i * (tm // hb) - 1, 0), qk_block)),
                  pl.BlockSpec((CONV_W, c), lambda bi, i: (0, 0)),
                  pl.BlockSpec((1, c), lambda bi, i: (0, 0))],
        out_specs=pl.BlockSpec((1, tm, c), lambda bi, i: (bi, i, 0)),
        compiler_params=_cp("parallel", "parallel"),
        name="mlstm_conv",
    )(u, u, w, bias)


def _mlstm_scan_kernel(qk_ref, v_ref, op_ref, sm_ref, gt_ref, bcol_ref, brow_ref, hg_ref, o_ref,
                       c_scr, n_scr, m_scr):
    tc = qk_ref.shape[1]
    L = CHUNK
    nqk = D_HEADS * D_QK

    @pl.when(pl.program_id(1) == 0)
    def _():
        c_scr[...] = jnp.zeros_like(c_scr)
        n_scr[...] = jnp.zeros_like(n_scr)
        m_scr[...] = jnp.zeros_like(m_scr)

    r = lax.broadcasted_iota(I32, (L, L), 0)
    c = lax.broadcasted_iota(I32, (L, L), 1)
    tril = c <= r
    tril_f = jnp.where(tril, 1.0, 0.0).astype(F32)
    triu_f = jnp.where(r <= c, 1.0, 0.0).astype(F32)
    lane = _lane((L, LANES))
    row128 = lax.broadcasted_iota(I32, (LANES, LANES), 0)
    i_lane, f_lane = IDX_HD + IDX_HEADS, IDX_HD + IDX_HEADS + D_HEADS

    def chunk(ci, carry):
        rows = pl.ds(pl.multiple_of(ci * L, L), L)
        qk = qk_ref[0, rows, :]
        sm = sm_ref[0, rows, :] + bcol_ref[...]
        bcol_all = jnp.dot(tril_f, _log_sigmoid(sm), preferred_element_type=F32, precision=HI)
        gt = gt_ref[0, ci] + brow_ref[...]
        brow_all = jnp.dot(_log_sigmoid(gt), triu_f, preferred_element_type=F32, precision=HI)
        for p in range(D_HEADS // 2):
            qpair = qk[:, p * LANES:(p + 1) * LANES]
            kpair = qk[:, nqk + p * LANES:nqk + (p + 1) * LANES]
            kpair_b = kpair.astype(BF16)
            c_prev = c_scr[p]
            n_prev = n_scr[p, 0:1, :]
            c_prev_b = c_prev.astype(BF16)
            upd = []
            for e in range(2):
                h = 2 * p + e
                in_half = (lane >= D_QK * e) & (lane < D_QK * (e + 1))
                qm = jnp.where(in_half, qpair, 0.0)
                qm_b = qm.astype(BF16)
                vh = v_ref[0, rows, h * D_VD:(h + 1) * D_VD]
                b_col = bcol_all[:, f_lane + h:f_lane + h + 1]
                i_col = sm[:, i_lane + h:i_lane + h + 1]
                b_row = brow_all[D_HEADS + h:D_HEADS + h + 1, :]
                i_row = gt[h:h + 1, :]
                m_prev = m_scr[h, 0:1, 0:1]
                dmat = jnp.where(tril, b_col - b_row + i_row, NEG)
                inter = b_col + m_prev
                m_t = jnp.maximum(inter, jnp.max(dmat, axis=-1, keepdims=True))
                a = jnp.exp(inter - m_t)
                w = _nt(qm_b, kpair_b) * jnp.exp(dmat - m_t)
                num = a * _mm(qm_b, c_prev_b) + _mm(w.astype(BF16), vh.astype(BF16))
                den = a * jnp.sum(qm * n_prev, axis=-1, keepdims=True) + jnp.sum(w, axis=-1, keepdims=True)
                hout = num / jnp.maximum(jnp.abs(den), jnp.exp(-m_t))
                b_last = b_col[L - 1:L, :]
                g_col = b_last - b_col + i_col
                m_new = jnp.maximum(b_last + m_prev, jnp.max(g_col, axis=0, keepdims=True))
                ws = jnp.exp(g_col - m_new)
                decay = jnp.exp(b_last + m_prev - m_new)
                u_mat = _tn(kpair_b, (ws * vh).astype(BF16))
                k_sum = jnp.sum(ws * kpair, axis=0, keepdims=True)
                upd.append((decay * c_prev + u_mat, decay * n_prev + k_sum))
                m_scr[h] = jnp.broadcast_to(m_new, m_scr.shape[1:])
                hn = _row_rms(hout) * hg_ref[...]
                y = _sigmoid(op_ref[0, rows, h * D_VD:(h + 1) * D_VD]) * hn
                o_ref[0, rows, h * D_VD:(h + 1) * D_VD] = y.astype(BF16)
            c_scr[p] = jnp.where(row128 < D_QK, upd[0][0], upd[1][0])
            n_new = jnp.where(lane[0:1] < D_QK, upd[0][1], upd[1][1])
            n_scr[p] = jnp.broadcast_to(n_new, n_scr.shape[1:])
        return carry

    lax.fori_loop(0, tc // L, chunk, 0)


def _mlstm_scan(u, cols, qk, gt, p):
    b, s, _ = u.shape
    tc = min(512, s)
    nv = D_HEADS * D_VD
    ublk = lambda name, w: pl.BlockSpec((1, tc, w), lambda bi, i, c=cols[name] // w: (bi, i, c))
    const = lambda shape: pl.BlockSpec(shape, lambda bi, i: (0,) * len(shape))
    return pl.pallas_call(
        _mlstm_scan_kernel,
        out_shape=jax.ShapeDtypeStruct((b, s, nv), BF16),
        grid=(b, s // tc),
        in_specs=[pl.BlockSpec((1, tc, qk.shape[-1]), lambda bi, i: (bi, i, 0)),
                  ublk("d_v", nv), ublk("d_o", nv), ublk("small", LANES),
                  pl.BlockSpec((1, tc // CHUNK, 8, CHUNK), lambda bi, i: (bi, i, 0, 0)),
                  const((1, LANES)), const((8, 1)), const((1, D_VD))],
        out_specs=pl.BlockSpec((1, tc, nv), lambda bi, i: (bi, i, 0)),
        scratch_shapes=[pltpu.VMEM((D_HEADS // 2, LANES, LANES), F32),
                        pltpu.VMEM((D_HEADS // 2, 8, LANES), F32),
                        pltpu.VMEM((D_HEADS, 8, LANES), F32)],
        compiler_params=_cp("parallel", "arbitrary"),
        name="mlstm_scan",
    )(qk, u, u, u, gt, p["bias_col"], p["bias_row"], p["h_g"])


EVEN_SRC = dict(a_ql=(0, 256), a_kvl=(256, 128), a_kr=(384, 32), a_gate=(416, 512), b_q=(928, 512),
                b_kc=(1440, 128), b_vc=(1568, 128), b_ks=(1696, 128), b_vs=(1824, 128), b_kw=(1952, 128),
                b_vw=(2080, 128), b_g=(2208, 24), b_gate=(2232, 512), m_q=(2744, 256), m_gate=(3000, 256))
EVEN_DST = dict(a_ql=0, a_kvl=256, a_kr=384, b_q=512, a_gate=1024, b_gate=1536, m_q=2048, m_gate=2304,
                b_g=2560, b_kc=2688, b_vc=2816, b_ks=2944, b_vs=3072, b_kw=3200, b_vw=3328)
EVEN_COLS_PAD = 3456

ODD_SRC = dict(c_q=(0, 512), c_k=(512, 64), c_v=(576, 64), c_iq=(640, 256), c_ik=(896, 32), c_iw=(928, 8),
               c_gate=(936, 512), d_q=(1448, 256), d_k=(1704, 256), d_v=(1960, 512), d_i=(2472, 4),
               d_f=(2476, 4), d_o=(2480, 512), d_gate=(2992, 512), m_q=(3504, 256), m_gate=(3760, 256))
ODD_DST = dict(c_q=0, c_gate=512, d_gate=1024, d_v=1536, d_o=2048, d_q=2560, d_k=2816, m_q=3072, m_gate=3328,
               c_iq=3584, c_k=3840, c_v=3904, c_ik=3968, c_iw=4000, d_i=4008, d_f=4012)
ODD_COLS_PAD = 4096


def _permute_cols(w, src, dst, total):
    idx = np.zeros((total,), np.int32)
    keep = np.zeros((total,), np.float32)
    for name, (start, width) in src.items():
        idx[dst[name]:dst[name] + width] = np.arange(start, start + width)
        keep[dst[name]:dst[name] + width] = 1.0
    return (jnp.take(w, jnp.asarray(idx), axis=1) * jnp.asarray(keep)).astype(BF16)


def _tile_lanes(v, reps):
    return jnp.tile(v.astype(F32).reshape(1, -1), (1, reps))


def _rope_tables(positions, d2):
    inv = ROPE_THETA ** (-jnp.arange(d2, dtype=F32) / d2)
    ang = positions.astype(F32)[..., None] * inv
    c, s = jnp.cos(ang), jnp.sin(ang)
    reps = LANES // (2 * d2)
    return (jnp.tile(jnp.concatenate([c, c], axis=-1), (1, 1, reps)),
            jnp.tile(jnp.concatenate([-s, s], axis=-1), (1, 1, reps)))


def _block_diag2(w):
    z = jnp.zeros_like(w)
    return jnp.concatenate([jnp.concatenate([w, z], axis=-1), jnp.concatenate([z, w], axis=-1)], axis=-2)


def kernel(x, mem, positions, ln_g, mem_norm_g, mem_w_kv, mem_q_norm_g, mem_k_norm_g, w_out, even_w_in, mla_q_lat_g, mla_kv_lat_g, mla_w_uq, mla_w_ukv, mla_q_norm_g, mla_k_norm_g, nsa_q_norm_g, nsa_k_norm_g, nsa_cmp_pos, nsa_cmp_w1, nsa_cmp_w2, odd_w_in, dsa_q_norm_g, dsa_k_norm_g, mlstm_conv_w, mlstm_conv_b, mlstm_i_bias, mlstm_f_bias, mlstm_h_norm_g):
    b, s, _ = x.shape
    depth = ln_g.shape[0]
    cos64, sin64 = _rope_tables(positions, 32)
    cos32, sin32 = _rope_tables(positions, 16)
    tabs = dict(cos64=cos64, sin64=sin64, cos32=cos32, sin32=sin32)
    n_pad = s // CMP_STRIDE
    cmp_pos = jnp.pad(positions[:, CMP_LEN - 1::CMP_STRIDE], ((0, 0), (0, 0)))[:, :n_pad]
    cmp_pos = jnp.pad(cmp_pos, ((0, 0), (0, n_pad - cmp_pos.shape[1])))
    cosc, sinc = _rope_tables(cmp_pos, 32)

    mem_k, mem_v = _mem_kv(mem, mem_norm_g.reshape(depth, 1, -1), mem_w_kv.astype(BF16),
                           jnp.tile(mem_k_norm_g, (1, 2)).reshape(depth, 1, LANES))

    hq = np.arange(A_HEADS)[:, None] * (A_NOPE + A_ROPE)
    uq_idx = np.concatenate([(hq + np.arange(A_NOPE)[None, :]).ravel(),
                             (hq + A_NOPE + np.arange(A_ROPE)[None, :]).ravel()])
    hk = np.arange(A_HEADS)[:, None] * (A_NOPE + A_VD)
    ukv_idx = np.concatenate([(hk + np.arange(A_NOPE)[None, :]).ravel(),
                              (hk + A_NOPE + np.arange(A_VD)[None, :]).ravel()])

    for layer in range(depth):
        li = layer // 2
        g_ln = ln_g[layer].reshape(1, -1)
        mq_g = _tile_lanes(mem_q_norm_g[layer], 2)
        if layer % 2 == 0:
            cols = EVEN_DST
            u = _in_proj(x, g_ln, _permute_cols(even_w_in[li], EVEN_SRC, EVEN_DST, EVEN_COLS_PAD))
            pa = dict(q_lat_g=mla_q_lat_g[li].reshape(1, -1), kv_lat_g=mla_kv_lat_g[li].reshape(1, -1),
                      w_uq=jnp.take(mla_w_uq[li], jnp.asarray(uq_idx), axis=1).astype(BF16),
                      w_ukv=jnp.take(mla_w_ukv[li], jnp.asarray(ukv_idx), axis=1).astype(BF16),
                      qn_g=_tile_lanes(mla_q_norm_g[li, :A_NOPE], 2), qr_g=_tile_lanes(mla_q_norm_g[li, A_NOPE:], 4),
                      kn_g=_tile_lanes(mla_k_norm_g[li, :A_NOPE], 2), kr_g=_tile_lanes(mla_k_norm_g[li, A_NOPE:], 4))
            qa, ka, va = _mla_prep(u, cos32, sin32, pa)
            y1 = _mla_attn(qa, ka, va)
            pb = dict(q_g=_tile_lanes(nsa_q_norm_g[li], 2), ks_g=_tile_lanes(nsa_k_norm_g[li, 1], 2),
                      kw_g=_tile_lanes(nsa_k_norm_g[li, 2], 2), kc_g=_tile_lanes(nsa_k_norm_g[li, 0], 2),
                      pe_k=jnp.tile(nsa_cmp_pos[li, 0], (1, 2)), pe_v=jnp.tile(nsa_cmp_pos[li, 1], (1, 2)),
                      w1k=_block_diag2(nsa_cmp_w1[li, 0].reshape(CMP_LEN, B_HD, B_HD)).astype(BF16),
                      w1v=_block_diag2(nsa_cmp_w1[li, 1].reshape(CMP_LEN, B_HD, B_HD)).astype(BF16),
                      w2k=_block_diag2(nsa_cmp_w2[li, 0]).astype(BF16),
                      w2v=_block_diag2(nsa_cmp_w2[li, 1]).astype(BF16))
            qb, ks, vs, kw, vw = _nsa_prep(u, cols, cos64, sin64, pb)
            kcmp, vcmp = _nsa_cmp(u, cols, cosc, sinc, pb)
            y2 = _nsa_attn(u, cols["b_g"] // LANES, qb, kcmp, vcmp, ks, vs, kw, vw)
            gate_blocks = (cols["a_gate"] // 512, cols["b_gate"] // 512, cols["m_gate"] // 256)
        else:
            cols = dict(ODD_DST, c_kv=ODD_DST["c_k"], small=ODD_DST["c_ik"])
            u = _in_proj(x, g_ln, _permute_cols(odd_w_in[li], ODD_SRC, ODD_DST, ODD_COLS_PAD))
            pc = dict(q_g=_tile_lanes(dsa_q_norm_g[li], 2), k_g=_tile_lanes(dsa_k_norm_g[li], 2))
            qc, kvc, iq, ik = _dsa_prep(u, cols, tabs, pc)
            y1 = _dsa_attn(u, cols["small"] // LANES, qc, iq, ik, kvc)
            qk = _mlstm_conv(u, cols["d_q"] // 512, mlstm_conv_w[li], mlstm_conv_b[li].reshape(1, -1))
            gates = u[:, :, cols["d_i"]:cols["d_i"] + 2 * D_HEADS]
            gt = gates.reshape(b, s // CHUNK, CHUNK, 2 * D_HEADS).transpose(0, 1, 3, 2)
            bias8 = jnp.concatenate([mlstm_i_bias[li], mlstm_f_bias[li]]).astype(F32)
            bias_col = jnp.zeros((1, LANES), F32).at[0, cols["d_i"] - cols["small"]:cols["d_i"] - cols["small"] + 8].set(bias8)
            pd = dict(bias_col=bias_col, bias_row=bias8.reshape(8, 1), h_g=mlstm_h_norm_g[li].reshape(1, -1))
            y2 = _mlstm_scan(u, cols, qk, gt, pd)
            gate_blocks = (cols["c_gate"] // 512, cols["d_gate"] // 512, cols["m_gate"] // 256)
        ym = _mem_attn(u, cols["m_q"] // 256, mem_k, mem_v, layer, mq_g)
        x = _out_proj(x, y1, y2, ym, u, gate_blocks, w_out[layer].astype(BF16))
    return x
```

```python
import functools

import numpy as np
import jax
import jax.numpy as jnp
from jax import lax
from jax.experimental import pallas as pl
from jax.experimental.pallas import tpu as pltpu

F32, BF16, I32 = jnp.float32, jnp.bfloat16, jnp.int32
HI = lax.Precision.HIGHEST
NEG = -1e30
EPS = 1e-6
ROPE_THETA = 10000.0
LANES = 128
VMEM_LIMIT_BYTES = 48 * 1024 * 1024

D_MODEL = 1024
DEPTH = 4
A_HEADS, A_NOPE, A_ROPE, A_VD, A_QLAT, A_KVLAT = 8, 64, 32, 64, 256, 128
B_HEADS, B_KV_HEADS, B_HD = 8, 2, 64
CMP_LEN, CMP_STRIDE, SEL_LEN, N_SEL, WINDOW = 32, 16, 64, 16, 512
C_HEADS, C_HD, IDX_HEADS, IDX_HD, TOPK_MAX = 8, 64, 8, 32, 256
D_HEADS, D_QK, D_VD, CONV_W, CHUNK = 4, 64, 128, 4, 64
M_HEADS, M_HD = 4, 64

INT_MIN = np.int32(-2 ** 31)
NEG_KEY = int(np.float32(NEG).view(np.int32) ^ np.int32(0x7FFFFFFF))


def _cp(*sem):
    return pltpu.CompilerParams(dimension_semantics=sem, vmem_limit_bytes=VMEM_LIMIT_BYTES)


def _nt(a, b):
    return lax.dot_general(a, b, (((1,), (1,)), ((), ())), preferred_element_type=F32)


def _tn(a, b):
    return lax.dot_general(a, b, (((0,), (0,)), ((), ())), preferred_element_type=F32)


def _mm(a, b):
    return jnp.dot(a, b, preferred_element_type=F32)


def _sigmoid(x):
    return 1.0 / (1.0 + jnp.exp(-x))


def _log_sigmoid(x):
    return jnp.minimum(x, 0.0) - jnp.log1p(jnp.exp(-jnp.abs(x)))


def _lane(shape):
    return lax.broadcasted_iota(I32, shape, len(shape) - 1)


def _group_mat(gs):
    r = lax.broadcasted_iota(I32, (LANES, LANES), 0)
    c = lax.broadcasted_iota(I32, (LANES, LANES), 1)
    sh = gs.bit_length() - 1
    return jnp.where((r >> sh) == (c >> sh), 1.0, 0.0).astype(F32)


def _group_rms(x, gs):
    ss = jnp.dot(x * x, _group_mat(gs), preferred_element_type=F32, precision=HI)
    return x * lax.rsqrt(ss * (1.0 / gs) + EPS)


def _rope(x, cosp, sinp, half):
    lane = _lane(x.shape)
    rot = jnp.where((lane & (2 * half - 1)) < half,
                    pltpu.roll(x, LANES - half, 1), pltpu.roll(x, half, 1))
    return x * cosp + rot * sinp


def _row_rms(x):
    return x * lax.rsqrt(jnp.mean(x * x, axis=-1, keepdims=True) + EPS)


def _in_proj_kernel(x_ref, g_ref, w_ref, o_ref):
    h = (_row_rms(x_ref[0]) * g_ref[...]).astype(BF16)
    ncol = o_ref.shape[-1]
    for c0 in range(0, ncol, 512):
        c1 = min(ncol, c0 + 512)
        o_ref[0, :, c0:c1] = _mm(h, w_ref[:, c0:c1])


def _in_proj(x, g, w):
    b, s, d = x.shape
    c = w.shape[1]
    tm = min(256, s)
    return pl.pallas_call(
        _in_proj_kernel,
        out_shape=jax.ShapeDtypeStruct((b, s, c), F32),
        grid=(b, s // tm),
        in_specs=[pl.BlockSpec((1, tm, d), lambda bi, i: (bi, i, 0)),
                  pl.BlockSpec((1, d), lambda bi, i: (0, 0)),
                  pl.BlockSpec((d, c), lambda bi, i: (0, 0))],
        out_specs=pl.BlockSpec((1, tm, c), lambda bi, i: (bi, i, 0)),
        compiler_params=_cp("parallel", "parallel"),
        name="in_proj",
    )(x, g, w)


def _out_proj_kernel(x_ref, y1_ref, y2_ref, ym_ref, g1_ref, g2_ref, gm_ref, w_ref, o_ref):
    def gated(y_ref, g_ref):
        g = g_ref[0]
        return (y_ref[0].astype(F32) * (g * _sigmoid(g))).astype(BF16)

    n1 = y1_ref.shape[-1]
    n2 = y2_ref.shape[-1]
    acc = x_ref[0]
    acc = acc + _mm(gated(y1_ref, g1_ref), w_ref[0:n1, :])
    acc = acc + _mm(gated(y2_ref, g2_ref), w_ref[n1:n1 + n2, :])
    acc = acc + _mm(gated(ym_ref, gm_ref), w_ref[n1 + n2:, :])
    o_ref[0] = acc


def _out_proj(x, y1, y2, ym, u, gate_blocks, w):
    b, s, d = x.shape
    tm = min(512, s)
    i1, i2, im = gate_blocks
    n1, n2, nm = y1.shape[-1], y2.shape[-1], ym.shape[-1]
    row = lambda bi, i: (bi, i, 0)
    return pl.pallas_call(
        _out_proj_kernel,
        out_shape=jax.ShapeDtypeStruct((b, s, d), F32),
        grid=(b, s // tm),
        in_specs=[pl.BlockSpec((1, tm, d), row),
                  pl.BlockSpec((1, tm, n1), row),
                  pl.BlockSpec((1, tm, n2), row),
                  pl.BlockSpec((1, tm, nm), row),
                  pl.BlockSpec((1, tm, n1), lambda bi, i: (bi, i, i1)),
                  pl.BlockSpec((1, tm, n2), lambda bi, i: (bi, i, i2)),
                  pl.BlockSpec((1, tm, nm), lambda bi, i: (bi, i, im)),
                  pl.BlockSpec(w.shape, lambda bi, i: (0, 0))],
        out_specs=pl.BlockSpec((1, tm, d), row),
        compiler_params=_cp("parallel", "parallel"),
        name="out_proj",
    )(x, y1, y2, ym, u, u, u, w)


def _mem_kv_kernel(mem_ref, g_ref, w_ref, kg_ref, k_ref, v_ref):
    h = (_row_rms(mem_ref[0]) * g_ref[0]).astype(BF16)
    kv = _mm(h, w_ref[0])
    nk = k_ref.shape[-1]
    for c0 in range(0, nk, LANES):
        k_ref[0, 0, :, c0:c0 + LANES] = (_group_rms(kv[:, c0:c0 + LANES], M_HD) * kg_ref[0]).astype(BF16)
    v_ref[0, 0] = kv[:, nk:].astype(BF16)


def _mem_kv(mem, g, w, kg):
    b, m, d = mem.shape
    depth = w.shape[0]
    nk = M_HEADS * M_HD
    out = jax.ShapeDtypeStruct((depth, b, m, nk), BF16)
    return pl.pallas_call(
        _mem_kv_kernel,
        out_shape=(out, out),
        grid=(depth, b),
        in_specs=[pl.BlockSpec((1, m, d), lambda l, bi: (bi, 0, 0)),
                  pl.BlockSpec((1, 1, d), lambda l, bi: (l, 0, 0)),
                  pl.BlockSpec((1, d, 2 * nk), lambda l, bi: (l, 0, 0)),
                  pl.BlockSpec((1, 1, LANES), lambda l, bi: (l, 0, 0))],
        out_specs=(pl.BlockSpec((1, 1, m, nk), lambda l, bi: (l, bi, 0, 0)),
                   pl.BlockSpec((1, 1, m, nk), lambda l, bi: (l, bi, 0, 0))),
        compiler_params=_cp("parallel", "parallel"),
        name="mem_kv",
    )(mem, g, w, kg)


def _mem_attn_kernel(q_ref, k_ref, v_ref, qg_ref, o_ref):
    tq = q_ref.shape[1]
    lane = _lane((tq, LANES))
    for p in range(M_HEADS // 2):
        sl = slice(p * LANES, (p + 1) * LANES)
        q = (_group_rms(q_ref[0, :, sl], M_HD) * qg_ref[...] * (M_HD ** -0.5)).astype(BF16)
        k = k_ref[0, 0, :, sl]
        v = v_ref[0, 0, :, sl]
        halves = []
        for e in range(2):
            in_half = (lane >= 64 * e) & (lane < 64 * e + 64)
            s = _nt(jnp.where(in_half, q, jnp.zeros_like(q)), k)
            ex = jnp.exp(s - jnp.max(s, axis=-1, keepdims=True))
            o = _mm(ex.astype(BF16), v) / jnp.sum(ex, axis=-1, keepdims=True)
            halves.append(o)
        o_ref[0, :, sl] = jnp.where(lane < 64, halves[0], halves[1]).astype(BF16)


def _mem_attn(u, q_block, k, v, layer, qg):
    b, s, _ = u.shape
    m, nk = k.shape[2], k.shape[3]
    tq = min(512, s)
    return pl.pallas_call(
        _mem_attn_kernel,
        out_shape=jax.ShapeDtypeStruct((b, s, nk), BF16),
        grid=(b, s // tq),
        in_specs=[pl.BlockSpec((1, tq, nk), lambda bi, i: (bi, i, q_block)),
                  pl.BlockSpec((1, 1, m, nk), lambda bi, i: (layer, bi, 0, 0)),
                  pl.BlockSpec((1, 1, m, nk), lambda bi, i: (layer, bi, 0, 0)),
                  pl.BlockSpec((1, LANES), lambda bi, i: (0, 0))],
        out_specs=pl.BlockSpec((1, tq, nk), lambda bi, i: (bi, i, 0)),
        compiler_params=_cp("parallel", "parallel"),
        name="mem_attn",
    )(u, k, v, qg)


def _flash(qs, k_at, v_at, j0, j1, bias_at, carry=None):
    def body(j, c):
        bias = bias_at(j)
        return tuple(_softmax_step(q, k_at(j, n), v_at(j), bias, cn) for n, (q, cn) in enumerate(zip(qs, c)))

    if carry is None:
        carry = tuple(_flash_init(q.shape[0]) for q in qs)
    return lax.fori_loop(j0, j1, body, carry)


def _flash_init(rows):
    return (jnp.full((rows, 1), NEG, F32), jnp.zeros((rows, 1), F32), jnp.zeros((rows, LANES), F32))


def _softmax_step(q, k, v, bias, carry):
    m, l, acc = carry
    s = _nt(q, k)
    if bias is not None:
        rep = q.shape[0] // bias.shape[0]
        s = (s.reshape(rep, bias.shape[0], s.shape[1]) + bias[None]).reshape(s.shape)
    m_new = jnp.maximum(m, jnp.max(s, axis=-1, keepdims=True))
    alpha = jnp.exp(m - m_new)
    p = jnp.exp(s - m_new)
    l = alpha * l + jnp.sum(p, axis=-1, keepdims=True)
    acc = alpha * acc + _mm(p.astype(BF16), v)
    return m_new, l, acc


def _mla_prep_kernel(u_ref, cos_ref, sin_ref, qlg_ref, kvlg_ref, wuq_ref, wukv_ref,
                     qng_ref, qrg_ref, kng_ref, krg_ref, q_ref, k_ref, v_ref):
    tm = u_ref.shape[1]
    lane = _lane((tm, LANES))
    cosp, sinp = cos_ref[0], sin_ref[0]
    half = A_ROPE // 2
    scale = (A_NOPE + A_ROPE) ** -0.5
    ql = (_row_rms(u_ref[0, :, 0:A_QLAT]) * qlg_ref[...]).astype(BF16)
    kvl = (_row_rms(u_ref[0, :, A_QLAT:A_QLAT + A_KVLAT]) * kvlg_ref[...]).astype(BF16)
    q = _mm(ql, wuq_ref[...])
    kv = _mm(kvl, wukv_ref[...])
    n_nope = A_HEADS * A_NOPE
    v_ref[0] = kv[:, n_nope:].astype(BF16)
    kr = u_ref[0, :, A_QLAT + A_KVLAT:A_QLAT + A_KVLAT + LANES]
    kpe = _rope(_group_rms(kr, A_ROPE) * krg_ref[...], cosp, sinp, half)
    kpe = pltpu.roll(kpe, A_NOPE, 1)
    qn = [_group_rms(q[:, c:c + LANES], A_NOPE) * qng_ref[...] for c in range(0, n_nope, LANES)]
    kn = [_group_rms(kv[:, c:c + LANES], A_NOPE) * kng_ref[...] for c in range(0, n_nope, LANES)]
    qr = [_rope(_group_rms(q[:, n_nope + c:n_nope + c + LANES], A_ROPE) * qrg_ref[...], cosp, sinp, half)
          for c in range(0, A_HEADS * A_ROPE, LANES)]
    for h in range(A_HEADS):
        qn_h = qn[h // 2] if h % 2 == 0 else pltpu.roll(qn[h // 2], A_NOPE, 1)
        kn_h = kn[h // 2] if h % 2 == 0 else pltpu.roll(kn[h // 2], A_NOPE, 1)
        shift = (A_NOPE - (h % 4) * A_ROPE) % LANES
        qr_h = qr[h // 4] if shift == 0 else pltpu.roll(qr[h // 4], shift, 1)
        qf = jnp.where(lane < A_NOPE, qn_h, jnp.where(lane < A_NOPE + A_ROPE, qr_h, 0.0))
        kf = jnp.where(lane < A_NOPE, kn_h, jnp.where(lane < A_NOPE + A_ROPE, kpe, 0.0))
        q_ref[0, h] = (qf * scale).astype(BF16)
        k_ref[0, h] = kf.astype(BF16)


def _mla_prep(u, cos32, sin32, p):
    b, s, _ = u.shape
    tm = min(256, s)
    hd = jax.ShapeDtypeStruct((b, A_HEADS, s, LANES), BF16)
    const = lambda shape: pl.BlockSpec(shape, lambda bi, i: (0,) * len(shape))
    return pl.pallas_call(
        _mla_prep_kernel,
        out_shape=(hd, hd, jax.ShapeDtypeStruct((b, s, A_HEADS * A_VD), BF16)),
        grid=(b, s // tm),
        in_specs=[pl.BlockSpec((1, tm, 512), lambda bi, i: (bi, i, 0)),
                  pl.BlockSpec((1, tm, LANES), lambda bi, i: (bi, i, 0)),
                  pl.BlockSpec((1, tm, LANES), lambda bi, i: (bi, i, 0)),
                  const((1, A_QLAT)), const((1, A_KVLAT)),
                  const(p["w_uq"].shape), const(p["w_ukv"].shape),
                  const((1, LANES)), const((1, LANES)), const((1, LANES)), const((1, LANES))],
        out_specs=(pl.BlockSpec((1, A_HEADS, tm, LANES), lambda bi, i: (bi, 0, i, 0)),
                   pl.BlockSpec((1, A_HEADS, tm, LANES), lambda bi, i: (bi, 0, i, 0)),
                   pl.BlockSpec((1, tm, A_HEADS * A_VD), lambda bi, i: (bi, i, 0))),
        compiler_params=_cp("parallel", "parallel"),
        name="mla_prep",
    )(u, cos32, sin32, p["q_lat_g"], p["kv_lat_g"], p["w_uq"], p["w_ukv"],
      p["qn_g"], p["qr_g"], p["kn_g"], p["kr_g"])


def _mla_attn_kernel(q_ref, k_ref, v_ref, o_ref):
    tq = q_ref.shape[2]
    i = pl.program_id(2)
    lane = _lane((tq, LANES))
    r = lax.broadcasted_iota(I32, (tq, tq), 0)
    c = lax.broadcasted_iota(I32, (tq, tq), 1)
    diag_bias = jnp.where(c <= r, 0.0, NEG).astype(F32)
    qs = [q_ref[0, 0], q_ref[0, 1]]
    k_at = lambda j, e: k_ref[0, e, pl.ds(pl.multiple_of(j * tq, tq), tq), :]
    v_at = lambda j: v_ref[0, pl.ds(pl.multiple_of(j * tq, tq), tq), :]
    carry = _flash(qs, k_at, v_at, 0, i, lambda j: None)
    res = [_softmax_step(qs[e], k_at(i, e), v_at(i), diag_bias, carry[e]) for e in range(2)]
    res = [acc / l for (_, l, acc) in res]
    o_ref[0] = jnp.where(lane < A_VD, res[0], res[1]).astype(BF16)


def _mla_attn(q, k, v):
    b, h, s, _ = q.shape
    tq = min(512, s)
    return pl.pallas_call(
        _mla_attn_kernel,
        out_shape=jax.ShapeDtypeStruct((b, s, h * A_VD), BF16),
        grid=(b, h // 2, s // tq),
        in_specs=[pl.BlockSpec((1, 2, tq, LANES), lambda bi, p, i: (bi, p, i, 0)),
                  pl.BlockSpec((1, 2, s, LANES), lambda bi, p, i: (bi, p, 0, 0)),
                  pl.BlockSpec((1, s, LANES), lambda bi, p, i: (bi, 0, p))],
        out_specs=pl.BlockSpec((1, tq, LANES), lambda bi, p, i: (bi, i, p)),
        compiler_params=_cp("parallel", "parallel", "parallel"),
        name="mla_attn",
    )(q, k, v)


def _nsa_prep_kernel(q_in, ks_in, vs_in, kw_in, vw_in, cos_ref, sin_ref, qg_ref, ksg_ref, kwg_ref,
                     q_ref, ks_ref, vs_ref, kw_ref, vw_ref):
    tm = q_in.shape[1]
    lane = _lane((tm, LANES))
    cosp, sinp = cos_ref[0], sin_ref[0]
    half = B_HD // 2
    rep = B_HEADS // B_KV_HEADS
    for p in range(B_HEADS // 2):
        y = _rope(_group_rms(q_in[0, :, p * LANES:(p + 1) * LANES], B_HD) * qg_ref[...], cosp, sinp, half)
        y = y * (B_HD ** -0.5)
        y_sw = pltpu.roll(y, B_HD, 1)
        for e in range(2):
            h = 2 * p + e
            g = h // rep
            src = y if e == g else y_sw
            in_grp = (lane >= B_HD * g) & (lane < B_HD * (g + 1))
            q_ref[0, h] = jnp.where(in_grp, src, 0.0).astype(BF16)
    ks_ref[0] = _rope(_group_rms(ks_in[0], B_HD) * ksg_ref[...], cosp, sinp, half).astype(BF16)
    kw_ref[0] = _rope(_group_rms(kw_in[0], B_HD) * kwg_ref[...], cosp, sinp, half).astype(BF16)
    vs_ref[0] = vs_in[0].astype(BF16)
    vw_ref[0] = vw_in[0].astype(BF16)


def _nsa_prep(u, cols, cos64, sin64, p):
    b, s, _ = u.shape
    tm = min(256, s)
    blk = lambda name: pl.BlockSpec((1, tm, LANES), lambda bi, i, c=cols[name] // LANES: (bi, i, c))
    row = pl.BlockSpec((1, tm, LANES), lambda bi, i: (bi, i, 0))
    const = pl.BlockSpec((1, LANES), lambda bi, i: (0, 0))
    kvs = jax.ShapeDtypeStruct((b, s, LANES), BF16)
    return pl.pallas_call(
        _nsa_prep_kernel,
        out_shape=(jax.ShapeDtypeStruct((b, B_HEADS, s, LANES), BF16), kvs, kvs, kvs, kvs),
        grid=(b, s // tm),
        in_specs=[pl.BlockSpec((1, tm, 512), lambda bi, i, c=cols["b_q"] // 512: (bi, i, c)),
                  blk("b_ks"), blk("b_vs"), blk("b_kw"), blk("b_vw"), row, row, const, const, const],
        out_specs=(pl.BlockSpec((1, B_HEADS, tm, LANES), lambda bi, i: (bi, 0, i, 0)), row, row, row, row),
        compiler_params=_cp("parallel", "parallel"),
        name="nsa_prep",
    )(u, u, u, u, u, cos64, sin64, p["q_g"], p["ks_g"], p["kw_g"])


def _nsa_cmp_kernel(kc_in, vc_in, pek_ref, pev_ref, w1k_ref, w1v_ref, w2k_ref, w2v_ref, kg_ref,
                    cos_ref, sin_ref, ko_ref, vo_ref, pad_ref):
    s = kc_in.shape[1]
    n_pad = ko_ref.shape[1]

    def compress(x_in, pe_ref, w1_ref, w2_ref):
        pad_ref[0:s, :] = x_in[0]
        pad_ref[s:s + CMP_STRIDE, :] = jnp.zeros((CMP_STRIDE, LANES), F32)
        acc = jnp.zeros((n_pad, LANES), F32)
        for l in range(CMP_LEN):
            xl = pad_ref[pl.ds(l, n_pad, stride=CMP_STRIDE), :] + pe_ref[l:l + 1, :]
            acc = acc + _mm(xl.astype(BF16), w1_ref[l])
        mid = acc * _sigmoid(acc)
        return _mm(mid.astype(BF16), w2_ref[...])

    kc = compress(kc_in, pek_ref, w1k_ref, w2k_ref)
    ko_ref[0] = _rope(_group_rms(kc, B_HD) * kg_ref[...], cos_ref[0], sin_ref[0], B_HD // 2).astype(BF16)
    vo_ref[0] = compress(vc_in, pev_ref, w1v_ref, w2v_ref).astype(BF16)


def _nsa_cmp(u, cols, cosc, sinc, p):
    b, s, _ = u.shape
    n_pad = s // CMP_STRIDE
    blk = lambda name: pl.BlockSpec((1, s, LANES), lambda bi, c=cols[name] // LANES: (bi, 0, c))
    const = lambda shape: pl.BlockSpec(shape, lambda bi: (0,) * len(shape))
    out = jax.ShapeDtypeStruct((b, n_pad, LANES), BF16)
    ospec = pl.BlockSpec((1, n_pad, LANES), lambda bi: (bi, 0, 0))
    return pl.pallas_call(
        _nsa_cmp_kernel,
        out_shape=(out, out),
        grid=(b,),
        in_specs=[blk("b_kc"), blk("b_vc"), const((CMP_LEN, LANES)), const((CMP_LEN, LANES)),
                  const((CMP_LEN, LANES, LANES)), const((CMP_LEN, LANES, LANES)),
                  const((LANES, LANES)), const((LANES, LANES)), const((1, LANES)), ospec, ospec],
        out_specs=(ospec, ospec),
        scratch_shapes=[pltpu.VMEM((s + CMP_STRIDE, LANES), F32)],
        compiler_params=_cp("parallel"),
        name="nsa_cmp",
    )(u, u, p["pe_k"], p["pe_v"], p["w1k"], p["w1v"], p["w2k"], p["w2v"], p["kc_g"], cosc, sinc)


def _nsa_attn_kernel(q_ref, kc_ref, vc_ref, ks_ref, vs_ref, kw_ref, vw_ref, g_ref, e_ref, ovl_ref, o_ref,
                     *, n_blk, n_sel, tks, ww):
    tq = q_ref.shape[2]
    n_pad = kc_ref.shape[1]
    i = pl.program_id(1)
    t = i * tq + lax.broadcasted_iota(I32, (tq, 1), 0)
    lane = _lane((tq, LANES))
    gates = _sigmoid(g_ref[0])
    rep = B_HEADS // B_KV_HEADS
    ncol = lax.broadcasted_iota(I32, (1, n_pad), 1)
    valid_c = (ncol * CMP_STRIDE + (CMP_LEN - 1)) <= t
    kc = kc_ref[0]
    vc = vc_ref[0]
    cur = t >> (SEL_LEN.bit_length() - 1)
    forced = jnp.where(lane == cur, 3e4, jnp.where(lane == cur - 1, 2e4, jnp.where(lane == 0, 1e4, 0.0)))
    adm = (lane * SEL_LEN <= t) & (lane < n_blk)
    rows = rep * tq
    n_rv = n_blk // 8
    sub = lax.broadcasted_iota(I32, (8, tq), 0)
    w0 = pl.multiple_of(jnp.maximum(i * tq + tq - ww, 0), tq)
    kp_w = w0 + lax.broadcasted_iota(I32, (1, ww), 1)
    win_bias = jnp.where((kp_w <= t) & (kp_w > t - WINDOW), 0.0, NEG)
    for g in range(B_KV_HEADS):
        qs = jnp.concatenate([q_ref[0, rep * g + r] for r in range(rep)], axis=0)
        gate = lambda c: jnp.concatenate(
            [gates[:, (rep * g + r) * 3 + c:(rep * g + r) * 3 + c + 1] for r in range(rep)], axis=0)
        s = jnp.where(valid_c[None], _nt(qs, kc).reshape(rep, tq, n_pad), NEG)
        ex = jnp.exp(s - jnp.max(s, axis=-1, keepdims=True))
        pc = jnp.where(valid_c[None], ex / jnp.sum(ex, axis=-1, keepdims=True), 0.0)
        out = gate(0) * _mm(pc.reshape(rows, n_pad).astype(BF16), vc)
        imp = jnp.dot(jnp.sum(pc, axis=0), ovl_ref[...], preferred_element_type=F32, precision=HI)
        score_t = jnp.where(adm, imp + forced, NEG).T
        sc = [score_t[8 * v:8 * v + 8] for v in range(n_rv)]
        rank = [jnp.zeros((8, tq), F32) for _ in range(n_rv)]
        for jp in range(n_blk):
            col = score_t[jp:jp + 1]
            for v in range(n_rv):
                if v > jp // 8:
                    beats = col >= sc[v]
                elif v < jp // 8:
                    beats = col > sc[v]
                else:
                    beats = (col > sc[v]) | ((col == sc[v]) & (sub > jp % 8))
                rank[v] = rank[v] + jnp.where(beats, 1.0, 0.0)
        sel_t = jnp.where(jnp.concatenate(rank, axis=0) < n_sel, 1.0, 0.0).astype(BF16)

        def sel_bias(j):
            kp = j * tks + lax.broadcasted_iota(I32, (1, tks), 1)
            hit = _tn(sel_t, e_ref[:, pl.ds(pl.multiple_of(j * tks, tks), tks)])
            return jnp.where((hit > 0.5) & (kp <= t), 0.0, NEG)

        n_tile = ((i + 1) * tq + tks - 1) // tks
        (_, l_s, acc_s), = _flash([qs],
                                  lambda j, n: ks_ref[0, pl.ds(pl.multiple_of(j * tks, tks), tks), :],
                                  lambda j: vs_ref[0, pl.ds(pl.multiple_of(j * tks, tks), tks), :],
                                  0, n_tile, sel_bias)
        out = out + gate(1) * (acc_s / l_s)
        _, l_w, acc_w = _softmax_step(qs, kw_ref[0, pl.ds(w0, ww), :], vw_ref[0, pl.ds(w0, ww), :],
                                      win_bias, _flash_init(rows))
        out = out + gate(2) * (acc_w / l_w)
        outs = [out[r * tq:(r + 1) * tq] for r in range(rep)]
        for pp in range(rep // 2):
            a, bb = outs[2 * pp], outs[2 * pp + 1]
            if g == 0:
                bb = pltpu.roll(bb, B_HD, 1)
            else:
                a = pltpu.roll(a, B_HD, 1)
            c0 = (rep // 2 * g + pp) * LANES
            o_ref[0, :, c0:c0 + LANES] = jnp.where(lane < B_HD, a, bb).astype(BF16)


def _nsa_attn(u, g_block, qx, kcmp, vcmp, ks, vs, kw, vw):
    b, h, s, _ = qx.shape
    tq = min(128, s)
    tks = min(512, s)
    n_pad = kcmp.shape[1]
    n_blk = s // SEL_LEN
    n_sel = min(N_SEL, n_blk)
    n_cmp = (s - CMP_LEN) // CMP_STRIDE + 1
    ww = min(WINDOW + tq, s)
    expand = np.zeros((n_blk, s), np.float32)
    expand[np.arange(s) // SEL_LEN, np.arange(s)] = 1.0
    nn = np.arange(n_pad)[:, None]
    jj = np.arange(LANES)[None, :]
    ovl = ((nn * CMP_STRIDE <= jj * SEL_LEN + SEL_LEN - 1) & (nn * CMP_STRIDE + CMP_LEN - 1 >= jj * SEL_LEN)
           & (jj < n_blk) & (nn < n_cmp)).astype(np.float32)
    full = lambda shape: pl.BlockSpec(shape, lambda bi, i: (bi,) + (0,) * (len(shape) - 1))
    const = lambda shape: pl.BlockSpec(shape, lambda bi, i: (0,) * len(shape))
    return pl.pallas_call(
        functools.partial(_nsa_attn_kernel, n_blk=n_blk, n_sel=n_sel, tks=tks, ww=ww),
        out_shape=jax.ShapeDtypeStruct((b, s, h * B_HD), BF16),
        grid=(b, s // tq),
        in_specs=[pl.BlockSpec((1, h, tq, LANES), lambda bi, i: (bi, 0, i, 0)),
                  full((1, n_pad, LANES)), full((1, n_pad, LANES)),
                  full((1, s, LANES)), full((1, s, LANES)), full((1, s, LANES)), full((1, s, LANES)),
                  pl.BlockSpec((1, tq, LANES), lambda bi, i: (bi, i, g_block)),
                  const((n_blk, s)), const((n_pad, LANES))],
        out_specs=pl.BlockSpec((1, tq, h * B_HD), lambda bi, i: (bi, i, 0)),
        compiler_params=_cp("parallel", "parallel"),
        name="nsa_attn",
    )(qx, kcmp, vcmp, ks, vs, kw, vw, u, jnp.asarray(expand, BF16), jnp.asarray(ovl, F32))


def _dsa_prep_kernel(q_in, iq_in, kv_in, sm_in, cos64_ref, sin64_ref, cos32_ref, sin32_ref, qg_ref, kg_ref,
                     q_ref, kv_ref, iq_ref, ik_ref):
    tm = q_in.shape[1]
    lane = _lane((tm, LANES))
    c64, s64, c32, s32 = cos64_ref[0], sin64_ref[0], cos32_ref[0], sin32_ref[0]
    for p in range(C_HEADS // 2):
        y = _rope(_group_rms(q_in[0, :, p * LANES:(p + 1) * LANES], C_HD) * qg_ref[...], c64, s64, C_HD // 2)
        y = y * (C_HD ** -0.5)
        q_ref[0, 2 * p] = jnp.where(lane < C_HD, y, 0.0).astype(BF16)
        q_ref[0, 2 * p + 1] = jnp.where(lane < C_HD, pltpu.roll(y, C_HD, 1), 0.0).astype(BF16)
    kv = kv_in[0]
    kn = _rope(_group_rms(kv, C_HD) * kg_ref[...], c64, s64, C_HD // 2)
    kv_ref[0] = jnp.where(lane < C_HD, kn, kv).astype(BF16)
    for c0 in range(0, IDX_HEADS * IDX_HD, LANES):
        iq_ref[0, :, c0:c0 + LANES] = _rope(iq_in[0, :, c0:c0 + LANES], c32, s32, IDX_HD // 2).astype(BF16)
    ik = jnp.where(lane < IDX_HD, _rope(sm_in[0], c32, s32, IDX_HD // 2), 0.0)
    ik = ik + pltpu.roll(ik, IDX_HD, 1)
    ik = ik + pltpu.roll(ik, 2 * IDX_HD, 1)
    ik_ref[0] = ik.astype(BF16)


def _dsa_prep(u, cols, tabs, p):
    b, s, _ = u.shape
    tm = min(256, s)
    row = pl.BlockSpec((1, tm, LANES), lambda bi, i: (bi, i, 0))
    const = pl.BlockSpec((1, LANES), lambda bi, i: (0, 0))
    ublk = lambda name, w: pl.BlockSpec((1, tm, w), lambda bi, i, c=cols[name] // w: (bi, i, c))
    dense = jax.ShapeDtypeStruct((b, s, LANES), BF16)
    return pl.pallas_call(
        _dsa_prep_kernel,
        out_shape=(jax.ShapeDtypeStruct((b, C_HEADS, s, LANES), BF16), dense,
                   jax.ShapeDtypeStruct((b, s, IDX_HEADS * IDX_HD), BF16), dense),
        grid=(b, s // tm),
        in_specs=[ublk("c_q", 512), ublk("c_iq", 256), ublk("c_kv", LANES), ublk("small", LANES),
                  row, row, row, row, const, const],
        out_specs=(pl.BlockSpec((1, C_HEADS, tm, LANES), lambda bi, i: (bi, 0, i, 0)), row,
                   pl.BlockSpec((1, tm, IDX_HEADS * IDX_HD), lambda bi, i: (bi, i, 0)), row),
        compiler_params=_cp("parallel", "parallel"),
        name="dsa_prep",
    )(u, u, u, u, tabs["cos64"], tabs["sin64"], tabs["cos32"], tabs["sin32"], p["q_g"], p["k_g"])


def _dsa_attn_kernel(q_ref, iq_ref, sm_ref, ik_ref, kv_ref, tri_ref, o_ref, key_scr, bias_scr, *, tk, topk):
    tq = q_ref.shape[2]
    i = pl.program_id(1)
    t = i * tq + lax.broadcasted_iota(I32, (tq, 1), 0)
    lane = _lane((tq, LANES))
    n_tile = ((i + 1) * tq + tk - 1) // tk
    tile = lambda j: pl.ds(pl.multiple_of(j * tk, tk), tk)
    kpos = lambda j: j * tk + lax.broadcasted_iota(I32, (1, tk), 1)

    iw = sm_ref[0] * (IDX_HEADS ** -0.5)
    parts = []
    for h in range(IDX_HEADS):
        blk = iq_ref[0, :, (h // 4) * LANES:(h // 4 + 1) * LANES]
        lo = (h % 4) * IDX_HD
        parts.append(jnp.where((lane >= lo) & (lane < lo + IDX_HD), blk, jnp.zeros_like(blk)))
    iqs = jnp.concatenate(parts, axis=0)

    def idx_body(j, carry):
        lg = _nt(iqs, ik_ref[0, tile(j), :])
        acc = jnp.zeros((tq, tk), F32)
        for h in range(IDX_HEADS):
            acc = acc + iw[:, IDX_HD + h:IDX_HD + h + 1] * jnp.maximum(lg[h * tq:(h + 1) * tq], 0.0)
        sc = jnp.where(kpos(j) <= t, acc, NEG)
        bits = pltpu.bitcast(sc, I32)
        key = jnp.where(bits < 0, bits ^ 0x7FFFFFFF, bits)
        key_scr[:, tile(j)] = jnp.where(sc == 0.0, 0, key)
        return carry

    lax.fori_loop(0, n_tile, idx_body, 0)

    def count_ge(thr_key):
        def body(j, c):
            ge = jnp.where(key_scr[:, tile(j)] >= thr_key, 1, 0)
            for c0 in range(0, tk, LANES):
                c = c + ge[:, c0:c0 + LANES]
            return c

        c = lax.fori_loop(0, n_tile, body, jnp.zeros((tq, LANES), I32))
        return jnp.sum(c, axis=-1, keepdims=True)

    def bit_body(bi, ucand):
        utrial = ucand | jnp.left_shift(jnp.int32(1), 31 - bi)
        return jnp.where(count_ge(utrial ^ INT_MIN) >= topk, utrial, ucand)

    thr = lax.fori_loop(0, 32, bit_body, jnp.zeros((tq, 1), I32)) ^ INT_MIN
    n_ge = count_ge(thr)
    n_gt = count_ge(thr + 1)
    need = topk - n_gt
    row_ok = (n_ge - n_gt == need) | (thr == NEG_KEY) | (n_ge < topk)
    simple = jnp.min(jnp.where(row_ok, 1.0, 0.0)) > 0.5

    def fast_bias():
        def body(j, carry):
            keep = (key_scr[:, tile(j)] >= thr) & (kpos(j) <= t)
            bias_scr[:, tile(j)] = jnp.where(keep, 0.0, NEG)
            return carry

        lax.fori_loop(0, n_tile, body, 0)

    def tie_bias():
        need_f = need.astype(F32)

        def body(j, run):
            key = key_scr[:, tile(j)]
            kp = kpos(j)
            for c0 in range(0, tk, LANES):
                kc = key[:, c0:c0 + LANES]
                eq = kc == thr
                eq_f = jnp.where(eq, 1.0, 0.0)
                pref = _mm(eq_f.astype(BF16), tri_ref[...]) + run
                keep = ((kc > thr) | (eq & (pref <= need_f))) & (kp[:, c0:c0 + LANES] <= t)
                bias_scr[:, pl.ds(pl.multiple_of(j * tk + c0, LANES), LANES)] = jnp.where(keep, 0.0, NEG)
                run = run + jnp.sum(eq_f, axis=-1, keepdims=True)
            return run

        lax.fori_loop(0, n_tile, body, jnp.zeros((tq, 1), F32))

    lax.cond(simple, fast_bias, tie_bias)

    qs = jnp.concatenate([q_ref[0, h] for h in range(C_HEADS)], axis=0)

    def att_body(j, carry):
        m, l, acc = carry
        kv = kv_ref[0, tile(j), :]
        s = _nt(qs, kv).reshape(C_HEADS, tq, tk) + bias_scr[:, tile(j)][None]
        s = s.reshape(C_HEADS * tq, tk)
        m_new = jnp.maximum(m, jnp.max(s, axis=-1, keepdims=True))
        alpha = jnp.exp(m - m_new)
        p = jnp.exp(s - m_new)
        l = alpha * l + jnp.sum(p, axis=-1, keepdims=True)
        acc = alpha * acc + _mm(p.astype(BF16), kv)
        return m_new, l, acc

    rows = C_HEADS * tq
    init = (jnp.full((rows, 1), NEG, F32), jnp.zeros((rows, 1), F32), jnp.zeros((rows, LANES), F32))
    _, l, acc = lax.fori_loop(0, n_tile, att_body, init)
    o = acc / l
    for p in range(C_HEADS // 2):
        a = pltpu.roll(o[(2 * p) * tq:(2 * p + 1) * tq], C_HD, 1)
        bb = o[(2 * p + 1) * tq:(2 * p + 2) * tq]
        o_ref[0, :, p * LANES:(p + 1) * LANES] = jnp.where(lane < C_HD, a, bb).astype(BF16)


def _dsa_attn(u, small_block, qc, iq, ik, kv):
    b, h, s, _ = qc.shape
    tq = min(128, s)
    tk = min(512, s)
    topk = min(TOPK_MAX, s // 4)
    tri = np.triu(np.ones((LANES, LANES), np.float32))
    full = lambda shape: pl.BlockSpec(shape, lambda bi, i: (bi,) + (0,) * (len(shape) - 1))
    return pl.pallas_call(
        functools.partial(_dsa_attn_kernel, tk=tk, topk=topk),
        out_shape=jax.ShapeDtypeStruct((b, s, h * C_HD), BF16),
        grid=(b, s // tq),
        in_specs=[pl.BlockSpec((1, h, tq, LANES), lambda bi, i: (bi, 0, i, 0)),
                  pl.BlockSpec((1, tq, IDX_HEADS * IDX_HD), lambda bi, i: (bi, i, 0)),
                  pl.BlockSpec((1, tq, LANES), lambda bi, i: (bi, i, small_block)),
                  full((1, s, LANES)), full((1, s, LANES)),
                  pl.BlockSpec((LANES, LANES), lambda bi, i: (0, 0))],
        out_specs=pl.BlockSpec((1, tq, h * C_HD), lambda bi, i: (bi, i, 0)),
        scratch_shapes=[pltpu.VMEM((tq, s), I32), pltpu.VMEM((tq, s), F32)],
        compiler_params=_cp("parallel", "parallel"),
        name="dsa_attn",
    )(qc, iq, u, ik, kv, jnp.asarray(tri, BF16))


def _mlstm_conv_kernel(x_ref, halo_ref, w_ref, b_ref, o_ref):
    tm = x_ref.shape[1]
    i = pl.program_id(1)
    lane = _lane((tm, x_ref.shape[2]))
    halo = jnp.where(i > 0, halo_ref[0], 0.0)
    xc = jnp.concatenate([halo, x_ref[0]], axis=0)
    off = halo.shape[0] - (CONV_W - 1)
    y = b_ref[...] + jnp.zeros_like(x_ref[0])
    for j in range(CONV_W):
        y = y + w_ref[j:j + 1, :] * xc[off + j:off + j + tm, :]
    y = y * _sigmoid(y)
    o_ref[0] = jnp.where(lane >= D_HEADS * D_QK, y * (D_QK ** -0.5), y)


def _mlstm_conv(u, qk_block, w, bias):
    b, s, _ = u.shape
    c = w.shape[1]
    tm = min(512, s)
    hb = 8
    return pl.pallas_call(
        _mlstm_conv_kernel,
        out_shape=jax.ShapeDtypeStruct((b, s, c), F32),
        grid=(b, s // tm),
        in_specs=[pl.BlockSpec((1, tm, c), lambda bi, i: (bi, i, qk_block)),
                  pl.BlockSpec((1, hb, c), lambda bi, i: (bi, jnp.maximum(i * (tm // hb) - 1, 0), qk_block)),
                  pl.BlockSpec((CONV_W, c), lambda bi, i: (0, 0)),
                  pl.BlockSpec((1, c), lambda bi, i: (0, 0))],
        out_specs=pl.BlockSpec((1, tm, c), lambda bi, i: (bi, i, 0)),
        compiler_params=_cp("parallel", "parallel"),
        name="mlstm_conv",
    )(u, u, w, bias)


def _mlstm_scan_kernel(qk_ref, v_ref, op_ref, sm_ref, gt_ref, bcol_ref, brow_ref, hg_ref, o_ref,
                       c_scr, n_scr, m_scr):
    tc = qk_ref.shape[1]
    L = CHUNK
    nqk = D_HEADS * D_QK

    @pl.when(pl.program_id(1) == 0)
    def _():
        c_scr[...] = jnp.zeros_like(c_scr)
        n_scr[...] = jnp.zeros_like(n_scr)
        m_scr[...] = jnp.zeros_like(m_scr)

    r = lax.broadcasted_iota(I32, (L, L), 0)
    c = lax.broadcasted_iota(I32, (L, L), 1)
    tril = c <= r
    tril_f = jnp.where(tril, 1.0, 0.0).astype(F32)
    triu_f = jnp.where(r <= c, 1.0, 0.0).astype(F32)
    lane = _lane((L, LANES))
    row128 = lax.broadcasted_iota(I32, (LANES, LANES), 0)
    i_lane, f_lane = IDX_HD + IDX_HEADS, IDX_HD + IDX_HEADS + D_HEADS

    def chunk(ci, carry):
        rows = pl.ds(pl.multiple_of(ci * L, L), L)
        qk = qk_ref[0, rows, :]
        sm = sm_ref[0, rows, :] + bcol_ref[...]
        bcol_all = jnp.dot(tril_f, _log_sigmoid(sm), preferred_element_type=F32, precision=HI)
        gt = gt_ref[0, ci] + brow_ref[...]
        brow_all = jnp.dot(_log_sigmoid(gt), triu_f, preferred_element_type=F32, precision=HI)
        for p in range(D_HEADS // 2):
            qpair = qk[:, p * LANES:(p + 1) * LANES]
            kpair = qk[:, nqk + p * LANES:nqk + (p + 1) * LANES]
            kpair_b = kpair.astype(BF16)
            c_prev = c_scr[p]
            n_prev = n_scr[p, 0:1, :]
            c_prev_b = c_prev.astype(BF16)
            upd = []
            for e in range(2):
                h = 2 * p + e
                in_half = (lane >= D_QK * e) & (lane < D_QK * (e + 1))
                qm = jnp.where(in_half, qpair, 0.0)
                qm_b = qm.astype(BF16)
                vh = v_ref[0, rows, h * D_VD:(h + 1) * D_VD]
                b_col = bcol_all[:, f_lane + h:f_lane + h + 1]
                i_col = sm[:, i_lane + h:i_lane + h + 1]
                b_row = brow_all[D_HEADS + h:D_HEADS + h + 1, :]
                i_row = gt[h:h + 1, :]
                m_prev = m_scr[h, 0:1, 0:1]
                dmat = jnp.where(tril, b_col - b_row + i_row, NEG)
                inter = b_col + m_prev
                m_t = jnp.maximum(inter, jnp.max(dmat, axis=-1, keepdims=True))
                a = jnp.exp(inter - m_t)
                w = _nt(qm_b, kpair_b) * jnp.exp(dmat - m_t)
                num = a * _mm(qm_b, c_prev_b) + _mm(w.astype(BF16), vh.astype(BF16))
                den = a * jnp.sum(qm * n_prev, axis=-1, keepdims=True) + jnp.sum(w, axis=-1, keepdims=True)
                hout = num / jnp.maximum(jnp.abs(den), jnp.exp(-m_t))
                b_last = b_col[L - 1:L, :]
                g_col = b_last - b_col + i_col
                m_new = jnp.maximum(b_last + m_prev, jnp.max(g_col, axis=0, keepdims=True))
                ws = jnp.exp(g_col - m_new)
                decay = jnp.exp(b_last + m_prev - m_new)
                u_mat = _tn(kpair_b, (ws * vh).astype(BF16))
                k_sum = jnp.sum(ws * kpair, axis=0, keepdims=True)
                upd.append((decay * c_prev + u_mat, decay * n_prev + k_sum))
                m_scr[h] = jnp.broadcast_to(m_new, m_scr.shape[1:])
                hn = _row_rms(hout) * hg_ref[...]
                y = _sigmoid(op_ref[0, rows, h * D_VD:(h + 1) * D_VD]) * hn
                o_ref[0, rows, h * D_VD:(h + 1) * D_VD] = y.astype(BF16)
            c_scr[p] = jnp.where(row128 < D_QK, upd[0][0], upd[1][0])
            n_new = jnp.where(lane[0:1] < D_QK, upd[0][1], upd[1][1])
            n_scr[p] = jnp.broadcast_to(n_new, n_scr.shape[1:])
        return carry

    lax.fori_loop(0, tc // L, chunk, 0)


def _mlstm_scan(u, cols, qk, gt, p):
    b, s, _ = u.shape
    tc = min(512, s)
    nv = D_HEADS * D_VD
    ublk = lambda name, w: pl.BlockSpec((1, tc, w), lambda bi, i, c=cols[name] // w: (bi, i, c))
    const = lambda shape: pl.BlockSpec(shape, lambda bi, i: (0,) * len(shape))
    return pl.pallas_call(
        _mlstm_scan_kernel,
        out_shape=jax.ShapeDtypeStruct((b, s, nv), BF16),
        grid=(b, s // tc),
        in_specs=[pl.BlockSpec((1, tc, qk.shape[-1]), lambda bi, i: (bi, i, 0)),
                  ublk("d_v", nv), ublk("d_o", nv), ublk("small", LANES),
                  pl.BlockSpec((1, tc // CHUNK, 8, CHUNK), lambda bi, i: (bi, i, 0, 0)),
                  const((1, LANES)), const((8, 1)), const((1, D_VD))],
        out_specs=pl.BlockSpec((1, tc, nv), lambda bi, i: (bi, i, 0)),
        scratch_shapes=[pltpu.VMEM((D_HEADS // 2, LANES, LANES), F32),
                        pltpu.VMEM((D_HEADS // 2, 8, LANES), F32),
                        pltpu.VMEM((D_HEADS, 8, LANES), F32)],
        compiler_params=_cp("parallel", "arbitrary"),
        name="mlstm_scan",
    )(qk, u, u, u, gt, p["bias_col"], p["bias_row"], p["h_g"])


EVEN_SRC = dict(a_ql=(0, 256), a_kvl=(256, 128), a_kr=(384, 32), a_gate=(416, 512), b_q=(928, 512),
                b_kc=(1440, 128), b_vc=(1568, 128), b_ks=(1696, 128), b_vs=(1824, 128), b_kw=(1952, 128),
                b_vw=(2080, 128), b_g=(2208, 24), b_gate=(2232, 512), m_q=(2744, 256), m_gate=(3000, 256))
EVEN_DST = dict(a_ql=0, a_kvl=256, a_kr=384, b_q=512, a_gate=1024, b_gate=1536, m_q=2048, m_gate=2304,
                b_g=2560, b_kc=2688, b_vc=2816, b_ks=2944, b_vs=3072, b_kw=3200, b_vw=3328)
EVEN_COLS_PAD = 3456

ODD_SRC = dict(c_q=(0, 512), c_k=(512, 64), c_v=(576, 64), c_iq=(640, 256), c_ik=(896, 32), c_iw=(928, 8),
               c_gate=(936, 512), d_q=(1448, 256), d_k=(1704, 256), d_v=(1960, 512), d_i=(2472, 4),
               d_f=(2476, 4), d_o=(2480, 512), d_gate=(2992, 512), m_q=(3504, 256), m_gate=(3760, 256))
ODD_DST = dict(c_q=0, c_gate=512, d_gate=1024, d_v=1536, d_o=2048, d_q=2560, d_k=2816, m_q=3072, m_gate=3328,
               c_iq=3584, c_k=3840, c_v=3904, c_ik=3968, c_iw=4000, d_i=4008, d_f=4012)
ODD_COLS_PAD = 4096


def _permute_cols(w, src, dst, total):
    idx = np.zeros((total,), np.int32)
    keep = np.zeros((total,), np.float32)
    for name, (start, width) in src.items():
        idx[dst[name]:dst[name] + width] = np.arange(start, start + width)
        keep[dst[name]:dst[name] + width] = 1.0
    return (jnp.take(w, jnp.asarray(idx), axis=1) * jnp.asarray(keep)).astype(BF16)


def _tile_lanes(v, reps):
    return jnp.tile(v.astype(F32).reshape(1, -1), (1, reps))


def _rope_tables(positions, d2):
    inv = ROPE_THETA ** (-jnp.arange(d2, dtype=F32) / d2)
    ang = positions.astype(F32)[..., None] * inv
    c, s = jnp.cos(ang), jnp.sin(ang)
    reps = LANES // (2 * d2)
    return (jnp.tile(jnp.concatenate([c, c], axis=-1), (1, 1, reps)),
            jnp.tile(jnp.concatenate([-s, s], axis=-1), (1, 1, reps)))


def _block_diag2(w):
    z = jnp.zeros_like(w)
    return jnp.concatenate([jnp.concatenate([w, z], axis=-1), jnp.concatenate([z, w], axis=-1)], axis=-2)


def kernel(x, mem, positions, ln_g, mem_norm_g, mem_w_kv, mem_q_norm_g, mem_k_norm_g, w_out, even_w_in, mla_q_lat_g, mla_kv_lat_g, mla_w_uq, mla_w_ukv, mla_q_norm_g, mla_k_norm_g, nsa_q_norm_g, nsa_k_norm_g, nsa_cmp_pos, nsa_cmp_w1, nsa_cmp_w2, odd_w_in, dsa_q_norm_g, dsa_k_norm_g, mlstm_conv_w, mlstm_conv_b, mlstm_i_bias, mlstm_f_bias, mlstm_h_norm_g):
    b, s, _ = x.shape
    depth = ln_g.shape[0]
    cos64, sin64 = _rope_tables(positions, 32)
    cos32, sin32 = _rope_tables(positions, 16)
    tabs = dict(cos64=cos64, sin64=sin64, cos32=cos32, sin32=sin32)
    n_pad = s // CMP_STRIDE
    cmp_pos = jnp.pad(positions[:, CMP_LEN - 1::CMP_STRIDE], ((0, 0), (0, 0)))[:, :n_pad]
    cmp_pos = jnp.pad(cmp_pos, ((0, 0), (0, n_pad - cmp_pos.shape[1])))
    cosc, sinc = _rope_tables(cmp_pos, 32)

    mem_k, mem_v = _mem_kv(mem, mem_norm_g.reshape(depth, 1, -1), mem_w_kv.astype(BF16),
                           jnp.tile(mem_k_norm_g, (1, 2)).reshape(depth, 1, LANES))

    hq = np.arange(A_HEADS)[:, None] * (A_NOPE + A_ROPE)
    uq_idx = np.concatenate([(hq + np.arange(A_NOPE)[None, :]).ravel(),
                             (hq + A_NOPE + np.arange(A_ROPE)[None, :]).ravel()])
    hk = np.arange(A_HEADS)[:, None] * (A_NOPE + A_VD)
    ukv_idx = np.concatenate([(hk + np.arange(A_NOPE)[None, :]).ravel(),
                              (hk + A_NOPE + np.arange(A_VD)[None, :]).ravel()])

    for layer in range(depth):
        li = layer // 2
        g_ln = ln_g[layer].reshape(1, -1)
        mq_g = _tile_lanes(mem_q_norm_g[layer], 2)
        if layer % 2 == 0:
            cols = EVEN_DST
            u = _in_proj(x, g_ln, _permute_cols(even_w_in[li], EVEN_SRC, EVEN_DST, EVEN_COLS_PAD))
            pa = dict(q_lat_g=mla_q_lat_g[li].reshape(1, -1), kv_lat_g=mla_kv_lat_g[li].reshape(1, -1),
                      w_uq=jnp.take(mla_w_uq[li], jnp.asarray(uq_idx), axis=1).astype(BF16),
                      w_ukv=jnp.take(mla_w_ukv[li], jnp.asarray(ukv_idx), axis=1).astype(BF16),
                      qn_g=_tile_lanes(mla_q_norm_g[li, :A_NOPE], 2), qr_g=_tile_lanes(mla_q_norm_g[li, A_NOPE:], 4),
                      kn_g=_tile_lanes(mla_k_norm_g[li, :A_NOPE], 2), kr_g=_tile_lanes(mla_k_norm_g[li, A_NOPE:], 4))
            qa, ka, va = _mla_prep(u, cos32, sin32, pa)
            y1 = _mla_attn(qa, ka, va)
            pb = dict(q_g=_tile_lanes(nsa_q_norm_g[li], 2), ks_g=_tile_lanes(nsa_k_norm_g[li, 1], 2),
                      kw_g=_tile_lanes(nsa_k_norm_g[li, 2], 2), kc_g=_tile_lanes(nsa_k_norm_g[li, 0], 2),
                      pe_k=jnp.tile(nsa_cmp_pos[li, 0], (1, 2)), pe_v=jnp.tile(nsa_cmp_pos[li, 1], (1, 2)),
                      w1k=_block_diag2(nsa_cmp_w1[li, 0].reshape(CMP_LEN, B_HD, B_HD)).astype(BF16),
                      w1v=_block_diag2(nsa_cmp_w1[li, 1].reshape(CMP_LEN, B_HD, B_HD)).astype(BF16),
                      w2k=_block_diag2(nsa_cmp_w2[li, 0]).astype(BF16),
                      w2v=_block_diag2(nsa_cmp_w2[li, 1]).astype(BF16))
            qb, ks, vs, kw, vw = _nsa_prep(u, cols, cos64, sin64, pb)
            kcmp, vcmp = _nsa_cmp(u, cols, cosc, sinc, pb)
            y2 = _nsa_attn(u, cols["b_g"] // LANES, qb, kcmp, vcmp, ks, vs, kw, vw)
            gate_blocks = (cols["a_gate"] // 512, cols["b_gate"] // 512, cols["m_gate"] // 256)
        else:
            cols = dict(ODD_DST, c_kv=ODD_DST["c_k"], small=ODD_DST["c_ik"])
            u = _in_proj(x, g_ln, _permute_cols(odd_w_in[li], ODD_SRC, ODD_DST, ODD_COLS_PAD))
            pc = dict(q_g=_tile_lanes(dsa_q_norm_g[li], 2), k_g=_tile_lanes(dsa_k_norm_g[li], 2))
            qc, kvc, iq, ik = _dsa_prep(u, cols, tabs, pc)
            y1 = _dsa_attn(u, cols["small"] // LANES, qc, iq, ik, kvc)
            qk = _mlstm_conv(u, cols["d_q"] // 512, mlstm_conv_w[li], mlstm_conv_b[li].reshape(1, -1))
            gates = u[:, :, cols["d_i"]:cols["d_i"] + 2 * D_HEADS]
            gt = gates.reshape(b, s // CHUNK, CHUNK, 2 * D_HEADS).transpose(0, 1, 3, 2)
            bias8 = jnp.concatenate([mlstm_i_bias[li], mlstm_f_bias[li]]).astype(F32)
            bias_col = jnp.zeros((1, LANES), F32).at[0, cols["d_i"] - cols["small"]:cols["d_i"] - cols["small"] + 8].set(bias8)
            pd = dict(bias_col=bias_col, bias_row=bias8.reshape(8, 1), h_g=mlstm_h_norm_g[li].reshape(1, -1))
            y2 = _mlstm_scan(u, cols, qk, gt, pd)
            gate_blocks = (cols["c_gate"] // 512, cols["d_gate"] // 512, cols["m_gate"] // 256)
        ym = _mem_attn(u, cols["m_q"] // 256, mem_k, mem_v, layer, mq_g)
        x = _out_proj(x, y1, y2, ym, u, gate_blocks, w_out[layer].astype(BF16))
    return x
```

```python
import functools

import numpy as np
import jax
import jax.numpy as jnp
from jax import lax
from jax.experimental import pallas as pl
from jax.experimental.pallas import tpu as pltpu

F32, BF16, I32 = jnp.float32, jnp.bfloat16, jnp.int32
HI = lax.Precision.HIGHEST
NEG = -1e30
EPS = 1e-6
ROPE_THETA = 10000.0
LANES = 128
VMEM_LIMIT_BYTES = 48 * 1024 * 1024

D_MODEL = 1024
DEPTH = 4
A_HEADS, A_NOPE, A_ROPE, A_VD, A_QLAT, A_KVLAT = 8, 64, 32, 64, 256, 128
B_HEADS, B_KV_HEADS, B_HD = 8, 2, 64
CMP_LEN, CMP_STRIDE, SEL_LEN, N_SEL, WINDOW = 32, 16, 64, 16, 512
C_HEADS, C_HD, IDX_HEADS, IDX_HD, TOPK_MAX = 8, 64, 8, 32, 256
D_HEADS, D_QK, D_VD, CONV_W, CHUNK = 4, 64, 128, 4, 64
M_HEADS, M_HD = 4, 64

INT_MIN = np.int32(-2 ** 31)
NEG_KEY = int(np.float32(NEG).view(np.int32) ^ np.int32(0x7FFFFFFF))
LOG2E = float(np.log2(np.e))


def _cp(*sem):
    return pltpu.CompilerParams(dimension_semantics=sem, vmem_limit_bytes=VMEM_LIMIT_BYTES)


def _nt(a, b):
    return lax.dot_general(a, b, (((1,), (1,)), ((), ())), preferred_element_type=F32)


def _tn(a, b):
    return lax.dot_general(a, b, (((0,), (0,)), ((), ())), preferred_element_type=F32)


def _mm(a, b):
    return jnp.dot(a, b, preferred_element_type=F32)


def _sigmoid(x):
    return 1.0 / (1.0 + jnp.exp(-x))


def _log_sigmoid(x):
    return jnp.minimum(x, 0.0) - jnp.log1p(jnp.exp(-jnp.abs(x)))


def _lane(shape):
    return lax.broadcasted_iota(I32, shape, len(shape) - 1)


def _group_mat(gs):
    r = lax.broadcasted_iota(I32, (LANES, LANES), 0)
    c = lax.broadcasted_iota(I32, (LANES, LANES), 1)
    sh = gs.bit_length() - 1
    return jnp.where((r >> sh) == (c >> sh), 1.0, 0.0).astype(F32)


def _group_rms(x, gs):
    ss = jnp.dot(x * x, _group_mat(gs), preferred_element_type=F32, precision=HI)
    return x * lax.rsqrt(ss * (1.0 / gs) + EPS)


def _rope(x, cosp, sinp, half):
    lane = _lane(x.shape)
    rot = jnp.where((lane & (2 * half - 1)) < half,
                    pltpu.roll(x, LANES - half, 1), pltpu.roll(x, half, 1))
    return x * cosp + rot * sinp


def _row_rms(x):
    return x * lax.rsqrt(jnp.mean(x * x, axis=-1, keepdims=True) + EPS)


def _in_proj_kernel(x_ref, g_ref, w_ref, o_ref):
    h = (_row_rms(x_ref[0]) * g_ref[...]).astype(BF16)
    ncol = o_ref.shape[-1]
    for c0 in range(0, ncol, 512):
        c1 = min(ncol, c0 + 512)
        o_ref[0, :, c0:c1] = _mm(h, w_ref[:, c0:c1])


def _in_proj(x, g, w):
    b, s, d = x.shape
    c = w.shape[1]
    tm = min(256, s)
    return pl.pallas_call(
        _in_proj_kernel,
        out_shape=jax.ShapeDtypeStruct((b, s, c), F32),
        grid=(b, s // tm),
        in_specs=[pl.BlockSpec((1, tm, d), lambda bi, i: (bi, i, 0)),
                  pl.BlockSpec((1, d), lambda bi, i: (0, 0)),
                  pl.BlockSpec((d, c), lambda bi, i: (0, 0))],
        out_specs=pl.BlockSpec((1, tm, c), lambda bi, i: (bi, i, 0)),
        compiler_params=_cp("parallel", "parallel"),
        name="in_proj",
    )(x, g, w)


def _out_proj_kernel(x_ref, y1_ref, y2_ref, ym_ref, g1_ref, g2_ref, gm_ref, w_ref, o_ref):
    def gated(y_ref, g_ref):
        g = g_ref[0]
        return (y_ref[0].astype(F32) * (g * _sigmoid(g))).astype(BF16)

    n1 = y1_ref.shape[-1]
    n2 = y2_ref.shape[-1]
    acc = x_ref[0]
    acc = acc + _mm(gated(y1_ref, g1_ref), w_ref[0:n1, :])
    acc = acc + _mm(gated(y2_ref, g2_ref), w_ref[n1:n1 + n2, :])
    acc = acc + _mm(gated(ym_ref, gm_ref), w_ref[n1 + n2:, :])
    o_ref[0] = acc


def _out_proj(x, y1, y2, ym, u, gate_blocks, w):
    b, s, d = x.shape
    tm = min(512, s)
    i1, i2, im = gate_blocks
    n1, n2, nm = y1.shape[-1], y2.shape[-1], ym.shape[-1]
    row = lambda bi, i: (bi, i, 0)
    return pl.pallas_call(
        _out_proj_kernel,
        out_shape=jax.ShapeDtypeStruct((b, s, d), F32),
        grid=(b, s // tm),
        in_specs=[pl.BlockSpec((1, tm, d), row),
                  pl.BlockSpec((1, tm, n1), row),
                  pl.BlockSpec((1, tm, n2), row),
                  pl.BlockSpec((1, tm, nm), row),
                  pl.BlockSpec((1, tm, n1), lambda bi, i: (bi, i, i1)),
                  pl.BlockSpec((1, tm, n2), lambda bi, i: (bi, i, i2)),
                  pl.BlockSpec((1, tm, nm), lambda bi, i: (bi, i, im)),
                  pl.BlockSpec(w.shape, lambda bi, i: (0, 0))],
        out_specs=pl.BlockSpec((1, tm, d), row),
        compiler_params=_cp("parallel", "parallel"),
        name="out_proj",
    )(x, y1, y2, ym, u, u, u, w)


def _mem_kv_kernel(mem_ref, g_ref, w_ref, kg_ref, k_ref, v_ref):
    h = (_row_rms(mem_ref[0]) * g_ref[0]).astype(BF16)
    kv = _mm(h, w_ref[0])
    nk = k_ref.shape[-1]
    for c0 in range(0, nk, LANES):
        k_ref[0, 0, :, c0:c0 + LANES] = (_group_rms(kv[:, c0:c0 + LANES], M_HD) * kg_ref[0]).astype(BF16)
    v_ref[0, 0] = kv[:, nk:].astype(BF16)


def _mem_kv(mem, g, w, kg):
    b, m, d = mem.shape
    depth = w.shape[0]
    nk = M_HEADS * M_HD
    out = jax.ShapeDtypeStruct((depth, b, m, nk), BF16)
    return pl.pallas_call(
        _mem_kv_kernel,
        out_shape=(out, out),
        grid=(depth, b),
        in_specs=[pl.BlockSpec((1, m, d), lambda l, bi: (bi, 0, 0)),
                  pl.BlockSpec((1, 1, d), lambda l, bi: (l, 0, 0)),
                  pl.BlockSpec((1, d, 2 * nk), lambda l, bi: (l, 0, 0)),
                  pl.BlockSpec((1, 1, LANES), lambda l, bi: (l, 0, 0))],
        out_specs=(pl.BlockSpec((1, 1, m, nk), lambda l, bi: (l, bi, 0, 0)),
                   pl.BlockSpec((1, 1, m, nk), lambda l, bi: (l, bi, 0, 0))),
        compiler_params=_cp("parallel", "parallel"),
        name="mem_kv",
    )(mem, g, w, kg)


def _mem_attn_kernel(q_ref, k_ref, v_ref, qg_ref, o_ref):
    tq = q_ref.shape[1]
    lane = _lane((tq, LANES))
    for p in range(M_HEADS // 2):
        sl = slice(p * LANES, (p + 1) * LANES)
        q = (_group_rms(q_ref[0, :, sl], M_HD) * qg_ref[...] * (M_HD ** -0.5)).astype(BF16)
        k = k_ref[0, 0, :, sl]
        v = v_ref[0, 0, :, sl]
        halves = []
        for e in range(2):
            in_half = (lane >= 64 * e) & (lane < 64 * e + 64)
            s = _nt(jnp.where(in_half, q, jnp.zeros_like(q)), k)
            ex = jnp.exp(s - jnp.max(s, axis=-1, keepdims=True))
            o = _mm(ex.astype(BF16), v) / jnp.sum(ex, axis=-1, keepdims=True)
            halves.append(o)
        o_ref[0, :, sl] = jnp.where(lane < 64, halves[0], halves[1]).astype(BF16)


def _mem_attn(u, q_block, k, v, layer, qg):
    b, s, _ = u.shape
    m, nk = k.shape[2], k.shape[3]
    tq = min(512, s)
    return pl.pallas_call(
        _mem_attn_kernel,
        out_shape=jax.ShapeDtypeStruct((b, s, nk), BF16),
        grid=(b, s // tq),
        in_specs=[pl.BlockSpec((1, tq, nk), lambda bi, i: (bi, i, q_block)),
                  pl.BlockSpec((1, 1, m, nk), lambda bi, i: (layer, bi, 0, 0)),
                  pl.BlockSpec((1, 1, m, nk), lambda bi, i: (layer, bi, 0, 0)),
                  pl.BlockSpec((1, LANES), lambda bi, i: (0, 0))],
        out_specs=pl.BlockSpec((1, tq, nk), lambda bi, i: (bi, i, 0)),
        compiler_params=_cp("parallel", "parallel"),
        name="mem_attn",
    )(u, k, v, qg)


def _flash(qs, k_at, v_at, j0, j1, bias_at, carry):
    def body(j, c):
        bias = None if bias_at is None else bias_at(j)
        return tuple(_softmax_step(q, k_at(j, n), v_at(j), bias, cn) for n, (q, cn) in enumerate(zip(qs, c)))

    return lax.fori_loop(j0, j1, body, carry)


def _flash_init(rows):
    return (jnp.full((rows, 1), NEG, F32), jnp.zeros((rows, 1), F32), jnp.zeros((rows, LANES), F32))


def _softmax_update(s, v, bias, carry):
    m, l, acc = carry
    if bias is not None:
        rep = s.shape[0] // bias.shape[0]
        s = (s.reshape(rep, bias.shape[0], s.shape[1]) + bias[None]).reshape(s.shape)
    m_new = jnp.maximum(m, jnp.max(s, axis=-1, keepdims=True))
    alpha = jnp.exp2(m - m_new)
    p = jnp.exp2(s - m_new)
    l = alpha * l + jnp.sum(p, axis=-1, keepdims=True)
    acc = alpha * acc + _mm(p.astype(BF16), v)
    return m_new, l, acc


def _softmax_step(q, k, v, bias, carry):
    return _softmax_update(_nt(q, k), v, bias, carry)


def _flash_out(carry):
    _, l, acc = carry
    return acc / l


def _mla_prep_kernel(u_ref, cos_ref, sin_ref, qlg_ref, kvlg_ref, wuq_ref, wukv_ref,
                     qng_ref, qrg_ref, kng_ref, krg_ref, q_ref, k_ref, v_ref):
    tm = u_ref.shape[1]
    lane = _lane((tm, LANES))
    cosp, sinp = cos_ref[0], sin_ref[0]
    half = A_ROPE // 2
    scale = (A_NOPE + A_ROPE) ** -0.5 * LOG2E
    ql = (_row_rms(u_ref[0, :, 0:A_QLAT]) * qlg_ref[...]).astype(BF16)
    kvl = (_row_rms(u_ref[0, :, A_QLAT:A_QLAT + A_KVLAT]) * kvlg_ref[...]).astype(BF16)
    q = _mm(ql, wuq_ref[...])
    kv = _mm(kvl, wukv_ref[...])
    n_nope = A_HEADS * A_NOPE
    v_ref[0] = kv[:, n_nope:].astype(BF16)
    kr = u_ref[0, :, A_QLAT + A_KVLAT:A_QLAT + A_KVLAT + LANES]
    kpe = _rope(_group_rms(kr, A_ROPE) * krg_ref[...], cosp, sinp, half)
    kpe = pltpu.roll(kpe, A_NOPE, 1)
    qn = [_group_rms(q[:, c:c + LANES], A_NOPE) * qng_ref[...] for c in range(0, n_nope, LANES)]
    kn = [_group_rms(kv[:, c:c + LANES], A_NOPE) * kng_ref[...] for c in range(0, n_nope, LANES)]
    qr = [_rope(_group_rms(q[:, n_nope + c:n_nope + c + LANES], A_ROPE) * qrg_ref[...], cosp, sinp, half)
          for c in range(0, A_HEADS * A_ROPE, LANES)]
    for h in range(A_HEADS):
        qn_h = qn[h // 2] if h % 2 == 0 else pltpu.roll(qn[h // 2], A_NOPE, 1)
        kn_h = kn[h // 2] if h % 2 == 0 else pltpu.roll(kn[h // 2], A_NOPE, 1)
        shift = (A_NOPE - (h % 4) * A_ROPE) % LANES
        qr_h = qr[h // 4] if shift == 0 else pltpu.roll(qr[h // 4], shift, 1)
        qf = jnp.where(lane < A_NOPE, qn_h, jnp.where(lane < A_NOPE + A_ROPE, qr_h, 0.0))
        kf = jnp.where(lane < A_NOPE, kn_h, jnp.where(lane < A_NOPE + A_ROPE, kpe, 0.0))
        q_ref[0, h] = (qf * scale).astype(BF16)
        k_ref[0, h] = kf.astype(BF16)


def _mla_prep(u, cos32, sin32, p):
    b, s, _ = u.shape
    tm = min(256, s)
    hd = jax.ShapeDtypeStruct((b, A_HEADS, s, LANES), BF16)
    const = lambda shape: pl.BlockSpec(shape, lambda bi, i: (0,) * len(shape))
    return pl.pallas_call(
        _mla_prep_kernel,
        out_shape=(hd, hd, jax.ShapeDtypeStruct((b, s, A_HEADS * A_VD), BF16)),
        grid=(b, s // tm),
        in_specs=[pl.BlockSpec((1, tm, 512), lambda bi, i: (bi, i, 0)),
                  pl.BlockSpec((1, tm, LANES), lambda bi, i: (bi, i, 0)),
                  pl.BlockSpec((1, tm, LANES), lambda bi, i: (bi, i, 0)),
                  const((1, A_QLAT)), const((1, A_KVLAT)),
                  const(p["w_uq"].shape), const(p["w_ukv"].shape),
                  const((1, LANES)), const((1, LANES)), const((1, LANES)), const((1, LANES))],
        out_specs=(pl.BlockSpec((1, A_HEADS, tm, LANES), lambda bi, i: (bi, 0, i, 0)),
                   pl.BlockSpec((1, A_HEADS, tm, LANES), lambda bi, i: (bi, 0, i, 0)),
                   pl.BlockSpec((1, tm, A_HEADS * A_VD), lambda bi, i: (bi, i, 0))),
        compiler_params=_cp("parallel", "parallel"),
        name="mla_prep",
    )(u, cos32, sin32, p["q_lat_g"], p["kv_lat_g"], p["w_uq"], p["w_ukv"],
      p["qn_g"], p["qr_g"], p["kn_g"], p["kr_g"])


def _mla_attn_kernel(q_ref, k_ref, v_ref, o_ref):
    tq = q_ref.shape[2]
    i = pl.program_id(2)
    lane = _lane((tq, LANES))
    r = lax.broadcasted_iota(I32, (tq, tq), 0)
    c = lax.broadcasted_iota(I32, (tq, tq), 1)
    diag_bias = jnp.where(c <= r, 0.0, NEG).astype(F32)
    qs = [q_ref[0, 0], q_ref[0, 1]]
    tile = lambda j: pl.ds(pl.multiple_of(j * tq, tq), tq)
    k_at = lambda j, e: k_ref[0, e, tile(j), :]
    v_at = lambda j: v_ref[0, tile(j), :]
    carry = _flash(qs, k_at, v_at, 0, i, None, (_flash_init(tq), _flash_init(tq)))
    res = [_flash_out(_softmax_step(qs[e], k_at(i, e), v_at(i), diag_bias, carry[e])) for e in range(2)]
    o_ref[0] = jnp.where(lane < A_VD, res[0], res[1]).astype(BF16)


def _mla_attn(q, k, v):
    b, h, s, _ = q.shape
    tq = min(512, s)
    return pl.pallas_call(
        _mla_attn_kernel,
        out_shape=jax.ShapeDtypeStruct((b, s, h * A_VD), BF16),
        grid=(b, h // 2, s // tq),
        in_specs=[pl.BlockSpec((1, 2, tq, LANES), lambda bi, p, i: (bi, p, i, 0)),
                  pl.BlockSpec((1, 2, s, LANES), lambda bi, p, i: (bi, p, 0, 0)),
                  pl.BlockSpec((1, s, LANES), lambda bi, p, i: (bi, 0, p))],
        out_specs=pl.BlockSpec((1, tq, LANES), lambda bi, p, i: (bi, i, p)),
        compiler_params=_cp("parallel", "parallel", "parallel"),
        name="mla_attn",
    )(q, k, v)


def _nsa_prep_kernel(q_in, ks_in, vs_in, kw_in, vw_in, cos_ref, sin_ref, qg_ref, ksg_ref, kwg_ref,
                     q_ref, ks_ref, vs_ref, kw_ref, vw_ref):
    tm = q_in.shape[1]
    lane = _lane((tm, LANES))
    cosp, sinp = cos_ref[0], sin_ref[0]
    half = B_HD // 2
    rep = B_HEADS // B_KV_HEADS
    for p in range(B_HEADS // 2):
        y = _rope(_group_rms(q_in[0, :, p * LANES:(p + 1) * LANES], B_HD) * qg_ref[...], cosp, sinp, half)
        y = y * (B_HD ** -0.5 * LOG2E)
        y_sw = pltpu.roll(y, B_HD, 1)
        for e in range(2):
            h = 2 * p + e
            g = h // rep
            src = y if e == g else y_sw
            in_grp = (lane >= B_HD * g) & (lane < B_HD * (g + 1))
            q_ref[0, h] = jnp.where(in_grp, src, 0.0).astype(BF16)
    ks_ref[0] = _rope(_group_rms(ks_in[0], B_HD) * ksg_ref[...], cosp, sinp, half).astype(BF16)
    kw_ref[0] = _rope(_group_rms(kw_in[0], B_HD) * kwg_ref[...], cosp, sinp, half).astype(BF16)
    vs_ref[0] = vs_in[0].astype(BF16)
    vw_ref[0] = vw_in[0].astype(BF16)


def _nsa_prep(u, cols, cos64, sin64, p):
    b, s, _ = u.shape
    tm = min(256, s)
    blk = lambda name: pl.BlockSpec((1, tm, LANES), lambda bi, i, c=cols[name] // LANES: (bi, i, c))
    row = pl.BlockSpec((1, tm, LANES), lambda bi, i: (bi, i, 0))
    const = pl.BlockSpec((1, LANES), lambda bi, i: (0, 0))
    kvs = jax.ShapeDtypeStruct((b, s, LANES), BF16)
    return pl.pallas_call(
        _nsa_prep_kernel,
        out_shape=(jax.ShapeDtypeStruct((b, B_HEADS, s, LANES), BF16), kvs, kvs, kvs, kvs),
        grid=(b, s // tm),
        in_specs=[pl.BlockSpec((1, tm, 512), lambda bi, i, c=cols["b_q"] // 512: (bi, i, c)),
                  blk("b_ks"), blk("b_vs"), blk("b_kw"), blk("b_vw"), row, row, const, const, const],
        out_specs=(pl.BlockSpec((1, B_HEADS, tm, LANES), lambda bi, i: (bi, 0, i, 0)), row, row, row, row),
        compiler_params=_cp("parallel", "parallel"),
        name="nsa_prep",
    )(u, u, u, u, u, cos64, sin64, p["q_g"], p["ks_g"], p["kw_g"])


def _nsa_cmp_kernel(kc_in, vc_in, pek_ref, pev_ref, w1k_ref, w1v_ref, w2k_ref, w2v_ref, kg_ref,
                    cos_ref, sin_ref, ko_ref, vo_ref, pad_ref):
    s = kc_in.shape[1]
    n_pad = ko_ref.shape[1]

    def compress(x_in, pe_ref, w1_ref, w2_ref):
        pad_ref[0:s, :] = x_in[0]
        pad_ref[s:s + CMP_STRIDE, :] = jnp.zeros((CMP_STRIDE, LANES), F32)
        acc = jnp.zeros((n_pad, LANES), F32)
        for l in range(CMP_LEN):
            xl = pad_ref[pl.ds(l, n_pad, stride=CMP_STRIDE), :] + pe_ref[l:l + 1, :]
            acc = acc + _mm(xl.astype(BF16), w1_ref[l])
        mid = acc * _sigmoid(acc)
        return _mm(mid.astype(BF16), w2_ref[...])

    kc = compress(kc_in, pek_ref, w1k_ref, w2k_ref)
    ko_ref[0] = _rope(_group_rms(kc, B_HD) * kg_ref[...], cos_ref[0], sin_ref[0], B_HD // 2).astype(BF16)
    vo_ref[0] = compress(vc_in, pev_ref, w1v_ref, w2v_ref).astype(BF16)


def _nsa_cmp(u, cols, cosc, sinc, p):
    b, s, _ = u.shape
    n_pad = s // CMP_STRIDE
    blk = lambda name: pl.BlockSpec((1, s, LANES), lambda bi, c=cols[name] // LANES: (bi, 0, c))
    const = lambda shape: pl.BlockSpec(shape, lambda bi: (0,) * len(shape))
    out = jax.ShapeDtypeStruct((b, n_pad, LANES), BF16)
    ospec = pl.BlockSpec((1, n_pad, LANES), lambda bi: (bi, 0, 0))
    return pl.pallas_call(
        _nsa_cmp_kernel,
        out_shape=(out, out),
        grid=(b,),
        in_specs=[blk("b_kc"), blk("b_vc"), const((CMP_LEN, LANES)), const((CMP_LEN, LANES)),
                  const((CMP_LEN, LANES, LANES)), const((CMP_LEN, LANES, LANES)),
                  const((LANES, LANES)), const((LANES, LANES)), const((1, LANES)), ospec, ospec],
        out_specs=(ospec, ospec),
        scratch_shapes=[pltpu.VMEM((s + CMP_STRIDE, LANES), F32)],
        compiler_params=_cp("parallel"),
        name="nsa_cmp",
    )(u, u, p["pe_k"], p["pe_v"], p["w1k"], p["w1v"], p["w2k"], p["w2v"], p["kc_g"], cosc, sinc)


def _nsa_attn_kernel(q_ref, kc_ref, vc_ref, ks_ref, vs_ref, kw_ref, vw_ref, g_ref, e_ref, ovl_ref, o_ref,
                     *, n_blk, n_sel, tks, ww):
    tq = q_ref.shape[2]
    n_pad = kc_ref.shape[1]
    i = pl.program_id(1)
    t = i * tq + lax.broadcasted_iota(I32, (tq, 1), 0)
    lane = _lane((tq, LANES))
    gates = _sigmoid(g_ref[0])
    rep = B_HEADS // B_KV_HEADS
    ncol = lax.broadcasted_iota(I32, (1, n_pad), 1)
    valid_c = (ncol * CMP_STRIDE + (CMP_LEN - 1)) <= t
    kc = kc_ref[0]
    vc = vc_ref[0]
    cur = t >> (SEL_LEN.bit_length() - 1)
    forced = jnp.where(lane == cur, 3e4, jnp.where(lane == cur - 1, 2e4, jnp.where(lane == 0, 1e4, 0.0)))
    adm = (lane * SEL_LEN <= t) & (lane < n_blk)
    rows = rep * tq
    n_rv = n_blk // 8
    sub = lax.broadcasted_iota(I32, (8, tq), 0)
    w0 = pl.multiple_of(jnp.maximum(i * tq + tq - ww, 0), tq)
    kp_w = w0 + lax.broadcasted_iota(I32, (1, ww), 1)
    win_bias = jnp.where((kp_w <= t) & (kp_w > t - WINDOW), 0.0, NEG)
    for g in range(B_KV_HEADS):
        qs = jnp.concatenate([q_ref[0, rep * g + r] for r in range(rep)], axis=0)
        gate = lambda c: jnp.concatenate(
            [gates[:, (rep * g + r) * 3 + c:(rep * g + r) * 3 + c + 1] for r in range(rep)], axis=0)
        s = jnp.where(valid_c[None], _nt(qs, kc).reshape(rep, tq, n_pad), NEG)
        ex = jnp.exp2(s - jnp.max(s, axis=-1, keepdims=True))
        pc = jnp.where(valid_c[None], ex / jnp.sum(ex, axis=-1, keepdims=True), 0.0)
        out = gate(0) * _mm(pc.reshape(rows, n_pad).astype(BF16), vc)
        imp = jnp.dot(jnp.sum(pc, axis=0), ovl_ref[...], preferred_element_type=F32, precision=HI)
        score_t = jnp.where(adm, imp + forced, NEG).T
        sc = [score_t[8 * v:8 * v + 8] for v in range(n_rv)]
        rank = [jnp.zeros((8, tq), F32) for _ in range(n_rv)]
        for jp in range(n_blk):
            col = score_t[jp:jp + 1]
            for v in range(n_rv):
                if v > jp // 8:
                    beats = col >= sc[v]
                elif v < jp // 8:
                    beats = col > sc[v]
                else:
                    beats = (col > sc[v]) | ((col == sc[v]) & (sub > jp % 8))
                rank[v] = rank[v] + jnp.where(beats, 1.0, 0.0)
        sel_t = jnp.where(jnp.concatenate(rank, axis=0) < n_sel, 1.0, 0.0).astype(BF16)

        tile = lambda j: pl.ds(pl.multiple_of(j * tks, tks), tks)

        def sel_bias(j):
            kp = j * tks + lax.broadcasted_iota(I32, (1, tks), 1)
            hit = _tn(sel_t, e_ref[:, tile(j)])
            return jnp.where((hit > 0.5) & (kp <= t), 0.0, NEG)

        n_tile = ((i + 1) * tq + tks - 1) // tks
        c_s, = _flash([qs], lambda j, n: ks_ref[0, tile(j), :], lambda j: vs_ref[0, tile(j), :],
                      0, n_tile, sel_bias, (_flash_init(rows),))
        out = out + gate(1) * _flash_out(c_s)
        c_w = _softmax_step(qs, kw_ref[0, pl.ds(w0, ww), :], vw_ref[0, pl.ds(w0, ww), :],
                            win_bias, _flash_init(rows))
        out = out + gate(2) * _flash_out(c_w)
        outs = [out[r * tq:(r + 1) * tq] for r in range(rep)]
        for pp in range(rep // 2):
            a, bb = outs[2 * pp], outs[2 * pp + 1]
            if g == 0:
                bb = pltpu.roll(bb, B_HD, 1)
            else:
                a = pltpu.roll(a, B_HD, 1)
            c0 = (rep // 2 * g + pp) * LANES
            o_ref[0, :, c0:c0 + LANES] = jnp.where(lane < B_HD, a, bb).astype(BF16)


def _nsa_attn(u, g_block, qx, kcmp, vcmp, ks, vs, kw, vw):
    b, h, s, _ = qx.shape
    tq = min(128, s)
    tks = min(512, s)
    n_pad = kcmp.shape[1]
    n_blk = s // SEL_LEN
    n_sel = min(N_SEL, n_blk)
    n_cmp = (s - CMP_LEN) // CMP_STRIDE + 1
    ww = min(WINDOW + tq, s)
    expand = np.zeros((n_blk, s), np.float32)
    expand[np.arange(s) // SEL_LEN, np.arange(s)] = 1.0
    nn = np.arange(n_pad)[:, None]
    jj = np.arange(LANES)[None, :]
    ovl = ((nn * CMP_STRIDE <= jj * SEL_LEN + SEL_LEN - 1) & (nn * CMP_STRIDE + CMP_LEN - 1 >= jj * SEL_LEN)
           & (jj < n_blk) & (nn < n_cmp)).astype(np.float32)
    full = lambda shape: pl.BlockSpec(shape, lambda bi, i: (bi,) + (0,) * (len(shape) - 1))
    const = lambda shape: pl.BlockSpec(shape, lambda bi, i: (0,) * len(shape))
    return pl.pallas_call(
        functools.partial(_nsa_attn_kernel, n_blk=n_blk, n_sel=n_sel, tks=tks, ww=ww),
        out_shape=jax.ShapeDtypeStruct((b, s, h * B_HD), BF16),
        grid=(b, s // tq),
        in_specs=[pl.BlockSpec((1, h, tq, LANES), lambda bi, i: (bi, 0, i, 0)),
                  full((1, n_pad, LANES)), full((1, n_pad, LANES)),
                  full((1, s, LANES)), full((1, s, LANES)), full((1, s, LANES)), full((1, s, LANES)),
                  pl.BlockSpec((1, tq, LANES), lambda bi, i: (bi, i, g_block)),
                  const((n_blk, s)), const((n_pad, LANES))],
        out_specs=pl.BlockSpec((1, tq, h * B_HD), lambda bi, i: (bi, i, 0)),
        compiler_params=_cp("parallel", "parallel"),
        name="nsa_attn",
    )(qx, kcmp, vcmp, ks, vs, kw, vw, u, jnp.asarray(expand, BF16), jnp.asarray(ovl, F32))


def _dsa_prep_kernel(q_in, iq_in, kv_in, sm_in, cos64_ref, sin64_ref, cos32_ref, sin32_ref, qg_ref, kg_ref,
                     q_ref, kv_ref, iq_ref, ik_ref):
    tm = q_in.shape[1]
    lane = _lane((tm, LANES))
    c64, s64, c32, s32 = cos64_ref[0], sin64_ref[0], cos32_ref[0], sin32_ref[0]
    for p in range(C_HEADS // 2):
        y = _rope(_group_rms(q_in[0, :, p * LANES:(p + 1) * LANES], C_HD) * qg_ref[...], c64, s64, C_HD // 2)
        y = y * (C_HD ** -0.5 * LOG2E)
        q_ref[0, 2 * p] = jnp.where(lane < C_HD, y, 0.0).astype(BF16)
        q_ref[0, 2 * p + 1] = jnp.where(lane < C_HD, pltpu.roll(y, C_HD, 1), 0.0).astype(BF16)
    kv = kv_in[0]
    kn = _rope(_group_rms(kv, C_HD) * kg_ref[...], c64, s64, C_HD // 2)
    kv_ref[0] = jnp.where(lane < C_HD, kn, kv).astype(BF16)
    for c0 in range(0, IDX_HEADS * IDX_HD, LANES):
        iq_ref[0, :, c0:c0 + LANES] = _rope(iq_in[0, :, c0:c0 + LANES], c32, s32, IDX_HD // 2).astype(BF16)
    ik = jnp.where(lane < IDX_HD, _rope(sm_in[0], c32, s32, IDX_HD // 2), 0.0)
    ik = ik + pltpu.roll(ik, IDX_HD, 1)
    ik = ik + pltpu.roll(ik, 2 * IDX_HD, 1)
    ik_ref[0] = ik.astype(BF16)


def _dsa_prep(u, cols, tabs, p):
    b, s, _ = u.shape
    tm = min(256, s)
    row = pl.BlockSpec((1, tm, LANES), lambda bi, i: (bi, i, 0))
    const = pl.BlockSpec((1, LANES), lambda bi, i: (0, 0))
    ublk = lambda name, w: pl.BlockSpec((1, tm, w), lambda bi, i, c=cols[name] // w: (bi, i, c))
    dense = jax.ShapeDtypeStruct((b, s, LANES), BF16)
    return pl.pallas_call(
        _dsa_prep_kernel,
        out_shape=(jax.ShapeDtypeStruct((b, C_HEADS, s, LANES), BF16), dense,
                   jax.ShapeDtypeStruct((b, s, IDX_HEADS * IDX_HD), BF16), dense),
        grid=(b, s // tm),
        in_specs=[ublk("c_q", 512), ublk("c_iq", 256), ublk("c_kv", LANES), ublk("small", LANES),
                  row, row, row, row, const, const],
        out_specs=(pl.BlockSpec((1, C_HEADS, tm, LANES), lambda bi, i: (bi, 0, i, 0)), row,
                   pl.BlockSpec((1, tm, IDX_HEADS * IDX_HD), lambda bi, i: (bi, i, 0)), row),
        compiler_params=_cp("parallel", "parallel"),
        name="dsa_prep",
    )(u, u, u, u, tabs["cos64"], tabs["sin64"], tabs["cos32"], tabs["sin32"], p["q_g"], p["k_g"])


def _dsa_attn_kernel(q_ref, iq_ref, sm_ref, ik_ref, kv_ref, tri_ref, o_ref, key_scr, bias_scr, *, tk, topk):
    tq = q_ref.shape[2]
    i = pl.program_id(1)
    t = i * tq + lax.broadcasted_iota(I32, (1, tq), 1)
    lane = _lane((tq, LANES))
    n_tile = ((i + 1) * tq + tk - 1) // tk
    tile = lambda j: pl.ds(pl.multiple_of(j * tk, tk), tk)
    kpos = lambda j: j * tk + lax.broadcasted_iota(I32, (tk, 1), 0)

    iw_t = (sm_ref[0] * (IDX_HEADS ** -0.5)).T
    parts = []
    for h in range(IDX_HEADS):
        blk = iq_ref[0, :, (h // 4) * LANES:(h // 4 + 1) * LANES]
        lo = (h % 4) * IDX_HD
        parts.append(jnp.where((lane >= lo) & (lane < lo + IDX_HD), blk, jnp.zeros_like(blk)))
    iqs = jnp.concatenate(parts, axis=0)

    def idx_body(j, carry):
        lg = _nt(ik_ref[0, tile(j), :], iqs)
        acc = jnp.zeros((tk, tq), F32)
        for h in range(IDX_HEADS):
            acc = acc + iw_t[IDX_HD + h:IDX_HD + h + 1, :] * jnp.maximum(lg[:, h * tq:(h + 1) * tq], 0.0)
        sc = jnp.where(kpos(j) <= t, acc, NEG)
        bits = pltpu.bitcast(sc, I32)
        key = jnp.where(bits < 0, bits ^ 0x7FFFFFFF, bits)
        key_scr[tile(j), :] = jnp.where(sc == 0.0, 0, key)
        return carry

    lax.fori_loop(0, n_tile, idx_body, 0)

    def count_ge(thr_key):
        def body(j, c):
            ge = jnp.where(key_scr[tile(j), :] >= thr_key, 1, 0)
            return c + jnp.sum(ge.reshape(tk // 8, 8, tq), axis=0)

        c = lax.fori_loop(0, n_tile, body, jnp.zeros((8, tq), I32))
        return jnp.sum(c, axis=0, keepdims=True)

    def bit_body(bi, ucand):
        utrial = ucand | jnp.left_shift(jnp.int32(1), 31 - bi)
        return jnp.where(count_ge(utrial ^ INT_MIN) >= topk, utrial, ucand)

    thr = lax.fori_loop(0, 32, bit_body, jnp.zeros((1, tq), I32)) ^ INT_MIN
    n_ge = count_ge(thr)
    n_gt = count_ge(thr + 1)
    need = topk - n_gt
    row_ok = (n_ge - n_gt == need) | (thr == NEG_KEY) | (n_ge < topk)
    simple = jnp.min(jnp.where(row_ok, 1.0, 0.0)) > 0.5

    def fast_bias():
        def body(j, carry):
            keep = (key_scr[tile(j), :] >= thr) & (kpos(j) <= t)
            bias_scr[:, tile(j)] = jnp.where(keep, 0.0, NEG).T
            return carry

        lax.fori_loop(0, n_tile, body, 0)

    def tie_bias():
        need_f = need.astype(F32)

        def body(j, run):
            key = key_scr[tile(j), :]
            kp = kpos(j)
            for c0 in range(0, tk, LANES):
                kc = key[c0:c0 + LANES]
                eq = kc == thr
                eq_f = jnp.where(eq, 1.0, 0.0)
                pref = _mm(tri_ref[...], eq_f.astype(BF16)) + run
                keep = ((kc > thr) | (eq & (pref <= need_f))) & (kp[c0:c0 + LANES] <= t)
                bias_scr[:, pl.ds(pl.multiple_of(j * tk + c0, LANES), LANES)] = jnp.where(keep, 0.0, NEG).T
                run = run + jnp.sum(eq_f, axis=0, keepdims=True)
            return run

        lax.fori_loop(0, n_tile, body, jnp.zeros((1, tq), F32))

    lax.cond(simple, fast_bias, tie_bias)

    qs = jnp.concatenate([q_ref[0, h] for h in range(C_HEADS)], axis=0)

    kv_at = lambda j, n=0: kv_ref[0, tile(j), :]
    carry, = _flash([qs], kv_at, kv_at, 0, n_tile, lambda j: bias_scr[:, tile(j)], (_flash_init(C_HEADS * tq),))
    o = _flash_out(carry)
    for p in range(C_HEADS // 2):
        a = pltpu.roll(o[(2 * p) * tq:(2 * p + 1) * tq], C_HD, 1)
        bb = o[(2 * p + 1) * tq:(2 * p + 2) * tq]
        o_ref[0, :, p * LANES:(p + 1) * LANES] = jnp.where(lane < C_HD, a, bb).astype(BF16)


def _dsa_attn(u, small_block, qc, iq, ik, kv):
    b, h, s, _ = qc.shape
    tq = min(128, s)
    tk = min(512, s)
    topk = min(TOPK_MAX, s // 4)
    tri = np.tril(np.ones((LANES, LANES), np.float32))
    full = lambda shape: pl.BlockSpec(shape, lambda bi, i: (bi,) + (0,) * (len(shape) - 1))
    return pl.pallas_call(
        functools.partial(_dsa_attn_kernel, tk=tk, topk=topk),
        out_shape=jax.ShapeDtypeStruct((b, s, h * C_HD), BF16),
        grid=(b, s // tq),
        in_specs=[pl.BlockSpec((1, h, tq, LANES), lambda bi, i: (bi, 0, i, 0)),
                  pl.BlockSpec((1, tq, IDX_HEADS * IDX_HD), lambda bi, i: (bi, i, 0)),
                  pl.BlockSpec((1, tq, LANES), lambda bi, i: (bi, i, small_block)),
                  full((1, s, LANES)), full((1, s, LANES)),
                  pl.BlockSpec((LANES, LANES), lambda bi, i: (0, 0))],
        out_specs=pl.BlockSpec((1, tq, h * C_HD), lambda bi, i: (bi, i, 0)),
        scratch_shapes=[pltpu.VMEM((s, tq), I32), pltpu.VMEM((tq, s), F32)],
        compiler_params=_cp("parallel", "parallel"),
        name="dsa_attn",
    )(qc, iq, u, ik, kv, jnp.asarray(tri, BF16))


def _mlstm_conv_kernel(x_ref, halo_ref, w_ref, b_ref, o_ref):
    tm = x_ref.shape[1]
    i = pl.program_id(1)
    lane = _lane((tm, x_ref.shape[2]))
    halo = jnp.where(i > 0, halo_ref[0], 0.0)
    xc = jnp.concatenate([halo, x_ref[0]], axis=0)
    off = halo.shape[0] - (CONV_W - 1)
    y = b_ref[...] + jnp.zeros_like(x_ref[0])
    for j in range(CONV_W):
        y = y + w_ref[j:j + 1, :] * xc[off + j:off + j + tm, :]
    y = y * _sigmoid(y)
    o_ref[0] = jnp.where(lane >= D_HEADS * D_QK, y * (D_QK ** -0.5), y)


def _mlstm_conv(u, qk_block, w, bias):
    b, s, _ = u.shape
    c = w.shape[1]
    tm = min(512, s)
    hb = 8
    return pl.pallas_call(
        _mlstm_conv_kernel,
        out_shape=jax.ShapeDtypeStruct((b, s, c), F32),
        grid=(b, s // tm),
        in_specs=[pl.BlockSpec((1, tm, c), lambda bi, i: (bi, i, qk_block)),
                  pl.BlockSpec((1, hb, c), lambda bi, i: (bi, jnp.maximum(i * (tm // hb) - 1, 0), qk_block)),
                  pl.BlockSpec((CONV_W, c), lambda bi, i: (0, 0)),
                  pl.BlockSpec((1, c), lambda bi, i: (0, 0))],
        out_specs=pl.BlockSpec((1, tm, c), lambda bi, i: (bi, i, 0)),
        compiler_params=_cp("parallel", "parallel"),
        name="mlstm_conv",
    )(u, u, w, bias)


def _mlstm_scan_kernel(qk_ref, v_ref, op_ref, sm_ref, gt_ref, bcol_ref, brow_ref, hg_ref, o_ref,
                       c_scr, n_scr, m_scr):
    tc = qk_ref.shape[1]
    L = CHUNK
    nqk = D_HEADS * D_QK

    @pl.when(pl.program_id(1) == 0)
    def _():
        c_scr[...] = jnp.zeros_like(c_scr)
        n_scr[...] = jnp.zeros_like(n_scr)
        m_scr[...] = jnp.zeros_like(m_scr)

    r = lax.broadcasted_iota(I32, (L, L), 0)
    c = lax.broadcasted_iota(I32, (L, L), 1)
    tril = c <= r
    tril_f = jnp.where(tril, 1.0, 0.0).astype(F32)
    triu_f = jnp.where(r <= c, 1.0, 0.0).astype(F32)
    lane = _lane((L, LANES))
    row128 = lax.broadcasted_iota(I32, (LANES, LANES), 0)
    i_lane, f_lane = IDX_HD + IDX_HEADS, IDX_HD + IDX_HEADS + D_HEADS

    def one(bb, ci):
        rows = pl.ds(pl.multiple_of(ci * L, L), L)
        qk = qk_ref[bb, rows, :]
        sm = sm_ref[bb, rows, :] + bcol_ref[...]
        bcol_all = jnp.dot(tril_f, _log_sigmoid(sm), preferred_element_type=F32, precision=HI)
        gt = gt_ref[bb, ci] + brow_ref[...]
        brow_all = jnp.dot(_log_sigmoid(gt), triu_f, preferred_element_type=F32, precision=HI)
        for p in range(D_HEADS // 2):
            qpair = qk[:, p * LANES:(p + 1) * LANES]
            kpair = qk[:, nqk + p * LANES:nqk + (p + 1) * LANES]
            kpair_b = kpair.astype(BF16)
            sp = bb * (D_HEADS // 2) + p
            c_prev = c_scr[sp]
            n_prev = n_scr[sp, 0:1, :]
            c_prev_b = c_prev.astype(BF16)
            upd = []
            for e in range(2):
                h = 2 * p + e
                in_half = (lane >= D_QK * e) & (lane < D_QK * (e + 1))
                qm = jnp.where(in_half, qpair, 0.0)
                qm_b = qm.astype(BF16)
                vh = v_ref[bb, rows, h * D_VD:(h + 1) * D_VD]
                b_col = bcol_all[:, f_lane + h:f_lane + h + 1]
                i_col = sm[:, i_lane + h:i_lane + h + 1]
                b_row = brow_all[D_HEADS + h:D_HEADS + h + 1, :]
                i_row = gt[h:h + 1, :]
                m_prev = m_scr[bb * D_HEADS + h, 0:1, 0:1]
                dmat = jnp.where(tril, b_col - b_row + i_row, NEG)
                inter = b_col + m_prev
                m_t = jnp.maximum(inter, jnp.max(dmat, axis=-1, keepdims=True))
                a = jnp.exp(inter - m_t)
                w = _nt(qm_b, kpair_b) * jnp.exp(dmat - m_t)
                num = a * _mm(qm_b, c_prev_b) + _mm(w.astype(BF16), vh.astype(BF16))
                den = a * jnp.sum(qm * n_prev, axis=-1, keepdims=True) + jnp.sum(w, axis=-1, keepdims=True)
                hout = num / jnp.maximum(jnp.abs(den), jnp.exp(-m_t))
                b_last = b_col[L - 1:L, :]
                g_col = b_last - b_col + i_col
                m_new = jnp.maximum(b_last + m_prev, jnp.max(g_col, axis=0, keepdims=True))
                ws = jnp.exp(g_col - m_new)
                decay = jnp.exp(b_last + m_prev - m_new)
                u_mat = _tn(kpair_b, (ws * vh).astype(BF16))
                k_sum = jnp.sum(ws * kpair, axis=0, keepdims=True)
                upd.append((decay * c_prev + u_mat, decay * n_prev + k_sum))
                m_scr[bb * D_HEADS + h] = jnp.broadcast_to(m_new, m_scr.shape[1:])
                hn = _row_rms(hout) * hg_ref[...]
                y = _sigmoid(op_ref[bb, rows, h * D_VD:(h + 1) * D_VD]) * hn
                o_ref[bb, rows, h * D_VD:(h + 1) * D_VD] = y.astype(BF16)
            c_scr[sp] = jnp.where(row128 < D_QK, upd[0][0], upd[1][0])
            n_new = jnp.where(lane[0:1] < D_QK, upd[0][1], upd[1][1])
            n_scr[sp] = jnp.broadcast_to(n_new, n_scr.shape[1:])

    def chunk(ci, carry):
        for bb in range(qk_ref.shape[0]):
            one(bb, ci)
        return carry

    lax.fori_loop(0, tc // L, chunk, 0)


def _mlstm_scan(u, cols, qk, gt, p):
    b, s, _ = u.shape
    tc = min(512, s)
    nb = next(n for n in (4, 2, 1) if b % n == 0)
    nv = D_HEADS * D_VD
    ublk = lambda name, w: pl.BlockSpec((nb, tc, w), lambda bi, i, c=cols[name] // w: (bi, i, c))
    const = lambda shape: pl.BlockSpec(shape, lambda bi, i: (0,) * len(shape))
    return pl.pallas_call(
        _mlstm_scan_kernel,
        out_shape=jax.ShapeDtypeStruct((b, s, nv), BF16),
        grid=(b // nb, s // tc),
        in_specs=[pl.BlockSpec((nb, tc, qk.shape[-1]), lambda bi, i: (bi, i, 0)),
                  ublk("d_v", nv), ublk("d_o", nv), ublk("small", LANES),
                  pl.BlockSpec((nb, tc // CHUNK, 8, CHUNK), lambda bi, i: (bi, i, 0, 0)),
                  const((1, LANES)), const((8, 1)), const((1, D_VD))],
        out_specs=pl.BlockSpec((nb, tc, nv), lambda bi, i: (bi, i, 0)),
        scratch_shapes=[pltpu.VMEM((nb * D_HEADS // 2, LANES, LANES), F32),
                        pltpu.VMEM((nb * D_HEADS // 2, 8, LANES), F32),
                        pltpu.VMEM((nb * D_HEADS, 8, LANES), F32)],
        compiler_params=_cp("parallel", "arbitrary"),
        name="mlstm_scan",
    )(qk, u, u, u, gt, p["bias_col"], p["bias_row"], p["h_g"])


EVEN_SRC = dict(a_ql=(0, 256), a_kvl=(256, 128), a_kr=(384, 32), a_gate=(416, 512), b_q=(928, 512),
                b_kc=(1440, 128), b_vc=(1568, 128), b_ks=(1696, 128), b_vs=(1824, 128), b_kw=(1952, 128),
                b_vw=(2080, 128), b_g=(2208, 24), b_gate=(2232, 512), m_q=(2744, 256), m_gate=(3000, 256))
EVEN_DST = dict(a_ql=0, a_kvl=256, a_kr=384, b_q=512, a_gate=1024, b_gate=1536, m_q=2048, m_gate=2304,
                b_g=2560, b_kc=2688, b_vc=2816, b_ks=2944, b_vs=3072, b_kw=3200, b_vw=3328)
EVEN_COLS_PAD = 3456

ODD_SRC = dict(c_q=(0, 512), c_k=(512, 64), c_v=(576, 64), c_iq=(640, 256), c_ik=(896, 32), c_iw=(928, 8),
               c_gate=(936, 512), d_q=(1448, 256), d_k=(1704, 256), d_v=(1960, 512), d_i=(2472, 4),
               d_f=(2476, 4), d_o=(2480, 512), d_gate=(2992, 512), m_q=(3504, 256), m_gate=(3760, 256))
ODD_DST = dict(c_q=0, c_gate=512, d_gate=1024, d_v=1536, d_o=2048, d_q=2560, d_k=2816, m_q=3072, m_gate=3328,
               c_iq=3584, c_k=3840, c_v=3904, c_ik=3968, c_iw=4000, d_i=4008, d_f=4012)
ODD_COLS_PAD = 4096


def _permute_cols(w, src, dst, total):
    idx = np.zeros((total,), np.int32)
    keep = np.zeros((total,), np.float32)
    for name, (start, width) in src.items():
        idx[dst[name]:dst[name] + width] = np.arange(start, start + width)
        keep[dst[name]:dst[name] + width] = 1.0
    return (jnp.take(w, jnp.asarray(idx), axis=1) * jnp.asarray(keep)).astype(BF16)


def _tile_lanes(v, reps):
    return jnp.tile(v.astype(F32).reshape(1, -1), (1, reps))


def _rope_tables(positions, d2):
    inv = ROPE_THETA ** (-jnp.arange(d2, dtype=F32) / d2)
    ang = positions.astype(F32)[..., None] * inv
    c, s = jnp.cos(ang), jnp.sin(ang)
    reps = LANES // (2 * d2)
    return (jnp.tile(jnp.concatenate([c, c], axis=-1), (1, 1, reps)),
            jnp.tile(jnp.concatenate([-s, s], axis=-1), (1, 1, reps)))


def _block_diag2(w):
    z = jnp.zeros_like(w)
    return jnp.concatenate([jnp.concatenate([w, z], axis=-1), jnp.concatenate([z, w], axis=-1)], axis=-2)


def kernel(x, mem, positions, ln_g, mem_norm_g, mem_w_kv, mem_q_norm_g, mem_k_norm_g, w_out, even_w_in, mla_q_lat_g, mla_kv_lat_g, mla_w_uq, mla_w_ukv, mla_q_norm_g, mla_k_norm_g, nsa_q_norm_g, nsa_k_norm_g, nsa_cmp_pos, nsa_cmp_w1, nsa_cmp_w2, odd_w_in, dsa_q_norm_g, dsa_k_norm_g, mlstm_conv_w, mlstm_conv_b, mlstm_i_bias, mlstm_f_bias, mlstm_h_norm_g):
    b, s, _ = x.shape
    depth = ln_g.shape[0]
    cos64, sin64 = _rope_tables(positions, 32)
    cos32, sin32 = _rope_tables(positions, 16)
    tabs = dict(cos64=cos64, sin64=sin64, cos32=cos32, sin32=sin32)
    n_pad = s // CMP_STRIDE
    cmp_pos = jnp.pad(positions[:, CMP_LEN - 1::CMP_STRIDE], ((0, 0), (0, 0)))[:, :n_pad]
    cmp_pos = jnp.pad(cmp_pos, ((0, 0), (0, n_pad - cmp_pos.shape[1])))
    cosc, sinc = _rope_tables(cmp_pos, 32)

    mem_k, mem_v = _mem_kv(mem, mem_norm_g.reshape(depth, 1, -1), mem_w_kv.astype(BF16),
                           jnp.tile(mem_k_norm_g, (1, 2)).reshape(depth, 1, LANES))

    hq = np.arange(A_HEADS)[:, None] * (A_NOPE + A_ROPE)
    uq_idx = np.concatenate([(hq + np.arange(A_NOPE)[None, :]).ravel(),
                             (hq + A_NOPE + np.arange(A_ROPE)[None, :]).ravel()])
    hk = np.arange(A_HEADS)[:, None] * (A_NOPE + A_VD)
    ukv_idx = np.concatenate([(hk + np.arange(A_NOPE)[None, :]).ravel(),
                              (hk + A_NOPE + np.arange(A_VD)[None, :]).ravel()])

    for layer in range(depth):
        li = layer // 2
        g_ln = ln_g[layer].reshape(1, -1)
        mq_g = _tile_lanes(mem_q_norm_g[layer], 2)
        if layer % 2 == 0:
            cols = EVEN_DST
            u = _in_proj(x, g_ln, _permute_cols(even_w_in[li], EVEN_SRC, EVEN_DST, EVEN_COLS_PAD))
            pa = dict(q_lat_g=mla_q_lat_g[li].reshape(1, -1), kv_lat_g=mla_kv_lat_g[li].reshape(1, -1),
                      w_uq=jnp.take(mla_w_uq[li], jnp.asarray(uq_idx), axis=1).astype(BF16),
                      w_ukv=jnp.take(mla_w_ukv[li], jnp.asarray(ukv_idx), axis=1).astype(BF16),
                      qn_g=_tile_lanes(mla_q_norm_g[li, :A_NOPE], 2), qr_g=_tile_lanes(mla_q_norm_g[li, A_NOPE:], 4),
                      kn_g=_tile_lanes(mla_k_norm_g[li, :A_NOPE], 2), kr_g=_tile_lanes(mla_k_norm_g[li, A_NOPE:], 4))
            qa, ka, va = _mla_prep(u, cos32, sin32, pa)
            y1 = _mla_attn(qa, ka, va)
            pb = dict(q_g=_tile_lanes(nsa_q_norm_g[li], 2), ks_g=_tile_lanes(nsa_k_norm_g[li, 1], 2),
                      kw_g=_tile_lanes(nsa_k_norm_g[li, 2], 2), kc_g=_tile_lanes(nsa_k_norm_g[li, 0], 2),
                      pe_k=jnp.tile(nsa_cmp_pos[li, 0], (1, 2)), pe_v=jnp.tile(nsa_cmp_pos[li, 1], (1, 2)),
                      w1k=_block_diag2(nsa_cmp_w1[li, 0].reshape(CMP_LEN, B_HD, B_HD)).astype(BF16),
                      w1v=_block_diag2(nsa_cmp_w1[li, 1].reshape(CMP_LEN, B_HD, B_HD)).astype(BF16),
                      w2k=_block_diag2(nsa_cmp_w2[li, 0]).astype(BF16),
                      w2v=_block_diag2(nsa_cmp_w2[li, 1]).astype(BF16))
            qb, ks, vs, kw, vw = _nsa_prep(u, cols, cos64, sin64, pb)
            kcmp, vcmp = _nsa_cmp(u, cols, cosc, sinc, pb)
            y2 = _nsa_attn(u, cols["b_g"] // LANES, qb, kcmp, vcmp, ks, vs, kw, vw)
            gate_blocks = (cols["a_gate"] // 512, cols["b_gate"] // 512, cols["m_gate"] // 256)
        else:
            cols = dict(ODD_DST, c_kv=ODD_DST["c_k"], small=ODD_DST["c_ik"])
            u = _in_proj(x, g_ln, _permute_cols(odd_w_in[li], ODD_SRC, ODD_DST, ODD_COLS_PAD))
            pc = dict(q_g=_tile_lanes(dsa_q_norm_g[li], 2), k_g=_tile_lanes(dsa_k_norm_g[li], 2))
            qc, kvc, iq, ik = _dsa_prep(u, cols, tabs, pc)
            y1 = _dsa_attn(u, cols["small"] // LANES, qc, iq, ik, kvc)
            qk = _mlstm_conv(u, cols["d_q"] // 512, mlstm_conv_w[li], mlstm_conv_b[li].reshape(1, -1))
            gates = u[:, :, cols["d_i"]:cols["d_i"] + 2 * D_HEADS]
            gt = gates.reshape(b, s // CHUNK, CHUNK, 2 * D_HEADS).transpose(0, 1, 3, 2)
            bias8 = jnp.concatenate([mlstm_i_bias[li], mlstm_f_bias[li]]).astype(F32)
            bias_col = jnp.zeros((1, LANES), F32).at[0, cols["d_i"] - cols["small"]:cols["d_i"] - cols["small"] + 8].set(bias8)
            pd = dict(bias_col=bias_col, bias_row=bias8.reshape(8, 1), h_g=mlstm_h_norm_g[li].reshape(1, -1))
            y2 = _mlstm_scan(u, cols, qk, gt, pd)
            gate_blocks = (cols["c_gate"] // 512, cols["d_gate"] // 512, cols["m_gate"] // 256)
        ym = _mem_attn(u, cols["m_q"] // 256, mem_k, mem_v, layer, mq_g)
        x = _out_proj(x, y1, y2, ym, u, gate_blocks, w_out[layer].astype(BF16))
    return x
```

```python
import functools

import numpy as np
import jax
import jax.numpy as jnp
from jax import lax
from jax.experimental import pallas as pl
from jax.experimental.pallas import tpu as pltpu

F32, BF16, I32 = jnp.float32, jnp.bfloat16, jnp.int32
HI = lax.Precision.HIGHEST
NEG = -1e30
EPS = 1e-6
ROPE_THETA = 10000.0
LANES = 128
VMEM_LIMIT_BYTES = 48 * 1024 * 1024

D_MODEL = 1024
DEPTH = 4
A_HEADS, A_NOPE, A_ROPE, A_VD, A_QLAT, A_KVLAT = 8, 64, 32, 64, 256, 128
B_HEADS, B_KV_HEADS, B_HD = 8, 2, 64
CMP_LEN, CMP_STRIDE, SEL_LEN, N_SEL, WINDOW = 32, 16, 64, 16, 512
C_HEADS, C_HD, IDX_HEADS, IDX_HD, TOPK_MAX = 8, 64, 8, 32, 256
D_HEADS, D_QK, D_VD, CONV_W, CHUNK = 4, 64, 128, 4, 64
M_HEADS, M_HD = 4, 64

INT_MIN = np.int32(-2 ** 31)
NEG_KEY = int(np.float32(NEG).view(np.int32) ^ np.int32(0x7FFFFFFF))
LOG2E = float(np.log2(np.e))


def _cp(*sem):
    return pltpu.CompilerParams(dimension_semantics=sem, vmem_limit_bytes=VMEM_LIMIT_BYTES)


def _nt(a, b):
    return lax.dot_general(a, b, (((1,), (1,)), ((), ())), preferred_element_type=F32)


def _tn(a, b):
    return lax.dot_general(a, b, (((0,), (0,)), ((), ())), preferred_element_type=F32)


def _mm(a, b):
    return jnp.dot(a, b, preferred_element_type=F32)


def _sigmoid(x):
    return 1.0 / (1.0 + jnp.exp(-x))


def _log_sigmoid(x):
    return jnp.minimum(x, 0.0) - jnp.log1p(jnp.exp(-jnp.abs(x)))


def _lane(shape):
    return lax.broadcasted_iota(I32, shape, len(shape) - 1)


def _group_mat(gs):
    r = lax.broadcasted_iota(I32, (LANES, LANES), 0)
    c = lax.broadcasted_iota(I32, (LANES, LANES), 1)
    sh = gs.bit_length() - 1
    return jnp.where((r >> sh) == (c >> sh), 1.0, 0.0).astype(BF16)


def _mm_split(x, w01):
    hi = x.astype(BF16)
    lo = (x - hi.astype(F32)).astype(BF16)
    return _mm(hi, w01) + _mm(lo, w01)


def _group_rms(x, gs):
    ss = _mm_split(x * x, _group_mat(gs))
    return x * lax.rsqrt(ss * (1.0 / gs) + EPS)


def _rope(x, cosp, sinp, half):
    lane = _lane(x.shape)
    rot = jnp.where((lane & (2 * half - 1)) < half,
                    pltpu.roll(x, LANES - half, 1), pltpu.roll(x, half, 1))
    return x * cosp + rot * sinp


def _row_rms(x):
    return x * lax.rsqrt(jnp.mean(x * x, axis=-1, keepdims=True) + EPS)


def _in_proj_kernel(x_ref, g_ref, w_ref, o_ref, o32_ref, *, c32):
    h = (_row_rms(x_ref[0]) * g_ref[...]).astype(BF16)
    ncol = o_ref.shape[-1]
    for c0 in range(0, ncol, 512):
        c1 = min(ncol, c0 + 512)
        o_ref[0, :, c0:c1] = _mm(h, w_ref[:, c0:c1]).astype(BF16)
    o32_ref[0] = _mm(h, w_ref[:, c32:c32 + LANES])


def _in_proj(x, g, w, c32):
    b, s, d = x.shape
    c = w.shape[1]
    tm = min(256, s)
    return pl.pallas_call(
        functools.partial(_in_proj_kernel, c32=c32),
        out_shape=(jax.ShapeDtypeStruct((b, s, c), BF16), jax.ShapeDtypeStruct((b, s, LANES), F32)),
        grid=(b, s // tm),
        in_specs=[pl.BlockSpec((1, tm, d), lambda bi, i: (bi, i, 0)),
                  pl.BlockSpec((1, d), lambda bi, i: (0, 0)),
                  pl.BlockSpec((d, c), lambda bi, i: (0, 0))],
        out_specs=(pl.BlockSpec((1, tm, c), lambda bi, i: (bi, i, 0)),
                   pl.BlockSpec((1, tm, LANES), lambda bi, i: (bi, i, 0))),
        compiler_params=_cp("parallel", "parallel"),
        name="in_proj",
    )(x, g, w)


def _out_proj_kernel(x_ref, y1_ref, y2_ref, ym_ref, g1_ref, g2_ref, gm_ref, w_ref, o_ref):
    def gated(y_ref, g_ref):
        g = g_ref[0].astype(F32)
        return (y_ref[0].astype(F32) * (g * _sigmoid(g))).astype(BF16)

    n1 = y1_ref.shape[-1]
    n2 = y2_ref.shape[-1]
    acc = x_ref[0]
    acc = acc + _mm(gated(y1_ref, g1_ref), w_ref[0:n1, :])
    acc = acc + _mm(gated(y2_ref, g2_ref), w_ref[n1:n1 + n2, :])
    acc = acc + _mm(gated(ym_ref, gm_ref), w_ref[n1 + n2:, :])
    o_ref[0] = acc


def _out_proj(x, y1, y2, ym, u, gate_blocks, w):
    b, s, d = x.shape
    tm = min(512, s)
    i1, i2, im = gate_blocks
    n1, n2, nm = y1.shape[-1], y2.shape[-1], ym.shape[-1]
    row = lambda bi, i: (bi, i, 0)
    return pl.pallas_call(
        _out_proj_kernel,
        out_shape=jax.ShapeDtypeStruct((b, s, d), F32),
        grid=(b, s // tm),
        in_specs=[pl.BlockSpec((1, tm, d), row),
                  pl.BlockSpec((1, tm, n1), row),
                  pl.BlockSpec((1, tm, n2), row),
                  pl.BlockSpec((1, tm, nm), row),
                  pl.BlockSpec((1, tm, n1), lambda bi, i: (bi, i, i1)),
                  pl.BlockSpec((1, tm, n2), lambda bi, i: (bi, i, i2)),
                  pl.BlockSpec((1, tm, nm), lambda bi, i: (bi, i, im)),
                  pl.BlockSpec(w.shape, lambda bi, i: (0, 0))],
        out_specs=pl.BlockSpec((1, tm, d), row),
        compiler_params=_cp("parallel", "parallel"),
        name="out_proj",
    )(x, y1, y2, ym, u, u, u, w)


def _mem_kv_kernel(mem_ref, g_ref, w_ref, kg_ref, k_ref, v_ref):
    h = (_row_rms(mem_ref[0]) * g_ref[0]).astype(BF16)
    kv = _mm(h, w_ref[0])
    nk = k_ref.shape[-1]
    for c0 in range(0, nk, LANES):
        k_ref[0, 0, :, c0:c0 + LANES] = (_group_rms(kv[:, c0:c0 + LANES], M_HD) * kg_ref[0]).astype(BF16)
    v_ref[0, 0] = kv[:, nk:].astype(BF16)


def _mem_kv(mem, g, w, kg):
    b, m, d = mem.shape
    depth = w.shape[0]
    nk = M_HEADS * M_HD
    out = jax.ShapeDtypeStruct((depth, b, m, nk), BF16)
    return pl.pallas_call(
        _mem_kv_kernel,
        out_shape=(out, out),
        grid=(depth, b),
        in_specs=[pl.BlockSpec((1, m, d), lambda l, bi: (bi, 0, 0)),
                  pl.BlockSpec((1, 1, d), lambda l, bi: (l, 0, 0)),
                  pl.BlockSpec((1, d, 2 * nk), lambda l, bi: (l, 0, 0)),
                  pl.BlockSpec((1, 1, LANES), lambda l, bi: (l, 0, 0))],
        out_specs=(pl.BlockSpec((1, 1, m, nk), lambda l, bi: (l, bi, 0, 0)),
                   pl.BlockSpec((1, 1, m, nk), lambda l, bi: (l, bi, 0, 0))),
        compiler_params=_cp("parallel", "parallel"),
        name="mem_kv",
    )(mem, g, w, kg)


def _mem_attn_kernel(q_ref, k_ref, v_ref, qg_ref, o_ref):
    tq = q_ref.shape[1]
    lane = _lane((tq, LANES))
    for p in range(M_HEADS // 2):
        sl = slice(p * LANES, (p + 1) * LANES)
        q = (_group_rms(q_ref[0, :, sl].astype(F32), M_HD) * qg_ref[...] * (M_HD ** -0.5)).astype(BF16)
        k = k_ref[0, 0, :, sl]
        v = v_ref[0, 0, :, sl]
        halves = []
        for e in range(2):
            in_half = (lane >= 64 * e) & (lane < 64 * e + 64)
            s = _nt(jnp.where(in_half, q, jnp.zeros_like(q)), k)
            ex = jnp.exp(s - jnp.max(s, axis=-1, keepdims=True))
            o = _mm(ex.astype(BF16), v) / jnp.sum(ex, axis=-1, keepdims=True)
            halves.append(o)
        o_ref[0, :, sl] = jnp.where(lane < 64, halves[0], halves[1]).astype(BF16)


def _mem_attn(u, q_block, k, v, layer, qg):
    b, s, _ = u.shape
    m, nk = k.shape[2], k.shape[3]
    tq = min(512, s)
    return pl.pallas_call(
        _mem_attn_kernel,
        out_shape=jax.ShapeDtypeStruct((b, s, nk), BF16),
        grid=(b, s // tq),
        in_specs=[pl.BlockSpec((1, tq, nk), lambda bi, i: (bi, i, q_block)),
                  pl.BlockSpec((1, 1, m, nk), lambda bi, i: (layer, bi, 0, 0)),
                  pl.BlockSpec((1, 1, m, nk), lambda bi, i: (layer, bi, 0, 0)),
                  pl.BlockSpec((1, LANES), lambda bi, i: (0, 0))],
        out_specs=pl.BlockSpec((1, tq, nk), lambda bi, i: (bi, i, 0)),
        compiler_params=_cp("parallel", "parallel"),
        name="mem_attn",
    )(u, k, v, qg)


def _flash(qs, k_at, v_at, j0, j1, bias_at, carry):
    def body(j, c):
        bias = None if bias_at is None else bias_at(j)
        return tuple(_softmax_step(q, k_at(j, n), v_at(j), bias, cn) for n, (q, cn) in enumerate(zip(qs, c)))

    return lax.fori_loop(j0, j1, body, carry)


def _flash_init(rows):
    return (jnp.full((rows, 1), NEG, F32), jnp.zeros((rows, 1), F32), jnp.zeros((rows, LANES), F32))


def _softmax_update(s, v, bias, carry):
    m, l, acc = carry
    if bias is not None:
        rep = s.shape[0] // bias.shape[0]
        s = (s.reshape(rep, bias.shape[0], s.shape[1]) + bias[None]).reshape(s.shape)
    m_new = jnp.maximum(m, jnp.max(s, axis=-1, keepdims=True))
    alpha = jnp.exp2(m - m_new)
    p = jnp.exp2(s - m_new)
    l = alpha * l + jnp.sum(p, axis=-1, keepdims=True)
    acc = alpha * acc + _mm(p.astype(BF16), v)
    return m_new, l, acc


def _softmax_step(q, k, v, bias, carry):
    return _softmax_update(_nt(q, k), v, bias, carry)


def _flash_out(carry):
    _, l, acc = carry
    return acc / l


def _mla_prep_kernel(u_ref, cos_ref, sin_ref, qlg_ref, kvlg_ref, wuq_ref, wukv_ref,
                     qng_ref, qrg_ref, kng_ref, krg_ref, q_ref, k_ref, v_ref):
    tm = u_ref.shape[1]
    lane = _lane((tm, LANES))
    cosp, sinp = cos_ref[0], sin_ref[0]
    half = A_ROPE // 2
    scale = (A_NOPE + A_ROPE) ** -0.5 * LOG2E
    ql = (_row_rms(u_ref[0, :, 0:A_QLAT].astype(F32)) * qlg_ref[...]).astype(BF16)
    kvl = (_row_rms(u_ref[0, :, A_QLAT:A_QLAT + A_KVLAT].astype(F32)) * kvlg_ref[...]).astype(BF16)
    q = _mm(ql, wuq_ref[...])
    kv = _mm(kvl, wukv_ref[...])
    n_nope = A_HEADS * A_NOPE
    v_ref[0] = kv[:, n_nope:].astype(BF16)
    kr = u_ref[0, :, A_QLAT + A_KVLAT:A_QLAT + A_KVLAT + LANES].astype(F32)
    kpe = _rope(_group_rms(kr, A_ROPE) * krg_ref[...], cosp, sinp, half)
    kpe = pltpu.roll(kpe, A_NOPE, 1)
    qn = [_group_rms(q[:, c:c + LANES], A_NOPE) * qng_ref[...] for c in range(0, n_nope, LANES)]
    kn = [_group_rms(kv[:, c:c + LANES], A_NOPE) * kng_ref[...] for c in range(0, n_nope, LANES)]
    qr = [_rope(_group_rms(q[:, n_nope + c:n_nope + c + LANES], A_ROPE) * qrg_ref[...], cosp, sinp, half)
          for c in range(0, A_HEADS * A_ROPE, LANES)]
    for h in range(A_HEADS):
        qn_h = qn[h // 2] if h % 2 == 0 else pltpu.roll(qn[h // 2], A_NOPE, 1)
        kn_h = kn[h // 2] if h % 2 == 0 else pltpu.roll(kn[h // 2], A_NOPE, 1)
        shift = (A_NOPE - (h % 4) * A_ROPE) % LANES
        qr_h = qr[h // 4] if shift == 0 else pltpu.roll(qr[h // 4], shift, 1)
        qf = jnp.where(lane < A_NOPE, qn_h, jnp.where(lane < A_NOPE + A_ROPE, qr_h, 0.0))
        kf = jnp.where(lane < A_NOPE, kn_h, jnp.where(lane < A_NOPE + A_ROPE, kpe, 0.0))
        q_ref[0, h] = (qf * scale).astype(BF16)
        k_ref[0, h] = kf.astype(BF16)


def _mla_prep(u, cos32, sin32, p):
    b, s, _ = u.shape
    tm = min(256, s)
    hd = jax.ShapeDtypeStruct((b, A_HEADS, s, LANES), BF16)
    const = lambda shape: pl.BlockSpec(shape, lambda bi, i: (0,) * len(shape))
    return pl.pallas_call(
        _mla_prep_kernel,
        out_shape=(hd, hd, jax.ShapeDtypeStruct((b, s, A_HEADS * A_VD), BF16)),
        grid=(b, s // tm),
        in_specs=[pl.BlockSpec((1, tm, 512), lambda bi, i: (bi, i, 0)),
                  pl.BlockSpec((1, tm, LANES), lambda bi, i: (bi, i, 0)),
                  pl.BlockSpec((1, tm, LANES), lambda bi, i: (bi, i, 0)),
                  const((1, A_QLAT)), const((1, A_KVLAT)),
                  const(p["w_uq"].shape), const(p["w_ukv"].shape),
                  const((1, LANES)), const((1, LANES)), const((1, LANES)), const((1, LANES))],
        out_specs=(pl.BlockSpec((1, A_HEADS, tm, LANES), lambda bi, i: (bi, 0, i, 0)),
                   pl.BlockSpec((1, A_HEADS, tm, LANES), lambda bi, i: (bi, 0, i, 0)),
                   pl.BlockSpec((1, tm, A_HEADS * A_VD), lambda bi, i: (bi, i, 0))),
        compiler_params=_cp("parallel", "parallel"),
        name="mla_prep",
    )(u, cos32, sin32, p["q_lat_g"], p["kv_lat_g"], p["w_uq"], p["w_ukv"],
      p["qn_g"], p["qr_g"], p["kn_g"], p["kr_g"])


def _mla_attn_kernel(q_ref, k_ref, v_ref, o_ref):
    tq = q_ref.shape[2]
    i = pl.program_id(2)
    lane = _lane((tq, LANES))
    r = lax.broadcasted_iota(I32, (tq, tq), 0)
    c = lax.broadcasted_iota(I32, (tq, tq), 1)
    diag_bias = jnp.where(c <= r, 0.0, NEG).astype(F32)
    qs = [q_ref[0, 0], q_ref[0, 1]]
    tile = lambda j: pl.ds(pl.multiple_of(j * tq, tq), tq)
    k_at = lambda j, e: k_ref[0, e, tile(j), :]
    v_at = lambda j: v_ref[0, tile(j), :]
    carry = _flash(qs, k_at, v_at, 0, i, None, (_flash_init(tq), _flash_init(tq)))
    res = [_flash_out(_softmax_step(qs[e], k_at(i, e), v_at(i), diag_bias, carry[e])) for e in range(2)]
    o_ref[0] = jnp.where(lane < A_VD, res[0], res[1]).astype(BF16)


def _mla_attn(q, k, v):
    b, h, s, _ = q.shape
    tq = min(512, s)
    return pl.pallas_call(
        _mla_attn_kernel,
        out_shape=jax.ShapeDtypeStruct((b, s, h * A_VD), BF16),
        grid=(b, h // 2, s // tq),
        in_specs=[pl.BlockSpec((1, 2, tq, LANES), lambda bi, p, i: (bi, p, i, 0)),
                  pl.BlockSpec((1, 2, s, LANES), lambda bi, p, i: (bi, p, 0, 0)),
                  pl.BlockSpec((1, s, LANES), lambda bi, p, i: (bi, 0, p))],
        out_specs=pl.BlockSpec((1, tq, LANES), lambda bi, p, i: (bi, i, p)),
        compiler_params=_cp("parallel", "parallel", "parallel"),
        name="mla_attn",
    )(q, k, v)


def _nsa_prep_kernel(q_in, ks_in, kw_in, cos_ref, sin_ref, qg_ref, ksg_ref, kwg_ref, q_ref, ks_ref, kw_ref):
    tm = q_in.shape[1]
    lane = _lane((tm, LANES))
    cosp, sinp = cos_ref[0], sin_ref[0]
    half = B_HD // 2
    rep = B_HEADS // B_KV_HEADS
    for p in range(B_HEADS // 2):
        y = _rope(_group_rms(q_in[0, :, p * LANES:(p + 1) * LANES].astype(F32), B_HD) * qg_ref[...], cosp, sinp, half)
        y = y * (B_HD ** -0.5 * LOG2E)
        y_sw = pltpu.roll(y, B_HD, 1)
        for e in range(2):
            h = 2 * p + e
            g = h // rep
            src = y if e == g else y_sw
            in_grp = (lane >= B_HD * g) & (lane < B_HD * (g + 1))
            q_ref[0, h] = jnp.where(in_grp, src, 0.0).astype(BF16)
    ks_ref[0] = _rope(_group_rms(ks_in[0].astype(F32), B_HD) * ksg_ref[...], cosp, sinp, half).astype(BF16)
    kw_ref[0] = _rope(_group_rms(kw_in[0].astype(F32), B_HD) * kwg_ref[...], cosp, sinp, half).astype(BF16)


def _nsa_prep(u, cols, cos64, sin64, p):
    b, s, _ = u.shape
    tm = min(256, s)
    blk = lambda name: pl.BlockSpec((1, tm, LANES), lambda bi, i, c=cols[name] // LANES: (bi, i, c))
    row = pl.BlockSpec((1, tm, LANES), lambda bi, i: (bi, i, 0))
    const = pl.BlockSpec((1, LANES), lambda bi, i: (0, 0))
    kvs = jax.ShapeDtypeStruct((b, s, LANES), BF16)
    return pl.pallas_call(
        _nsa_prep_kernel,
        out_shape=(jax.ShapeDtypeStruct((b, B_HEADS, s, LANES), BF16), kvs, kvs),
        grid=(b, s // tm),
        in_specs=[pl.BlockSpec((1, tm, 512), lambda bi, i, c=cols["b_q"] // 512: (bi, i, c)),
                  blk("b_ks"), blk("b_kw"), row, row, const, const, const],
        out_specs=(pl.BlockSpec((1, B_HEADS, tm, LANES), lambda bi, i: (bi, 0, i, 0)), row, row),
        compiler_params=_cp("parallel", "parallel"),
        name="nsa_prep",
    )(u, u, u, cos64, sin64, p["q_g"], p["ks_g"], p["kw_g"])


def _nsa_cmp_kernel(kc_in, vc_in, pek_ref, pev_ref, w1k_ref, w1v_ref, w2k_ref, w2v_ref, kg_ref,
                    cos_ref, sin_ref, ko_ref, vo_ref, pad_ref):
    s = kc_in.shape[1]
    n_pad = ko_ref.shape[1]

    def compress(x_in, pe_ref, w1_ref, w2_ref):
        pad_ref[0:s, :] = x_in[0].astype(F32)
        pad_ref[s:s + CMP_STRIDE, :] = jnp.zeros((CMP_STRIDE, LANES), F32)
        acc = jnp.zeros((n_pad, LANES), F32)
        for l in range(CMP_LEN):
            xl = pad_ref[pl.ds(l, n_pad, stride=CMP_STRIDE), :] + pe_ref[l:l + 1, :]
            acc = acc + _mm(xl.astype(BF16), w1_ref[l])
        mid = acc * _sigmoid(acc)
        return _mm(mid.astype(BF16), w2_ref[...])

    kc = compress(kc_in, pek_ref, w1k_ref, w2k_ref)
    ko_ref[0] = _rope(_group_rms(kc, B_HD) * kg_ref[...], cos_ref[0], sin_ref[0], B_HD // 2).astype(BF16)
    vo_ref[0] = compress(vc_in, pev_ref, w1v_ref, w2v_ref).astype(BF16)


def _nsa_cmp(u, cols, cosc, sinc, p):
    b, s, _ = u.shape
    n_pad = s // CMP_STRIDE
    blk = lambda name: pl.BlockSpec((1, s, LANES), lambda bi, c=cols[name] // LANES: (bi, 0, c))
    const = lambda shape: pl.BlockSpec(shape, lambda bi: (0,) * len(shape))
    out = jax.ShapeDtypeStruct((b, n_pad, LANES), BF16)
    ospec = pl.BlockSpec((1, n_pad, LANES), lambda bi: (bi, 0, 0))
    return pl.pallas_call(
        _nsa_cmp_kernel,
        out_shape=(out, out),
        grid=(b,),
        in_specs=[blk("b_kc"), blk("b_vc"), const((CMP_LEN, LANES)), const((CMP_LEN, LANES)),
                  const((CMP_LEN, LANES, LANES)), const((CMP_LEN, LANES, LANES)),
                  const((LANES, LANES)), const((LANES, LANES)), const((1, LANES)), ospec, ospec],
        out_specs=(ospec, ospec),
        scratch_shapes=[pltpu.VMEM((s + CMP_STRIDE, LANES), F32)],
        compiler_params=_cp("parallel"),
        name="nsa_cmp",
    )(u, u, p["pe_k"], p["pe_v"], p["w1k"], p["w1v"], p["w2k"], p["w2v"], p["kc_g"], cosc, sinc)


def _nsa_attn_kernel(q_ref, kc_ref, vc_ref, ks_ref, vs_ref, kw_ref, vw_ref, g_ref, e_ref, ovl_ref, o_ref,
                     *, n_blk, n_sel, tks, ww):
    tq = q_ref.shape[2]
    n_pad = kc_ref.shape[1]
    i = pl.program_id(1)
    t = i * tq + lax.broadcasted_iota(I32, (tq, 1), 0)
    lane = _lane((tq, LANES))
    gates = _sigmoid(g_ref[0])
    rep = B_HEADS // B_KV_HEADS
    ncol = lax.broadcasted_iota(I32, (1, n_pad), 1)
    valid_c = (ncol * CMP_STRIDE + (CMP_LEN - 1)) <= t
    kc = kc_ref[0]
    vc = vc_ref[0]
    cur = t >> (SEL_LEN.bit_length() - 1)
    forced = jnp.where(lane == cur, 3e4, jnp.where(lane == cur - 1, 2e4, jnp.where(lane == 0, 1e4, 0.0)))
    adm = (lane * SEL_LEN <= t) & (lane < n_blk)
    rows = rep * tq
    n_rv = n_blk // 8
    sub = lax.broadcasted_iota(I32, (8, tq), 0)
    w0 = pl.multiple_of(jnp.maximum(i * tq + tq - ww, 0), tq)
    kp_w = w0 + lax.broadcasted_iota(I32, (1, ww), 1)
    win_bias = jnp.where((kp_w <= t) & (kp_w > t - WINDOW), 0.0, NEG)
    for g in range(B_KV_HEADS):
        qs = jnp.concatenate([q_ref[0, rep * g + r] for r in range(rep)], axis=0)
        gate = lambda c: jnp.concatenate(
            [gates[:, (rep * g + r) * 3 + c:(rep * g + r) * 3 + c + 1] for r in range(rep)], axis=0)
        s = jnp.where(valid_c[None], _nt(qs, kc).reshape(rep, tq, n_pad), NEG)
        ex = jnp.exp2(s - jnp.max(s, axis=-1, keepdims=True))
        pc = jnp.where(valid_c[None], ex / jnp.sum(ex, axis=-1, keepdims=True), 0.0)
        out = gate(0) * _mm(pc.reshape(rows, n_pad).astype(BF16), vc)
        imp = _mm_split(jnp.sum(pc, axis=0), ovl_ref[...])
        score_t = jnp.where(adm, imp + forced, NEG).T
        sc = [score_t[8 * v:8 * v + 8] for v in range(n_rv)]
        rank = [jnp.zeros((8, tq), F32) for _ in range(n_rv)]
        for jp in range(n_blk):
            col = score_t[jp:jp + 1]
            for v in range(n_rv):
                if v > jp // 8:
                    beats = col >= sc[v]
                elif v < jp // 8:
                    beats = col > sc[v]
                else:
                    beats = (col > sc[v]) | ((col == sc[v]) & (sub > jp % 8))
                rank[v] = rank[v] + jnp.where(beats, 1.0, 0.0)
        sel_t = jnp.where(jnp.concatenate(rank, axis=0) < n_sel, 1.0, 0.0).astype(BF16)

        tile = lambda j: pl.ds(pl.multiple_of(j * tks, tks), tks)

        def sel_bias(j):
            kp = j * tks + lax.broadcasted_iota(I32, (1, tks), 1)
            hit = _tn(sel_t, e_ref[:, tile(j)])
            return jnp.where((hit > 0.5) & (kp <= t), 0.0, NEG)

        n_tile = ((i + 1) * tq + tks - 1) // tks
        c_s, = _flash([qs], lambda j, n: ks_ref[0, tile(j), :], lambda j: vs_ref[0, tile(j), :],
                      0, n_tile, sel_bias, (_flash_init(rows),))
        out = out + gate(1) * _flash_out(c_s)
        c_w = _softmax_step(qs, kw_ref[0, pl.ds(w0, ww), :], vw_ref[0, pl.ds(w0, ww), :],
                            win_bias, _flash_init(rows))
        out = out + gate(2) * _flash_out(c_w)
        outs = [out[r * tq:(r + 1) * tq] for r in range(rep)]
        for pp in range(rep // 2):
            a, bb = outs[2 * pp], outs[2 * pp + 1]
            if g == 0:
                bb = pltpu.roll(bb, B_HD, 1)
            else:
                a = pltpu.roll(a, B_HD, 1)
            c0 = (rep // 2 * g + pp) * LANES
            o_ref[0, :, c0:c0 + LANES] = jnp.where(lane < B_HD, a, bb).astype(BF16)


def _nsa_attn(u, u32, cols, qx, kcmp, vcmp, ks, kw):
    b, h, s, _ = qx.shape
    tq = min(256, s)
    tks = min(512, s)
    n_pad = kcmp.shape[1]
    n_blk = s // SEL_LEN
    n_sel = min(N_SEL, n_blk)
    n_cmp = (s - CMP_LEN) // CMP_STRIDE + 1
    ww = min(WINDOW + tq, s)
    expand = np.zeros((n_blk, s), np.float32)
    expand[np.arange(s) // SEL_LEN, np.arange(s)] = 1.0
    nn = np.arange(n_pad)[:, None]
    jj = np.arange(LANES)[None, :]
    ovl = ((nn * CMP_STRIDE <= jj * SEL_LEN + SEL_LEN - 1) & (nn * CMP_STRIDE + CMP_LEN - 1 >= jj * SEL_LEN)
           & (jj < n_blk) & (nn < n_cmp)).astype(np.float32)
    full = lambda shape: pl.BlockSpec(shape, lambda bi, i: (bi,) + (0,) * (len(shape) - 1))
    ucol = lambda name: pl.BlockSpec((1, s, LANES), lambda bi, i, c=cols[name] // LANES: (bi, 0, c))
    const = lambda shape: pl.BlockSpec(shape, lambda bi, i: (0,) * len(shape))
    return pl.pallas_call(
        functools.partial(_nsa_attn_kernel, n_blk=n_blk, n_sel=n_sel, tks=tks, ww=ww),
        out_shape=jax.ShapeDtypeStruct((b, s, h * B_HD), BF16),
        grid=(b, s // tq),
        in_specs=[pl.BlockSpec((1, h, tq, LANES), lambda bi, i: (bi, 0, i, 0)),
                  full((1, n_pad, LANES)), full((1, n_pad, LANES)),
                  full((1, s, LANES)), ucol("b_vs"), full((1, s, LANES)), ucol("b_vw"),
                  pl.BlockSpec((1, tq, LANES), lambda bi, i: (bi, i, 0)),
                  const((n_blk, s)), const((n_pad, LANES))],
        out_specs=pl.BlockSpec((1, tq, h * B_HD), lambda bi, i: (bi, i, 0)),
        compiler_params=_cp("parallel", "parallel"),
        name="nsa_attn",
    )(qx, kcmp, vcmp, ks, u, kw, u, u32, jnp.asarray(expand, BF16), jnp.asarray(ovl, BF16))


def _dsa_prep_kernel(q_in, iq_in, kv_in, sm_in, cos64_ref, sin64_ref, cos32_ref, sin32_ref, qg_ref, kg_ref,
                     q_ref, kv_ref, iq_ref, ik_ref):
    tm = q_in.shape[1]
    lane = _lane((tm, LANES))
    c64, s64, c32, s32 = cos64_ref[0], sin64_ref[0], cos32_ref[0], sin32_ref[0]
    for p in range(C_HEADS // 2):
        y = _rope(_group_rms(q_in[0, :, p * LANES:(p + 1) * LANES].astype(F32), C_HD) * qg_ref[...], c64, s64, C_HD // 2)
        y = y * (C_HD ** -0.5 * LOG2E)
        q_ref[0, 2 * p] = jnp.where(lane < C_HD, y, 0.0).astype(BF16)
        q_ref[0, 2 * p + 1] = jnp.where(lane < C_HD, pltpu.roll(y, C_HD, 1), 0.0).astype(BF16)
    kv = kv_in[0].astype(F32)
    kn = _rope(_group_rms(kv, C_HD) * kg_ref[...], c64, s64, C_HD // 2)
    kv_ref[0] = jnp.where(lane < C_HD, kn, kv).astype(BF16)
    for c0 in range(0, IDX_HEADS * IDX_HD, LANES):
        iq_ref[0, :, c0:c0 + LANES] = _rope(iq_in[0, :, c0:c0 + LANES].astype(F32), c32, s32, IDX_HD // 2).astype(BF16)
    ik = jnp.where(lane < IDX_HD, _rope(sm_in[0], c32, s32, IDX_HD // 2), 0.0)
    ik = ik + pltpu.roll(ik, IDX_HD, 1)
    ik = ik + pltpu.roll(ik, 2 * IDX_HD, 1)
    ik_ref[0] = ik.astype(BF16)


def _dsa_prep(u, u32, cols, tabs, p):
    b, s, _ = u.shape
    tm = min(256, s)
    row = pl.BlockSpec((1, tm, LANES), lambda bi, i: (bi, i, 0))
    const = pl.BlockSpec((1, LANES), lambda bi, i: (0, 0))
    ublk = lambda name, w: pl.BlockSpec((1, tm, w), lambda bi, i, c=cols[name] // w: (bi, i, c))
    dense = jax.ShapeDtypeStruct((b, s, LANES), BF16)
    return pl.pallas_call(
        _dsa_prep_kernel,
        out_shape=(jax.ShapeDtypeStruct((b, C_HEADS, s, LANES), BF16), dense,
                   jax.ShapeDtypeStruct((b, s, IDX_HEADS * IDX_HD), BF16), dense),
        grid=(b, s // tm),
        in_specs=[ublk("c_q", 512), ublk("c_iq", 256), ublk("c_kv", LANES), row,
                  row, row, row, row, const, const],
        out_specs=(pl.BlockSpec((1, C_HEADS, tm, LANES), lambda bi, i: (bi, 0, i, 0)), row,
                   pl.BlockSpec((1, tm, IDX_HEADS * IDX_HD), lambda bi, i: (bi, i, 0)), row),
        compiler_params=_cp("parallel", "parallel"),
        name="dsa_prep",
    )(u, u, u, u32, tabs["cos64"], tabs["sin64"], tabs["cos32"], tabs["sin32"], p["q_g"], p["k_g"])


def _dsa_attn_kernel(q_ref, iq_ref, sm_ref, ik_ref, kv_ref, tri_ref, o_ref, key_scr, bias_scr, *, tk, topk):
    tq = q_ref.shape[2]
    i = pl.program_id(1)
    t = i * tq + lax.broadcasted_iota(I32, (1, tq), 1)
    lane = _lane((tq, LANES))
    n_tile = ((i + 1) * tq + tk - 1) // tk
    tile = lambda j: pl.ds(pl.multiple_of(j * tk, tk), tk)
    kpos = lambda j: j * tk + lax.broadcasted_iota(I32, (tk, 1), 0)

    iw_t = (sm_ref[0] * (IDX_HEADS ** -0.5)).T
    parts = []
    for h in range(IDX_HEADS):
        blk = iq_ref[0, :, (h // 4) * LANES:(h // 4 + 1) * LANES]
        lo = (h % 4) * IDX_HD
        parts.append(jnp.where((lane >= lo) & (lane < lo + IDX_HD), blk, jnp.zeros_like(blk)))
    iqs = jnp.concatenate(parts, axis=0)

    def idx_body(j, carry):
        lg = _nt(ik_ref[0, tile(j), :], iqs)
        acc = jnp.zeros((tk, tq), F32)
        for h in range(IDX_HEADS):
            acc = acc + iw_t[IDX_HD + h:IDX_HD + h + 1, :] * jnp.maximum(lg[:, h * tq:(h + 1) * tq], 0.0)
        sc = jnp.where(kpos(j) <= t, acc, NEG)
        bits = pltpu.bitcast(sc, I32)
        key = jnp.where(bits < 0, bits ^ 0x7FFFFFFF, bits)
        key_scr[tile(j), :] = jnp.where(sc == 0.0, 0, key)
        return carry

    lax.fori_loop(0, n_tile, idx_body, 0)

    def count_ge(thr_key):
        def body(j, c):
            ge = jnp.where(key_scr[tile(j), :] >= thr_key, 1, 0)
            return c + jnp.sum(ge.reshape(tk // 8, 8, tq), axis=0)

        c = lax.fori_loop(0, n_tile, body, jnp.zeros((8, tq), I32))
        return jnp.sum(c, axis=0, keepdims=True)

    def bit_body(bi, ucand):
        utrial = ucand | jnp.left_shift(jnp.int32(1), 31 - bi)
        return jnp.where(count_ge(utrial ^ INT_MIN) >= topk, utrial, ucand)

    thr = lax.fori_loop(0, 32, bit_body, jnp.zeros((1, tq), I32)) ^ INT_MIN
    n_ge = count_ge(thr)
    n_gt = count_ge(thr + 1)
    need = topk - n_gt
    row_ok = (n_ge - n_gt == need) | (thr == NEG_KEY) | (n_ge < topk)
    simple = jnp.min(jnp.where(row_ok, 1.0, 0.0)) > 0.5

    def fast_bias():
        def body(j, carry):
            keep = (key_scr[tile(j), :] >= thr) & (kpos(j) <= t)
            bias_scr[:, tile(j)] = jnp.where(keep, 0.0, NEG).T
            return carry

        lax.fori_loop(0, n_tile, body, 0)

    def tie_bias():
        need_f = need.astype(F32)

        def body(j, run):
            key = key_scr[tile(j), :]
            kp = kpos(j)
            for c0 in range(0, tk, LANES):
                kc = key[c0:c0 + LANES]
                eq = kc == thr
                eq_f = jnp.where(eq, 1.0, 0.0)
                pref = _mm(tri_ref[...], eq_f.astype(BF16)) + run
                keep = ((kc > thr) | (eq & (pref <= need_f))) & (kp[c0:c0 + LANES] <= t)
                bias_scr[:, pl.ds(pl.multiple_of(j * tk + c0, LANES), LANES)] = jnp.where(keep, 0.0, NEG).T
                run = run + jnp.sum(eq_f, axis=0, keepdims=True)
            return run

        lax.fori_loop(0, n_tile, body, jnp.zeros((1, tq), F32))

    lax.cond(simple, fast_bias, tie_bias)

    qs = jnp.concatenate([q_ref[0, h] for h in range(C_HEADS)], axis=0)

    kv_at = lambda j, n=0: kv_ref[0, tile(j), :]
    carry, = _flash([qs], kv_at, kv_at, 0, n_tile, lambda j: bias_scr[:, tile(j)], (_flash_init(C_HEADS * tq),))
    o = _flash_out(carry)
    for p in range(C_HEADS // 2):
        a = pltpu.roll(o[(2 * p) * tq:(2 * p + 1) * tq], C_HD, 1)
        bb = o[(2 * p + 1) * tq:(2 * p + 2) * tq]
        o_ref[0, :, p * LANES:(p + 1) * LANES] = jnp.where(lane < C_HD, a, bb).astype(BF16)


def _dsa_attn(u32, qc, iq, ik, kv):
    b, h, s, _ = qc.shape
    tq = min(128, s)
    tk = min(512, s)
    topk = min(TOPK_MAX, s // 4)
    tri = np.tril(np.ones((LANES, LANES), np.float32))
    full = lambda shape: pl.BlockSpec(shape, lambda bi, i: (bi,) + (0,) * (len(shape) - 1))
    return pl.pallas_call(
        functools.partial(_dsa_attn_kernel, tk=tk, topk=topk),
        out_shape=jax.ShapeDtypeStruct((b, s, h * C_HD), BF16),
        grid=(b, s // tq),
        in_specs=[pl.BlockSpec((1, h, tq, LANES), lambda bi, i: (bi, 0, i, 0)),
                  pl.BlockSpec((1, tq, IDX_HEADS * IDX_HD), lambda bi, i: (bi, i, 0)),
                  pl.BlockSpec((1, tq, LANES), lambda bi, i: (bi, i, 0)),
                  full((1, s, LANES)), full((1, s, LANES)),
                  pl.BlockSpec((LANES, LANES), lambda bi, i: (0, 0))],
        out_specs=pl.BlockSpec((1, tq, h * C_HD), lambda bi, i: (bi, i, 0)),
        scratch_shapes=[pltpu.VMEM((s, tq), I32), pltpu.VMEM((tq, s), F32)],
        compiler_params=_cp("parallel", "parallel"),
        name="dsa_attn",
    )(qc, iq, u32, ik, kv, jnp.asarray(tri, BF16))


def _mlstm_conv_kernel(x_ref, halo_ref, w_ref, b_ref, o_ref):
    tm = x_ref.shape[1]
    i = pl.program_id(1)
    lane = _lane((tm, x_ref.shape[2]))
    halo = jnp.where(i > 0, halo_ref[0].astype(F32), 0.0)
    x = x_ref[0].astype(F32)
    xc = jnp.concatenate([halo, x], axis=0)
    off = halo.shape[0] - (CONV_W - 1)
    y = b_ref[...] + jnp.zeros_like(x)
    for j in range(CONV_W):
        y = y + w_ref[j:j + 1, :] * xc[off + j:off + j + tm, :]
    y = y * _sigmoid(y)
    o_ref[0] = jnp.where(lane >= D_HEADS * D_QK, y * (D_QK ** -0.5), y)


def _mlstm_conv(u, qk_block, w, bias):
    b, s, _ = u.shape
    c = w.shape[1]
    tm = min(512, s)
    hb = 16
    return pl.pallas_call(
        _mlstm_conv_kernel,
        out_shape=jax.ShapeDtypeStruct((b, s, c), F32),
        grid=(b, s // tm),
        in_specs=[pl.BlockSpec((1, tm, c), lambda bi, i: (bi, i, qk_block)),
                  pl.BlockSpec((1, hb, c), lambda bi, i: (bi, jnp.maximum(i * (tm // hb) - 1, 0), qk_block)),
                  pl.BlockSpec((CONV_W, c), lambda bi, i: (0, 0)),
                  pl.BlockSpec((1, c), lambda bi, i: (0, 0))],
        out_specs=pl.BlockSpec((1, tm, c), lambda bi, i: (bi, i, 0)),
        compiler_params=_cp("parallel", "parallel"),
        name="mlstm_conv",
    )(u, u, w, bias)


def _mlstm_scan_kernel(qk_ref, v_ref, op_ref, sm_ref, gt_ref, bcol_ref, brow_ref, hg_ref, o_ref,
                       c_scr, n_scr, m_scr):
    tc = qk_ref.shape[1]
    L = CHUNK
    nqk = D_HEADS * D_QK

    @pl.when(pl.program_id(1) == 0)
    def _():
        c_scr[...] = jnp.zeros_like(c_scr)
        n_scr[...] = jnp.zeros_like(n_scr)
        m_scr[...] = jnp.zeros_like(m_scr)

    r = lax.broadcasted_iota(I32, (L, L), 0)
    c = lax.broadcasted_iota(I32, (L, L), 1)
    tril = c <= r
    tril_f = jnp.where(tril, 1.0, 0.0).astype(F32)
    triu_f = jnp.where(r <= c, 1.0, 0.0).astype(F32)
    lane = _lane((L, LANES))
    row128 = lax.broadcasted_iota(I32, (LANES, LANES), 0)
    i_lane, f_lane = IDX_HD + IDX_HEADS, IDX_HD + IDX_HEADS + D_HEADS

    def one(bb, ci):
        rows = pl.ds(pl.multiple_of(ci * L, L), L)
        qk = qk_ref[bb, rows, :]
        sm = sm_ref[bb, rows, :] + bcol_ref[...]
        bcol_all = jnp.dot(tril_f, _log_sigmoid(sm), preferred_element_type=F32, precision=HI)
        gt = gt_ref[bb, ci] + brow_ref[...]
        brow_all = jnp.dot(_log_sigmoid(gt), triu_f, preferred_element_type=F32, precision=HI)
        for p in range(D_HEADS // 2):
            qpair = qk[:, p * LANES:(p + 1) * LANES]
            kpair = qk[:, nqk + p * LANES:nqk + (p + 1) * LANES]
            kpair_b = kpair.astype(BF16)
            sp = bb * (D_HEADS // 2) + p
            c_prev = c_scr[sp]
            n_prev = n_scr[sp, 0:1, :]
            c_prev_b = c_prev.astype(BF16)
            upd = []
            for e in range(2):
                h = 2 * p + e
                in_half = (lane >= D_QK * e) & (lane < D_QK * (e + 1))
                qm = jnp.where(in_half, qpair, 0.0)
                qm_b = qm.astype(BF16)
                vh = v_ref[bb, rows, h * D_VD:(h + 1) * D_VD].astype(F32)
                b_col = bcol_all[:, f_lane + h:f_lane + h + 1]
                i_col = sm[:, i_lane + h:i_lane + h + 1]
                b_row = brow_all[D_HEADS + h:D_HEADS + h + 1, :]
                i_row = gt[h:h + 1, :]
                m_prev = m_scr[bb * D_HEADS + h, 0:1, 0:1]
                dmat = jnp.where(tril, b_col - b_row + i_row, NEG)
                inter = b_col + m_prev
                m_t = jnp.maximum(inter, jnp.max(dmat, axis=-1, keepdims=True))
                a = jnp.exp(inter - m_t)
                w = _nt(qm_b, kpair_b) * jnp.exp(dmat - m_t)
                num = a * _mm(qm_b, c_prev_b) + _mm(w.astype(BF16), vh.astype(BF16))
                den = a * jnp.sum(qm * n_prev, axis=-1, keepdims=True) + jnp.sum(w, axis=-1, keepdims=True)
                hout = num / jnp.maximum(jnp.abs(den), jnp.exp(-m_t))
                b_last = b_col[L - 1:L, :]
                g_col = b_last - b_col + i_col
                m_new = jnp.maximum(b_last + m_prev, jnp.max(g_col, axis=0, keepdims=True))
                ws = jnp.exp(g_col - m_new)
                decay = jnp.exp(b_last + m_prev - m_new)
                u_mat = _tn(kpair_b, (ws * vh).astype(BF16))
                k_sum = jnp.sum(ws * kpair, axis=0, keepdims=True)
                upd.append((decay * c_prev + u_mat, decay * n_prev + k_sum))
                m_scr[bb * D_HEADS + h] = jnp.broadcast_to(m_new, m_scr.shape[1:])
                hn = _row_rms(hout) * hg_ref[...]
                y = _sigmoid(op_ref[bb, rows, h * D_VD:(h + 1) * D_VD].astype(F32)) * hn
                o_ref[bb, rows, h * D_VD:(h + 1) * D_VD] = y.astype(BF16)
            c_scr[sp] = jnp.where(row128 < D_QK, upd[0][0], upd[1][0])
            n_new = jnp.where(lane[0:1] < D_QK, upd[0][1], upd[1][1])
            n_scr[sp] = jnp.broadcast_to(n_new, n_scr.shape[1:])

    def chunk(ci, carry):
        for bb in range(qk_ref.shape[0]):
            one(bb, ci)
        return carry

    lax.fori_loop(0, tc // L, chunk, 0)


def _mlstm_scan(u, u32, cols, qk, gt, p):
    b, s, _ = u.shape
    tc = min(512, s)
    nb = next(n for n in (4, 2, 1) if b % n == 0)
    nv = D_HEADS * D_VD
    ublk = lambda name, w: pl.BlockSpec((nb, tc, w), lambda bi, i, c=cols[name] // w: (bi, i, c))
    const = lambda shape: pl.BlockSpec(shape, lambda bi, i: (0,) * len(shape))
    return pl.pallas_call(
        _mlstm_scan_kernel,
        out_shape=jax.ShapeDtypeStruct((b, s, nv), BF16),
        grid=(b // nb, s // tc),
        in_specs=[pl.BlockSpec((nb, tc, qk.shape[-1]), lambda bi, i: (bi, i, 0)),
                  ublk("d_v", nv), ublk("d_o", nv), pl.BlockSpec((nb, tc, LANES), lambda bi, i: (bi, i, 0)),
                  pl.BlockSpec((nb, tc // CHUNK, 8, CHUNK), lambda bi, i: (bi, i, 0, 0)),
                  const((1, LANES)), const((8, 1)), const((1, D_VD))],
        out_specs=pl.BlockSpec((nb, tc, nv), lambda bi, i: (bi, i, 0)),
        scratch_shapes=[pltpu.VMEM((nb * D_HEADS // 2, LANES, LANES), F32),
                        pltpu.VMEM((nb * D_HEADS // 2, 8, LANES), F32),
                        pltpu.VMEM((nb * D_HEADS, 8, LANES), F32)],
        compiler_params=_cp("parallel", "arbitrary"),
        name="mlstm_scan",
    )(qk, u, u, u32, gt, p["bias_col"], p["bias_row"], p["h_g"])


EVEN_SRC = dict(a_ql=(0, 256), a_kvl=(256, 128), a_kr=(384, 32), a_gate=(416, 512), b_q=(928, 512),
                b_kc=(1440, 128), b_vc=(1568, 128), b_ks=(1696, 128), b_vs=(1824, 128), b_kw=(1952, 128),
                b_vw=(2080, 128), b_g=(2208, 24), b_gate=(2232, 512), m_q=(2744, 256), m_gate=(3000, 256))
EVEN_DST = dict(a_ql=0, a_kvl=256, a_kr=384, b_q=512, a_gate=1024, b_gate=1536, m_q=2048, m_gate=2304,
                b_g=2560, b_kc=2688, b_vc=2816, b_ks=2944, b_vs=3072, b_kw=3200, b_vw=3328)
EVEN_COLS_PAD = 3456

ODD_SRC = dict(c_q=(0, 512), c_k=(512, 64), c_v=(576, 64), c_iq=(640, 256), c_ik=(896, 32), c_iw=(928, 8),
               c_gate=(936, 512), d_q=(1448, 256), d_k=(1704, 256), d_v=(1960, 512), d_i=(2472, 4),
               d_f=(2476, 4), d_o=(2480, 512), d_gate=(2992, 512), m_q=(3504, 256), m_gate=(3760, 256))
ODD_DST = dict(c_q=0, c_gate=512, d_gate=1024, d_v=1536, d_o=2048, d_q=2560, d_k=2816, m_q=3072, m_gate=3328,
               c_iq=3584, c_k=3840, c_v=3904, c_ik=3968, c_iw=4000, d_i=4008, d_f=4012)
ODD_COLS_PAD = 4096


def _permute_cols(w, src, dst, total):
    idx = np.zeros((total,), np.int32)
    keep = np.zeros((total,), np.float32)
    for name, (start, width) in src.items():
        idx[dst[name]:dst[name] + width] = np.arange(start, start + width)
        keep[dst[name]:dst[name] + width] = 1.0
    return (jnp.take(w, jnp.asarray(idx), axis=1) * jnp.asarray(keep)).astype(BF16)


def _tile_lanes(v, reps):
    return jnp.tile(v.astype(F32).reshape(1, -1), (1, reps))


def _rope_tables(positions, d2):
    inv = ROPE_THETA ** (-jnp.arange(d2, dtype=F32) / d2)
    ang = positions.astype(F32)[..., None] * inv
    c, s = jnp.cos(ang), jnp.sin(ang)
    reps = LANES // (2 * d2)
    return (jnp.tile(jnp.concatenate([c, c], axis=-1), (1, 1, reps)),
            jnp.tile(jnp.concatenate([-s, s], axis=-1), (1, 1, reps)))


def _block_diag2(w):
    z = jnp.zeros_like(w)
    return jnp.concatenate([jnp.concatenate([w, z], axis=-1), jnp.concatenate([z, w], axis=-1)], axis=-2)


def kernel(x, mem, positions, ln_g, mem_norm_g, mem_w_kv, mem_q_norm_g, mem_k_norm_g, w_out, even_w_in, mla_q_lat_g, mla_kv_lat_g, mla_w_uq, mla_w_ukv, mla_q_norm_g, mla_k_norm_g, nsa_q_norm_g, nsa_k_norm_g, nsa_cmp_pos, nsa_cmp_w1, nsa_cmp_w2, odd_w_in, dsa_q_norm_g, dsa_k_norm_g, mlstm_conv_w, mlstm_conv_b, mlstm_i_bias, mlstm_f_bias, mlstm_h_norm_g):
    b, s, _ = x.shape
    depth = ln_g.shape[0]
    cos64, sin64 = _rope_tables(positions, 32)
    cos32, sin32 = _rope_tables(positions, 16)
    tabs = dict(cos64=cos64, sin64=sin64, cos32=cos32, sin32=sin32)
    n_pad = s // CMP_STRIDE
    cmp_pos = jnp.pad(positions[:, CMP_LEN - 1::CMP_STRIDE], ((0, 0), (0, 0)))[:, :n_pad]
    cmp_pos = jnp.pad(cmp_pos, ((0, 0), (0, n_pad - cmp_pos.shape[1])))
    cosc, sinc = _rope_tables(cmp_pos, 32)

    mem_k, mem_v = _mem_kv(mem, mem_norm_g.reshape(depth, 1, -1), mem_w_kv.astype(BF16),
                           jnp.tile(mem_k_norm_g, (1, 2)).reshape(depth, 1, LANES))

    hq = np.arange(A_HEADS)[:, None] * (A_NOPE + A_ROPE)
    uq_idx = np.concatenate([(hq + np.arange(A_NOPE)[None, :]).ravel(),
                             (hq + A_NOPE + np.arange(A_ROPE)[None, :]).ravel()])
    hk = np.arange(A_HEADS)[:, None] * (A_NOPE + A_VD)
    ukv_idx = np.concatenate([(hk + np.arange(A_NOPE)[None, :]).ravel(),
                              (hk + A_NOPE + np.arange(A_VD)[None, :]).ravel()])

    for layer in range(depth):
        li = layer // 2
        g_ln = ln_g[layer].reshape(1, -1)
        mq_g = _tile_lanes(mem_q_norm_g[layer], 2)
        if layer % 2 == 0:
            cols = EVEN_DST
            u, u32 = _in_proj(x, g_ln, _permute_cols(even_w_in[li], EVEN_SRC, EVEN_DST, EVEN_COLS_PAD), cols["b_g"])
            pa = dict(q_lat_g=mla_q_lat_g[li].reshape(1, -1), kv_lat_g=mla_kv_lat_g[li].reshape(1, -1),
                      w_uq=jnp.take(mla_w_uq[li], jnp.asarray(uq_idx), axis=1).astype(BF16),
                      w_ukv=jnp.take(mla_w_ukv[li], jnp.asarray(ukv_idx), axis=1).astype(BF16),
                      qn_g=_tile_lanes(mla_q_norm_g[li, :A_NOPE], 2), qr_g=_tile_lanes(mla_q_norm_g[li, A_NOPE:], 4),
                      kn_g=_tile_lanes(mla_k_norm_g[li, :A_NOPE], 2), kr_g=_tile_lanes(mla_k_norm_g[li, A_NOPE:], 4))
            qa, ka, va = _mla_prep(u, cos32, sin32, pa)
            y1 = _mla_attn(qa, ka, va)
            pb = dict(q_g=_tile_lanes(nsa_q_norm_g[li], 2), ks_g=_tile_lanes(nsa_k_norm_g[li, 1], 2),
                      kw_g=_tile_lanes(nsa_k_norm_g[li, 2], 2), kc_g=_tile_lanes(nsa_k_norm_g[li, 0], 2),
                      pe_k=jnp.tile(nsa_cmp_pos[li, 0], (1, 2)), pe_v=jnp.tile(nsa_cmp_pos[li, 1], (1, 2)),
                      w1k=_block_diag2(nsa_cmp_w1[li, 0].reshape(CMP_LEN, B_HD, B_HD)).astype(BF16),
                      w1v=_block_diag2(nsa_cmp_w1[li, 1].reshape(CMP_LEN, B_HD, B_HD)).astype(BF16),
                      w2k=_block_diag2(nsa_cmp_w2[li, 0]).astype(BF16),
                      w2v=_block_diag2(nsa_cmp_w2[li, 1]).astype(BF16))
            qb, ks, kw = _nsa_prep(u, cols, cos64, sin64, pb)
            kcmp, vcmp = _nsa_cmp(u, cols, cosc, sinc, pb)
            y2 = _nsa_attn(u, u32, cols, qb, kcmp, vcmp, ks, kw)
            gate_blocks = (cols["a_gate"] // 512, cols["b_gate"] // 512, cols["m_gate"] // 256)
        else:
            cols = dict(ODD_DST, c_kv=ODD_DST["c_k"], small=ODD_DST["c_ik"])
            u, u32 = _in_proj(x, g_ln, _permute_cols(odd_w_in[li], ODD_SRC, ODD_DST, ODD_COLS_PAD), cols["small"])
            pc = dict(q_g=_tile_lanes(dsa_q_norm_g[li], 2), k_g=_tile_lanes(dsa_k_norm_g[li], 2))
            qc, kvc, iq, ik = _dsa_prep(u, u32, cols, tabs, pc)
            y1 = _dsa_attn(u32, qc, iq, ik, kvc)
            qk = _mlstm_conv(u, cols["d_q"] // 512, mlstm_conv_w[li], mlstm_conv_b[li].reshape(1, -1))
            gates = u32[:, :, cols["d_i"] - cols["small"]:cols["d_i"] - cols["small"] + 2 * D_HEADS]
            gt = gates.reshape(b, s // CHUNK, CHUNK, 2 * D_HEADS).transpose(0, 1, 3, 2)
            bias8 = jnp.concatenate([mlstm_i_bias[li], mlstm_f_bias[li]]).astype(F32)
            bias_col = jnp.zeros((1, LANES), F32).at[0, cols["d_i"] - cols["small"]:cols["d_i"] - cols["small"] + 8].set(bias8)
            pd = dict(bias_col=bias_col, bias_row=bias8.reshape(8, 1), h_g=mlstm_h_norm_g[li].reshape(1, -1))
            y2 = _mlstm_scan(u, u32, cols, qk, gt, pd)
            gate_blocks = (cols["c_gate"] // 512, cols["d_gate"] // 512, cols["m_gate"] // 256)
        ym = _mem_attn(u, cols["m_q"] // 256, mem_k, mem_v, layer, mq_g)
        x = _out_proj(x, y1, y2, ym, u, gate_blocks, w_out[layer].astype(BF16))
    return x
```

```python
import functools

import numpy as np
import jax
import jax.numpy as jnp
from jax import lax
from jax.experimental import pallas as pl
from jax.experimental.pallas import tpu as pltpu

F32, BF16, I32 = jnp.float32, jnp.bfloat16, jnp.int32
HI = lax.Precision.HIGHEST
NEG = -1e30
EPS = 1e-6
ROPE_THETA = 10000.0
LANES = 128
VMEM_LIMIT_BYTES = 48 * 1024 * 1024

D_MODEL = 1024
DEPTH = 4
A_HEADS, A_NOPE, A_ROPE, A_VD, A_QLAT, A_KVLAT = 8, 64, 32, 64, 256, 128
B_HEADS, B_KV_HEADS, B_HD = 8, 2, 64
CMP_LEN, CMP_STRIDE, SEL_LEN, N_SEL, WINDOW = 32, 16, 64, 16, 512
C_HEADS, C_HD, IDX_HEADS, IDX_HD, TOPK_MAX = 8, 64, 8, 32, 256
D_HEADS, D_QK, D_VD, CONV_W, CHUNK = 4, 64, 128, 4, 64
M_HEADS, M_HD = 4, 64

INT_MIN = np.int32(-2 ** 31)
NEG_KEY = int(np.float32(NEG).view(np.int32) ^ np.int32(0x7FFFFFFF))
LOG2E = float(np.log2(np.e))


def _cp(*sem):
    return pltpu.CompilerParams(dimension_semantics=sem, vmem_limit_bytes=VMEM_LIMIT_BYTES)


def _nt(a, b):
    return lax.dot_general(a, b, (((1,), (1,)), ((), ())), preferred_element_type=F32)


def _tn(a, b):
    return lax.dot_general(a, b, (((0,), (0,)), ((), ())), preferred_element_type=F32)


def _mm(a, b):
    return jnp.dot(a, b, preferred_element_type=F32)


def _sigmoid(x):
    return 1.0 / (1.0 + jnp.exp(-x))


def _log_sigmoid(x):
    return jnp.minimum(x, 0.0) - jnp.log1p(jnp.exp(-jnp.abs(x)))


def _lane(shape):
    return lax.broadcasted_iota(I32, shape, len(shape) - 1)


def _group_mat(gs):
    r = lax.broadcasted_iota(I32, (LANES, LANES), 0)
    c = lax.broadcasted_iota(I32, (LANES, LANES), 1)
    sh = gs.bit_length() - 1
    return jnp.where((r >> sh) == (c >> sh), 1.0, 0.0).astype(BF16)


def _mm_split(x, w01):
    hi = x.astype(BF16)
    lo = (x - hi.astype(F32)).astype(BF16)
    return _mm(hi, w01) + _mm(lo, w01)


def _group_rms(x, gs):
    ss = _mm_split(x * x, _group_mat(gs))
    return x * lax.rsqrt(ss * (1.0 / gs) + EPS)


def _rope(x, cosp, sinp, half):
    lane = _lane(x.shape)
    rot = jnp.where((lane & (2 * half - 1)) < half,
                    pltpu.roll(x, LANES - half, 1), pltpu.roll(x, half, 1))
    return x * cosp + rot * sinp


def _row_rms(x):
    return x * lax.rsqrt(jnp.mean(x * x, axis=-1, keepdims=True) + EPS)


def _in_proj_kernel(x_ref, g_ref, w_ref, o_ref, o32_ref, *, c32):
    h = (_row_rms(x_ref[0]) * g_ref[...]).astype(BF16)
    ncol = o_ref.shape[-1]
    for c0 in range(0, ncol, 512):
        c1 = min(ncol, c0 + 512)
        o_ref[0, :, c0:c1] = _mm(h, w_ref[:, c0:c1]).astype(BF16)
    o32_ref[0] = _mm(h, w_ref[:, c32:c32 + LANES])


def _in_proj(x, g, w, c32):
    b, s, d = x.shape
    c = w.shape[1]
    tm = min(256, s)
    return pl.pallas_call(
        functools.partial(_in_proj_kernel, c32=c32),
        out_shape=(jax.ShapeDtypeStruct((b, s, c), BF16), jax.ShapeDtypeStruct((b, s, LANES), F32)),
        grid=(b, s // tm),
        in_specs=[pl.BlockSpec((1, tm, d), lambda bi, i: (bi, i, 0)),
                  pl.BlockSpec((1, d), lambda bi, i: (0, 0)),
                  pl.BlockSpec((d, c), lambda bi, i: (0, 0))],
        out_specs=(pl.BlockSpec((1, tm, c), lambda bi, i: (bi, i, 0)),
                   pl.BlockSpec((1, tm, LANES), lambda bi, i: (bi, i, 0))),
        compiler_params=_cp("parallel", "parallel"),
        name="in_proj",
    )(x, g, w)


def _out_proj_kernel(x_ref, y1_ref, y2_ref, ym_ref, g1_ref, g2_ref, gm_ref, w_ref, o_ref):
    def gated(y_ref, g_ref):
        g = g_ref[0].astype(F32)
        return (y_ref[0].astype(F32) * (g * _sigmoid(g))).astype(BF16)

    n1 = y1_ref.shape[-1]
    n2 = y2_ref.shape[-1]
    acc = x_ref[0]
    acc = acc + _mm(gated(y1_ref, g1_ref), w_ref[0:n1, :])
    acc = acc + _mm(gated(y2_ref, g2_ref), w_ref[n1:n1 + n2, :])
    acc = acc + _mm(gated(ym_ref, gm_ref), w_ref[n1 + n2:, :])
    o_ref[0] = acc


def _out_proj(x, y1, y2, ym, u, gate_blocks, w):
    b, s, d = x.shape
    tm = min(512, s)
    i1, i2, im = gate_blocks
    n1, n2, nm = y1.shape[-1], y2.shape[-1], ym.shape[-1]
    row = lambda bi, i: (bi, i, 0)
    return pl.pallas_call(
        _out_proj_kernel,
        out_shape=jax.ShapeDtypeStruct((b, s, d), F32),
        grid=(b, s // tm),
        in_specs=[pl.BlockSpec((1, tm, d), row),
                  pl.BlockSpec((1, tm, n1), row),
                  pl.BlockSpec((1, tm, n2), row),
                  pl.BlockSpec((1, tm, nm), row),
                  pl.BlockSpec((1, tm, n1), lambda bi, i: (bi, i, i1)),
                  pl.BlockSpec((1, tm, n2), lambda bi, i: (bi, i, i2)),
                  pl.BlockSpec((1, tm, nm), lambda bi, i: (bi, i, im)),
                  pl.BlockSpec(w.shape, lambda bi, i: (0, 0))],
        out_specs=pl.BlockSpec((1, tm, d), row),
        compiler_params=_cp("parallel", "parallel"),
        name="out_proj",
    )(x, y1, y2, ym, u, u, u, w)


def _mem_kv_kernel(mem_ref, g_ref, w_ref, kg_ref, k_ref, v_ref):
    h = (_row_rms(mem_ref[0]) * g_ref[0]).astype(BF16)
    kv = _mm(h, w_ref[0])
    nk = k_ref.shape[-1]
    for c0 in range(0, nk, LANES):
        k_ref[0, 0, :, c0:c0 + LANES] = (_group_rms(kv[:, c0:c0 + LANES], M_HD) * kg_ref[0]).astype(BF16)
    v_ref[0, 0] = kv[:, nk:].astype(BF16)


def _mem_kv(mem, g, w, kg):
    b, m, d = mem.shape
    depth = w.shape[0]
    nk = M_HEADS * M_HD
    out = jax.ShapeDtypeStruct((depth, b, m, nk), BF16)
    return pl.pallas_call(
        _mem_kv_kernel,
        out_shape=(out, out),
        grid=(depth, b),
        in_specs=[pl.BlockSpec((1, m, d), lambda l, bi: (bi, 0, 0)),
                  pl.BlockSpec((1, 1, d), lambda l, bi: (l, 0, 0)),
                  pl.BlockSpec((1, d, 2 * nk), lambda l, bi: (l, 0, 0)),
                  pl.BlockSpec((1, 1, LANES), lambda l, bi: (l, 0, 0))],
        out_specs=(pl.BlockSpec((1, 1, m, nk), lambda l, bi: (l, bi, 0, 0)),
                   pl.BlockSpec((1, 1, m, nk), lambda l, bi: (l, bi, 0, 0))),
        compiler_params=_cp("parallel", "parallel"),
        name="mem_kv",
    )(mem, g, w, kg)


def _mem_attn_kernel(q_ref, k_ref, v_ref, qg_ref, o_ref):
    tq = q_ref.shape[1]
    lane = _lane((tq, LANES))
    for p in range(M_HEADS // 2):
        sl = slice(p * LANES, (p + 1) * LANES)
        q = (_group_rms(q_ref[0, :, sl].astype(F32), M_HD) * qg_ref[...] * (M_HD ** -0.5)).astype(BF16)
        k = k_ref[0, 0, :, sl]
        v = v_ref[0, 0, :, sl]
        halves = []
        for e in range(2):
            in_half = (lane >= 64 * e) & (lane < 64 * e + 64)
            s = _nt(jnp.where(in_half, q, jnp.zeros_like(q)), k)
            ex = jnp.exp(s - jnp.max(s, axis=-1, keepdims=True))
            o = _mm(ex.astype(BF16), v) / jnp.sum(ex, axis=-1, keepdims=True)
            halves.append(o)
        o_ref[0, :, sl] = jnp.where(lane < 64, halves[0], halves[1]).astype(BF16)


def _mem_attn(u, q_block, k, v, layer, qg):
    b, s, _ = u.shape
    m, nk = k.shape[2], k.shape[3]
    tq = min(512, s)
    return pl.pallas_call(
        _mem_attn_kernel,
        out_shape=jax.ShapeDtypeStruct((b, s, nk), BF16),
        grid=(b, s // tq),
        in_specs=[pl.BlockSpec((1, tq, nk), lambda bi, i: (bi, i, q_block)),
                  pl.BlockSpec((1, 1, m, nk), lambda bi, i: (layer, bi, 0, 0)),
                  pl.BlockSpec((1, 1, m, nk), lambda bi, i: (layer, bi, 0, 0)),
                  pl.BlockSpec((1, LANES), lambda bi, i: (0, 0))],
        out_specs=pl.BlockSpec((1, tq, nk), lambda bi, i: (bi, i, 0)),
        compiler_params=_cp("parallel", "parallel"),
        name="mem_attn",
    )(u, k, v, qg)


def _flash(qs, k_at, v_at, j0, j1, bias_at, carry):
    def body(j, c):
        bias = None if bias_at is None else bias_at(j)
        return tuple(_softmax_step(q, k_at(j, n), v_at(j), bias, cn) for n, (q, cn) in enumerate(zip(qs, c)))

    return lax.fori_loop(j0, j1, body, carry)


def _flash_init(rows):
    return (jnp.full((rows, 1), NEG, F32), jnp.zeros((rows, 1), F32), jnp.zeros((rows, LANES), F32))


def _softmax_update(s, v, bias, carry):
    m, l, acc = carry
    if bias is not None:
        rep = s.shape[0] // bias.shape[0]
        s = (s.reshape(rep, bias.shape[0], s.shape[1]) + bias[None]).reshape(s.shape)
    m_new = jnp.maximum(m, jnp.max(s, axis=-1, keepdims=True))
    alpha = jnp.exp2(m - m_new)
    p = jnp.exp2(s - m_new)
    l = alpha * l + jnp.sum(p, axis=-1, keepdims=True)
    acc = alpha * acc + _mm(p.astype(BF16), v)
    return m_new, l, acc


def _softmax_step(q, k, v, bias, carry):
    return _softmax_update(_nt(q, k), v, bias, carry)


def _flash_out(carry):
    _, l, acc = carry
    return acc / l


def _mla_prep_kernel(u_ref, cos_ref, sin_ref, qlg_ref, kvlg_ref, wuq_ref, wukv_ref,
                     qng_ref, qrg_ref, kng_ref, krg_ref, q_ref, k_ref, v_ref):
    tm = u_ref.shape[1]
    lane = _lane((tm, LANES))
    cosp, sinp = cos_ref[0], sin_ref[0]
    half = A_ROPE // 2
    scale = (A_NOPE + A_ROPE) ** -0.5 * LOG2E
    ql = (_row_rms(u_ref[0, :, 0:A_QLAT].astype(F32)) * qlg_ref[...]).astype(BF16)
    kvl = (_row_rms(u_ref[0, :, A_QLAT:A_QLAT + A_KVLAT].astype(F32)) * kvlg_ref[...]).astype(BF16)
    q = _mm(ql, wuq_ref[...])
    kv = _mm(kvl, wukv_ref[...])
    n_nope = A_HEADS * A_NOPE
    v_ref[0] = kv[:, n_nope:].astype(BF16)
    kr = u_ref[0, :, A_QLAT + A_KVLAT:A_QLAT + A_KVLAT + LANES].astype(F32)
    kpe = _rope(_group_rms(kr, A_ROPE) * krg_ref[...], cosp, sinp, half)
    kpe = pltpu.roll(kpe, A_NOPE, 1)
    qn = [_group_rms(q[:, c:c + LANES], A_NOPE) * qng_ref[...] for c in range(0, n_nope, LANES)]
    kn = [_group_rms(kv[:, c:c + LANES], A_NOPE) * kng_ref[...] for c in range(0, n_nope, LANES)]
    qr = [_rope(_group_rms(q[:, n_nope + c:n_nope + c + LANES], A_ROPE) * qrg_ref[...], cosp, sinp, half)
          for c in range(0, A_HEADS * A_ROPE, LANES)]
    for h in range(A_HEADS):
        qn_h = qn[h // 2] if h % 2 == 0 else pltpu.roll(qn[h // 2], A_NOPE, 1)
        kn_h = kn[h // 2] if h % 2 == 0 else pltpu.roll(kn[h // 2], A_NOPE, 1)
        shift = (A_NOPE - (h % 4) * A_ROPE) % LANES
        qr_h = qr[h // 4] if shift == 0 else pltpu.roll(qr[h // 4], shift, 1)
        qf = jnp.where(lane < A_NOPE, qn_h, jnp.where(lane < A_NOPE + A_ROPE, qr_h, 0.0))
        kf = jnp.where(lane < A_NOPE, kn_h, jnp.where(lane < A_NOPE + A_ROPE, kpe, 0.0))
        q_ref[0, h] = (qf * scale).astype(BF16)
        k_ref[0, h] = kf.astype(BF16)


def _mla_prep(u, cos32, sin32, p):
    b, s, _ = u.shape
    tm = min(256, s)
    hd = jax.ShapeDtypeStruct((b, A_HEADS, s, LANES), BF16)
    const = lambda shape: pl.BlockSpec(shape, lambda bi, i: (0,) * len(shape))
    return pl.pallas_call(
        _mla_prep_kernel,
        out_shape=(hd, hd, jax.ShapeDtypeStruct((b, s, A_HEADS * A_VD), BF16)),
        grid=(b, s // tm),
        in_specs=[pl.BlockSpec((1, tm, 512), lambda bi, i: (bi, i, 0)),
                  pl.BlockSpec((1, tm, LANES), lambda bi, i: (bi, i, 0)),
                  pl.BlockSpec((1, tm, LANES), lambda bi, i: (bi, i, 0)),
                  const((1, A_QLAT)), const((1, A_KVLAT)),
                  const(p["w_uq"].shape), const(p["w_ukv"].shape),
                  const((1, LANES)), const((1, LANES)), const((1, LANES)), const((1, LANES))],
        out_specs=(pl.BlockSpec((1, A_HEADS, tm, LANES), lambda bi, i: (bi, 0, i, 0)),
                   pl.BlockSpec((1, A_HEADS, tm, LANES), lambda bi, i: (bi, 0, i, 0)),
                   pl.BlockSpec((1, tm, A_HEADS * A_VD), lambda bi, i: (bi, i, 0))),
        compiler_params=_cp("parallel", "parallel"),
        name="mla_prep",
    )(u, cos32, sin32, p["q_lat_g"], p["kv_lat_g"], p["w_uq"], p["w_ukv"],
      p["qn_g"], p["qr_g"], p["kn_g"], p["kr_g"])


def _flash_init_t(cols):
    return (jnp.full((1, cols), NEG, F32), jnp.zeros((1, cols), F32), jnp.zeros((LANES, cols), F32))


def _softmax_step_t(q, k, v, bias_t, carry):
    m, l, acc = carry
    s = _nt(k, q)
    if bias_t is not None:
        s = s + bias_t
    m_new = jnp.maximum(m, jnp.max(s, axis=0, keepdims=True))
    alpha = jnp.exp2(m - m_new)
    p = jnp.exp2(s - m_new)
    l = alpha * l + jnp.sum(p, axis=0, keepdims=True)
    acc = alpha * acc + _tn(v, p.astype(BF16))
    return m_new, l, acc


def _mla_attn_kernel(q_ref, k_ref, v_ref, o_ref, *, tk):
    tq = q_ref.shape[2]
    per_q = tq // tk
    i = pl.program_id(2)
    lane = _lane((tq, LANES))
    t = i * tq + lax.broadcasted_iota(I32, (1, tq), 1)
    nh = q_ref.shape[1]
    qs = [q_ref[0, e] for e in range(nh)]
    tile = lambda j: pl.ds(pl.multiple_of(j * tk, tk), tk)
    v_at = lambda j, e: v_ref[0, tile(j), (e // 2) * LANES:(e // 2 + 1) * LANES]

    def body(j, c):
        return tuple(_softmax_step_t(qs[e], k_ref[0, e, tile(j), :], v_at(j, e), None, c[e]) for e in range(nh))

    carry = lax.fori_loop(0, i * per_q, body, tuple(_flash_init_t(tq) for _ in range(nh)))
    for d in range(per_q):
        j = i * per_q + d
        bias_t = jnp.where(j * tk + lax.broadcasted_iota(I32, (tk, 1), 0) <= t[:, d * tk:], 0.0, NEG)
        new = []
        for e in range(nh):
            m, l, acc = carry[e]
            part = _softmax_step_t(qs[e][d * tk:], k_ref[0, e, tile(j), :], v_at(j, e), bias_t,
                                   (m[:, d * tk:], l[:, d * tk:], acc[:, d * tk:]))
            if d:
                part = tuple(jnp.concatenate([old[:, :d * tk], upd], axis=1) for old, upd in zip(carry[e], part))
            new.append(part)
        carry = tuple(new)
    res = [(acc / l).T for (_, l, acc) in carry]
    for p in range(nh // 2):
        o_ref[0, :, p * LANES:(p + 1) * LANES] = jnp.where(lane < A_VD, res[2 * p], res[2 * p + 1]).astype(BF16)


def _mla_attn(q, k, v):
    b, h, s, _ = q.shape
    tq = min(1024, s)
    tk = min(512, s)
    nh = 2
    return pl.pallas_call(
        functools.partial(_mla_attn_kernel, tk=tk),
        out_shape=jax.ShapeDtypeStruct((b, s, h * A_VD), BF16),
        grid=(b, h // nh, s // tq),
        in_specs=[pl.BlockSpec((1, nh, tq, LANES), lambda bi, p, i: (bi, p, i, 0)),
                  pl.BlockSpec((1, nh, s, LANES), lambda bi, p, i: (bi, p, 0, 0)),
                  pl.BlockSpec((1, s, nh * A_VD), lambda bi, p, i: (bi, 0, p))],
        out_specs=pl.BlockSpec((1, tq, nh * A_VD), lambda bi, p, i: (bi, i, p)),
        compiler_params=_cp("parallel", "parallel", "parallel"),
        name="mla_attn",
    )(q, k, v)


def _nsa_prep_kernel(q_in, ks_in, kw_in, cos_ref, sin_ref, qg_ref, ksg_ref, kwg_ref, q_ref, ks_ref, kw_ref):
    tm = q_in.shape[1]
    lane = _lane((tm, LANES))
    cosp, sinp = cos_ref[0], sin_ref[0]
    half = B_HD // 2
    rep = B_HEADS // B_KV_HEADS
    for p in range(B_HEADS // 2):
        y = _rope(_group_rms(q_in[0, :, p * LANES:(p + 1) * LANES].astype(F32), B_HD) * qg_ref[...], cosp, sinp, half)
        y = y * (B_HD ** -0.5 * LOG2E)
        y_sw = pltpu.roll(y, B_HD, 1)
        for e in range(2):
            h = 2 * p + e
            g = h // rep
            src = y if e == g else y_sw
            in_grp = (lane >= B_HD * g) & (lane < B_HD * (g + 1))
            q_ref[0, h] = jnp.where(in_grp, src, 0.0).astype(BF16)
    ks_ref[0] = _rope(_group_rms(ks_in[0].astype(F32), B_HD) * ksg_ref[...], cosp, sinp, half).astype(BF16)
    kw_ref[0] = _rope(_group_rms(kw_in[0].astype(F32), B_HD) * kwg_ref[...], cosp, sinp, half).astype(BF16)


def _nsa_prep(u, cols, cos64, sin64, p):
    b, s, _ = u.shape
    tm = min(256, s)
    blk = lambda name: pl.BlockSpec((1, tm, LANES), lambda bi, i, c=cols[name] // LANES: (bi, i, c))
    row = pl.BlockSpec((1, tm, LANES), lambda bi, i: (bi, i, 0))
    const = pl.BlockSpec((1, LANES), lambda bi, i: (0, 0))
    kvs = jax.ShapeDtypeStruct((b, s, LANES), BF16)
    return pl.pallas_call(
        _nsa_prep_kernel,
        out_shape=(jax.ShapeDtypeStruct((b, B_HEADS, s, LANES), BF16), kvs, kvs),
        grid=(b, s // tm),
        in_specs=[pl.BlockSpec((1, tm, 512), lambda bi, i, c=cols["b_q"] // 512: (bi, i, c)),
                  blk("b_ks"), blk("b_kw"), row, row, const, const, const],
        out_specs=(pl.BlockSpec((1, B_HEADS, tm, LANES), lambda bi, i: (bi, 0, i, 0)), row, row),
        compiler_params=_cp("parallel", "parallel"),
        name="nsa_prep",
    )(u, u, u, cos64, sin64, p["q_g"], p["ks_g"], p["kw_g"])


def _nsa_cmp_kernel(kc_in, vc_in, pek_ref, pev_ref, w1k_ref, w1v_ref, w2k_ref, w2v_ref, kg_ref,
                    cos_ref, sin_ref, ko_ref, vo_ref, pad_ref):
    s = kc_in.shape[1]
    n_pad = ko_ref.shape[1]

    def compress(x_in, pe_ref, w1_ref, w2_ref):
        pad_ref[0:s, :] = x_in[0].astype(F32)
        pad_ref[s:s + CMP_STRIDE, :] = jnp.zeros((CMP_STRIDE, LANES), F32)
        acc = jnp.zeros((n_pad, LANES), F32)
        for l in range(CMP_LEN):
            xl = pad_ref[pl.ds(l, n_pad, stride=CMP_STRIDE), :] + pe_ref[l:l + 1, :]
            acc = acc + _mm(xl.astype(BF16), w1_ref[l])
        mid = acc * _sigmoid(acc)
        return _mm(mid.astype(BF16), w2_ref[...])

    kc = compress(kc_in, pek_ref, w1k_ref, w2k_ref)
    ko_ref[0] = _rope(_group_rms(kc, B_HD) * kg_ref[...], cos_ref[0], sin_ref[0], B_HD // 2).astype(BF16)
    vo_ref[0] = compress(vc_in, pev_ref, w1v_ref, w2v_ref).astype(BF16)


def _nsa_cmp(u, cols, cosc, sinc, p):
    b, s, _ = u.shape
    n_pad = s // CMP_STRIDE
    blk = lambda name: pl.BlockSpec((1, s, LANES), lambda bi, c=cols[name] // LANES: (bi, 0, c))
    const = lambda shape: pl.BlockSpec(shape, lambda bi: (0,) * len(shape))
    out = jax.ShapeDtypeStruct((b, n_pad, LANES), BF16)
    ospec = pl.BlockSpec((1, n_pad, LANES), lambda bi: (bi, 0, 0))
    return pl.pallas_call(
        _nsa_cmp_kernel,
        out_shape=(out, out),
        grid=(b,),
        in_specs=[blk("b_kc"), blk("b_vc"), const((CMP_LEN, LANES)), const((CMP_LEN, LANES)),
                  const((CMP_LEN, LANES, LANES)), const((CMP_LEN, LANES, LANES)),
                  const((LANES, LANES)), const((LANES, LANES)), const((1, LANES)), ospec, ospec],
        out_specs=(ospec, ospec),
        scratch_shapes=[pltpu.VMEM((s + CMP_STRIDE, LANES), F32)],
        compiler_params=_cp("parallel"),
        name="nsa_cmp",
    )(u, u, p["pe_k"], p["pe_v"], p["w1k"], p["w1v"], p["w2k"], p["w2v"], p["kc_g"], cosc, sinc)


def _nsa_attn_kernel(q_ref, kc_ref, vc_ref, ks_ref, vs_ref, kw_ref, vw_ref, g_ref, e_ref, ovl_ref, o_ref,
                     *, n_blk, n_sel, tks, ww):
    tq = q_ref.shape[2]
    n_pad = kc_ref.shape[1]
    i = pl.program_id(1)
    t = i * tq + lax.broadcasted_iota(I32, (1, tq), 1)
    lane = _lane((tq, LANES))
    gates_t = _sigmoid(g_ref[0]).T
    rep = B_HEADS // B_KV_HEADS
    heads = lambda x: jnp.concatenate([x] * rep, axis=1)
    ncmp = lax.broadcasted_iota(I32, (n_pad, 1), 0)
    valid_c = heads((ncmp * CMP_STRIDE + (CMP_LEN - 1)) <= t)
    kc = kc_ref[0]
    vc = vc_ref[0]
    blk = lax.broadcasted_iota(I32, (LANES, 1), 0)
    cur = t >> (SEL_LEN.bit_length() - 1)
    forced = jnp.where(blk == cur, 3e4, jnp.where(blk == cur - 1, 2e4, jnp.where(blk == 0, 1e4, 0.0)))
    adm = (blk * SEL_LEN <= t) & (blk < n_blk)
    n_rv = n_blk // 8
    sub = lax.broadcasted_iota(I32, (8, tq), 0)
    w0 = pl.multiple_of(jnp.maximum(i * tq + tq - ww, 0), tq)
    kp_w = w0 + lax.broadcasted_iota(I32, (ww, 1), 0)
    win_bias = heads(jnp.where((kp_w <= t) & (kp_w > t - WINDOW), 0.0, NEG))
    for g in range(B_KV_HEADS):
        qs = jnp.concatenate([q_ref[0, rep * g + r] for r in range(rep)], axis=0)
        gate = lambda c: jnp.concatenate(
            [gates_t[(rep * g + r) * 3 + c:(rep * g + r) * 3 + c + 1, :] for r in range(rep)], axis=1)
        s = jnp.where(valid_c, _nt(kc, qs), NEG)
        ex = jnp.exp2(s - jnp.max(s, axis=0, keepdims=True))
        pc = jnp.where(valid_c, ex / jnp.sum(ex, axis=0, keepdims=True), 0.0)
        out = gate(0) * _tn(vc, pc.astype(BF16))
        psum = pc[:, 0:tq]
        for r in range(1, rep):
            psum = psum + pc[:, r * tq:(r + 1) * tq]
        p_hi = psum.astype(BF16)
        p_lo = (psum - p_hi.astype(F32)).astype(BF16)
        imp = _mm(ovl_ref[...], p_hi) + _mm(ovl_ref[...], p_lo)
        score_t = jnp.where(adm, imp + forced, NEG)
        sc = [score_t[8 * v:8 * v + 8] for v in range(n_rv)]
        rank = [jnp.zeros((8, tq), F32) for _ in range(n_rv)]
        for jp in range(n_blk):
            col = score_t[jp:jp + 1]
            for v in range(n_rv):
                if v > jp // 8:
                    beats = col >= sc[v]
                elif v < jp // 8:
                    beats = col > sc[v]
                else:
                    beats = (col > sc[v]) | ((col == sc[v]) & (sub > jp % 8))
                rank[v] = rank[v] + jnp.where(beats, 1.0, 0.0)
        sel_t = jnp.where(jnp.concatenate(rank, axis=0) < n_sel, 1.0, 0.0).astype(BF16)

        tile = lambda j: pl.ds(pl.multiple_of(j * tks, tks), tks)

        def sel_body(j, carry):
            kp = j * tks + lax.broadcasted_iota(I32, (tks, 1), 0)
            hit = _mm(e_ref[tile(j), :], sel_t)
            bias = jnp.where((hit > 0.5) & (kp <= t), 0.0, NEG)
            return _softmax_step_t(qs, ks_ref[0, tile(j), :], vs_ref[0, tile(j), :], heads(bias), carry)

        n_tile = ((i + 1) * tq + tks - 1) // tks
        _, l_s, acc_s = lax.fori_loop(0, n_tile, sel_body, _flash_init_t(rep * tq))
        out = out + gate(1) * (acc_s / l_s)
        _, l_w, acc_w = _softmax_step_t(qs, kw_ref[0, pl.ds(w0, ww), :], vw_ref[0, pl.ds(w0, ww), :],
                                        win_bias, _flash_init_t(rep * tq))
        out = out + gate(2) * (acc_w / l_w)
        outs = [out[:, r * tq:(r + 1) * tq].T for r in range(rep)]
        for pp in range(rep // 2):
            a, bb = outs[2 * pp], outs[2 * pp + 1]
            if g == 0:
                bb = pltpu.roll(bb, B_HD, 1)
            else:
                a = pltpu.roll(a, B_HD, 1)
            c0 = (rep // 2 * g + pp) * LANES
            o_ref[0, :, c0:c0 + LANES] = jnp.where(lane < B_HD, a, bb).astype(BF16)


def _nsa_attn(u, u32, cols, qx, kcmp, vcmp, ks, kw):
    b, h, s, _ = qx.shape
    tq = min(256, s)
    tks = min(512, s)
    n_pad = kcmp.shape[1]
    n_blk = s // SEL_LEN
    n_sel = min(N_SEL, n_blk)
    n_cmp = (s - CMP_LEN) // CMP_STRIDE + 1
    ww = min(WINDOW + tq, s)
    expand = np.zeros((s, n_blk), np.float32)
    expand[np.arange(s), np.arange(s) // SEL_LEN] = 1.0
    nn = np.arange(n_pad)[None, :]
    jj = np.arange(LANES)[:, None]
    ovl = ((nn * CMP_STRIDE <= jj * SEL_LEN + SEL_LEN - 1) & (nn * CMP_STRIDE + CMP_LEN - 1 >= jj * SEL_LEN)
           & (jj < n_blk) & (nn < n_cmp)).astype(np.float32)
    full = lambda shape: pl.BlockSpec(shape, lambda bi, i: (bi,) + (0,) * (len(shape) - 1))
    ucol = lambda name: pl.BlockSpec((1, s, LANES), lambda bi, i, c=cols[name] // LANES: (bi, 0, c))
    const = lambda shape: pl.BlockSpec(shape, lambda bi, i: (0,) * len(shape))
    return pl.pallas_call(
        functools.partial(_nsa_attn_kernel, n_blk=n_blk, n_sel=n_sel, tks=tks, ww=ww),
        out_shape=jax.ShapeDtypeStruct((b, s, h * B_HD), BF16),
        grid=(b, s // tq),
        in_specs=[pl.BlockSpec((1, h, tq, LANES), lambda bi, i: (bi, 0, i, 0)),
                  full((1, n_pad, LANES)), full((1, n_pad, LANES)),
                  full((1, s, LANES)), ucol("b_vs"), full((1, s, LANES)), ucol("b_vw"),
                  pl.BlockSpec((1, tq, LANES), lambda bi, i: (bi, i, 0)),
                  const((s, n_blk)), const((LANES, n_pad))],
        out_specs=pl.BlockSpec((1, tq, h * B_HD), lambda bi, i: (bi, i, 0)),
        compiler_params=_cp("parallel", "parallel"),
        name="nsa_attn",
    )(qx, kcmp, vcmp, ks, u, kw, u, u32, jnp.asarray(expand, BF16), jnp.asarray(ovl, BF16))


def _dsa_prep_kernel(q_in, iq_in, kv_in, sm_in, cos64_ref, sin64_ref, cos32_ref, sin32_ref, qg_ref, kg_ref,
                     q_ref, kv_ref, iq_ref, ik_ref):
    tm = q_in.shape[1]
    lane = _lane((tm, LANES))
    c64, s64, c32, s32 = cos64_ref[0], sin64_ref[0], cos32_ref[0], sin32_ref[0]
    for p in range(C_HEADS // 2):
        y = _rope(_group_rms(q_in[0, :, p * LANES:(p + 1) * LANES].astype(F32), C_HD) * qg_ref[...], c64, s64, C_HD // 2)
        y = y * (C_HD ** -0.5 * LOG2E)
        q_ref[0, 2 * p] = jnp.where(lane < C_HD, y, 0.0).astype(BF16)
        q_ref[0, 2 * p + 1] = jnp.where(lane < C_HD, pltpu.roll(y, C_HD, 1), 0.0).astype(BF16)
    kv = kv_in[0].astype(F32)
    kn = _rope(_group_rms(kv, C_HD) * kg_ref[...], c64, s64, C_HD // 2)
    kv_ref[0] = jnp.where(lane < C_HD, kn, kv).astype(BF16)
    for c0 in range(0, IDX_HEADS * IDX_HD, LANES):
        iq_ref[0, :, c0:c0 + LANES] = _rope(iq_in[0, :, c0:c0 + LANES].astype(F32), c32, s32, IDX_HD // 2).astype(BF16)
    ik = jnp.where(lane < IDX_HD, _rope(sm_in[0], c32, s32, IDX_HD // 2), 0.0)
    ik = ik + pltpu.roll(ik, IDX_HD, 1)
    ik = ik + pltpu.roll(ik, 2 * IDX_HD, 1)
    ik_ref[0] = ik.astype(BF16)


def _dsa_prep(u, u32, cols, tabs, p):
    b, s, _ = u.shape
    tm = min(256, s)
    row = pl.BlockSpec((1, tm, LANES), lambda bi, i: (bi, i, 0))
    const = pl.BlockSpec((1, LANES), lambda bi, i: (0, 0))
    ublk = lambda name, w: pl.BlockSpec((1, tm, w), lambda bi, i, c=cols[name] // w: (bi, i, c))
    dense = jax.ShapeDtypeStruct((b, s, LANES), BF16)
    return pl.pallas_call(
        _dsa_prep_kernel,
        out_shape=(jax.ShapeDtypeStruct((b, C_HEADS, s, LANES), BF16), dense,
                   jax.ShapeDtypeStruct((b, s, IDX_HEADS * IDX_HD), BF16), dense),
        grid=(b, s // tm),
        in_specs=[ublk("c_q", 512), ublk("c_iq", 256), ublk("c_kv", LANES), row,
                  row, row, row, row, const, const],
        out_specs=(pl.BlockSpec((1, C_HEADS, tm, LANES), lambda bi, i: (bi, 0, i, 0)), row,
                   pl.BlockSpec((1, tm, IDX_HEADS * IDX_HD), lambda bi, i: (bi, i, 0)), row),
        compiler_params=_cp("parallel", "parallel"),
        name="dsa_prep",
    )(u, u, u, u32, tabs["cos64"], tabs["sin64"], tabs["cos32"], tabs["sin32"], p["q_g"], p["k_g"])


def _dsa_attn_kernel(q_ref, iq_ref, sm_ref, ik_ref, kv_ref, tri_ref, o_ref, key_scr, bias_scr, *, tk, topk):
    tq = q_ref.shape[2]
    i = pl.program_id(1)
    t = i * tq + lax.broadcasted_iota(I32, (1, tq), 1)
    lane = _lane((tq, LANES))
    n_tile = ((i + 1) * tq + tk - 1) // tk
    tile = lambda j: pl.ds(pl.multiple_of(j * tk, tk), tk)
    kpos = lambda j: j * tk + lax.broadcasted_iota(I32, (tk, 1), 0)

    iw_t = (sm_ref[0] * (IDX_HEADS ** -0.5)).T
    parts = []
    for h in range(IDX_HEADS):
        blk = iq_ref[0, :, (h // 4) * LANES:(h // 4 + 1) * LANES]
        lo = (h % 4) * IDX_HD
        parts.append(jnp.where((lane >= lo) & (lane < lo + IDX_HD), blk, jnp.zeros_like(blk)))
    iqs = jnp.concatenate(parts, axis=0)

    def idx_body(j, carry):
        lg = _nt(ik_ref[0, tile(j), :], iqs)
        acc = jnp.zeros((tk, tq), F32)
        for h in range(IDX_HEADS):
            acc = acc + iw_t[IDX_HD + h:IDX_HD + h + 1, :] * jnp.maximum(lg[:, h * tq:(h + 1) * tq], 0.0)
        sc = jnp.where(kpos(j) <= t, acc, NEG)
        bits = pltpu.bitcast(sc, I32)
        key = jnp.where(bits < 0, bits ^ 0x7FFFFFFF, bits)
        key_scr[tile(j), :] = jnp.where(sc == 0.0, 0, key)
        return carry

    lax.fori_loop(0, n_tile, idx_body, 0)

    def count_ge(thr_key):
        def body(j, cs):
            key = key_scr[tile(j), :]
            cs = list(cs)
            for n, r0 in enumerate(range(0, tk, 8)):
                c = cs[n % len(cs)]
                cs[n % len(cs)] = jnp.where(key[r0:r0 + 8] >= thr_key, c + 1, c)
            return tuple(cs)

        cs = lax.fori_loop(0, n_tile, body, tuple(jnp.zeros((8, tq), I32) for _ in range(4)))
        return jnp.sum(cs[0] + cs[1] + cs[2] + cs[3], axis=0, keepdims=True)

    def bit_body(bi, ucand):
        utrial = ucand | jnp.left_shift(jnp.int32(1), 31 - bi)
        return jnp.where(count_ge(utrial ^ INT_MIN) >= topk, utrial, ucand)

    thr = lax.fori_loop(0, 32, bit_body, jnp.zeros((1, tq), I32)) ^ INT_MIN
    n_ge = count_ge(thr)
    n_gt = count_ge(thr + 1)
    need = topk - n_gt
    row_ok = (n_ge - n_gt == need) | (thr == NEG_KEY) | (n_ge < topk)
    simple = jnp.min(jnp.where(row_ok, 1.0, 0.0)) > 0.5

    def fast_bias():
        def body(j, carry):
            keep = (key_scr[tile(j), :] >= thr) & (kpos(j) <= t)
            bias_scr[tile(j), :] = jnp.where(keep, 0.0, NEG)
            return carry

        lax.fori_loop(0, n_tile, body, 0)

    def tie_bias():
        need_f = need.astype(F32)

        def body(j, run):
            key = key_scr[tile(j), :]
            kp = kpos(j)
            for c0 in range(0, tk, LANES):
                kc = key[c0:c0 + LANES]
                eq = kc == thr
                eq_f = jnp.where(eq, 1.0, 0.0)
                pref = _mm(tri_ref[...], eq_f.astype(BF16)) + run
                keep = ((kc > thr) | (eq & (pref <= need_f))) & (kp[c0:c0 + LANES] <= t)
                bias_scr[pl.ds(pl.multiple_of(j * tk + c0, LANES), LANES), :] = jnp.where(keep, 0.0, NEG)
                run = run + jnp.sum(eq_f, axis=0, keepdims=True)
            return run

        lax.fori_loop(0, n_tile, body, jnp.zeros((1, tq), F32))

    lax.cond(simple, fast_bias, tie_bias)

    qs = jnp.concatenate([q_ref[0, h] for h in range(C_HEADS)], axis=0)

    def att_body(j, carry):
        kv = kv_ref[0, tile(j), :]
        bias_t = bias_scr[tile(j), :]
        return _softmax_step_t(qs, kv, kv, jnp.concatenate([bias_t] * C_HEADS, axis=1), carry)

    _, l, acc = lax.fori_loop(0, n_tile, att_body, _flash_init_t(C_HEADS * tq))
    o_t = acc / l
    for p in range(C_HEADS // 2):
        a = pltpu.roll(o_t[:, (2 * p) * tq:(2 * p + 1) * tq].T, C_HD, 1)
        bb = o_t[:, (2 * p + 1) * tq:(2 * p + 2) * tq].T
        o_ref[0, :, p * LANES:(p + 1) * LANES] = jnp.where(lane < C_HD, a, bb).astype(BF16)


def _dsa_attn(u32, qc, iq, ik, kv):
    b, h, s, _ = qc.shape
    tq = min(128, s)
    tk = min(512, s)
    topk = min(TOPK_MAX, s // 4)
    tri = np.tril(np.ones((LANES, LANES), np.float32))
    full = lambda shape: pl.BlockSpec(shape, lambda bi, i: (bi,) + (0,) * (len(shape) - 1))
    return pl.pallas_call(
        functools.partial(_dsa_attn_kernel, tk=tk, topk=topk),
        out_shape=jax.ShapeDtypeStruct((b, s, h * C_HD), BF16),
        grid=(b, s // tq),
        in_specs=[pl.BlockSpec((1, h, tq, LANES), lambda bi, i: (bi, 0, i, 0)),
                  pl.BlockSpec((1, tq, IDX_HEADS * IDX_HD), lambda bi, i: (bi, i, 0)),
                  pl.BlockSpec((1, tq, LANES), lambda bi, i: (bi, i, 0)),
                  full((1, s, LANES)), full((1, s, LANES)),
                  pl.BlockSpec((LANES, LANES), lambda bi, i: (0, 0))],
        out_specs=pl.BlockSpec((1, tq, h * C_HD), lambda bi, i: (bi, i, 0)),
        scratch_shapes=[pltpu.VMEM((s, tq), I32), pltpu.VMEM((s, tq), F32)],
        compiler_params=_cp("parallel", "parallel"),
        name="dsa_attn",
    )(qc, iq, u32, ik, kv, jnp.asarray(tri, BF16))


def _mlstm_conv_kernel(x_ref, halo_ref, w_ref, b_ref, o_ref):
    tm = x_ref.shape[1]
    i = pl.program_id(1)
    lane = _lane((tm, x_ref.shape[2]))
    halo = jnp.where(i > 0, halo_ref[0].astype(F32), 0.0)
    x = x_ref[0].astype(F32)
    xc = jnp.concatenate([halo, x], axis=0)
    off = halo.shape[0] - (CONV_W - 1)
    y = b_ref[...] + jnp.zeros_like(x)
    for j in range(CONV_W):
        y = y + w_ref[j:j + 1, :] * xc[off + j:off + j + tm, :]
    y = y * _sigmoid(y)
    o_ref[0] = jnp.where(lane >= D_HEADS * D_QK, y * (D_QK ** -0.5), y)


def _mlstm_conv(u, qk_block, w, bias):
    b, s, _ = u.shape
    c = w.shape[1]
    tm = min(512, s)
    hb = 16
    return pl.pallas_call(
        _mlstm_conv_kernel,
        out_shape=jax.ShapeDtypeStruct((b, s, c), F32),
        grid=(b, s // tm),
        in_specs=[pl.BlockSpec((1, tm, c), lambda bi, i: (bi, i, qk_block)),
                  pl.BlockSpec((1, hb, c), lambda bi, i: (bi, jnp.maximum(i * (tm // hb) - 1, 0), qk_block)),
                  pl.BlockSpec((CONV_W, c), lambda bi, i: (0, 0)),
                  pl.BlockSpec((1, c), lambda bi, i: (0, 0))],
        out_specs=pl.BlockSpec((1, tm, c), lambda bi, i: (bi, i, 0)),
        compiler_params=_cp("parallel", "parallel"),
        name="mlstm_conv",
    )(u, u, w, bias)


def _mlstm_scan_kernel(qk_ref, v_ref, op_ref, sm_ref, gt_ref, bcol_ref, brow_ref, hg_ref, o_ref,
                       c_scr, n_scr, m_scr):
    tc = qk_ref.shape[1]
    L = CHUNK
    nqk = D_HEADS * D_QK

    @pl.when(pl.program_id(1) == 0)
    def _():
        c_scr[...] = jnp.zeros_like(c_scr)
        n_scr[...] = jnp.zeros_like(n_scr)
        m_scr[...] = jnp.zeros_like(m_scr)

    r = lax.broadcasted_iota(I32, (L, L), 0)
    c = lax.broadcasted_iota(I32, (L, L), 1)
    tril = c <= r
    tril_f = jnp.where(tril, 1.0, 0.0).astype(F32)
    triu_f = jnp.where(r <= c, 1.0, 0.0).astype(F32)
    lane = _lane((L, LANES))
    row128 = lax.broadcasted_iota(I32, (LANES, LANES), 0)
    i_lane, f_lane = IDX_HD + IDX_HEADS, IDX_HD + IDX_HEADS + D_HEADS

    def one(bb, ci):
        rows = pl.ds(pl.multiple_of(ci * L, L), L)
        qk = qk_ref[bb, rows, :]
        sm = sm_ref[bb, rows, :] + bcol_ref[...]
        bcol_all = jnp.dot(tril_f, _log_sigmoid(sm), preferred_element_type=F32, precision=HI)
        gt = gt_ref[bb, ci] + brow_ref[...]
        brow_all = jnp.dot(_log_sigmoid(gt), triu_f, preferred_element_type=F32, precision=HI)
        for p in range(D_HEADS // 2):
            qpair = qk[:, p * LANES:(p + 1) * LANES]
            kpair = qk[:, nqk + p * LANES:nqk + (p + 1) * LANES]
            kpair_b = kpair.astype(BF16)
            sp = bb * (D_HEADS // 2) + p
            c_prev = c_scr[sp]
            n_prev = n_scr[sp, 0:1, :]
            c_prev_b = c_prev.astype(BF16)
            upd = []
            for e in range(2):
                h = 2 * p + e
                in_half = (lane >= D_QK * e) & (lane < D_QK * (e + 1))
                qm = jnp.where(in_half, qpair, 0.0)
                qm_b = qm.astype(BF16)
                vh = v_ref[bb, rows, h * D_VD:(h + 1) * D_VD].astype(F32)
                b_col = bcol_all[:, f_lane + h:f_lane + h + 1]
                i_col = sm[:, i_lane + h:i_lane + h + 1]
                b_row = brow_all[D_HEADS + h:D_HEADS + h + 1, :]
                i_row = gt[h:h + 1, :]
                m_prev = m_scr[bb * D_HEADS + h, 0:1, 0:1]
                dmat = jnp.where(tril, b_col - b_row + i_row, NEG)
                inter = b_col + m_prev
                m_t = jnp.maximum(inter, jnp.max(dmat, axis=-1, keepdims=True))
                a = jnp.exp(inter - m_t)
                w = _nt(qm_b, kpair_b) * jnp.exp(dmat - m_t)
                num = a * _mm(qm_b, c_prev_b) + _mm(w.astype(BF16), vh.astype(BF16))
                den = a * jnp.sum(qm * n_prev, axis=-1, keepdims=True) + jnp.sum(w, axis=-1, keepdims=True)
                hout = num / jnp.maximum(jnp.abs(den), jnp.exp(-m_t))
                b_last = b_col[L - 1:L, :]
                g_col = b_last - b_col + i_col
                m_new = jnp.maximum(b_last + m_prev, jnp.max(g_col, axis=0, keepdims=True))
                ws = jnp.exp(g_col - m_new)
                decay = jnp.exp(b_last + m_prev - m_new)
                u_mat = _tn(kpair_b, (ws * vh).astype(BF16))
                k_sum = jnp.sum(ws * kpair, axis=0, keepdims=True)
                upd.append((decay * c_prev + u_mat, decay * n_prev + k_sum))
                m_scr[bb * D_HEADS + h] = jnp.broadcast_to(m_new, m_scr.shape[1:])
                hn = _row_rms(hout) * hg_ref[...]
                y = _sigmoid(op_ref[bb, rows, h * D_VD:(h + 1) * D_VD].astype(F32)) * hn
                o_ref[bb, rows, h * D_VD:(h + 1) * D_VD] = y.astype(BF16)
            c_scr[sp] = jnp.where(row128 < D_QK, upd[0][0], upd[1][0])
            n_new = jnp.where(lane[0:1] < D_QK, upd[0][1], upd[1][1])
            n_scr[sp] = jnp.broadcast_to(n_new, n_scr.shape[1:])

    def chunk(ci, carry):
        for bb in range(qk_ref.shape[0]):
            one(bb, ci)
        return carry

    lax.fori_loop(0, tc // L, chunk, 0)


def _mlstm_scan(u, u32, cols, qk, gt, p):
    b, s, _ = u.shape
    tc = min(512, s)
    nb = next(n for n in (4, 2, 1) if b % n == 0)
    nv = D_HEADS * D_VD
    ublk = lambda name, w: pl.BlockSpec((nb, tc, w), lambda bi, i, c=cols[name] // w: (bi, i, c))
    const = lambda shape: pl.BlockSpec(shape, lambda bi, i: (0,) * len(shape))
    return pl.pallas_call(
        _mlstm_scan_kernel,
        out_shape=jax.ShapeDtypeStruct((b, s, nv), BF16),
        grid=(b // nb, s // tc),
        in_specs=[pl.BlockSpec((nb, tc, qk.shape[-1]), lambda bi, i: (bi, i, 0)),
                  ublk("d_v", nv), ublk("d_o", nv), pl.BlockSpec((nb, tc, LANES), lambda bi, i: (bi, i, 0)),
                  pl.BlockSpec((nb, tc // CHUNK, 8, CHUNK), lambda bi, i: (bi, i, 0, 0)),
                  const((1, LANES)), const((8, 1)), const((1, D_VD))],
        out_specs=pl.BlockSpec((nb, tc, nv), lambda bi, i: (bi, i, 0)),
        scratch_shapes=[pltpu.VMEM((nb * D_HEADS // 2, LANES, LANES), F32),
                        pltpu.VMEM((nb * D_HEADS // 2, 8, LANES), F32),
                        pltpu.VMEM((nb * D_HEADS, 8, LANES), F32)],
        compiler_params=_cp("parallel", "arbitrary"),
        name="mlstm_scan",
    )(qk, u, u, u32, gt, p["bias_col"], p["bias_row"], p["h_g"])


EVEN_SRC = dict(a_ql=(0, 256), a_kvl=(256, 128), a_kr=(384, 32), a_gate=(416, 512), b_q=(928, 512),
                b_kc=(1440, 128), b_vc=(1568, 128), b_ks=(1696, 128), b_vs=(1824, 128), b_kw=(1952, 128),
                b_vw=(2080, 128), b_g=(2208, 24), b_gate=(2232, 512), m_q=(2744, 256), m_gate=(3000, 256))
EVEN_DST = dict(a_ql=0, a_kvl=256, a_kr=384, b_q=512, a_gate=1024, b_gate=1536, m_q=2048, m_gate=2304,
                b_g=2560, b_kc=2688, b_vc=2816, b_ks=2944, b_vs=3072, b_kw=3200, b_vw=3328)
EVEN_COLS_PAD = 3456

ODD_SRC = dict(c_q=(0, 512), c_k=(512, 64), c_v=(576, 64), c_iq=(640, 256), c_ik=(896, 32), c_iw=(928, 8),
               c_gate=(936, 512), d_q=(1448, 256), d_k=(1704, 256), d_v=(1960, 512), d_i=(2472, 4),
               d_f=(2476, 4), d_o=(2480, 512), d_gate=(2992, 512), m_q=(3504, 256), m_gate=(3760, 256))
ODD_DST = dict(c_q=0, c_gate=512, d_gate=1024, d_v=1536, d_o=2048, d_q=2560, d_k=2816, m_q=3072, m_gate=3328,
               c_iq=3584, c_k=3840, c_v=3904, c_ik=3968, c_iw=4000, d_i=4008, d_f=4012)
ODD_COLS_PAD = 4096


def _permute_cols(w, src, dst, total):
    idx = np.zeros((total,), np.int32)
    keep = np.zeros((total,), np.float32)
    for name, (start, width) in src.items():
        idx[dst[name]:dst[name] + width] = np.arange(start, start + width)
        keep[dst[name]:dst[name] + width] = 1.0
    return (jnp.take(w, jnp.asarray(idx), axis=1) * jnp.asarray(keep)).astype(BF16)


def _tile_lanes(v, reps):
    return jnp.tile(v.astype(F32).reshape(1, -1), (1, reps))


def _rope_tables(positions, d2):
    inv = ROPE_THETA ** (-jnp.arange(d2, dtype=F32) / d2)
    ang = positions.astype(F32)[..., None] * inv
    c, s = jnp.cos(ang), jnp.sin(ang)
    reps = LANES // (2 * d2)
    return (jnp.tile(jnp.concatenate([c, c], axis=-1), (1, 1, reps)),
            jnp.tile(jnp.concatenate([-s, s], axis=-1), (1, 1, reps)))


def _block_diag2(w):
    z = jnp.zeros_like(w)
    return jnp.concatenate([jnp.concatenate([w, z], axis=-1), jnp.concatenate([z, w], axis=-1)], axis=-2)


def kernel(x, mem, positions, ln_g, mem_norm_g, mem_w_kv, mem_q_norm_g, mem_k_norm_g, w_out, even_w_in, mla_q_lat_g, mla_kv_lat_g, mla_w_uq, mla_w_ukv, mla_q_norm_g, mla_k_norm_g, nsa_q_norm_g, nsa_k_norm_g, nsa_cmp_pos, nsa_cmp_w1, nsa_cmp_w2, odd_w_in, dsa_q_norm_g, dsa_k_norm_g, mlstm_conv_w, mlstm_conv_b, mlstm_i_bias, mlstm_f_bias, mlstm_h_norm_g):
    b, s, _ = x.shape
    depth = ln_g.shape[0]
    cos64, sin64 = _rope_tables(positions, 32)
    cos32, sin32 = _rope_tables(positions, 16)
    tabs = dict(cos64=cos64, sin64=sin64, cos32=cos32, sin32=sin32)
    n_pad = s // CMP_STRIDE
    cmp_pos = jnp.pad(positions[:, CMP_LEN - 1::CMP_STRIDE], ((0, 0), (0, 0)))[:, :n_pad]
    cmp_pos = jnp.pad(cmp_pos, ((0, 0), (0, n_pad - cmp_pos.shape[1])))
    cosc, sinc = _rope_tables(cmp_pos, 32)

    mem_k, mem_v = _mem_kv(mem, mem_norm_g.reshape(depth, 1, -1), mem_w_kv.astype(BF16),
                           jnp.tile(mem_k_norm_g, (1, 2)).reshape(depth, 1, LANES))

    hq = np.arange(A_HEADS)[:, None] * (A_NOPE + A_ROPE)
    uq_idx = np.concatenate([(hq + np.arange(A_NOPE)[None, :]).ravel(),
                             (hq + A_NOPE + np.arange(A_ROPE)[None, :]).ravel()])
    hk = np.arange(A_HEADS)[:, None] * (A_NOPE + A_VD)
    ukv_idx = np.concatenate([(hk + np.arange(A_NOPE)[None, :]).ravel(),
                              (hk + A_NOPE + np.arange(A_VD)[None, :]).ravel()])

    for layer in range(depth):
        li = layer // 2
        g_ln = ln_g[layer].reshape(1, -1)
        mq_g = _tile_lanes(mem_q_norm_g[layer], 2)
        if layer % 2 == 0:
            cols = EVEN_DST
            u, u32 = _in_proj(x, g_ln, _permute_cols(even_w_in[li], EVEN_SRC, EVEN_DST, EVEN_COLS_PAD), cols["b_g"])
            pa = dict(q_lat_g=mla_q_lat_g[li].reshape(1, -1), kv_lat_g=mla_kv_lat_g[li].reshape(1, -1),
                      w_uq=jnp.take(mla_w_uq[li], jnp.asarray(uq_idx), axis=1).astype(BF16),
                      w_ukv=jnp.take(mla_w_ukv[li], jnp.asarray(ukv_idx), axis=1).astype(BF16),
                      qn_g=_tile_lanes(mla_q_norm_g[li, :A_NOPE], 2), qr_g=_tile_lanes(mla_q_norm_g[li, A_NOPE:], 4),
                      kn_g=_tile_lanes(mla_k_norm_g[li, :A_NOPE], 2), kr_g=_tile_lanes(mla_k_norm_g[li, A_NOPE:], 4))
            qa, ka, va = _mla_prep(u, cos32, sin32, pa)
            y1 = _mla_attn(qa, ka, va)
            pb = dict(q_g=_tile_lanes(nsa_q_norm_g[li], 2), ks_g=_tile_lanes(nsa_k_norm_g[li, 1], 2),
                      kw_g=_tile_lanes(nsa_k_norm_g[li, 2], 2), kc_g=_tile_lanes(nsa_k_norm_g[li, 0], 2),
                      pe_k=jnp.tile(nsa_cmp_pos[li, 0], (1, 2)), pe_v=jnp.tile(nsa_cmp_pos[li, 1], (1, 2)),
                      w1k=_block_diag2(nsa_cmp_w1[li, 0].reshape(CMP_LEN, B_HD, B_HD)).astype(BF16),
                      w1v=_block_diag2(nsa_cmp_w1[li, 1].reshape(CMP_LEN, B_HD, B_HD)).astype(BF16),
                      w2k=_block_diag2(nsa_cmp_w2[li, 0]).astype(BF16),
                      w2v=_block_diag2(nsa_cmp_w2[li, 1]).astype(BF16))
            qb, ks, kw = _nsa_prep(u, cols, cos64, sin64, pb)
            kcmp, vcmp = _nsa_cmp(u, cols, cosc, sinc, pb)
            y2 = _nsa_attn(u, u32, cols, qb, kcmp, vcmp, ks, kw)
            gate_blocks = (cols["a_gate"] // 512, cols["b_gate"] // 512, cols["m_gate"] // 256)
        else:
            cols = dict(ODD_DST, c_kv=ODD_DST["c_k"], small=ODD_DST["c_ik"])
            u, u32 = _in_proj(x, g_ln, _permute_cols(odd_w_in[li], ODD_SRC, ODD_DST, ODD_COLS_PAD), cols["small"])
            pc = dict(q_g=_tile_lanes(dsa_q_norm_g[li], 2), k_g=_tile_lanes(dsa_k_norm_g[li], 2))
            qc, kvc, iq, ik = _dsa_prep(u, u32, cols, tabs, pc)
            y1 = _dsa_attn(u32, qc, iq, ik, kvc)
            qk = _mlstm_conv(u, cols["d_q"] // 512, mlstm_conv_w[li], mlstm_conv_b[li].reshape(1, -1))
            gates = u32[:, :, cols["d_i"] - cols["small"]:cols["d_i"] - cols["small"] + 2 * D_HEADS]
            gt = gates.reshape(b, s // CHUNK, CHUNK, 2 * D_HEADS).transpose(0, 1, 3, 2)
            bias8 = jnp.concatenate([mlstm_i_bias[li], mlstm_f_bias[li]]).astype(F32)
            bias_col = jnp.zeros((1, LANES), F32).at[0, cols["d_i"] - cols["small"]:cols["d_i"] - cols["small"] + 8].set(bias8)
            pd = dict(bias_col=bias_col, bias_row=bias8.reshape(8, 1), h_g=mlstm_h_norm_g[li].reshape(1, -1))
            y2 = _mlstm_scan(u, u32, cols, qk, gt, pd)
            gate_blocks = (cols["c_gate"] // 512, cols["d_gate"] // 512, cols["m_gate"] // 256)
        ym = _mem_attn(u, cols["m_q"] // 256, mem_k, mem_v, layer, mq_g)
        x = _out_proj(x, y1, y2, ym, u, gate_blocks, w_out[layer].astype(BF16))
    return x
```

```python
import functools

import numpy as np
import jax
import jax.numpy as jnp
from jax import lax
from jax.experimental import pallas as pl
from jax.experimental.pallas import tpu as pltpu

F32, BF16, I32 = jnp.float32, jnp.bfloat16, jnp.int32
NEG = -1e30
EPS = 1e-6
ROPE_THETA = 10000.0
LANES = 128
VMEM_LIMIT_BYTES = 48 * 1024 * 1024

D_MODEL = 1024
DEPTH = 4
A_HEADS, A_NOPE, A_ROPE, A_VD, A_QLAT, A_KVLAT = 8, 64, 32, 64, 256, 128
B_HEADS, B_KV_HEADS, B_HD = 8, 2, 64
CMP_LEN, CMP_STRIDE, SEL_LEN, N_SEL, WINDOW = 32, 16, 64, 16, 512
C_HEADS, C_HD, IDX_HEADS, IDX_HD, TOPK_MAX = 8, 64, 8, 32, 256
D_HEADS, D_QK, D_VD, CONV_W, CHUNK = 4, 64, 128, 4, 64
M_HEADS, M_HD = 4, 64

INT_MIN = np.int32(-2 ** 31)
NEG_KEY = int(np.float32(NEG).view(np.int32) ^ np.int32(0x7FFFFFFF))
LOG2E = float(np.log2(np.e))


def _cp(*sem):
    return pltpu.CompilerParams(dimension_semantics=sem, vmem_limit_bytes=VMEM_LIMIT_BYTES)


def _nt(a, b):
    return lax.dot_general(a, b, (((1,), (1,)), ((), ())), preferred_element_type=F32)


def _tn(a, b):
    return lax.dot_general(a, b, (((0,), (0,)), ((), ())), preferred_element_type=F32)


def _mm(a, b):
    return jnp.dot(a, b, preferred_element_type=F32)


def _sigmoid(x):
    return 1.0 / (1.0 + jnp.exp(-x))


def _log_sigmoid(x):
    return jnp.minimum(x, 0.0) - jnp.log1p(jnp.exp(-jnp.abs(x)))


def _lane(shape):
    return lax.broadcasted_iota(I32, shape, len(shape) - 1)


def _group_mat(gs):
    r = lax.broadcasted_iota(I32, (LANES, LANES), 0)
    c = lax.broadcasted_iota(I32, (LANES, LANES), 1)
    sh = gs.bit_length() - 1
    return jnp.where((r >> sh) == (c >> sh), 1.0, 0.0).astype(BF16)


def _split3(x):
    hi = x.astype(BF16)
    r1 = x - hi.astype(F32)
    mid = r1.astype(BF16)
    return hi, mid, (r1 - mid.astype(F32)).astype(BF16)


def _mm_split(x, w01):
    hi = x.astype(BF16)
    lo = (x - hi.astype(F32)).astype(BF16)
    return _mm(hi, w01) + _mm(lo, w01)


def _group_rms(x, gs):
    ss = _mm_split(x * x, _group_mat(gs))
    return x * lax.rsqrt(ss * (1.0 / gs) + EPS)


def _rope(x, cosp, sinp, half):
    lane = _lane(x.shape)
    rot = jnp.where((lane & (2 * half - 1)) < half,
                    pltpu.roll(x, LANES - half, 1), pltpu.roll(x, half, 1))
    return x * cosp + rot * sinp


def _row_rms(x):
    return x * lax.rsqrt(jnp.mean(x * x, axis=-1, keepdims=True) + EPS)


def _in_proj_kernel(x_ref, g_ref, w_ref, o_ref, o32_ref, *, c32):
    h = (_row_rms(x_ref[0]) * g_ref[...]).astype(BF16)
    ncol = o_ref.shape[-1]
    for c0 in range(0, ncol, 1024):
        c1 = min(ncol, c0 + 1024)
        o_ref[0, :, c0:c1] = _mm(h, w_ref[:, c0:c1]).astype(BF16)
    o32_ref[0] = _mm(h, w_ref[:, c32:c32 + LANES])


def _in_proj(x, g, w, c32):
    b, s, d = x.shape
    c = w.shape[1]
    tm = min(1024, s)
    return pl.pallas_call(
        functools.partial(_in_proj_kernel, c32=c32),
        out_shape=(jax.ShapeDtypeStruct((b, s, c), BF16), jax.ShapeDtypeStruct((b, s, LANES), F32)),
        grid=(b, s // tm),
        in_specs=[pl.BlockSpec((1, tm, d), lambda bi, i: (bi, i, 0)),
                  pl.BlockSpec((1, d), lambda bi, i: (0, 0)),
                  pl.BlockSpec((d, c), lambda bi, i: (0, 0))],
        out_specs=(pl.BlockSpec((1, tm, c), lambda bi, i: (bi, i, 0)),
                   pl.BlockSpec((1, tm, LANES), lambda bi, i: (bi, i, 0))),
        compiler_params=_cp("parallel", "parallel"),
        name="in_proj",
    )(x, g, w)


def _out_proj_kernel(x_ref, y1_ref, y2_ref, ym_ref, g1_ref, g2_ref, gm_ref, w_ref, o_ref):
    def gated(y_ref, g_ref):
        g = g_ref[0].astype(F32)
        return (y_ref[0].astype(F32) * (g * _sigmoid(g))).astype(BF16)

    n1 = y1_ref.shape[-1]
    n2 = y2_ref.shape[-1]
    acc = x_ref[0]
    acc = acc + _mm(gated(y1_ref, g1_ref), w_ref[0:n1, :])
    acc = acc + _mm(gated(y2_ref, g2_ref), w_ref[n1:n1 + n2, :])
    acc = acc + _mm(gated(ym_ref, gm_ref), w_ref[n1 + n2:, :])
    o_ref[0] = acc


def _out_proj(x, y1, y2, ym, u, gate_blocks, w):
    b, s, d = x.shape
    tm = min(512, s)
    i1, i2, im = gate_blocks
    n1, n2, nm = y1.shape[-1], y2.shape[-1], ym.shape[-1]
    row = lambda bi, i: (bi, i, 0)
    return pl.pallas_call(
        _out_proj_kernel,
        out_shape=jax.ShapeDtypeStruct((b, s, d), F32),
        grid=(b, s // tm),
        in_specs=[pl.BlockSpec((1, tm, d), row),
                  pl.BlockSpec((1, tm, n1), row),
                  pl.BlockSpec((1, tm, n2), row),
                  pl.BlockSpec((1, tm, nm), row),
                  pl.BlockSpec((1, tm, n1), lambda bi, i: (bi, i, i1)),
                  pl.BlockSpec((1, tm, n2), lambda bi, i: (bi, i, i2)),
                  pl.BlockSpec((1, tm, nm), lambda bi, i: (bi, i, im)),
                  pl.BlockSpec(w.shape, lambda bi, i: (0, 0))],
        out_specs=pl.BlockSpec((1, tm, d), row),
        compiler_params=_cp("parallel", "parallel"),
        name="out_proj",
    )(x, y1, y2, ym, u, u, u, w)


def _mem_kv_kernel(mem_ref, g_ref, w_ref, kg_ref, k_ref, v_ref):
    h = (_row_rms(mem_ref[0]) * g_ref[0]).astype(BF16)
    kv = _mm(h, w_ref[0])
    nk = k_ref.shape[-1]
    for c0 in range(0, nk, LANES):
        k_ref[0, 0, :, c0:c0 + LANES] = (_group_rms(kv[:, c0:c0 + LANES], M_HD) * kg_ref[0]).astype(BF16)
    v_ref[0, 0] = kv[:, nk:].astype(BF16)


def _mem_kv(mem, g, w, kg):
    b, m, d = mem.shape
    depth = w.shape[0]
    nk = M_HEADS * M_HD
    out = jax.ShapeDtypeStruct((depth, b, m, nk), BF16)
    return pl.pallas_call(
        _mem_kv_kernel,
        out_shape=(out, out),
        grid=(depth, b),
        in_specs=[pl.BlockSpec((1, m, d), lambda l, bi: (bi, 0, 0)),
                  pl.BlockSpec((1, 1, d), lambda l, bi: (l, 0, 0)),
                  pl.BlockSpec((1, d, 2 * nk), lambda l, bi: (l, 0, 0)),
                  pl.BlockSpec((1, 1, LANES), lambda l, bi: (l, 0, 0))],
        out_specs=(pl.BlockSpec((1, 1, m, nk), lambda l, bi: (l, bi, 0, 0)),
                   pl.BlockSpec((1, 1, m, nk), lambda l, bi: (l, bi, 0, 0))),
        compiler_params=_cp("parallel", "parallel"),
        name="mem_kv",
    )(mem, g, w, kg)


def _mem_attn_kernel(q_ref, k_ref, v_ref, qg_ref, o_ref):
    tq = q_ref.shape[1]
    lane = _lane((tq, LANES))
    for p in range(M_HEADS // 2):
        sl = slice(p * LANES, (p + 1) * LANES)
        q = (_group_rms(q_ref[0, :, sl].astype(F32), M_HD) * qg_ref[...] * (M_HD ** -0.5)).astype(BF16)
        k = k_ref[0, 0, :, sl]
        v = v_ref[0, 0, :, sl]
        halves = []
        for e in range(2):
            in_half = (lane >= 64 * e) & (lane < 64 * e + 64)
            s = _nt(jnp.where(in_half, q, jnp.zeros_like(q)), k)
            ex = jnp.exp(s - jnp.max(s, axis=-1, keepdims=True))
            o = _mm(ex.astype(BF16), v) / jnp.sum(ex, axis=-1, keepdims=True)
            halves.append(o)
        o_ref[0, :, sl] = jnp.where(lane < 64, halves[0], halves[1]).astype(BF16)


def _mem_attn(u, q_block, k, v, layer, qg):
    b, s, _ = u.shape
    m, nk = k.shape[2], k.shape[3]
    tq = min(512, s)
    return pl.pallas_call(
        _mem_attn_kernel,
        out_shape=jax.ShapeDtypeStruct((b, s, nk), BF16),
        grid=(b, s // tq),
        in_specs=[pl.BlockSpec((1, tq, nk), lambda bi, i: (bi, i, q_block)),
                  pl.BlockSpec((1, 1, m, nk), lambda bi, i: (layer, bi, 0, 0)),
                  pl.BlockSpec((1, 1, m, nk), lambda bi, i: (layer, bi, 0, 0)),
                  pl.BlockSpec((1, LANES), lambda bi, i: (0, 0))],
        out_specs=pl.BlockSpec((1, tq, nk), lambda bi, i: (bi, i, 0)),
        compiler_params=_cp("parallel", "parallel"),
        name="mem_attn",
    )(u, k, v, qg)


def _flash(qs, k_at, v_at, j0, j1, bias_at, carry):
    def body(j, c):
        bias = None if bias_at is None else bias_at(j)
        return tuple(_softmax_step(q, k_at(j, n), v_at(j), bias, cn) for n, (q, cn) in enumerate(zip(qs, c)))

    return lax.fori_loop(j0, j1, body, carry)


def _flash_init(rows):
    return (jnp.full((rows, 1), NEG, F32), jnp.zeros((rows, 1), F32), jnp.zeros((rows, LANES), F32))


def _softmax_update(s, v, bias, carry):
    m, l, acc = carry
    if bias is not None:
        rep = s.shape[0] // bias.shape[0]
        s = (s.reshape(rep, bias.shape[0], s.shape[1]) + bias[None]).reshape(s.shape)
    m_new = jnp.maximum(m, jnp.max(s, axis=-1, keepdims=True))
    alpha = jnp.exp2(m - m_new)
    p = jnp.exp2(s - m_new)
    l = alpha * l + jnp.sum(p, axis=-1, keepdims=True)
    acc = alpha * acc + _mm(p.astype(BF16), v)
    return m_new, l, acc


def _softmax_step(q, k, v, bias, carry):
    return _softmax_update(_nt(q, k), v, bias, carry)


def _flash_out(carry):
    _, l, acc = carry
    return acc / l


def _mla_prep_kernel(u_ref, cos_ref, sin_ref, qlg_ref, kvlg_ref, wuq_ref, wukv_ref,
                     qng_ref, qrg_ref, kng_ref, krg_ref, q_ref, k_ref, v_ref):
    tm = u_ref.shape[1]
    lane = _lane((tm, LANES))
    cosp, sinp = cos_ref[0], sin_ref[0]
    half = A_ROPE // 2
    scale = (A_NOPE + A_ROPE) ** -0.5 * LOG2E
    ql = (_row_rms(u_ref[0, :, 0:A_QLAT].astype(F32)) * qlg_ref[...]).astype(BF16)
    kvl = (_row_rms(u_ref[0, :, A_QLAT:A_QLAT + A_KVLAT].astype(F32)) * kvlg_ref[...]).astype(BF16)
    q = _mm(ql, wuq_ref[...])
    kv = _mm(kvl, wukv_ref[...])
    n_nope = A_HEADS * A_NOPE
    v_ref[0] = kv[:, n_nope:].astype(BF16)
    kr = u_ref[0, :, A_QLAT + A_KVLAT:A_QLAT + A_KVLAT + LANES].astype(F32)
    kpe = _rope(_group_rms(kr, A_ROPE) * krg_ref[...], cosp, sinp, half)
    kpe = pltpu.roll(kpe, A_NOPE, 1)
    qn = [_group_rms(q[:, c:c + LANES], A_NOPE) * qng_ref[...] for c in range(0, n_nope, LANES)]
    kn = [_group_rms(kv[:, c:c + LANES], A_NOPE) * kng_ref[...] for c in range(0, n_nope, LANES)]
    qr = [_rope(_group_rms(q[:, n_nope + c:n_nope + c + LANES], A_ROPE) * qrg_ref[...], cosp, sinp, half)
          for c in range(0, A_HEADS * A_ROPE, LANES)]
    for h in range(A_HEADS):
        qn_h = qn[h // 2] if h % 2 == 0 else pltpu.roll(qn[h // 2], A_NOPE, 1)
        kn_h = kn[h // 2] if h % 2 == 0 else pltpu.roll(kn[h // 2], A_NOPE, 1)
        shift = (A_NOPE - (h % 4) * A_ROPE) % LANES
        qr_h = qr[h // 4] if shift == 0 else pltpu.roll(qr[h // 4], shift, 1)
        qf = jnp.where(lane < A_NOPE, qn_h, jnp.where(lane < A_NOPE + A_ROPE, qr_h, 0.0))
        kf = jnp.where(lane < A_NOPE, kn_h, jnp.where(lane < A_NOPE + A_ROPE, kpe, 0.0))
        q_ref[0, h] = (qf * scale).astype(BF16)
        k_ref[0, h] = kf.astype(BF16)


def _mla_prep(u, cos32, sin32, p):
    b, s, _ = u.shape
    tm = min(256, s)
    hd = jax.ShapeDtypeStruct((b, A_HEADS, s, LANES), BF16)
    const = lambda shape: pl.BlockSpec(shape, lambda bi, i: (0,) * len(shape))
    return pl.pallas_call(
        _mla_prep_kernel,
        out_shape=(hd, hd, jax.ShapeDtypeStruct((b, s, A_HEADS * A_VD), BF16)),
        grid=(b, s // tm),
        in_specs=[pl.BlockSpec((1, tm, 512), lambda bi, i: (bi, i, 0)),
                  pl.BlockSpec((1, tm, LANES), lambda bi, i: (bi, i, 0)),
                  pl.BlockSpec((1, tm, LANES), lambda bi, i: (bi, i, 0)),
                  const((1, A_QLAT)), const((1, A_KVLAT)),
                  const(p["w_uq"].shape), const(p["w_ukv"].shape),
                  const((1, LANES)), const((1, LANES)), const((1, LANES)), const((1, LANES))],
        out_specs=(pl.BlockSpec((1, A_HEADS, tm, LANES), lambda bi, i: (bi, 0, i, 0)),
                   pl.BlockSpec((1, A_HEADS, tm, LANES), lambda bi, i: (bi, 0, i, 0)),
                   pl.BlockSpec((1, tm, A_HEADS * A_VD), lambda bi, i: (bi, i, 0))),
        compiler_params=_cp("parallel", "parallel"),
        name="mla_prep",
    )(u, cos32, sin32, p["q_lat_g"], p["kv_lat_g"], p["w_uq"], p["w_ukv"],
      p["qn_g"], p["qr_g"], p["kn_g"], p["kr_g"])


def _flash_init_t(cols):
    return (jnp.full((1, cols), NEG, F32), jnp.zeros((1, cols), F32), jnp.zeros((LANES, cols), F32))


def _softmax_step_t(q, k, v, bias_t, carry):
    m, l, acc = carry
    s = _nt(k, q)
    if bias_t is not None:
        s = s + bias_t
    m_new = jnp.maximum(m, jnp.max(s, axis=0, keepdims=True))
    alpha = jnp.exp2(m - m_new)
    p = jnp.exp2(s - m_new)
    l = alpha * l + jnp.sum(p, axis=0, keepdims=True)
    acc = alpha * acc + _tn(v, p.astype(BF16))
    return m_new, l, acc


def _mla_attn_kernel(q_ref, k_ref, v_ref, o_ref, *, tk):
    tq = q_ref.shape[2]
    per_q = tq // tk
    i = pl.program_id(2)
    lane = _lane((tq, LANES))
    t = i * tq + lax.broadcasted_iota(I32, (1, tq), 1)
    nh = q_ref.shape[1]
    qs = [q_ref[0, e] for e in range(nh)]
    tile = lambda j: pl.ds(pl.multiple_of(j * tk, tk), tk)
    v_at = lambda j, e: v_ref[0, tile(j), (e // 2) * LANES:(e // 2 + 1) * LANES]

    def body(j, c):
        return tuple(_softmax_step_t(qs[e], k_ref[0, e, tile(j), :], v_at(j, e), None, c[e]) for e in range(nh))

    carry = lax.fori_loop(0, i * per_q, body, tuple(_flash_init_t(tq) for _ in range(nh)))
    for d in range(per_q):
        j = i * per_q + d
        bias_t = jnp.where(j * tk + lax.broadcasted_iota(I32, (tk, 1), 0) <= t[:, d * tk:], 0.0, NEG)
        new = []
        for e in range(nh):
            m, l, acc = carry[e]
            part = _softmax_step_t(qs[e][d * tk:], k_ref[0, e, tile(j), :], v_at(j, e), bias_t,
                                   (m[:, d * tk:], l[:, d * tk:], acc[:, d * tk:]))
            if d:
                part = tuple(jnp.concatenate([old[:, :d * tk], upd], axis=1) for old, upd in zip(carry[e], part))
            new.append(part)
        carry = tuple(new)
    res = [(acc / l).T for (_, l, acc) in carry]
    for p in range(nh // 2):
        o_ref[0, :, p * LANES:(p + 1) * LANES] = jnp.where(lane < A_VD, res[2 * p], res[2 * p + 1]).astype(BF16)


def _mla_attn(q, k, v):
    b, h, s, _ = q.shape
    tq = min(1024, s)
    tk = min(512, s)
    nh = 4
    return pl.pallas_call(
        functools.partial(_mla_attn_kernel, tk=tk),
        out_shape=jax.ShapeDtypeStruct((b, s, h * A_VD), BF16),
        grid=(b, h // nh, s // tq),
        in_specs=[pl.BlockSpec((1, nh, tq, LANES), lambda bi, p, i: (bi, p, i, 0)),
                  pl.BlockSpec((1, nh, s, LANES), lambda bi, p, i: (bi, p, 0, 0)),
                  pl.BlockSpec((1, s, nh * A_VD), lambda bi, p, i: (bi, 0, p))],
        out_specs=pl.BlockSpec((1, tq, nh * A_VD), lambda bi, p, i: (bi, i, p)),
        compiler_params=_cp("parallel", "parallel", "parallel"),
        name="mla_attn",
    )(q, k, v)


def _nsa_prep_kernel(q_in, ks_in, kw_in, cos_ref, sin_ref, qg_ref, ksg_ref, kwg_ref, q_ref, ks_ref, kw_ref):
    tm = q_in.shape[1]
    lane = _lane((tm, LANES))
    cosp, sinp = cos_ref[0], sin_ref[0]
    half = B_HD // 2
    rep = B_HEADS // B_KV_HEADS
    for p in range(B_HEADS // 2):
        y = _rope(_group_rms(q_in[0, :, p * LANES:(p + 1) * LANES].astype(F32), B_HD) * qg_ref[...], cosp, sinp, half)
        y = y * (B_HD ** -0.5 * LOG2E)
        y_sw = pltpu.roll(y, B_HD, 1)
        for e in range(2):
            h = 2 * p + e
            g = h // rep
            src = y if e == g else y_sw
            in_grp = (lane >= B_HD * g) & (lane < B_HD * (g + 1))
            q_ref[0, h] = jnp.where(in_grp, src, 0.0).astype(BF16)
    ks_ref[0] = _rope(_group_rms(ks_in[0].astype(F32), B_HD) * ksg_ref[...], cosp, sinp, half).astype(BF16)
    kw_ref[0] = _rope(_group_rms(kw_in[0].astype(F32), B_HD) * kwg_ref[...], cosp, sinp, half).astype(BF16)


def _nsa_prep(u, cols, cos64, sin64, p):
    b, s, _ = u.shape
    tm = min(256, s)
    blk = lambda name: pl.BlockSpec((1, tm, LANES), lambda bi, i, c=cols[name] // LANES: (bi, i, c))
    row = pl.BlockSpec((1, tm, LANES), lambda bi, i: (bi, i, 0))
    const = pl.BlockSpec((1, LANES), lambda bi, i: (0, 0))
    kvs = jax.ShapeDtypeStruct((b, s, LANES), BF16)
    return pl.pallas_call(
        _nsa_prep_kernel,
        out_shape=(jax.ShapeDtypeStruct((b, B_HEADS, s, LANES), BF16), kvs, kvs),
        grid=(b, s // tm),
        in_specs=[pl.BlockSpec((1, tm, 512), lambda bi, i, c=cols["b_q"] // 512: (bi, i, c)),
                  blk("b_ks"), blk("b_kw"), row, row, const, const, const],
        out_specs=(pl.BlockSpec((1, B_HEADS, tm, LANES), lambda bi, i: (bi, 0, i, 0)), row, row),
        compiler_params=_cp("parallel", "parallel"),
        name="nsa_prep",
    )(u, u, u, cos64, sin64, p["q_g"], p["ks_g"], p["kw_g"])


def _nsa_cmp_kernel(kc_in, vc_in, pek_ref, pev_ref, w1k_ref, w1v_ref, w2k_ref, w2v_ref, kg_ref,
                    cos_ref, sin_ref, ko_ref, vo_ref, pad_ref):
    s = kc_in.shape[1]
    n_pad = ko_ref.shape[1]

    def compress(x_in, pe_ref, w1_ref, w2_ref):
        pad_ref[0:s, :] = x_in[0].astype(F32)
        pad_ref[s:s + CMP_STRIDE, :] = jnp.zeros((CMP_STRIDE, LANES), F32)
        acc = jnp.zeros((n_pad, LANES), F32)
        for l in range(CMP_LEN):
            xl = pad_ref[pl.ds(l, n_pad, stride=CMP_STRIDE), :] + pe_ref[l:l + 1, :]
            acc = acc + _mm(xl.astype(BF16), w1_ref[l])
        mid = acc * _sigmoid(acc)
        return _mm(mid.astype(BF16), w2_ref[...])

    kc = compress(kc_in, pek_ref, w1k_ref, w2k_ref)
    ko_ref[0] = _rope(_group_rms(kc, B_HD) * kg_ref[...], cos_ref[0], sin_ref[0], B_HD // 2).astype(BF16)
    vo_ref[0] = compress(vc_in, pev_ref, w1v_ref, w2v_ref).astype(BF16)


def _nsa_cmp(u, cols, cosc, sinc, p):
    b, s, _ = u.shape
    n_pad = s // CMP_STRIDE
    blk = lambda name: pl.BlockSpec((1, s, LANES), lambda bi, c=cols[name] // LANES: (bi, 0, c))
    const = lambda shape: pl.BlockSpec(shape, lambda bi: (0,) * len(shape))
    out = jax.ShapeDtypeStruct((b, n_pad, LANES), BF16)
    ospec = pl.BlockSpec((1, n_pad, LANES), lambda bi: (bi, 0, 0))
    return pl.pallas_call(
        _nsa_cmp_kernel,
        out_shape=(out, out),
        grid=(b,),
        in_specs=[blk("b_kc"), blk("b_vc"), const((CMP_LEN, LANES)), const((CMP_LEN, LANES)),
                  const((CMP_LEN, LANES, LANES)), const((CMP_LEN, LANES, LANES)),
                  const((LANES, LANES)), const((LANES, LANES)), const((1, LANES)), ospec, ospec],
        out_specs=(ospec, ospec),
        scratch_shapes=[pltpu.VMEM((s + CMP_STRIDE, LANES), F32)],
        compiler_params=_cp("parallel"),
        name="nsa_cmp",
    )(u, u, p["pe_k"], p["pe_v"], p["w1k"], p["w1v"], p["w2k"], p["w2v"], p["kc_g"], cosc, sinc)


def _nsa_attn_kernel(q_ref, kc_ref, vc_ref, ks_ref, vs_ref, kw_ref, vw_ref, g_ref, e_ref, ovl_ref, o_ref,
                     *, n_blk, n_sel, tks, ww):
    tq = q_ref.shape[2]
    n_pad = kc_ref.shape[1]
    i = pl.program_id(1)
    t = i * tq + lax.broadcasted_iota(I32, (1, tq), 1)
    lane = _lane((tq, LANES))
    gates_t = _sigmoid(g_ref[0]).T
    rep = B_HEADS // B_KV_HEADS
    heads = lambda x: jnp.concatenate([x] * rep, axis=1)
    ncmp = lax.broadcasted_iota(I32, (n_pad, 1), 0)
    valid_c = heads((ncmp * CMP_STRIDE + (CMP_LEN - 1)) <= t)
    kc = kc_ref[0]
    vc = vc_ref[0]
    blk = lax.broadcasted_iota(I32, (LANES, 1), 0)
    cur = t >> (SEL_LEN.bit_length() - 1)
    forced = jnp.where(blk == cur, 3e4, jnp.where(blk == cur - 1, 2e4, jnp.where(blk == 0, 1e4, 0.0)))
    adm = (blk * SEL_LEN <= t) & (blk < n_blk)
    n_rv = n_blk // 8
    sub = lax.broadcasted_iota(I32, (8, tq), 0)
    w0 = pl.multiple_of(jnp.maximum(i * tq + tq - ww, 0), tq)
    kp_w = w0 + lax.broadcasted_iota(I32, (ww, 1), 0)
    win_bias = heads(jnp.where((kp_w <= t) & (kp_w > t - WINDOW), 0.0, NEG))
    for g in range(B_KV_HEADS):
        qs = jnp.concatenate([q_ref[0, rep * g + r] for r in range(rep)], axis=0)
        gate = lambda c: jnp.concatenate(
            [gates_t[(rep * g + r) * 3 + c:(rep * g + r) * 3 + c + 1, :] for r in range(rep)], axis=1)
        s = jnp.where(valid_c, _nt(kc, qs), NEG)
        ex = jnp.exp2(s - jnp.max(s, axis=0, keepdims=True))
        pc = jnp.where(valid_c, ex / jnp.sum(ex, axis=0, keepdims=True), 0.0)
        out = gate(0) * _tn(vc, pc.astype(BF16))
        psum = pc[:, 0:tq]
        for r in range(1, rep):
            psum = psum + pc[:, r * tq:(r + 1) * tq]
        p_hi = psum.astype(BF16)
        p_lo = (psum - p_hi.astype(F32)).astype(BF16)
        imp = _mm(ovl_ref[...], p_hi) + _mm(ovl_ref[...], p_lo)
        score_t = jnp.where(adm, imp + forced, NEG)
        sc = [score_t[8 * v:8 * v + 8] for v in range(n_rv)]
        rank = [jnp.zeros((8, tq), F32) for _ in range(n_rv)]
        for jp in range(n_blk):
            col = score_t[jp:jp + 1]
            for v in range(n_rv):
                if v > jp // 8:
                    beats = col >= sc[v]
                elif v < jp // 8:
                    beats = col > sc[v]
                else:
                    beats = (col > sc[v]) | ((col == sc[v]) & (sub > jp % 8))
                rank[v] = rank[v] + jnp.where(beats, 1.0, 0.0)
        sel_t = jnp.where(jnp.concatenate(rank, axis=0) < n_sel, 1.0, 0.0).astype(BF16)

        tile = lambda j: pl.ds(pl.multiple_of(j * tks, tks), tks)

        def sel_body(j, carry):
            kp = j * tks + lax.broadcasted_iota(I32, (tks, 1), 0)
            hit = _mm(e_ref[tile(j), :], sel_t)
            bias = jnp.where((hit > 0.5) & (kp <= t), 0.0, NEG)
            return _softmax_step_t(qs, ks_ref[0, tile(j), :], vs_ref[0, tile(j), :], heads(bias), carry)

        n_tile = ((i + 1) * tq + tks - 1) // tks
        _, l_s, acc_s = lax.fori_loop(0, n_tile, sel_body, _flash_init_t(rep * tq))
        out = out + gate(1) * (acc_s / l_s)
        _, l_w, acc_w = _softmax_step_t(qs, kw_ref[0, pl.ds(w0, ww), :], vw_ref[0, pl.ds(w0, ww), :],
                                        win_bias, _flash_init_t(rep * tq))
        out = out + gate(2) * (acc_w / l_w)
        outs = [out[:, r * tq:(r + 1) * tq].T for r in range(rep)]
        for pp in range(rep // 2):
            a, bb = outs[2 * pp], outs[2 * pp + 1]
            if g == 0:
                bb = pltpu.roll(bb, B_HD, 1)
            else:
                a = pltpu.roll(a, B_HD, 1)
            c0 = (rep // 2 * g + pp) * LANES
            o_ref[0, :, c0:c0 + LANES] = jnp.where(lane < B_HD, a, bb).astype(BF16)


def _nsa_attn(u, u32, cols, qx, kcmp, vcmp, ks, kw):
    b, h, s, _ = qx.shape
    tq = min(256, s)
    tks = min(512, s)
    n_pad = kcmp.shape[1]
    n_blk = s // SEL_LEN
    n_sel = min(N_SEL, n_blk)
    n_cmp = (s - CMP_LEN) // CMP_STRIDE + 1
    ww = min(WINDOW + tq, s)
    expand = np.zeros((s, n_blk), np.float32)
    expand[np.arange(s), np.arange(s) // SEL_LEN] = 1.0
    nn = np.arange(n_pad)[None, :]
    jj = np.arange(LANES)[:, None]
    ovl = ((nn * CMP_STRIDE <= jj * SEL_LEN + SEL_LEN - 1) & (nn * CMP_STRIDE + CMP_LEN - 1 >= jj * SEL_LEN)
           & (jj < n_blk) & (nn < n_cmp)).astype(np.float32)
    full = lambda shape: pl.BlockSpec(shape, lambda bi, i: (bi,) + (0,) * (len(shape) - 1))
    ucol = lambda name: pl.BlockSpec((1, s, LANES), lambda bi, i, c=cols[name] // LANES: (bi, 0, c))
    const = lambda shape: pl.BlockSpec(shape, lambda bi, i: (0,) * len(shape))
    return pl.pallas_call(
        functools.partial(_nsa_attn_kernel, n_blk=n_blk, n_sel=n_sel, tks=tks, ww=ww),
        out_shape=jax.ShapeDtypeStruct((b, s, h * B_HD), BF16),
        grid=(b, s // tq),
        in_specs=[pl.BlockSpec((1, h, tq, LANES), lambda bi, i: (bi, 0, i, 0)),
                  full((1, n_pad, LANES)), full((1, n_pad, LANES)),
                  full((1, s, LANES)), ucol("b_vs"), full((1, s, LANES)), ucol("b_vw"),
                  pl.BlockSpec((1, tq, LANES), lambda bi, i: (bi, i, 0)),
                  const((s, n_blk)), const((LANES, n_pad))],
        out_specs=pl.BlockSpec((1, tq, h * B_HD), lambda bi, i: (bi, i, 0)),
        compiler_params=_cp("parallel", "parallel"),
        name="nsa_attn",
    )(qx, kcmp, vcmp, ks, u, kw, u, u32, jnp.asarray(expand, BF16), jnp.asarray(ovl, BF16))


def _dsa_prep_kernel(q_in, iq_in, kv_in, sm_in, cos64_ref, sin64_ref, cos32_ref, sin32_ref, qg_ref, kg_ref,
                     q_ref, kv_ref, iq_ref, ik_ref):
    tm = q_in.shape[1]
    lane = _lane((tm, LANES))
    c64, s64, c32, s32 = cos64_ref[0], sin64_ref[0], cos32_ref[0], sin32_ref[0]
    for p in range(C_HEADS // 2):
        y = _rope(_group_rms(q_in[0, :, p * LANES:(p + 1) * LANES].astype(F32), C_HD) * qg_ref[...], c64, s64, C_HD // 2)
        y = y * (C_HD ** -0.5 * LOG2E)
        q_ref[0, 2 * p] = jnp.where(lane < C_HD, y, 0.0).astype(BF16)
        q_ref[0, 2 * p + 1] = jnp.where(lane < C_HD, pltpu.roll(y, C_HD, 1), 0.0).astype(BF16)
    kv = kv_in[0].astype(F32)
    kn = _rope(_group_rms(kv, C_HD) * kg_ref[...], c64, s64, C_HD // 2)
    kv_ref[0] = jnp.where(lane < C_HD, kn, kv).astype(BF16)
    for c0 in range(0, IDX_HEADS * IDX_HD, LANES):
        iq_ref[0, :, c0:c0 + LANES] = _rope(iq_in[0, :, c0:c0 + LANES].astype(F32), c32, s32, IDX_HD // 2).astype(BF16)
    ik = jnp.where(lane < IDX_HD, _rope(sm_in[0], c32, s32, IDX_HD // 2), 0.0)
    ik = ik + pltpu.roll(ik, IDX_HD, 1)
    ik = ik + pltpu.roll(ik, 2 * IDX_HD, 1)
    ik_ref[0] = ik.astype(BF16)


def _dsa_prep(u, u32, cols, tabs, p):
    b, s, _ = u.shape
    tm = min(256, s)
    row = pl.BlockSpec((1, tm, LANES), lambda bi, i: (bi, i, 0))
    const = pl.BlockSpec((1, LANES), lambda bi, i: (0, 0))
    ublk = lambda name, w: pl.BlockSpec((1, tm, w), lambda bi, i, c=cols[name] // w: (bi, i, c))
    dense = jax.ShapeDtypeStruct((b, s, LANES), BF16)
    return pl.pallas_call(
        _dsa_prep_kernel,
        out_shape=(jax.ShapeDtypeStruct((b, C_HEADS, s, LANES), BF16), dense,
                   jax.ShapeDtypeStruct((b, s, IDX_HEADS * IDX_HD), BF16), dense),
        grid=(b, s // tm),
        in_specs=[ublk("c_q", 512), ublk("c_iq", 256), ublk("c_kv", LANES), row,
                  row, row, row, row, const, const],
        out_specs=(pl.BlockSpec((1, C_HEADS, tm, LANES), lambda bi, i: (bi, 0, i, 0)), row,
                   pl.BlockSpec((1, tm, IDX_HEADS * IDX_HD), lambda bi, i: (bi, i, 0)), row),
        compiler_params=_cp("parallel", "parallel"),
        name="dsa_prep",
    )(u, u, u, u32, tabs["cos64"], tabs["sin64"], tabs["cos32"], tabs["sin32"], p["q_g"], p["k_g"])


def _dsa_attn_kernel(q_ref, iq_ref, sm_ref, ik_ref, kv_ref, tri_ref, o_ref, key_scr, bias_scr, *, tk, topk):
    tq = q_ref.shape[2]
    i = pl.program_id(1)
    t = i * tq + lax.broadcasted_iota(I32, (1, tq), 1)
    lane = _lane((tq, LANES))
    n_tile = ((i + 1) * tq + tk - 1) // tk
    tile = lambda j: pl.ds(pl.multiple_of(j * tk, tk), tk)
    kpos = lambda j: j * tk + lax.broadcasted_iota(I32, (tk, 1), 0)

    iw_t = (sm_ref[0] * (IDX_HEADS ** -0.5)).T
    parts = []
    for h in range(IDX_HEADS):
        blk = iq_ref[0, :, (h // 4) * LANES:(h // 4 + 1) * LANES]
        lo = (h % 4) * IDX_HD
        parts.append(jnp.where((lane >= lo) & (lane < lo + IDX_HD), blk, jnp.zeros_like(blk)))
    iqs = jnp.concatenate(parts, axis=0)

    def idx_body(j, carry):
        lg = _nt(ik_ref[0, tile(j), :], iqs)
        acc = jnp.zeros((tk, tq), F32)
        for h in range(IDX_HEADS):
            acc = acc + iw_t[IDX_HD + h:IDX_HD + h + 1, :] * jnp.maximum(lg[:, h * tq:(h + 1) * tq], 0.0)
        sc = jnp.where(kpos(j) <= t, acc, NEG)
        bits = pltpu.bitcast(sc, I32)
        key = jnp.where(bits < 0, bits ^ 0x7FFFFFFF, bits)
        key_scr[tile(j), :] = jnp.where(sc == 0.0, 0, key)
        return carry

    lax.fori_loop(0, n_tile, idx_body, 0)

    def count_ge(thr_key):
        def body(j, cs):
            key = key_scr[tile(j), :]
            cs = list(cs)
            for n, r0 in enumerate(range(0, tk, 8)):
                c = cs[n % len(cs)]
                cs[n % len(cs)] = jnp.where(key[r0:r0 + 8] >= thr_key, c + 1, c)
            return tuple(cs)

        cs = lax.fori_loop(0, n_tile, body, tuple(jnp.zeros((8, tq), I32) for _ in range(4)))
        return jnp.sum(cs[0] + cs[1] + cs[2] + cs[3], axis=0, keepdims=True)

    def bit_body(bi, ucand):
        utrial = ucand | jnp.left_shift(jnp.int32(1), 31 - bi)
        return jnp.where(count_ge(utrial ^ INT_MIN) >= topk, utrial, ucand)

    thr = lax.fori_loop(0, 32, bit_body, jnp.zeros((1, tq), I32)) ^ INT_MIN
    n_ge = count_ge(thr)
    n_gt = count_ge(thr + 1)
    need = topk - n_gt
    row_ok = (n_ge - n_gt == need) | (thr == NEG_KEY) | (n_ge < topk)
    simple = jnp.min(jnp.where(row_ok, 1.0, 0.0)) > 0.5

    def fast_bias():
        def body(j, carry):
            keep = (key_scr[tile(j), :] >= thr) & (kpos(j) <= t)
            bias_scr[tile(j), :] = jnp.where(keep, 0.0, NEG)
            return carry

        lax.fori_loop(0, n_tile, body, 0)

    def tie_bias():
        need_f = need.astype(F32)

        def body(j, run):
            key = key_scr[tile(j), :]
            kp = kpos(j)
            for c0 in range(0, tk, LANES):
                kc = key[c0:c0 + LANES]
                eq = kc == thr
                eq_f = jnp.where(eq, 1.0, 0.0)
                pref = _mm(tri_ref[...], eq_f.astype(BF16)) + run
                keep = ((kc > thr) | (eq & (pref <= need_f))) & (kp[c0:c0 + LANES] <= t)
                bias_scr[pl.ds(pl.multiple_of(j * tk + c0, LANES), LANES), :] = jnp.where(keep, 0.0, NEG)
                run = run + jnp.sum(eq_f, axis=0, keepdims=True)
            return run

        lax.fori_loop(0, n_tile, body, jnp.zeros((1, tq), F32))

    lax.cond(simple, fast_bias, tie_bias)

    qs = jnp.concatenate([q_ref[0, h] for h in range(C_HEADS)], axis=0)

    def att_body(j, carry):
        kv = kv_ref[0, tile(j), :]
        bias_t = bias_scr[tile(j), :]
        return _softmax_step_t(qs, kv, kv, jnp.concatenate([bias_t] * C_HEADS, axis=1), carry)

    _, l, acc = lax.fori_loop(0, n_tile, att_body, _flash_init_t(C_HEADS * tq))
    o_t = acc / l
    for p in range(C_HEADS // 2):
        a = pltpu.roll(o_t[:, (2 * p) * tq:(2 * p + 1) * tq].T, C_HD, 1)
        bb = o_t[:, (2 * p + 1) * tq:(2 * p + 2) * tq].T
        o_ref[0, :, p * LANES:(p + 1) * LANES] = jnp.where(lane < C_HD, a, bb).astype(BF16)


def _dsa_attn(u32, qc, iq, ik, kv):
    b, h, s, _ = qc.shape
    tq = min(128, s)
    tk = min(1024, s)
    topk = min(TOPK_MAX, s // 4)
    tri = np.tril(np.ones((LANES, LANES), np.float32))
    full = lambda shape: pl.BlockSpec(shape, lambda bi, i: (bi,) + (0,) * (len(shape) - 1))
    return pl.pallas_call(
        functools.partial(_dsa_attn_kernel, tk=tk, topk=topk),
        out_shape=jax.ShapeDtypeStruct((b, s, h * C_HD), BF16),
        grid=(b, s // tq),
        in_specs=[pl.BlockSpec((1, h, tq, LANES), lambda bi, i: (bi, 0, i, 0)),
                  pl.BlockSpec((1, tq, IDX_HEADS * IDX_HD), lambda bi, i: (bi, i, 0)),
                  pl.BlockSpec((1, tq, LANES), lambda bi, i: (bi, i, 0)),
                  full((1, s, LANES)), full((1, s, LANES)),
                  pl.BlockSpec((LANES, LANES), lambda bi, i: (0, 0))],
        out_specs=pl.BlockSpec((1, tq, h * C_HD), lambda bi, i: (bi, i, 0)),
        scratch_shapes=[pltpu.VMEM((s, tq), I32), pltpu.VMEM((s, tq), F32)],
        compiler_params=_cp("parallel", "parallel"),
        name="dsa_attn",
    )(qc, iq, u32, ik, kv, jnp.asarray(tri, BF16))


def _mlstm_conv_kernel(x_ref, halo_ref, w_ref, b_ref, o_ref):
    tm = x_ref.shape[1]
    i = pl.program_id(1)
    lane = _lane((tm, x_ref.shape[2]))
    halo = jnp.where(i > 0, halo_ref[0].astype(F32), 0.0)
    x = x_ref[0].astype(F32)
    xc = jnp.concatenate([halo, x], axis=0)
    off = halo.shape[0] - (CONV_W - 1)
    y = b_ref[...] + jnp.zeros_like(x)
    for j in range(CONV_W):
        y = y + w_ref[j:j + 1, :] * xc[off + j:off + j + tm, :]
    y = y * _sigmoid(y)
    o_ref[0] = jnp.where(lane >= D_HEADS * D_QK, y * (D_QK ** -0.5), y)


def _mlstm_conv(u, qk_block, w, bias):
    b, s, _ = u.shape
    c = w.shape[1]
    tm = min(512, s)
    hb = 16
    return pl.pallas_call(
        _mlstm_conv_kernel,
        out_shape=jax.ShapeDtypeStruct((b, s, c), F32),
        grid=(b, s // tm),
        in_specs=[pl.BlockSpec((1, tm, c), lambda bi, i: (bi, i, qk_block)),
                  pl.BlockSpec((1, hb, c), lambda bi, i: (bi, jnp.maximum(i * (tm // hb) - 1, 0), qk_block)),
                  pl.BlockSpec((CONV_W, c), lambda bi, i: (0, 0)),
                  pl.BlockSpec((1, c), lambda bi, i: (0, 0))],
        out_specs=pl.BlockSpec((1, tm, c), lambda bi, i: (bi, i, 0)),
        compiler_params=_cp("parallel", "parallel"),
        name="mlstm_conv",
    )(u, u, w, bias)


def _mlstm_scan_kernel(qk_ref, v_ref, op_ref, sm_ref, gt_ref, bcol_ref, brow_ref, hg_ref, o_ref,
                       c_scr, n_scr, m_scr):
    tc = qk_ref.shape[1]
    L = CHUNK
    nqk = D_HEADS * D_QK

    @pl.when(pl.program_id(1) == 0)
    def _():
        c_scr[...] = jnp.zeros_like(c_scr)
        n_scr[...] = jnp.zeros_like(n_scr)
        m_scr[...] = jnp.zeros_like(m_scr)

    r = lax.broadcasted_iota(I32, (L, L), 0)
    c = lax.broadcasted_iota(I32, (L, L), 1)
    tril = c <= r
    tril_f = jnp.where(tril, 1.0, 0.0).astype(BF16)
    triu_f = jnp.where(r <= c, 1.0, 0.0).astype(BF16)
    lane = _lane((L, LANES))
    row128 = lax.broadcasted_iota(I32, (LANES, LANES), 0)
    i_lane, f_lane = IDX_HD + IDX_HEADS, IDX_HD + IDX_HEADS + D_HEADS

    def one(bb, ci):
        rows = pl.ds(pl.multiple_of(ci * L, L), L)
        qk = qk_ref[bb, rows, :]
        sm = sm_ref[bb, rows, :] + bcol_ref[...]
        bcol_all = sum(_mm(tril_f, part) for part in _split3(_log_sigmoid(sm)))
        gt = gt_ref[bb, ci] + brow_ref[...]
        brow_all = sum(_mm(part, triu_f) for part in _split3(_log_sigmoid(gt)))
        for p in range(D_HEADS // 2):
            qpair = qk[:, p * LANES:(p + 1) * LANES]
            kpair = qk[:, nqk + p * LANES:nqk + (p + 1) * LANES]
            kpair_b = kpair.astype(BF16)
            sp = bb * (D_HEADS // 2) + p
            c_prev = c_scr[sp]
            n_prev = n_scr[sp, 0:1, :]
            c_prev_b = c_prev.astype(BF16)
            upd = []
            for e in range(2):
                h = 2 * p + e
                in_half = (lane >= D_QK * e) & (lane < D_QK * (e + 1))
                qm = jnp.where(in_half, qpair, 0.0)
                qm_b = qm.astype(BF16)
                vh = v_ref[bb, rows, h * D_VD:(h + 1) * D_VD].astype(F32)
                b_col = bcol_all[:, f_lane + h:f_lane + h + 1]
                i_col = sm[:, i_lane + h:i_lane + h + 1]
                b_row = brow_all[D_HEADS + h:D_HEADS + h + 1, :]
                i_row = gt[h:h + 1, :]
                m_prev = m_scr[bb * D_HEADS + h, 0:1, 0:1]
                dmat = jnp.where(tril, b_col - b_row + i_row, NEG)
                inter = b_col + m_prev
                m_t = jnp.maximum(inter, jnp.max(dmat, axis=-1, keepdims=True))
                a = jnp.exp(inter - m_t)
                w = _nt(qm_b, kpair_b) * jnp.exp(dmat - m_t)
                num = a * _mm(qm_b, c_prev_b) + _mm(w.astype(BF16), vh.astype(BF16))
                den = a * jnp.sum(qm * n_prev, axis=-1, keepdims=True) + jnp.sum(w, axis=-1, keepdims=True)
                hout = num / jnp.maximum(jnp.abs(den), jnp.exp(-m_t))
                b_last = b_col[L - 1:L, :]
                g_col = b_last - b_col + i_col
                m_new = jnp.maximum(b_last + m_prev, jnp.max(g_col, axis=0, keepdims=True))
                ws = jnp.exp(g_col - m_new)
                decay = jnp.exp(b_last + m_prev - m_new)
                u_mat = _tn(kpair_b, (ws * vh).astype(BF16))
                k_sum = jnp.sum(ws * kpair, axis=0, keepdims=True)
                upd.append((decay * c_prev + u_mat, decay * n_prev + k_sum))
                m_scr[bb * D_HEADS + h] = jnp.broadcast_to(m_new, m_scr.shape[1:])
                hn = _row_rms(hout) * hg_ref[...]
                y = _sigmoid(op_ref[bb, rows, h * D_VD:(h + 1) * D_VD].astype(F32)) * hn
                o_ref[bb, rows, h * D_VD:(h + 1) * D_VD] = y.astype(BF16)
            c_scr[sp] = jnp.where(row128 < D_QK, upd[0][0], upd[1][0])
            n_new = jnp.where(lane[0:1] < D_QK, upd[0][1], upd[1][1])
            n_scr[sp] = jnp.broadcast_to(n_new, n_scr.shape[1:])

    def chunk(ci, carry):
        for bb in range(qk_ref.shape[0]):
            one(bb, ci)
        return carry

    lax.fori_loop(0, tc // L, chunk, 0)


def _mlstm_scan(u, u32, cols, qk, gt, p):
    b, s, _ = u.shape
    tc = min(512, s)
    nb = next(n for n in (4, 2, 1) if b % n == 0)
    nv = D_HEADS * D_VD
    ublk = lambda name, w: pl.BlockSpec((nb, tc, w), lambda bi, i, c=cols[name] // w: (bi, i, c))
    const = lambda shape: pl.BlockSpec(shape, lambda bi, i: (0,) * len(shape))
    return pl.pallas_call(
        _mlstm_scan_kernel,
        out_shape=jax.ShapeDtypeStruct((b, s, nv), BF16),
        grid=(b // nb, s // tc),
        in_specs=[pl.BlockSpec((nb, tc, qk.shape[-1]), lambda bi, i: (bi, i, 0)),
                  ublk("d_v", nv), ublk("d_o", nv), pl.BlockSpec((nb, tc, LANES), lambda bi, i: (bi, i, 0)),
                  pl.BlockSpec((nb, tc // CHUNK, 8, CHUNK), lambda bi, i: (bi, i, 0, 0)),
                  const((1, LANES)), const((8, 1)), const((1, D_VD))],
        out_specs=pl.BlockSpec((nb, tc, nv), lambda bi, i: (bi, i, 0)),
        scratch_shapes=[pltpu.VMEM((nb * D_HEADS // 2, LANES, LANES), F32),
                        pltpu.VMEM((nb * D_HEADS // 2, 8, LANES), F32),
                        pltpu.VMEM((nb * D_HEADS, 8, LANES), F32)],
        compiler_params=_cp("parallel", "arbitrary"),
        name="mlstm_scan",
    )(qk, u, u, u32, gt, p["bias_col"], p["bias_row"], p["h_g"])


EVEN_SRC = dict(a_ql=(0, 256), a_kvl=(256, 128), a_kr=(384, 32), a_gate=(416, 512), b_q=(928, 512),
                b_kc=(1440, 128), b_vc=(1568, 128), b_ks=(1696, 128), b_vs=(1824, 128), b_kw=(1952, 128),
                b_vw=(2080, 128), b_g=(2208, 24), b_gate=(2232, 512), m_q=(2744, 256), m_gate=(3000, 256))
EVEN_DST = dict(a_ql=0, a_kvl=256, a_kr=384, b_q=512, a_gate=1024, b_gate=1536, m_q=2048, m_gate=2304,
                b_g=2560, b_kc=2688, b_vc=2816, b_ks=2944, b_vs=3072, b_kw=3200, b_vw=3328)
EVEN_COLS_PAD = 3456

ODD_SRC = dict(c_q=(0, 512), c_k=(512, 64), c_v=(576, 64), c_iq=(640, 256), c_ik=(896, 32), c_iw=(928, 8),
               c_gate=(936, 512), d_q=(1448, 256), d_k=(1704, 256), d_v=(1960, 512), d_i=(2472, 4),
               d_f=(2476, 4), d_o=(2480, 512), d_gate=(2992, 512), m_q=(3504, 256), m_gate=(3760, 256))
ODD_DST = dict(c_q=0, c_gate=512, d_gate=1024, d_v=1536, d_o=2048, d_q=2560, d_k=2816, m_q=3072, m_gate=3328,
               c_iq=3584, c_k=3840, c_v=3904, c_ik=3968, c_iw=4000, d_i=4008, d_f=4012)
ODD_COLS_PAD = 4096


def _permute_cols(w, src, dst, total):
    pieces, pos = [], 0
    for name in sorted(src, key=lambda n: dst[n]):
        start, width = src[name]
        if dst[name] > pos:
            pieces.append(jnp.zeros((w.shape[0], dst[name] - pos), BF16))
        pieces.append(w[:, start:start + width].astype(BF16))
        pos = dst[name] + width
    if total > pos:
        pieces.append(jnp.zeros((w.shape[0], total - pos), BF16))
    return jnp.concatenate(pieces, axis=1)


def _heads_split(w, heads, first):
    w3 = w.reshape(w.shape[0], heads, -1)
    return jnp.concatenate([w3[:, :, :first].reshape(w.shape[0], -1),
                            w3[:, :, first:].reshape(w.shape[0], -1)], axis=1).astype(BF16)


def _tile_lanes(v, reps):
    return jnp.tile(v.astype(F32).reshape(1, -1), (1, reps))


def _rope_tables(positions, d2):
    inv = ROPE_THETA ** (-jnp.arange(d2, dtype=F32) / d2)
    ang = positions.astype(F32)[..., None] * inv
    c, s = jnp.cos(ang), jnp.sin(ang)
    reps = LANES // (2 * d2)
    return (jnp.tile(jnp.concatenate([c, c], axis=-1), (1, 1, reps)),
            jnp.tile(jnp.concatenate([-s, s], axis=-1), (1, 1, reps)))


def _block_diag2(w):
    z = jnp.zeros_like(w)
    return jnp.concatenate([jnp.concatenate([w, z], axis=-1), jnp.concatenate([z, w], axis=-1)], axis=-2)


def kernel(x, mem, positions, ln_g, mem_norm_g, mem_w_kv, mem_q_norm_g, mem_k_norm_g, w_out, even_w_in, mla_q_lat_g, mla_kv_lat_g, mla_w_uq, mla_w_ukv, mla_q_norm_g, mla_k_norm_g, nsa_q_norm_g, nsa_k_norm_g, nsa_cmp_pos, nsa_cmp_w1, nsa_cmp_w2, odd_w_in, dsa_q_norm_g, dsa_k_norm_g, mlstm_conv_w, mlstm_conv_b, mlstm_i_bias, mlstm_f_bias, mlstm_h_norm_g):
    b, s, _ = x.shape
    depth = ln_g.shape[0]
    cos64, sin64 = _rope_tables(positions, 32)
    cos32, sin32 = _rope_tables(positions, 16)
    tabs = dict(cos64=cos64, sin64=sin64, cos32=cos32, sin32=sin32)
    n_pad = s // CMP_STRIDE
    cmp_pos = jnp.pad(positions[:, CMP_LEN - 1::CMP_STRIDE], ((0, 0), (0, 0)))[:, :n_pad]
    cmp_pos = jnp.pad(cmp_pos, ((0, 0), (0, n_pad - cmp_pos.shape[1])))
    cosc, sinc = _rope_tables(cmp_pos, 32)

    mem_k, mem_v = _mem_kv(mem, mem_norm_g.reshape(depth, 1, -1), mem_w_kv.astype(BF16),
                           jnp.tile(mem_k_norm_g, (1, 2)).reshape(depth, 1, LANES))

    for layer in range(depth):
        li = layer // 2
        g_ln = ln_g[layer].reshape(1, -1)
        mq_g = _tile_lanes(mem_q_norm_g[layer], 2)
        if layer % 2 == 0:
            cols = EVEN_DST
            u, u32 = _in_proj(x, g_ln, _permute_cols(even_w_in[li], EVEN_SRC, EVEN_DST, EVEN_COLS_PAD), cols["b_g"])
            pa = dict(q_lat_g=mla_q_lat_g[li].reshape(1, -1), kv_lat_g=mla_kv_lat_g[li].reshape(1, -1),
                      w_uq=_heads_split(mla_w_uq[li], A_HEADS, A_NOPE), w_ukv=_heads_split(mla_w_ukv[li], A_HEADS, A_NOPE),
                      qn_g=_tile_lanes(mla_q_norm_g[li, :A_NOPE], 2), qr_g=_tile_lanes(mla_q_norm_g[li, A_NOPE:], 4),
                      kn_g=_tile_lanes(mla_k_norm_g[li, :A_NOPE], 2), kr_g=_tile_lanes(mla_k_norm_g[li, A_NOPE:], 4))
            qa, ka, va = _mla_prep(u, cos32, sin32, pa)
            y1 = _mla_attn(qa, ka, va)
            pb = dict(q_g=_tile_lanes(nsa_q_norm_g[li], 2), ks_g=_tile_lanes(nsa_k_norm_g[li, 1], 2),
                      kw_g=_tile_lanes(nsa_k_norm_g[li, 2], 2), kc_g=_tile_lanes(nsa_k_norm_g[li, 0], 2),
                      pe_k=jnp.tile(nsa_cmp_pos[li, 0], (1, 2)), pe_v=jnp.tile(nsa_cmp_pos[li, 1], (1, 2)),
                      w1k=_block_diag2(nsa_cmp_w1[li, 0].reshape(CMP_LEN, B_HD, B_HD)).astype(BF16),
                      w1v=_block_diag2(nsa_cmp_w1[li, 1].reshape(CMP_LEN, B_HD, B_HD)).astype(BF16),
                      w2k=_block_diag2(nsa_cmp_w2[li, 0]).astype(BF16),
                      w2v=_block_diag2(nsa_cmp_w2[li, 1]).astype(BF16))
            qb, ks, kw = _nsa_prep(u, cols, cos64, sin64, pb)
            kcmp, vcmp = _nsa_cmp(u, cols, cosc, sinc, pb)
            y2 = _nsa_attn(u, u32, cols, qb, kcmp, vcmp, ks, kw)
            gate_blocks = (cols["a_gate"] // 512, cols["b_gate"] // 512, cols["m_gate"] // 256)
        else:
            cols = dict(ODD_DST, c_kv=ODD_DST["c_k"], small=ODD_DST["c_ik"])
            u, u32 = _in_proj(x, g_ln, _permute_cols(odd_w_in[li], ODD_SRC, ODD_DST, ODD_COLS_PAD), cols["small"])
            pc = dict(q_g=_tile_lanes(dsa_q_norm_g[li], 2), k_g=_tile_lanes(dsa_k_norm_g[li], 2))
            qc, kvc, iq, ik = _dsa_prep(u, u32, cols, tabs, pc)
            y1 = _dsa_attn(u32, qc, iq, ik, kvc)
            qk = _mlstm_conv(u, cols["d_q"] // 512, mlstm_conv_w[li], mlstm_conv_b[li].reshape(1, -1))
            gates = u32[:, :, cols["d_i"] - cols["small"]:cols["d_i"] - cols["small"] + 2 * D_HEADS]
            gt = gates.reshape(b, s // CHUNK, CHUNK, 2 * D_HEADS).transpose(0, 1, 3, 2)
            bias8 = jnp.concatenate([mlstm_i_bias[li], mlstm_f_bias[li]]).astype(F32)
            bias_col = jnp.zeros((1, LANES), F32).at[0, cols["d_i"] - cols["small"]:cols["d_i"] - cols["small"] + 8].set(bias8)
            pd = dict(bias_col=bias_col, bias_row=bias8.reshape(8, 1), h_g=mlstm_h_norm_g[li].reshape(1, -1))
            y2 = _mlstm_scan(u, u32, cols, qk, gt, pd)
            gate_blocks = (cols["c_gate"] // 512, cols["d_gate"] // 512, cols["m_gate"] // 256)
        ym = _mem_attn(u, cols["m_q"] // 256, mem_k, mem_v, layer, mq_g)
        x = _out_proj(x, y1, y2, ym, u, gate_blocks, w_out[layer].astype(BF16))
    return x
```

```python
import functools

import numpy as np
import jax
import jax.numpy as jnp
from jax import lax
from jax.experimental import pallas as pl
from jax.experimental.pallas import tpu as pltpu

F32, BF16, I32 = jnp.float32, jnp.bfloat16, jnp.int32
NEG = -1e30
EPS = 1e-6
ROPE_THETA = 10000.0
LANES = 128
VMEM_LIMIT_BYTES = 48 * 1024 * 1024

D_MODEL = 1024
DEPTH = 4
A_HEADS, A_NOPE, A_ROPE, A_VD, A_QLAT, A_KVLAT = 8, 64, 32, 64, 256, 128
B_HEADS, B_KV_HEADS, B_HD = 8, 2, 64
CMP_LEN, CMP_STRIDE, SEL_LEN, N_SEL, WINDOW = 32, 16, 64, 16, 512
C_HEADS, C_HD, IDX_HEADS, IDX_HD, TOPK_MAX = 8, 64, 8, 32, 256
D_HEADS, D_QK, D_VD, CONV_W, CHUNK = 4, 64, 128, 4, 64
M_HEADS, M_HD = 4, 64

INT_MIN = np.int32(-2 ** 31)
NEG_KEY = int(np.float32(NEG).view(np.int32) ^ np.int32(0x7FFFFFFF))
LOG2E = float(np.log2(np.e))


def _cp(*sem):
    return pltpu.CompilerParams(dimension_semantics=sem, vmem_limit_bytes=VMEM_LIMIT_BYTES)


def _nt(a, b):
    return lax.dot_general(a, b, (((1,), (1,)), ((), ())), preferred_element_type=F32)


def _tn(a, b):
    return lax.dot_general(a, b, (((0,), (0,)), ((), ())), preferred_element_type=F32)


def _mm(a, b):
    return jnp.dot(a, b, preferred_element_type=F32)


def _sigmoid(x):
    return 1.0 / (1.0 + jnp.exp(-x))


def _log_sigmoid(x):
    return jnp.minimum(x, 0.0) - jnp.log1p(jnp.exp(-jnp.abs(x)))


def _lane(shape):
    return lax.broadcasted_iota(I32, shape, len(shape) - 1)


def _group_mat(gs):
    r = lax.broadcasted_iota(I32, (LANES, LANES), 0)
    c = lax.broadcasted_iota(I32, (LANES, LANES), 1)
    sh = gs.bit_length() - 1
    return jnp.where((r >> sh) == (c >> sh), 1.0, 0.0).astype(BF16)


def _split3(x):
    hi = x.astype(BF16)
    r1 = x - hi.astype(F32)
    mid = r1.astype(BF16)
    return hi, mid, (r1 - mid.astype(F32)).astype(BF16)


def _mm_split(x, w01):
    hi = x.astype(BF16)
    lo = (x - hi.astype(F32)).astype(BF16)
    return _mm(hi, w01) + _mm(lo, w01)


def _group_rms(x, gs):
    ss = _mm_split(x * x, _group_mat(gs))
    return x * lax.rsqrt(ss * (1.0 / gs) + EPS)


def _rope(x, cosp, sinp, half):
    lane = _lane(x.shape)
    rot = jnp.where((lane & (2 * half - 1)) < half,
                    pltpu.roll(x, LANES - half, 1), pltpu.roll(x, half, 1))
    return x * cosp + rot * sinp


def _row_rms(x):
    return x * lax.rsqrt(jnp.mean(x * x, axis=-1, keepdims=True) + EPS)


def _in_proj_kernel(x_ref, g_ref, w_ref, o_ref, o32_ref, *, c32):
    h = (_row_rms(x_ref[0]) * g_ref[...]).astype(BF16)
    ncol = o_ref.shape[-1]
    for c0 in range(0, ncol, 1024):
        c1 = min(ncol, c0 + 1024)
        o_ref[0, :, c0:c1] = _mm(h, w_ref[:, c0:c1]).astype(BF16)
    o32_ref[0] = _mm(h, w_ref[:, c32:c32 + LANES])


def _in_proj(x, g, w, c32):
    b, s, d = x.shape
    c = w.shape[1]
    tm = min(1024, s)
    return pl.pallas_call(
        functools.partial(_in_proj_kernel, c32=c32),
        out_shape=(jax.ShapeDtypeStruct((b, s, c), BF16), jax.ShapeDtypeStruct((b, s, LANES), F32)),
        grid=(b, s // tm),
        in_specs=[pl.BlockSpec((1, tm, d), lambda bi, i: (bi, i, 0)),
                  pl.BlockSpec((1, d), lambda bi, i: (0, 0)),
                  pl.BlockSpec((d, c), lambda bi, i: (0, 0))],
        out_specs=(pl.BlockSpec((1, tm, c), lambda bi, i: (bi, i, 0)),
                   pl.BlockSpec((1, tm, LANES), lambda bi, i: (bi, i, 0))),
        compiler_params=_cp("parallel", "parallel"),
        name="in_proj",
    )(x, g, w)


def _out_proj_kernel(x_ref, y1_ref, y2_ref, ym_ref, g1_ref, g2_ref, gm_ref, w_ref, o_ref):
    def gated(y_ref, g_ref):
        g = g_ref[0].astype(F32)
        return (y_ref[0].astype(F32) * (g * _sigmoid(g))).astype(BF16)

    n1 = y1_ref.shape[-1]
    n2 = y2_ref.shape[-1]
    acc = x_ref[0]
    acc = acc + _mm(gated(y1_ref, g1_ref), w_ref[0:n1, :])
    acc = acc + _mm(gated(y2_ref, g2_ref), w_ref[n1:n1 + n2, :])
    acc = acc + _mm(gated(ym_ref, gm_ref), w_ref[n1 + n2:, :])
    o_ref[0] = acc


def _out_proj(x, y1, y2, ym, u, gate_blocks, w):
    b, s, d = x.shape
    tm = min(512, s)
    i1, i2, im = gate_blocks
    n1, n2, nm = y1.shape[-1], y2.shape[-1], ym.shape[-1]
    row = lambda bi, i: (bi, i, 0)
    return pl.pallas_call(
        _out_proj_kernel,
        out_shape=jax.ShapeDtypeStruct((b, s, d), F32),
        grid=(b, s // tm),
        in_specs=[pl.BlockSpec((1, tm, d), row),
                  pl.BlockSpec((1, tm, n1), row),
                  pl.BlockSpec((1, tm, n2), row),
                  pl.BlockSpec((1, tm, nm), row),
                  pl.BlockSpec((1, tm, n1), lambda bi, i: (bi, i, i1)),
                  pl.BlockSpec((1, tm, n2), lambda bi, i: (bi, i, i2)),
                  pl.BlockSpec((1, tm, nm), lambda bi, i: (bi, i, im)),
                  pl.BlockSpec(w.shape, lambda bi, i: (0, 0))],
        out_specs=pl.BlockSpec((1, tm, d), row),
        compiler_params=_cp("parallel", "parallel"),
        name="out_proj",
    )(x, y1, y2, ym, u, u, u, w)


def _mem_kv_kernel(mem_ref, g_ref, w_ref, kg_ref, k_ref, v_ref):
    h = (_row_rms(mem_ref[0]) * g_ref[0]).astype(BF16)
    kv = _mm(h, w_ref[0])
    nk = k_ref.shape[-1]
    for c0 in range(0, nk, LANES):
        k_ref[0, 0, :, c0:c0 + LANES] = (_group_rms(kv[:, c0:c0 + LANES], M_HD) * kg_ref[0]).astype(BF16)
    v_ref[0, 0] = kv[:, nk:].astype(BF16)


def _mem_kv(mem, g, w, kg):
    b, m, d = mem.shape
    depth = w.shape[0]
    nk = M_HEADS * M_HD
    out = jax.ShapeDtypeStruct((depth, b, m, nk), BF16)
    return pl.pallas_call(
        _mem_kv_kernel,
        out_shape=(out, out),
        grid=(depth, b),
        in_specs=[pl.BlockSpec((1, m, d), lambda l, bi: (bi, 0, 0)),
                  pl.BlockSpec((1, 1, d), lambda l, bi: (l, 0, 0)),
                  pl.BlockSpec((1, d, 2 * nk), lambda l, bi: (l, 0, 0)),
                  pl.BlockSpec((1, 1, LANES), lambda l, bi: (l, 0, 0))],
        out_specs=(pl.BlockSpec((1, 1, m, nk), lambda l, bi: (l, bi, 0, 0)),
                   pl.BlockSpec((1, 1, m, nk), lambda l, bi: (l, bi, 0, 0))),
        compiler_params=_cp("parallel", "parallel"),
        name="mem_kv",
    )(mem, g, w, kg)


def _mem_attn_kernel(q_ref, k_ref, v_ref, qg_ref, o_ref):
    tq = q_ref.shape[1]
    lane = _lane((tq, LANES))
    for p in range(M_HEADS // 2):
        sl = slice(p * LANES, (p + 1) * LANES)
        q = (_group_rms(q_ref[0, :, sl].astype(F32), M_HD) * qg_ref[...] * (M_HD ** -0.5)).astype(BF16)
        k = k_ref[0, 0, :, sl]
        v = v_ref[0, 0, :, sl]
        halves = []
        for e in range(2):
            in_half = (lane >= 64 * e) & (lane < 64 * e + 64)
            s = _nt(jnp.where(in_half, q, jnp.zeros_like(q)), k)
            ex = jnp.exp(s - jnp.max(s, axis=-1, keepdims=True))
            o = _mm(ex.astype(BF16), v) / jnp.sum(ex, axis=-1, keepdims=True)
            halves.append(o)
        o_ref[0, :, sl] = jnp.where(lane < 64, halves[0], halves[1]).astype(BF16)


def _mem_attn(u, q_block, k, v, layer, qg):
    b, s, _ = u.shape
    m, nk = k.shape[2], k.shape[3]
    tq = min(512, s)
    return pl.pallas_call(
        _mem_attn_kernel,
        out_shape=jax.ShapeDtypeStruct((b, s, nk), BF16),
        grid=(b, s // tq),
        in_specs=[pl.BlockSpec((1, tq, nk), lambda bi, i: (bi, i, q_block)),
                  pl.BlockSpec((1, 1, m, nk), lambda bi, i: (layer, bi, 0, 0)),
                  pl.BlockSpec((1, 1, m, nk), lambda bi, i: (layer, bi, 0, 0)),
                  pl.BlockSpec((1, LANES), lambda bi, i: (0, 0))],
        out_specs=pl.BlockSpec((1, tq, nk), lambda bi, i: (bi, i, 0)),
        compiler_params=_cp("parallel", "parallel"),
        name="mem_attn",
    )(u, k, v, qg)


def _flash(qs, k_at, v_at, j0, j1, bias_at, carry):
    def body(j, c):
        bias = None if bias_at is None else bias_at(j)
        return tuple(_softmax_step(q, k_at(j, n), v_at(j), bias, cn) for n, (q, cn) in enumerate(zip(qs, c)))

    return lax.fori_loop(j0, j1, body, carry)


def _flash_init(rows):
    return (jnp.full((rows, 1), NEG, F32), jnp.zeros((rows, 1), F32), jnp.zeros((rows, LANES), F32))


def _softmax_update(s, v, bias, carry):
    m, l, acc = carry
    if bias is not None:
        rep = s.shape[0] // bias.shape[0]
        s = (s.reshape(rep, bias.shape[0], s.shape[1]) + bias[None]).reshape(s.shape)
    m_new = jnp.maximum(m, jnp.max(s, axis=-1, keepdims=True))
    alpha = jnp.exp2(m - m_new)
    p = jnp.exp2(s - m_new)
    l = alpha * l + jnp.sum(p, axis=-1, keepdims=True)
    acc = alpha * acc + _mm(p.astype(BF16), v)
    return m_new, l, acc


def _softmax_step(q, k, v, bias, carry):
    return _softmax_update(_nt(q, k), v, bias, carry)


def _flash_out(carry):
    _, l, acc = carry
    return acc / l


def _mla_prep_kernel(u_ref, cos_ref, sin_ref, qlg_ref, kvlg_ref, wuq_ref, wukv_ref,
                     qng_ref, qrg_ref, kng_ref, krg_ref, q_ref, k_ref, v_ref):
    tm = u_ref.shape[1]
    lane = _lane((tm, LANES))
    cosp, sinp = cos_ref[0], sin_ref[0]
    half = A_ROPE // 2
    scale = (A_NOPE + A_ROPE) ** -0.5 * LOG2E
    ql = (_row_rms(u_ref[0, :, 0:A_QLAT].astype(F32)) * qlg_ref[...]).astype(BF16)
    kvl = (_row_rms(u_ref[0, :, A_QLAT:A_QLAT + A_KVLAT].astype(F32)) * kvlg_ref[...]).astype(BF16)
    q = _mm(ql, wuq_ref[...])
    kv = _mm(kvl, wukv_ref[...])
    n_nope = A_HEADS * A_NOPE
    v_ref[0] = kv[:, n_nope:].astype(BF16)
    kr = u_ref[0, :, A_QLAT + A_KVLAT:A_QLAT + A_KVLAT + LANES].astype(F32)
    kpe = _rope(_group_rms(kr, A_ROPE) * krg_ref[...], cosp, sinp, half)
    kpe = pltpu.roll(kpe, A_NOPE, 1)
    qn = [_group_rms(q[:, c:c + LANES], A_NOPE) * qng_ref[...] for c in range(0, n_nope, LANES)]
    kn = [_group_rms(kv[:, c:c + LANES], A_NOPE) * kng_ref[...] for c in range(0, n_nope, LANES)]
    qr = [_rope(_group_rms(q[:, n_nope + c:n_nope + c + LANES], A_ROPE) * qrg_ref[...], cosp, sinp, half)
          for c in range(0, A_HEADS * A_ROPE, LANES)]
    for h in range(A_HEADS):
        qn_h = qn[h // 2] if h % 2 == 0 else pltpu.roll(qn[h // 2], A_NOPE, 1)
        kn_h = kn[h // 2] if h % 2 == 0 else pltpu.roll(kn[h // 2], A_NOPE, 1)
        shift = (A_NOPE - (h % 4) * A_ROPE) % LANES
        qr_h = qr[h // 4] if shift == 0 else pltpu.roll(qr[h // 4], shift, 1)
        qf = jnp.where(lane < A_NOPE, qn_h, jnp.where(lane < A_NOPE + A_ROPE, qr_h, 0.0))
        kf = jnp.where(lane < A_NOPE, kn_h, jnp.where(lane < A_NOPE + A_ROPE, kpe, 0.0))
        q_ref[0, h] = (qf * scale).astype(BF16)
        k_ref[0, h] = kf.astype(BF16)


def _mla_prep(u, cos32, sin32, p):
    b, s, _ = u.shape
    tm = min(256, s)
    hd = jax.ShapeDtypeStruct((b, A_HEADS, s, LANES), BF16)
    const = lambda shape: pl.BlockSpec(shape, lambda bi, i: (0,) * len(shape))
    return pl.pallas_call(
        _mla_prep_kernel,
        out_shape=(hd, hd, jax.ShapeDtypeStruct((b, s, A_HEADS * A_VD), BF16)),
        grid=(b, s // tm),
        in_specs=[pl.BlockSpec((1, tm, 512), lambda bi, i: (bi, i, 0)),
                  pl.BlockSpec((1, tm, LANES), lambda bi, i: (bi, i, 0)),
                  pl.BlockSpec((1, tm, LANES), lambda bi, i: (bi, i, 0)),
                  const((1, A_QLAT)), const((1, A_KVLAT)),
                  const(p["w_uq"].shape), const(p["w_ukv"].shape),
                  const((1, LANES)), const((1, LANES)), const((1, LANES)), const((1, LANES))],
        out_specs=(pl.BlockSpec((1, A_HEADS, tm, LANES), lambda bi, i: (bi, 0, i, 0)),
                   pl.BlockSpec((1, A_HEADS, tm, LANES), lambda bi, i: (bi, 0, i, 0)),
                   pl.BlockSpec((1, tm, A_HEADS * A_VD), lambda bi, i: (bi, i, 0))),
        compiler_params=_cp("parallel", "parallel"),
        name="mla_prep",
    )(u, cos32, sin32, p["q_lat_g"], p["kv_lat_g"], p["w_uq"], p["w_ukv"],
      p["qn_g"], p["qr_g"], p["kn_g"], p["kr_g"])


def _flash_init_t(cols):
    return (jnp.full((1, cols), NEG, F32), jnp.zeros((1, cols), F32), jnp.zeros((LANES, cols), F32))


def _softmax_step_t(q, k, v, bias_t, carry):
    s = _nt(k, q)
    if bias_t is not None:
        s = s + bias_t
    return _softmax_update_t(s, v, carry)


def _softmax_update_t(s, v, carry):
    m, l, acc = carry
    m_new = jnp.maximum(m, jnp.max(s, axis=0, keepdims=True))
    alpha = jnp.exp2(m - m_new)
    p = jnp.exp2(s - m_new)
    l = alpha * l + jnp.sum(p, axis=0, keepdims=True)
    acc = alpha * acc + _tn(v, p.astype(BF16))
    return m_new, l, acc


def _mla_attn_kernel(q_ref, k_ref, v_ref, o_ref, *, tk):
    tq = q_ref.shape[2]
    per_q = tq // tk
    i = pl.program_id(2)
    lane = _lane((tq, LANES))
    t = i * tq + lax.broadcasted_iota(I32, (1, tq), 1)
    nh = q_ref.shape[1]
    qs = [q_ref[0, e] for e in range(nh)]
    tile = lambda j: pl.ds(pl.multiple_of(j * tk, tk), tk)
    v_at = lambda j, e: v_ref[0, tile(j), (e // 2) * LANES:(e // 2 + 1) * LANES]

    def step(j, q0, bias_t, c):
        m, l, acc = c
        w = tq - q0
        s = jnp.concatenate([_nt(k_ref[0, e, tile(j), :], qs[e][q0:]) for e in range(nh)], axis=1)
        if bias_t is not None:
            s = s + jnp.concatenate([bias_t] * nh, axis=1)
        m_new = jnp.maximum(m, jnp.max(s, axis=0, keepdims=True))
        alpha = jnp.exp2(m - m_new)
        p = jnp.exp2(s - m_new)
        l = alpha * l + jnp.sum(p, axis=0, keepdims=True)
        pv = jnp.concatenate([_tn(v_at(j, 2 * pp), p[:, 2 * pp * w:(2 * pp + 2) * w].astype(BF16))
                              for pp in range(nh // 2)], axis=1)
        return m_new, l, alpha * acc + pv

    carry = lax.fori_loop(0, i * per_q, lambda j, c: step(j, 0, None, c), _flash_init_t(nh * tq))
    for d in range(per_q):
        j = i * per_q + d
        q0 = d * tk
        bias_t = jnp.where(j * tk + lax.broadcasted_iota(I32, (tk, 1), 0) <= t[:, q0:], 0.0, NEG)
        take = lambda x: jnp.concatenate([x[:, e * tq + q0:(e + 1) * tq] for e in range(nh)], axis=1)
        upd = step(j, q0, bias_t, tuple(take(x) for x in carry))
        if d:
            w = tq - q0
            upd = tuple(jnp.concatenate(
                [part for e in range(nh) for part in (old[:, e * tq:e * tq + q0], new[:, e * w:(e + 1) * w])], axis=1)
                for old, new in zip(carry, upd))
        carry = upd
    res = [(carry[2][:, e * tq:(e + 1) * tq] / carry[1][:, e * tq:(e + 1) * tq]).T for e in range(nh)]
    for p in range(nh // 2):
        o_ref[0, :, p * LANES:(p + 1) * LANES] = jnp.where(lane < A_VD, res[2 * p], res[2 * p + 1]).astype(BF16)


def _mla_attn(q, k, v):
    b, h, s, _ = q.shape
    tq = min(1024, s)
    tk = min(512, s)
    nh = 4
    return pl.pallas_call(
        functools.partial(_mla_attn_kernel, tk=tk),
        out_shape=jax.ShapeDtypeStruct((b, s, h * A_VD), BF16),
        grid=(b, h // nh, s // tq),
        in_specs=[pl.BlockSpec((1, nh, tq, LANES), lambda bi, p, i: (bi, p, i, 0)),
                  pl.BlockSpec((1, nh, s, LANES), lambda bi, p, i: (bi, p, 0, 0)),
                  pl.BlockSpec((1, s, nh * A_VD), lambda bi, p, i: (bi, 0, p))],
        out_specs=pl.BlockSpec((1, tq, nh * A_VD), lambda bi, p, i: (bi, i, p)),
        compiler_params=_cp("parallel", "parallel", "parallel"),
        name="mla_attn",
    )(q, k, v)


def _nsa_prep_kernel(q_in, ks_in, kw_in, cos_ref, sin_ref, qg_ref, ksg_ref, kwg_ref, q_ref, ks_ref, kw_ref):
    tm = q_in.shape[1]
    lane = _lane((tm, LANES))
    cosp, sinp = cos_ref[0], sin_ref[0]
    half = B_HD // 2
    rep = B_HEADS // B_KV_HEADS
    for p in range(B_HEADS // 2):
        y = _rope(_group_rms(q_in[0, :, p * LANES:(p + 1) * LANES].astype(F32), B_HD) * qg_ref[...], cosp, sinp, half)
        y = y * (B_HD ** -0.5 * LOG2E)
        y_sw = pltpu.roll(y, B_HD, 1)
        for e in range(2):
            h = 2 * p + e
            g = h // rep
            src = y if e == g else y_sw
            in_grp = (lane >= B_HD * g) & (lane < B_HD * (g + 1))
            q_ref[0, h] = jnp.where(in_grp, src, 0.0).astype(BF16)
    ks_ref[0] = _rope(_group_rms(ks_in[0].astype(F32), B_HD) * ksg_ref[...], cosp, sinp, half).astype(BF16)
    kw_ref[0] = _rope(_group_rms(kw_in[0].astype(F32), B_HD) * kwg_ref[...], cosp, sinp, half).astype(BF16)


def _nsa_prep(u, cols, cos64, sin64, p):
    b, s, _ = u.shape
    tm = min(256, s)
    blk = lambda name: pl.BlockSpec((1, tm, LANES), lambda bi, i, c=cols[name] // LANES: (bi, i, c))
    row = pl.BlockSpec((1, tm, LANES), lambda bi, i: (bi, i, 0))
    const = pl.BlockSpec((1, LANES), lambda bi, i: (0, 0))
    kvs = jax.ShapeDtypeStruct((b, s, LANES), BF16)
    return pl.pallas_call(
        _nsa_prep_kernel,
        out_shape=(jax.ShapeDtypeStruct((b, B_HEADS, s, LANES), BF16), kvs, kvs),
        grid=(b, s // tm),
        in_specs=[pl.BlockSpec((1, tm, 512), lambda bi, i, c=cols["b_q"] // 512: (bi, i, c)),
                  blk("b_ks"), blk("b_kw"), row, row, const, const, const],
        out_specs=(pl.BlockSpec((1, B_HEADS, tm, LANES), lambda bi, i: (bi, 0, i, 0)), row, row),
        compiler_params=_cp("parallel", "parallel"),
        name="nsa_prep",
    )(u, u, u, cos64, sin64, p["q_g"], p["ks_g"], p["kw_g"])


def _nsa_cmp_kernel(kc_in, vc_in, pek_ref, pev_ref, w1k_ref, w1v_ref, w2k_ref, w2v_ref, kg_ref,
                    cos_ref, sin_ref, ko_ref, vo_ref, pad_ref):
    s = kc_in.shape[1]
    n_pad = ko_ref.shape[1]

    def compress(x_in, pe_ref, w1_ref, w2_ref):
        pad_ref[0:s, :] = x_in[0].astype(F32)
        pad_ref[s:s + CMP_STRIDE, :] = jnp.zeros((CMP_STRIDE, LANES), F32)
        acc = jnp.zeros((n_pad, LANES), F32)
        for l in range(CMP_LEN):
            xl = pad_ref[pl.ds(l, n_pad, stride=CMP_STRIDE), :] + pe_ref[l:l + 1, :]
            acc = acc + _mm(xl.astype(BF16), w1_ref[l])
        mid = acc * _sigmoid(acc)
        return _mm(mid.astype(BF16), w2_ref[...])

    kc = compress(kc_in, pek_ref, w1k_ref, w2k_ref)
    ko_ref[0] = _rope(_group_rms(kc, B_HD) * kg_ref[...], cos_ref[0], sin_ref[0], B_HD // 2).astype(BF16)
    vo_ref[0] = compress(vc_in, pev_ref, w1v_ref, w2v_ref).astype(BF16)


def _nsa_cmp(u, cols, cosc, sinc, p):
    b, s, _ = u.shape
    n_pad = s // CMP_STRIDE
    blk = lambda name: pl.BlockSpec((1, s, LANES), lambda bi, c=cols[name] // LANES: (bi, 0, c))
    const = lambda shape: pl.BlockSpec(shape, lambda bi: (0,) * len(shape))
    out = jax.ShapeDtypeStruct((b, n_pad, LANES), BF16)
    ospec = pl.BlockSpec((1, n_pad, LANES), lambda bi: (bi, 0, 0))
    return pl.pallas_call(
        _nsa_cmp_kernel,
        out_shape=(out, out),
        grid=(b,),
        in_specs=[blk("b_kc"), blk("b_vc"), const((CMP_LEN, LANES)), const((CMP_LEN, LANES)),
                  const((CMP_LEN, LANES, LANES)), const((CMP_LEN, LANES, LANES)),
                  const((LANES, LANES)), const((LANES, LANES)), const((1, LANES)), ospec, ospec],
        out_specs=(ospec, ospec),
        scratch_shapes=[pltpu.VMEM((s + CMP_STRIDE, LANES), F32)],
        compiler_params=_cp("parallel"),
        name="nsa_cmp",
    )(u, u, p["pe_k"], p["pe_v"], p["w1k"], p["w1v"], p["w2k"], p["w2v"], p["kc_g"], cosc, sinc)


def _nsa_attn_kernel(q_ref, kc_ref, vc_ref, ks_ref, vs_ref, kw_ref, vw_ref, g_ref, e_ref, ovl_ref, o_ref,
                     *, n_blk, n_sel, tks, ww):
    tq = q_ref.shape[2]
    n_pad = kc_ref.shape[1]
    i = pl.program_id(1)
    t = i * tq + lax.broadcasted_iota(I32, (1, tq), 1)
    lane = _lane((tq, LANES))
    gates_t = _sigmoid(g_ref[0]).T
    rep = B_HEADS // B_KV_HEADS
    heads = lambda x: jnp.concatenate([x] * rep, axis=1)
    ncmp = lax.broadcasted_iota(I32, (n_pad, 1), 0)
    valid_c = heads((ncmp * CMP_STRIDE + (CMP_LEN - 1)) <= t)
    kc = kc_ref[0]
    vc = vc_ref[0]
    blk = lax.broadcasted_iota(I32, (LANES, 1), 0)
    cur = t >> (SEL_LEN.bit_length() - 1)
    forced = jnp.where(blk == cur, 3e4, jnp.where(blk == cur - 1, 2e4, jnp.where(blk == 0, 1e4, 0.0)))
    adm = (blk * SEL_LEN <= t) & (blk < n_blk)
    n_rv = n_blk // 8
    sub = lax.broadcasted_iota(I32, (8, tq), 0)
    w0 = pl.multiple_of(jnp.maximum(i * tq + tq - ww, 0), tq)
    kp_w = w0 + lax.broadcasted_iota(I32, (ww, 1), 0)
    win_bias = heads(jnp.where((kp_w <= t) & (kp_w > t - WINDOW), 0.0, NEG))
    qs = jnp.concatenate([q_ref[0, h] for h in range(B_HEADS)], axis=0)
    gate = lambda c: jnp.concatenate([gates_t[h * 3 + c:h * 3 + c + 1, :] for h in range(B_HEADS)], axis=1)
    groups = lambda xs: jnp.concatenate([heads(x) for x in xs], axis=1)
    valid_all = groups([valid_c[:, 0:tq]] * B_KV_HEADS)
    s = jnp.where(valid_all, _nt(kc, qs), NEG)
    ex = jnp.exp2(s - jnp.max(s, axis=0, keepdims=True))
    pc = jnp.where(valid_all, ex / jnp.sum(ex, axis=0, keepdims=True), 0.0)
    out = gate(0) * _tn(vc, pc.astype(BF16))
    sel_ts = []
    for g in range(B_KV_HEADS):
        psum = pc[:, rep * g * tq:(rep * g + 1) * tq]
        for r in range(1, rep):
            psum = psum + pc[:, (rep * g + r) * tq:(rep * g + r + 1) * tq]
        p_hi = psum.astype(BF16)
        p_lo = (psum - p_hi.astype(F32)).astype(BF16)
        imp = _mm(ovl_ref[...], p_hi) + _mm(ovl_ref[...], p_lo)
        score_t = jnp.where(adm, imp + forced, NEG)
        sc = [score_t[8 * v:8 * v + 8] for v in range(n_rv)]
        rank = [jnp.zeros((8, tq), F32) for _ in range(n_rv)]
        for jp in range(n_blk):
            col = score_t[jp:jp + 1]
            for v in range(n_rv):
                if v > jp // 8:
                    beats = col >= sc[v]
                elif v < jp // 8:
                    beats = col > sc[v]
                else:
                    beats = (col > sc[v]) | ((col == sc[v]) & (sub > jp % 8))
                rank[v] = rank[v] + jnp.where(beats, 1.0, 0.0)
        sel_ts.append(jnp.where(jnp.concatenate(rank, axis=0) < n_sel, 1.0, 0.0).astype(BF16))

    tile = lambda j: pl.ds(pl.multiple_of(j * tks, tks), tks)

    def sel_body(j, carry):
        kp = j * tks + lax.broadcasted_iota(I32, (tks, 1), 0)
        bias = [jnp.where((_mm(e_ref[tile(j), :], sel_t) > 0.5) & (kp <= t), 0.0, NEG) for sel_t in sel_ts]
        return _softmax_step_t(qs, ks_ref[0, tile(j), :], vs_ref[0, tile(j), :], groups(bias), carry)

    n_tile = ((i + 1) * tq + tks - 1) // tks
    _, l_s, acc_s = lax.fori_loop(0, n_tile, sel_body, _flash_init_t(B_HEADS * tq))
    out = out + gate(1) * (acc_s / l_s)
    s_w = _nt(kw_ref[0, pl.ds(w0, ww), :], qs) + groups([win_bias[:, 0:tq]] * B_KV_HEADS)
    p_w = jnp.exp2(s_w - jnp.max(s_w, axis=0, keepdims=True))
    o_w = _tn(vw_ref[0, pl.ds(w0, ww), :], p_w.astype(BF16)) / jnp.sum(p_w, axis=0, keepdims=True)
    out = out + gate(2) * o_w
    for g in range(B_KV_HEADS):
        outs = [out[:, (rep * g + r) * tq:(rep * g + r + 1) * tq].T for r in range(rep)]
        for pp in range(rep // 2):
            a, bb = outs[2 * pp], outs[2 * pp + 1]
            if g == 0:
                bb = pltpu.roll(bb, B_HD, 1)
            else:
                a = pltpu.roll(a, B_HD, 1)
            c0 = (rep // 2 * g + pp) * LANES
            o_ref[0, :, c0:c0 + LANES] = jnp.where(lane < B_HD, a, bb).astype(BF16)


def _nsa_attn(u, u32, cols, qx, kcmp, vcmp, ks, kw):
    b, h, s, _ = qx.shape
    tq = min(256, s)
    tks = min(512, s)
    n_pad = kcmp.shape[1]
    n_blk = s // SEL_LEN
    n_sel = min(N_SEL, n_blk)
    n_cmp = (s - CMP_LEN) // CMP_STRIDE + 1
    ww = min(WINDOW + tq, s)
    expand = np.zeros((s, n_blk), np.float32)
    expand[np.arange(s), np.arange(s) // SEL_LEN] = 1.0
    nn = np.arange(n_pad)[None, :]
    jj = np.arange(LANES)[:, None]
    ovl = ((nn * CMP_STRIDE <= jj * SEL_LEN + SEL_LEN - 1) & (nn * CMP_STRIDE + CMP_LEN - 1 >= jj * SEL_LEN)
           & (jj < n_blk) & (nn < n_cmp)).astype(np.float32)
    full = lambda shape: pl.BlockSpec(shape, lambda bi, i: (bi,) + (0,) * (len(shape) - 1))
    ucol = lambda name: pl.BlockSpec((1, s, LANES), lambda bi, i, c=cols[name] // LANES: (bi, 0, c))
    const = lambda shape: pl.BlockSpec(shape, lambda bi, i: (0,) * len(shape))
    return pl.pallas_call(
        functools.partial(_nsa_attn_kernel, n_blk=n_blk, n_sel=n_sel, tks=tks, ww=ww),
        out_shape=jax.ShapeDtypeStruct((b, s, h * B_HD), BF16),
        grid=(b, s // tq),
        in_specs=[pl.BlockSpec((1, h, tq, LANES), lambda bi, i: (bi, 0, i, 0)),
                  full((1, n_pad, LANES)), full((1, n_pad, LANES)),
                  full((1, s, LANES)), ucol("b_vs"), full((1, s, LANES)), ucol("b_vw"),
                  pl.BlockSpec((1, tq, LANES), lambda bi, i: (bi, i, 0)),
                  const((s, n_blk)), const((LANES, n_pad))],
        out_specs=pl.BlockSpec((1, tq, h * B_HD), lambda bi, i: (bi, i, 0)),
        compiler_params=_cp("parallel", "parallel"),
        name="nsa_attn",
    )(qx, kcmp, vcmp, ks, u, kw, u, u32, jnp.asarray(expand, BF16), jnp.asarray(ovl, BF16))


def _dsa_prep_kernel(q_in, iq_in, kv_in, sm_in, cos64_ref, sin64_ref, cos32_ref, sin32_ref, qg_ref, kg_ref,
                     q_ref, kv_ref, iq_ref, ik_ref):
    tm = q_in.shape[1]
    lane = _lane((tm, LANES))
    c64, s64, c32, s32 = cos64_ref[0], sin64_ref[0], cos32_ref[0], sin32_ref[0]
    for p in range(C_HEADS // 2):
        y = _rope(_group_rms(q_in[0, :, p * LANES:(p + 1) * LANES].astype(F32), C_HD) * qg_ref[...], c64, s64, C_HD // 2)
        y = y * (C_HD ** -0.5 * LOG2E)
        q_ref[0, 2 * p] = jnp.where(lane < C_HD, y, 0.0).astype(BF16)
        q_ref[0, 2 * p + 1] = jnp.where(lane < C_HD, pltpu.roll(y, C_HD, 1), 0.0).astype(BF16)
    kv = kv_in[0].astype(F32)
    kn = _rope(_group_rms(kv, C_HD) * kg_ref[...], c64, s64, C_HD // 2)
    kv_ref[0] = jnp.where(lane < C_HD, kn, kv).astype(BF16)
    for c0 in range(0, IDX_HEADS * IDX_HD, LANES):
        iq_ref[0, :, c0:c0 + LANES] = _rope(iq_in[0, :, c0:c0 + LANES].astype(F32), c32, s32, IDX_HD // 2).astype(BF16)
    ik = jnp.where(lane < IDX_HD, _rope(sm_in[0], c32, s32, IDX_HD // 2), 0.0)
    ik = ik + pltpu.roll(ik, IDX_HD, 1)
    ik = ik + pltpu.roll(ik, 2 * IDX_HD, 1)
    ik_ref[0] = ik.astype(BF16)


def _dsa_prep(u, u32, cols, tabs, p):
    b, s, _ = u.shape
    tm = min(256, s)
    row = pl.BlockSpec((1, tm, LANES), lambda bi, i: (bi, i, 0))
    const = pl.BlockSpec((1, LANES), lambda bi, i: (0, 0))
    ublk = lambda name, w: pl.BlockSpec((1, tm, w), lambda bi, i, c=cols[name] // w: (bi, i, c))
    dense = jax.ShapeDtypeStruct((b, s, LANES), BF16)
    return pl.pallas_call(
        _dsa_prep_kernel,
        out_shape=(jax.ShapeDtypeStruct((b, C_HEADS, s, LANES), BF16), dense,
                   jax.ShapeDtypeStruct((b, s, IDX_HEADS * IDX_HD), BF16), dense),
        grid=(b, s // tm),
        in_specs=[ublk("c_q", 512), ublk("c_iq", 256), ublk("c_kv", LANES), row,
                  row, row, row, row, const, const],
        out_specs=(pl.BlockSpec((1, C_HEADS, tm, LANES), lambda bi, i: (bi, 0, i, 0)), row,
                   pl.BlockSpec((1, tm, IDX_HEADS * IDX_HD), lambda bi, i: (bi, i, 0)), row),
        compiler_params=_cp("parallel", "parallel"),
        name="dsa_prep",
    )(u, u, u, u32, tabs["cos64"], tabs["sin64"], tabs["cos32"], tabs["sin32"], p["q_g"], p["k_g"])


def _dsa_attn_kernel(q_ref, iq_ref, sm_ref, ik_ref, kv_ref, tri_ref, o_ref, key_scr, bias_scr, *, tk, topk):
    tq = q_ref.shape[2]
    i = pl.program_id(1)
    t = i * tq + lax.broadcasted_iota(I32, (1, tq), 1)
    lane = _lane((tq, LANES))
    n_tile = ((i + 1) * tq + tk - 1) // tk
    tile = lambda j: pl.ds(pl.multiple_of(j * tk, tk), tk)
    kpos = lambda j: j * tk + lax.broadcasted_iota(I32, (tk, 1), 0)

    iw_t = (sm_ref[0] * (IDX_HEADS ** -0.5)).T
    parts = []
    for h in range(IDX_HEADS):
        blk = iq_ref[0, :, (h // 4) * LANES:(h // 4 + 1) * LANES]
        lo = (h % 4) * IDX_HD
        parts.append(jnp.where((lane >= lo) & (lane < lo + IDX_HD), blk, jnp.zeros_like(blk)))
    iqs = jnp.concatenate(parts, axis=0)

    def idx_body(j, carry):
        lg = _nt(ik_ref[0, tile(j), :], iqs)
        acc = jnp.zeros((tk, tq), F32)
        for h in range(IDX_HEADS):
            acc = acc + iw_t[IDX_HD + h:IDX_HD + h + 1, :] * jnp.maximum(lg[:, h * tq:(h + 1) * tq], 0.0)
        sc = jnp.where(kpos(j) <= t, acc, NEG)
        bits = pltpu.bitcast(sc, I32)
        key = jnp.where(bits < 0, bits ^ 0x7FFFFFFF, bits)
        key_scr[tile(j), :] = jnp.where(sc == 0.0, 0, key)
        return carry

    lax.fori_loop(0, n_tile, idx_body, 0)

    tr = min(tk, 2 * tq)
    n_sub = ((i + 1) * tq + tr - 1) // tr

    def count_ge(thr_key):
        def body(j, cs):
            key = key_scr[pl.ds(pl.multiple_of(j * tr, tr), tr), :]
            cs = list(cs)
            for n, r0 in enumerate(range(0, tr, 8)):
                c = cs[n % len(cs)]
                cs[n % len(cs)] = jnp.where(key[r0:r0 + 8] >= thr_key, c + 1, c)
            return tuple(cs)

        cs = lax.fori_loop(0, n_sub, body, tuple(jnp.zeros((8, tq), I32) for _ in range(4)))
        return jnp.sum(cs[0] + cs[1] + cs[2] + cs[3], axis=0, keepdims=True)

    def bit_body(bi, ucand):
        utrial = ucand | jnp.left_shift(jnp.int32(1), 31 - bi)
        return jnp.where(count_ge(utrial ^ INT_MIN) >= topk, utrial, ucand)

    thr = lax.fori_loop(0, 32, bit_body, jnp.zeros((1, tq), I32)) ^ INT_MIN
    n_ge = count_ge(thr)
    n_gt = count_ge(thr + 1)
    need = topk - n_gt
    row_ok = (n_ge - n_gt == need) | (thr == NEG_KEY) | (n_ge < topk)
    simple = jnp.min(jnp.where(row_ok, 1.0, 0.0)) > 0.5

    def fast_bias():
        def body(j, carry):
            keep = (key_scr[tile(j), :] >= thr) & (kpos(j) <= t)
            bias_scr[tile(j), :] = jnp.where(keep, 0.0, NEG)
            return carry

        lax.fori_loop(0, n_tile, body, 0)

    def tie_bias():
        need_f = need.astype(F32)

        def body(j, run):
            key = key_scr[tile(j), :]
            kp = kpos(j)
            for c0 in range(0, tk, LANES):
                kc = key[c0:c0 + LANES]
                eq = kc == thr
                eq_f = jnp.where(eq, 1.0, 0.0)
                pref = _mm(tri_ref[...], eq_f.astype(BF16)) + run
                keep = ((kc > thr) | (eq & (pref <= need_f))) & (kp[c0:c0 + LANES] <= t)
                bias_scr[pl.ds(pl.multiple_of(j * tk + c0, LANES), LANES), :] = jnp.where(keep, 0.0, NEG)
                run = run + jnp.sum(eq_f, axis=0, keepdims=True)
            return run

        lax.fori_loop(0, n_tile, body, jnp.zeros((1, tq), F32))

    lax.cond(simple, fast_bias, tie_bias)

    n_chain = 1
    hpc = C_HEADS // n_chain
    qs = [jnp.concatenate([q_ref[0, c * hpc + h] for h in range(hpc)], axis=0) for c in range(n_chain)]

    def att_body(j, carry):
        kv = kv_ref[0, tile(j), :]
        bias_t = jnp.concatenate([bias_scr[tile(j), :]] * hpc, axis=1)
        return tuple(_softmax_step_t(qs[c], kv, kv, bias_t, carry[c]) for c in range(n_chain))

    carry = lax.fori_loop(0, n_tile, att_body, tuple(_flash_init_t(hpc * tq) for _ in range(n_chain)))
    o_t = jnp.concatenate([acc / l for (_, l, acc) in carry], axis=1)
    for p in range(C_HEADS // 2):
        a = pltpu.roll(o_t[:, (2 * p) * tq:(2 * p + 1) * tq].T, C_HD, 1)
        bb = o_t[:, (2 * p + 1) * tq:(2 * p + 2) * tq].T
        o_ref[0, :, p * LANES:(p + 1) * LANES] = jnp.where(lane < C_HD, a, bb).astype(BF16)


def _dsa_attn(u32, qc, iq, ik, kv):
    b, h, s, _ = qc.shape
    tq = min(128, s)
    tk = min(1024, s)
    topk = min(TOPK_MAX, s // 4)
    tri = np.tril(np.ones((LANES, LANES), np.float32))
    full = lambda shape: pl.BlockSpec(shape, lambda bi, i: (bi,) + (0,) * (len(shape) - 1))
    return pl.pallas_call(
        functools.partial(_dsa_attn_kernel, tk=tk, topk=topk),
        out_shape=jax.ShapeDtypeStruct((b, s, h * C_HD), BF16),
        grid=(b, s // tq),
        in_specs=[pl.BlockSpec((1, h, tq, LANES), lambda bi, i: (bi, 0, i, 0)),
                  pl.BlockSpec((1, tq, IDX_HEADS * IDX_HD), lambda bi, i: (bi, i, 0)),
                  pl.BlockSpec((1, tq, LANES), lambda bi, i: (bi, i, 0)),
                  full((1, s, LANES)), full((1, s, LANES)),
                  pl.BlockSpec((LANES, LANES), lambda bi, i: (0, 0))],
        out_specs=pl.BlockSpec((1, tq, h * C_HD), lambda bi, i: (bi, i, 0)),
        scratch_shapes=[pltpu.VMEM((s, tq), I32), pltpu.VMEM((s, tq), F32)],
        compiler_params=_cp("parallel", "parallel"),
        name="dsa_attn",
    )(qc, iq, u32, ik, kv, jnp.asarray(tri, BF16))


def _mlstm_conv_kernel(x_ref, halo_ref, w_ref, b_ref, o_ref):
    tm = x_ref.shape[1]
    i = pl.program_id(1)
    lane = _lane((tm, x_ref.shape[2]))
    halo = jnp.where(i > 0, halo_ref[0].astype(F32), 0.0)
    x = x_ref[0].astype(F32)
    xc = jnp.concatenate([halo, x], axis=0)
    off = halo.shape[0] - (CONV_W - 1)
    y = b_ref[...] + jnp.zeros_like(x)
    for j in range(CONV_W):
        y = y + w_ref[j:j + 1, :] * xc[off + j:off + j + tm, :]
    y = y * _sigmoid(y)
    o_ref[0] = jnp.where(lane >= D_HEADS * D_QK, y * (D_QK ** -0.5), y)


def _mlstm_conv(u, qk_block, w, bias):
    b, s, _ = u.shape
    c = w.shape[1]
    tm = min(512, s)
    hb = 16
    return pl.pallas_call(
        _mlstm_conv_kernel,
        out_shape=jax.ShapeDtypeStruct((b, s, c), F32),
        grid=(b, s // tm),
        in_specs=[pl.BlockSpec((1, tm, c), lambda bi, i: (bi, i, qk_block)),
                  pl.BlockSpec((1, hb, c), lambda bi, i: (bi, jnp.maximum(i * (tm // hb) - 1, 0), qk_block)),
                  pl.BlockSpec((CONV_W, c), lambda bi, i: (0, 0)),
                  pl.BlockSpec((1, c), lambda bi, i: (0, 0))],
        out_specs=pl.BlockSpec((1, tm, c), lambda bi, i: (bi, i, 0)),
        compiler_params=_cp("parallel", "parallel"),
        name="mlstm_conv",
    )(u, u, w, bias)


def _mlstm_scan_kernel(qk_ref, v_ref, op_ref, sm_ref, gt_ref, bcol_ref, brow_ref, hg_ref, o_ref,
                       c_scr, n_scr, m_scr):
    tc = qk_ref.shape[1]
    L = CHUNK
    nqk = D_HEADS * D_QK

    @pl.when(pl.program_id(1) == 0)
    def _():
        c_scr[...] = jnp.zeros_like(c_scr)
        n_scr[...] = jnp.zeros_like(n_scr)
        m_scr[...] = jnp.zeros_like(m_scr)

    r = lax.broadcasted_iota(I32, (L, L), 0)
    c = lax.broadcasted_iota(I32, (L, L), 1)
    tril = c <= r
    tril_f = jnp.where(tril, 1.0, 0.0).astype(BF16)
    triu_f = jnp.where(r <= c, 1.0, 0.0).astype(BF16)
    lane = _lane((L, LANES))
    row128 = lax.broadcasted_iota(I32, (LANES, LANES), 0)
    i_lane, f_lane = IDX_HD + IDX_HEADS, IDX_HD + IDX_HEADS + D_HEADS

    def one(bb, ci):
        rows = pl.ds(pl.multiple_of(ci * L, L), L)
        qk = qk_ref[bb, rows, :]
        sm = sm_ref[bb, rows, :] + bcol_ref[...]
        bcol_all = sum(_mm(tril_f, part) for part in _split3(_log_sigmoid(sm)))
        gt = gt_ref[bb, ci] + brow_ref[...]
        brow_all = sum(_mm(part, triu_f) for part in _split3(_log_sigmoid(gt)))
        for p in range(D_HEADS // 2):
            qpair = qk[:, p * LANES:(p + 1) * LANES]
            kpair = qk[:, nqk + p * LANES:nqk + (p + 1) * LANES]
            kpair_b = kpair.astype(BF16)
            sp = bb * (D_HEADS // 2) + p
            c_prev = c_scr[sp]
            n_prev = n_scr[sp, 0:1, :]
            c_prev_b = c_prev.astype(BF16)
            upd = []
            for e in range(2):
                h = 2 * p + e
                in_half = (lane >= D_QK * e) & (lane < D_QK * (e + 1))
                qm = jnp.where(in_half, qpair, 0.0)
                qm_b = qm.astype(BF16)
                vh = v_ref[bb, rows, h * D_VD:(h + 1) * D_VD].astype(F32)
                b_col = bcol_all[:, f_lane + h:f_lane + h + 1]
                i_col = sm[:, i_lane + h:i_lane + h + 1]
                b_row = brow_all[D_HEADS + h:D_HEADS + h + 1, :]
                i_row = gt[h:h + 1, :]
                m_prev = m_scr[bb * D_HEADS + h, 0:1, 0:1]
                dmat = jnp.where(tril, b_col - b_row + i_row, NEG)
                inter = b_col + m_prev
                m_t = jnp.maximum(inter, jnp.max(dmat, axis=-1, keepdims=True))
                a = jnp.exp(inter - m_t)
                w = _nt(qm_b, kpair_b) * jnp.exp(dmat - m_t)
                num = a * _mm(qm_b, c_prev_b) + _mm(w.astype(BF16), vh.astype(BF16))
                den = a * jnp.sum(qm * n_prev, axis=-1, keepdims=True) + jnp.sum(w, axis=-1, keepdims=True)
                hout = num / jnp.maximum(jnp.abs(den), jnp.exp(-m_t))
                b_last = b_col[L - 1:L, :]
                g_col = b_last - b_col + i_col
                m_new = jnp.maximum(b_last + m_prev, jnp.max(g_col, axis=0, keepdims=True))
                ws = jnp.exp(g_col - m_new)
                decay = jnp.exp(b_last + m_prev - m_new)
                u_mat = _tn(kpair_b, (ws * vh).astype(BF16))
                k_sum = jnp.sum(ws * kpair, axis=0, keepdims=True)
                upd.append((decay * c_prev + u_mat, decay * n_prev + k_sum))
                m_scr[bb * D_HEADS + h] = jnp.broadcast_to(m_new, m_scr.shape[1:])
                hn = _row_rms(hout) * hg_ref[...]
                y = _sigmoid(op_ref[bb, rows, h * D_VD:(h + 1) * D_VD].astype(F32)) * hn
                o_ref[bb, rows, h * D_VD:(h + 1) * D_VD] = y.astype(BF16)
            c_scr[sp] = jnp.where(row128 < D_QK, upd[0][0], upd[1][0])
            n_new = jnp.where(lane[0:1] < D_QK, upd[0][1], upd[1][1])
            n_scr[sp] = jnp.broadcast_to(n_new, n_scr.shape[1:])

    def chunk(ci, carry):
        for bb in range(qk_ref.shape[0]):
            one(bb, ci)
        return carry

    lax.fori_loop(0, tc // L, chunk, 0)


def _mlstm_scan(u, u32, cols, qk, gt, p):
    b, s, _ = u.shape
    tc = min(512, s)
    nb = next(n for n in (4, 2, 1) if b % n == 0)
    nv = D_HEADS * D_VD
    ublk = lambda name, w: pl.BlockSpec((nb, tc, w), lambda bi, i, c=cols[name] // w: (bi, i, c))
    const = lambda shape: pl.BlockSpec(shape, lambda bi, i: (0,) * len(shape))
    return pl.pallas_call(
        _mlstm_scan_kernel,
        out_shape=jax.ShapeDtypeStruct((b, s, nv), BF16),
        grid=(b // nb, s // tc),
        in_specs=[pl.BlockSpec((nb, tc, qk.shape[-1]), lambda bi, i: (bi, i, 0)),
                  ublk("d_v", nv), ublk("d_o", nv), pl.BlockSpec((nb, tc, LANES), lambda bi, i: (bi, i, 0)),
                  pl.BlockSpec((nb, tc // CHUNK, 8, CHUNK), lambda bi, i: (bi, i, 0, 0)),
                  const((1, LANES)), const((8, 1)), const((1, D_VD))],
        out_specs=pl.BlockSpec((nb, tc, nv), lambda bi, i: (bi, i, 0)),
        scratch_shapes=[pltpu.VMEM((nb * D_HEADS // 2, LANES, LANES), F32),
                        pltpu.VMEM((nb * D_HEADS // 2, 8, LANES), F32),
                        pltpu.VMEM((nb * D_HEADS, 8, LANES), F32)],
        compiler_params=_cp("parallel", "arbitrary"),
        name="mlstm_scan",
    )(qk, u, u, u32, gt, p["bias_col"], p["bias_row"], p["h_g"])


EVEN_SRC = dict(a_ql=(0, 256), a_kvl=(256, 128), a_kr=(384, 32), a_gate=(416, 512), b_q=(928, 512),
                b_kc=(1440, 128), b_vc=(1568, 128), b_ks=(1696, 128), b_vs=(1824, 128), b_kw=(1952, 128),
                b_vw=(2080, 128), b_g=(2208, 24), b_gate=(2232, 512), m_q=(2744, 256), m_gate=(3000, 256))
EVEN_DST = dict(a_ql=0, a_kvl=256, a_kr=384, b_q=512, a_gate=1024, b_gate=1536, m_q=2048, m_gate=2304,
                b_g=2560, b_kc=2688, b_vc=2816, b_ks=2944, b_vs=3072, b_kw=3200, b_vw=3328)
EVEN_COLS_PAD = 3456

ODD_SRC = dict(c_q=(0, 512), c_k=(512, 64), c_v=(576, 64), c_iq=(640, 256), c_ik=(896, 32), c_iw=(928, 8),
               c_gate=(936, 512), d_q=(1448, 256), d_k=(1704, 256), d_v=(1960, 512), d_i=(2472, 4),
               d_f=(2476, 4), d_o=(2480, 512), d_gate=(2992, 512), m_q=(3504, 256), m_gate=(3760, 256))
ODD_DST = dict(c_q=0, c_gate=512, d_gate=1024, d_v=1536, d_o=2048, d_q=2560, d_k=2816, m_q=3072, m_gate=3328,
               c_iq=3584, c_k=3840, c_v=3904, c_ik=3968, c_iw=4000, d_i=4008, d_f=4012)
ODD_COLS_PAD = 4096


def _permute_cols(w, src, dst, total):
    pieces, pos = [], 0
    for name in sorted(src, key=lambda n: dst[n]):
        start, width = src[name]
        if dst[name] > pos:
            pieces.append(jnp.zeros((w.shape[0], dst[name] - pos), BF16))
        pieces.append(w[:, start:start + width].astype(BF16))
        pos = dst[name] + width
    if total > pos:
        pieces.append(jnp.zeros((w.shape[0], total - pos), BF16))
    return jnp.concatenate(pieces, axis=1)


def _heads_split(w, heads, first):
    w3 = w.reshape(w.shape[0], heads, -1)
    return jnp.concatenate([w3[:, :, :first].reshape(w.shape[0], -1),
                            w3[:, :, first:].reshape(w.shape[0], -1)], axis=1).astype(BF16)


def _tile_lanes(v, reps):
    return jnp.tile(v.astype(F32).reshape(1, -1), (1, reps))


def _rope_tables(positions, d2):
    inv = ROPE_THETA ** (-jnp.arange(d2, dtype=F32) / d2)
    ang = positions.astype(F32)[..., None] * inv
    c, s = jnp.cos(ang), jnp.sin(ang)
    reps = LANES // (2 * d2)
    return (jnp.tile(jnp.concatenate([c, c], axis=-1), (1, 1, reps)),
            jnp.tile(jnp.concatenate([-s, s], axis=-1), (1, 1, reps)))


def _block_diag2(w):
    z = jnp.zeros_like(w)
    return jnp.concatenate([jnp.concatenate([w, z], axis=-1), jnp.concatenate([z, w], axis=-1)], axis=-2)


def kernel(x, mem, positions, ln_g, mem_norm_g, mem_w_kv, mem_q_norm_g, mem_k_norm_g, w_out, even_w_in, mla_q_lat_g, mla_kv_lat_g, mla_w_uq, mla_w_ukv, mla_q_norm_g, mla_k_norm_g, nsa_q_norm_g, nsa_k_norm_g, nsa_cmp_pos, nsa_cmp_w1, nsa_cmp_w2, odd_w_in, dsa_q_norm_g, dsa_k_norm_g, mlstm_conv_w, mlstm_conv_b, mlstm_i_bias, mlstm_f_bias, mlstm_h_norm_g):
    b, s, _ = x.shape
    depth = ln_g.shape[0]
    cos64, sin64 = _rope_tables(positions, 32)
    cos32, sin32 = _rope_tables(positions, 16)
    tabs = dict(cos64=cos64, sin64=sin64, cos32=cos32, sin32=sin32)
    n_pad = s // CMP_STRIDE
    cmp_pos = jnp.pad(positions[:, CMP_LEN - 1::CMP_STRIDE], ((0, 0), (0, 0)))[:, :n_pad]
    cmp_pos = jnp.pad(cmp_pos, ((0, 0), (0, n_pad - cmp_pos.shape[1])))
    cosc, sinc = _rope_tables(cmp_pos, 32)

    mem_k, mem_v = _mem_kv(mem, mem_norm_g.reshape(depth, 1, -1), mem_w_kv.astype(BF16),
                           jnp.tile(mem_k_norm_g, (1, 2)).reshape(depth, 1, LANES))

    for layer in range(depth):
        li = layer // 2
        g_ln = ln_g[layer].reshape(1, -1)
        mq_g = _tile_lanes(mem_q_norm_g[layer], 2)
        if layer % 2 == 0:
            cols = EVEN_DST
            u, u32 = _in_proj(x, g_ln, _permute_cols(even_w_in[li], EVEN_SRC, EVEN_DST, EVEN_COLS_PAD), cols["b_g"])
            pa = dict(q_lat_g=mla_q_lat_g[li].reshape(1, -1), kv_lat_g=mla_kv_lat_g[li].reshape(1, -1),
                      w_uq=_heads_split(mla_w_uq[li], A_HEADS, A_NOPE), w_ukv=_heads_split(mla_w_ukv[li], A_HEADS, A_NOPE),
                      qn_g=_tile_lanes(mla_q_norm_g[li, :A_NOPE], 2), qr_g=_tile_lanes(mla_q_norm_g[li, A_NOPE:], 4),
                      kn_g=_tile_lanes(mla_k_norm_g[li, :A_NOPE], 2), kr_g=_tile_lanes(mla_k_norm_g[li, A_NOPE:], 4))
            qa, ka, va = _mla_prep(u, cos32, sin32, pa)
            y1 = _mla_attn(qa, ka, va)
            pb = dict(q_g=_tile_lanes(nsa_q_norm_g[li], 2), ks_g=_tile_lanes(nsa_k_norm_g[li, 1], 2),
                      kw_g=_tile_lanes(nsa_k_norm_g[li, 2], 2), kc_g=_tile_lanes(nsa_k_norm_g[li, 0], 2),
                      pe_k=jnp.tile(nsa_cmp_pos[li, 0], (1, 2)), pe_v=jnp.tile(nsa_cmp_pos[li, 1], (1, 2)),
                      w1k=_block_diag2(nsa_cmp_w1[li, 0].reshape(CMP_LEN, B_HD, B_HD)).astype(BF16),
                      w1v=_block_diag2(nsa_cmp_w1[li, 1].reshape(CMP_LEN, B_HD, B_HD)).astype(BF16),
                      w2k=_block_diag2(nsa_cmp_w2[li, 0]).astype(BF16),
                      w2v=_block_diag2(nsa_cmp_w2[li, 1]).astype(BF16))
            qb, ks, kw = _nsa_prep(u, cols, cos64, sin64, pb)
            kcmp, vcmp = _nsa_cmp(u, cols, cosc, sinc, pb)
            y2 = _nsa_attn(u, u32, cols, qb, kcmp, vcmp, ks, kw)
            gate_blocks = (cols["a_gate"] // 512, cols["b_gate"] // 512, cols["m_gate"] // 256)
        else:
            cols = dict(ODD_DST, c_kv=ODD_DST["c_k"], small=ODD_DST["c_ik"])
            u, u32 = _in_proj(x, g_ln, _permute_cols(odd_w_in[li], ODD_SRC, ODD_DST, ODD_COLS_PAD), cols["small"])
            pc = dict(q_g=_tile_lanes(dsa_q_norm_g[li], 2), k_g=_tile_lanes(dsa_k_norm_g[li], 2))
            qc, kvc, iq, ik = _dsa_prep(u, u32, cols, tabs, pc)
            y1 = _dsa_attn(u32, qc, iq, ik, kvc)
            qk = _mlstm_conv(u, cols["d_q"] // 512, mlstm_conv_w[li], mlstm_conv_b[li].reshape(1, -1))
            gates = u32[:, :, cols["d_i"] - cols["small"]:cols["d_i"] - cols["small"] + 2 * D_HEADS]
            gt = gates.reshape(b, s // CHUNK, CHUNK, 2 * D_HEADS).transpose(0, 1, 3, 2)
            bias8 = jnp.concatenate([mlstm_i_bias[li], mlstm_f_bias[li]]).astype(F32)
            bias_col = jnp.zeros((1, LANES), F32).at[0, cols["d_i"] - cols["small"]:cols["d_i"] - cols["small"] + 8].set(bias8)
            pd = dict(bias_col=bias_col, bias_row=bias8.reshape(8, 1), h_g=mlstm_h_norm_g[li].reshape(1, -1))
            y2 = _mlstm_scan(u, u32, cols, qk, gt, pd)
            gate_blocks = (cols["c_gate"] // 512, cols["d_gate"] // 512, cols["m_gate"] // 256)
        ym = _mem_attn(u, cols["m_q"] // 256, mem_k, mem_v, layer, mq_g)
        x = _out_proj(x, y1, y2, ym, u, gate_blocks, w_out[layer].astype(BF16))
    return x
```

```python
import functools

import numpy as np
import jax
import jax.numpy as jnp
from jax import lax
from jax.experimental import pallas as pl
from jax.experimental.pallas import tpu as pltpu

F32, BF16, I32 = jnp.float32, jnp.bfloat16, jnp.int32
NEG = -1e30
EPS = 1e-6
ROPE_THETA = 10000.0
LANES = 128
VMEM_LIMIT_BYTES = 48 * 1024 * 1024

D_MODEL = 1024
DEPTH = 4
A_HEADS, A_NOPE, A_ROPE, A_VD, A_QLAT, A_KVLAT = 8, 64, 32, 64, 256, 128
B_HEADS, B_KV_HEADS, B_HD = 8, 2, 64
CMP_LEN, CMP_STRIDE, SEL_LEN, N_SEL, WINDOW = 32, 16, 64, 16, 512
C_HEADS, C_HD, IDX_HEADS, IDX_HD, TOPK_MAX = 8, 64, 8, 32, 256
D_HEADS, D_QK, D_VD, CONV_W, CHUNK = 4, 64, 128, 4, 64
M_HEADS, M_HD = 4, 64

INT_MIN = np.int32(-2 ** 31)
NEG_KEY = int(np.float32(NEG).view(np.int32) ^ np.int32(0x7FFFFFFF))
LOG2E = float(np.log2(np.e))


def _cp(*sem):
    return pltpu.CompilerParams(dimension_semantics=sem, vmem_limit_bytes=VMEM_LIMIT_BYTES)


def _nt(a, b):
    return lax.dot_general(a, b, (((1,), (1,)), ((), ())), preferred_element_type=F32)


def _tn(a, b):
    return lax.dot_general(a, b, (((0,), (0,)), ((), ())), preferred_element_type=F32)


def _mm(a, b):
    return jnp.dot(a, b, preferred_element_type=F32)


def _sigmoid(x):
    return 1.0 / (1.0 + jnp.exp(-x))


def _log_sigmoid(x):
    return jnp.minimum(x, 0.0) - jnp.log1p(jnp.exp(-jnp.abs(x)))


def _lane(shape):
    return lax.broadcasted_iota(I32, shape, len(shape) - 1)


def _group_mat(gs):
    r = lax.broadcasted_iota(I32, (LANES, LANES), 0)
    c = lax.broadcasted_iota(I32, (LANES, LANES), 1)
    sh = gs.bit_length() - 1
    return jnp.where((r >> sh) == (c >> sh), 1.0, 0.0).astype(BF16)


def _split3(x):
    hi = x.astype(BF16)
    r1 = x - hi.astype(F32)
    mid = r1.astype(BF16)
    return hi, mid, (r1 - mid.astype(F32)).astype(BF16)


def _mm_split(x, w01):
    hi = x.astype(BF16)
    lo = (x - hi.astype(F32)).astype(BF16)
    return _mm(hi, w01) + _mm(lo, w01)


def _group_rms(x, gs):
    ss = _mm_split(x * x, _group_mat(gs))
    return x * lax.rsqrt(ss * (1.0 / gs) + EPS)


def _rope(x, cosp, sinp, half):
    lane = _lane(x.shape)
    rot = jnp.where((lane & (2 * half - 1)) < half,
                    pltpu.roll(x, LANES - half, 1), pltpu.roll(x, half, 1))
    return x * cosp + rot * sinp


def _row_rms(x):
    return x * lax.rsqrt(jnp.mean(x * x, axis=-1, keepdims=True) + EPS)


def _in_proj_kernel(x_ref, g_ref, w_ref, o_ref, o32_ref, *, c32):
    h = (_row_rms(x_ref[0]) * g_ref[...]).astype(BF16)
    ncol = o_ref.shape[-1]
    for c0 in range(0, ncol, 1024):
        c1 = min(ncol, c0 + 1024)
        o_ref[0, :, c0:c1] = _mm(h, w_ref[:, c0:c1]).astype(BF16)
    o32_ref[0] = _mm(h, w_ref[:, c32:c32 + LANES])


def _in_proj(x, g, w, c32):
    b, s, d = x.shape
    c = w.shape[1]
    tm = min(1024, s)
    return pl.pallas_call(
        functools.partial(_in_proj_kernel, c32=c32),
        out_shape=(jax.ShapeDtypeStruct((b, s, c), BF16), jax.ShapeDtypeStruct((b, s, LANES), F32)),
        grid=(b, s // tm),
        in_specs=[pl.BlockSpec((1, tm, d), lambda bi, i: (bi, i, 0)),
                  pl.BlockSpec((1, d), lambda bi, i: (0, 0)),
                  pl.BlockSpec((d, c), lambda bi, i: (0, 0))],
        out_specs=(pl.BlockSpec((1, tm, c), lambda bi, i: (bi, i, 0)),
                   pl.BlockSpec((1, tm, LANES), lambda bi, i: (bi, i, 0))),
        compiler_params=_cp("parallel", "parallel"),
        name="in_proj",
    )(x, g, w)


def _out_proj_kernel(x_ref, y1_ref, y2_ref, ym_ref, g1_ref, g2_ref, gm_ref, w_ref, o_ref):
    def gated(y_ref, g_ref):
        g = g_ref[0].astype(F32)
        return (y_ref[0].astype(F32) * (g * _sigmoid(g))).astype(BF16)

    n1 = y1_ref.shape[-1]
    n2 = y2_ref.shape[-1]
    acc = x_ref[0]
    acc = acc + _mm(gated(y1_ref, g1_ref), w_ref[0:n1, :])
    acc = acc + _mm(gated(y2_ref, g2_ref), w_ref[n1:n1 + n2, :])
    acc = acc + _mm(gated(ym_ref, gm_ref), w_ref[n1 + n2:, :])
    o_ref[0] = acc


def _out_proj(x, y1, y2, ym, u, gate_blocks, w):
    b, s, d = x.shape
    tm = min(512, s)
    i1, i2, im = gate_blocks
    n1, n2, nm = y1.shape[-1], y2.shape[-1], ym.shape[-1]
    row = lambda bi, i: (bi, i, 0)
    return pl.pallas_call(
        _out_proj_kernel,
        out_shape=jax.ShapeDtypeStruct((b, s, d), F32),
        grid=(b, s // tm),
        in_specs=[pl.BlockSpec((1, tm, d), row),
                  pl.BlockSpec((1, tm, n1), row),
                  pl.BlockSpec((1, tm, n2), row),
                  pl.BlockSpec((1, tm, nm), row),
                  pl.BlockSpec((1, tm, n1), lambda bi, i: (bi, i, i1)),
                  pl.BlockSpec((1, tm, n2), lambda bi, i: (bi, i, i2)),
                  pl.BlockSpec((1, tm, nm), lambda bi, i: (bi, i, im)),
                  pl.BlockSpec(w.shape, lambda bi, i: (0, 0))],
        out_specs=pl.BlockSpec((1, tm, d), row),
        compiler_params=_cp("parallel", "parallel"),
        name="out_proj",
    )(x, y1, y2, ym, u, u, u, w)


def _mem_kv_kernel(mem_ref, g_ref, w_ref, kg_ref, k_ref, v_ref):
    h = (_row_rms(mem_ref[0]) * g_ref[0]).astype(BF16)
    kv = _mm(h, w_ref[0])
    nk = k_ref.shape[-1]
    for c0 in range(0, nk, LANES):
        k_ref[0, 0, :, c0:c0 + LANES] = (_group_rms(kv[:, c0:c0 + LANES], M_HD) * kg_ref[0]).astype(BF16)
    v_ref[0, 0] = kv[:, nk:].astype(BF16)


def _mem_kv(mem, g, w, kg):
    b, m, d = mem.shape
    depth = w.shape[0]
    nk = M_HEADS * M_HD
    out = jax.ShapeDtypeStruct((depth, b, m, nk), BF16)
    return pl.pallas_call(
        _mem_kv_kernel,
        out_shape=(out, out),
        grid=(depth, b),
        in_specs=[pl.BlockSpec((1, m, d), lambda l, bi: (bi, 0, 0)),
                  pl.BlockSpec((1, 1, d), lambda l, bi: (l, 0, 0)),
                  pl.BlockSpec((1, d, 2 * nk), lambda l, bi: (l, 0, 0)),
                  pl.BlockSpec((1, 1, LANES), lambda l, bi: (l, 0, 0))],
        out_specs=(pl.BlockSpec((1, 1, m, nk), lambda l, bi: (l, bi, 0, 0)),
                   pl.BlockSpec((1, 1, m, nk), lambda l, bi: (l, bi, 0, 0))),
        compiler_params=_cp("parallel", "parallel"),
        name="mem_kv",
    )(mem, g, w, kg)


def _mem_attn_kernel(q_ref, k_ref, v_ref, qg_ref, o_ref):
    tq = q_ref.shape[1]
    lane = _lane((tq, LANES))
    row = lax.broadcasted_iota(I32, (LANES, tq), 0)
    parts = []
    for p in range(M_HEADS // 2):
        sl = slice(p * LANES, (p + 1) * LANES)
        q = (_group_rms(q_ref[0, :, sl].astype(F32), M_HD) * qg_ref[...] * (M_HD ** -0.5 * LOG2E)).astype(BF16)
        for e in range(2):
            in_half = (lane >= M_HD * e) & (lane < M_HD * (e + 1))
            parts.append(_nt(k_ref[0, 0, :, sl], jnp.where(in_half, q, jnp.zeros_like(q))))
    s = jnp.concatenate(parts, axis=1)
    ex = jnp.exp2(s - jnp.max(s, axis=0, keepdims=True))
    inv = 1.0 / jnp.sum(ex, axis=0, keepdims=True)
    for p in range(M_HEADS // 2):
        sl = slice(p * LANES, (p + 1) * LANES)
        pv = _tn(v_ref[0, 0, :, sl], ex[:, 2 * p * tq:(2 * p + 2) * tq].astype(BF16)) * inv[:, 2 * p * tq:(2 * p + 2) * tq]
        o_ref[0, :, sl] = jnp.where(row < M_HD, pv[:, :tq], pv[:, tq:]).T.astype(BF16)


def _mem_attn(u, q_block, k, v, layer, qg):
    b, s, _ = u.shape
    m, nk = k.shape[2], k.shape[3]
    tq = min(512, s)
    return pl.pallas_call(
        _mem_attn_kernel,
        out_shape=jax.ShapeDtypeStruct((b, s, nk), BF16),
        grid=(b, s // tq),
        in_specs=[pl.BlockSpec((1, tq, nk), lambda bi, i: (bi, i, q_block)),
                  pl.BlockSpec((1, 1, m, nk), lambda bi, i: (layer, bi, 0, 0)),
                  pl.BlockSpec((1, 1, m, nk), lambda bi, i: (layer, bi, 0, 0)),
                  pl.BlockSpec((1, LANES), lambda bi, i: (0, 0))],
        out_specs=pl.BlockSpec((1, tq, nk), lambda bi, i: (bi, i, 0)),
        compiler_params=_cp("parallel", "parallel"),
        name="mem_attn",
    )(u, k, v, qg)


def _flash_init_t(cols):
    return (jnp.full((1, cols), NEG, F32), jnp.zeros((1, cols), F32), jnp.zeros((LANES, cols), F32))


def _softmax_step_t(q, k, v, bias_t, carry):
    s = _nt(k, q)
    if bias_t is not None:
        s = s + bias_t
    m, l, acc = carry
    m_new = jnp.maximum(m, jnp.max(s, axis=0, keepdims=True))
    alpha = jnp.exp2(m - m_new)
    p = jnp.exp2(s - m_new)
    l = alpha * l + jnp.sum(p, axis=0, keepdims=True)
    acc = alpha * acc + _tn(v, p.astype(BF16))
    return m_new, l, acc


def _mla_prep_kernel(u_ref, cos_ref, sin_ref, qlg_ref, kvlg_ref, wuq_ref, wukv_ref,
                     qng_ref, qrg_ref, kng_ref, krg_ref, q_ref, k_ref, v_ref):
    tm = u_ref.shape[1]
    lane = _lane((tm, LANES))
    cosp, sinp = cos_ref[0], sin_ref[0]
    half = A_ROPE // 2
    scale = (A_NOPE + A_ROPE) ** -0.5 * LOG2E
    ql = (_row_rms(u_ref[0, :, 0:A_QLAT].astype(F32)) * qlg_ref[...]).astype(BF16)
    kvl = (_row_rms(u_ref[0, :, A_QLAT:A_QLAT + A_KVLAT].astype(F32)) * kvlg_ref[...]).astype(BF16)
    q = _mm(ql, wuq_ref[...])
    kv = _mm(kvl, wukv_ref[...])
    n_nope = A_HEADS * A_NOPE
    v_ref[0] = kv[:, n_nope:].astype(BF16)
    kr = u_ref[0, :, A_QLAT + A_KVLAT:A_QLAT + A_KVLAT + LANES].astype(F32)
    kpe = _rope(_group_rms(kr, A_ROPE) * krg_ref[...], cosp, sinp, half)
    kpe = pltpu.roll(kpe, A_NOPE, 1)
    qn = [_group_rms(q[:, c:c + LANES], A_NOPE) * qng_ref[...] for c in range(0, n_nope, LANES)]
    kn = [_group_rms(kv[:, c:c + LANES], A_NOPE) * kng_ref[...] for c in range(0, n_nope, LANES)]
    qr = [_rope(_group_rms(q[:, n_nope + c:n_nope + c + LANES], A_ROPE) * qrg_ref[...], cosp, sinp, half)
          for c in range(0, A_HEADS * A_ROPE, LANES)]
    for h in range(A_HEADS):
        qn_h = qn[h // 2] if h % 2 == 0 else pltpu.roll(qn[h // 2], A_NOPE, 1)
        kn_h = kn[h // 2] if h % 2 == 0 else pltpu.roll(kn[h // 2], A_NOPE, 1)
        shift = (A_NOPE - (h % 4) * A_ROPE) % LANES
        qr_h = qr[h // 4] if shift == 0 else pltpu.roll(qr[h // 4], shift, 1)
        qf = jnp.where(lane < A_NOPE, qn_h, jnp.where(lane < A_NOPE + A_ROPE, qr_h, 0.0))
        kf = jnp.where(lane < A_NOPE, kn_h, jnp.where(lane < A_NOPE + A_ROPE, kpe, 0.0))
        q_ref[0, h] = (qf * scale).astype(BF16)
        k_ref[0, h] = kf.astype(BF16)


def _mla_prep(u, cos32, sin32, p):
    b, s, _ = u.shape
    tm = min(256, s)
    hd = jax.ShapeDtypeStruct((b, A_HEADS, s, LANES), BF16)
    const = lambda shape: pl.BlockSpec(shape, lambda bi, i: (0,) * len(shape))
    return pl.pallas_call(
        _mla_prep_kernel,
        out_shape=(hd, hd, jax.ShapeDtypeStruct((b, s, A_HEADS * A_VD), BF16)),
        grid=(b, s // tm),
        in_specs=[pl.BlockSpec((1, tm, 512), lambda bi, i: (bi, i, 0)),
                  pl.BlockSpec((1, tm, LANES), lambda bi, i: (bi, i, 0)),
                  pl.BlockSpec((1, tm, LANES), lambda bi, i: (bi, i, 0)),
                  const((1, A_QLAT)), const((1, A_KVLAT)),
                  const(p["w_uq"].shape), const(p["w_ukv"].shape),
                  const((1, LANES)), const((1, LANES)), const((1, LANES)), const((1, LANES))],
        out_specs=(pl.BlockSpec((1, A_HEADS, tm, LANES), lambda bi, i: (bi, 0, i, 0)),
                   pl.BlockSpec((1, A_HEADS, tm, LANES), lambda bi, i: (bi, 0, i, 0)),
                   pl.BlockSpec((1, tm, A_HEADS * A_VD), lambda bi, i: (bi, i, 0))),
        compiler_params=_cp("parallel", "parallel"),
        name="mla_prep",
    )(u, cos32, sin32, p["q_lat_g"], p["kv_lat_g"], p["w_uq"], p["w_ukv"],
      p["qn_g"], p["qr_g"], p["kn_g"], p["kr_g"])


def _mla_attn_kernel(q_ref, k_ref, v_ref, o_ref, *, tk):
    tq = q_ref.shape[2]
    per_q = tq // tk
    i = pl.program_id(2)
    lane = _lane((tq, LANES))
    t = i * tq + lax.broadcasted_iota(I32, (1, tq), 1)
    nh = q_ref.shape[1]
    qs = [q_ref[0, e] for e in range(nh)]
    tile = lambda j: pl.ds(pl.multiple_of(j * tk, tk), tk)
    v_at = lambda j, e: v_ref[0, tile(j), (e // 2) * LANES:(e // 2 + 1) * LANES]

    def step(j, q0, bias_t, c):
        m, l, acc = c
        w = tq - q0
        s = jnp.concatenate([_nt(k_ref[0, e, tile(j), :], qs[e][q0:]) for e in range(nh)], axis=1)
        if bias_t is not None:
            s = s + jnp.concatenate([bias_t] * nh, axis=1)
        m_new = jnp.maximum(m, jnp.max(s, axis=0, keepdims=True))
        alpha = jnp.exp2(m - m_new)
        p = jnp.exp2(s - m_new)
        l = alpha * l + jnp.sum(p, axis=0, keepdims=True)
        pv = jnp.concatenate([_tn(v_at(j, 2 * pp), p[:, 2 * pp * w:(2 * pp + 2) * w].astype(BF16))
                              for pp in range(nh // 2)], axis=1)
        return m_new, l, alpha * acc + pv

    carry = lax.fori_loop(0, i * per_q, lambda j, c: step(j, 0, None, c), _flash_init_t(nh * tq))
    for d in range(per_q):
        j = i * per_q + d
        q0 = d * tk
        bias_t = jnp.where(j * tk + lax.broadcasted_iota(I32, (tk, 1), 0) <= t[:, q0:], 0.0, NEG)
        take = lambda x: jnp.concatenate([x[:, e * tq + q0:(e + 1) * tq] for e in range(nh)], axis=1)
        upd = step(j, q0, bias_t, tuple(take(x) for x in carry))
        if d:
            w = tq - q0
            upd = tuple(jnp.concatenate(
                [part for e in range(nh) for part in (old[:, e * tq:e * tq + q0], new[:, e * w:(e + 1) * w])], axis=1)
                for old, new in zip(carry, upd))
        carry = upd
    res = [(carry[2][:, e * tq:(e + 1) * tq] / carry[1][:, e * tq:(e + 1) * tq]).T for e in range(nh)]
    for p in range(nh // 2):
        o_ref[0, :, p * LANES:(p + 1) * LANES] = jnp.where(lane < A_VD, res[2 * p], res[2 * p + 1]).astype(BF16)


def _mla_attn(q, k, v):
    b, h, s, _ = q.shape
    tq = min(1024, s)
    tk = min(512, s)
    nh = 4
    return pl.pallas_call(
        functools.partial(_mla_attn_kernel, tk=tk),
        out_shape=jax.ShapeDtypeStruct((b, s, h * A_VD), BF16),
        grid=(b, h // nh, s // tq),
        in_specs=[pl.BlockSpec((1, nh, tq, LANES), lambda bi, p, i: (bi, p, i, 0)),
                  pl.BlockSpec((1, nh, s, LANES), lambda bi, p, i: (bi, p, 0, 0)),
                  pl.BlockSpec((1, s, nh * A_VD), lambda bi, p, i: (bi, 0, p))],
        out_specs=pl.BlockSpec((1, tq, nh * A_VD), lambda bi, p, i: (bi, i, p)),
        compiler_params=_cp("parallel", "parallel", "parallel"),
        name="mla_attn",
    )(q, k, v)


def _nsa_prep_kernel(q_in, ks_in, kw_in, cos_ref, sin_ref, qg_ref, ksg_ref, kwg_ref, q_ref, ks_ref, kw_ref):
    tm = q_in.shape[1]
    lane = _lane((tm, LANES))
    cosp, sinp = cos_ref[0], sin_ref[0]
    half = B_HD // 2
    rep = B_HEADS // B_KV_HEADS
    for p in range(B_HEADS // 2):
        y = _rope(_group_rms(q_in[0, :, p * LANES:(p + 1) * LANES].astype(F32), B_HD) * qg_ref[...], cosp, sinp, half)
        y = y * (B_HD ** -0.5 * LOG2E)
        y_sw = pltpu.roll(y, B_HD, 1)
        for e in range(2):
            h = 2 * p + e
            g = h // rep
            src = y if e == g else y_sw
            in_grp = (lane >= B_HD * g) & (lane < B_HD * (g + 1))
            q_ref[0, h] = jnp.where(in_grp, src, 0.0).astype(BF16)
    ks_ref[0] = _rope(_group_rms(ks_in[0].astype(F32), B_HD) * ksg_ref[...], cosp, sinp, half).astype(BF16)
    kw_ref[0] = _rope(_group_rms(kw_in[0].astype(F32), B_HD) * kwg_ref[...], cosp, sinp, half).astype(BF16)


def _nsa_prep(u, cols, cos64, sin64, p):
    b, s, _ = u.shape
    tm = min(256, s)
    blk = lambda name: pl.BlockSpec((1, tm, LANES), lambda bi, i, c=cols[name] // LANES: (bi, i, c))
    row = pl.BlockSpec((1, tm, LANES), lambda bi, i: (bi, i, 0))
    const = pl.BlockSpec((1, LANES), lambda bi, i: (0, 0))
    kvs = jax.ShapeDtypeStruct((b, s, LANES), BF16)
    return pl.pallas_call(
        _nsa_prep_kernel,
        out_shape=(jax.ShapeDtypeStruct((b, B_HEADS, s, LANES), BF16), kvs, kvs),
        grid=(b, s // tm),
        in_specs=[pl.BlockSpec((1, tm, 512), lambda bi, i, c=cols["b_q"] // 512: (bi, i, c)),
                  blk("b_ks"), blk("b_kw"), row, row, const, const, const],
        out_specs=(pl.BlockSpec((1, B_HEADS, tm, LANES), lambda bi, i: (bi, 0, i, 0)), row, row),
        compiler_params=_cp("parallel", "parallel"),
        name="nsa_prep",
    )(u, u, u, cos64, sin64, p["q_g"], p["ks_g"], p["kw_g"])


def _nsa_cmp_kernel(kc_in, vc_in, pek_ref, pev_ref, w1k_ref, w1v_ref, w2k_ref, w2v_ref, kg_ref,
                    cos_ref, sin_ref, ko_ref, vo_ref, pad_ref):
    s = kc_in.shape[1]
    n_pad = ko_ref.shape[1]

    def compress(x_in, pe_ref, w1_ref, w2_ref):
        pad_ref[0:s, :] = x_in[0].astype(F32)
        pad_ref[s:s + CMP_STRIDE, :] = jnp.zeros((CMP_STRIDE, LANES), F32)
        acc = jnp.zeros((n_pad, LANES), F32)
        for l in range(CMP_LEN):
            xl = pad_ref[pl.ds(l, n_pad, stride=CMP_STRIDE), :] + pe_ref[l:l + 1, :]
            acc = acc + _mm(xl.astype(BF16), w1_ref[l])
        mid = acc * _sigmoid(acc)
        return _mm(mid.astype(BF16), w2_ref[...])

    kc = compress(kc_in, pek_ref, w1k_ref, w2k_ref)
    ko_ref[0] = _rope(_group_rms(kc, B_HD) * kg_ref[...], cos_ref[0], sin_ref[0], B_HD // 2).astype(BF16)
    vo_ref[0] = compress(vc_in, pev_ref, w1v_ref, w2v_ref).astype(BF16)


def _nsa_cmp(u, cols, cosc, sinc, p):
    b, s, _ = u.shape
    n_pad = s // CMP_STRIDE
    blk = lambda name: pl.BlockSpec((1, s, LANES), lambda bi, c=cols[name] // LANES: (bi, 0, c))
    const = lambda shape: pl.BlockSpec(shape, lambda bi: (0,) * len(shape))
    out = jax.ShapeDtypeStruct((b, n_pad, LANES), BF16)
    ospec = pl.BlockSpec((1, n_pad, LANES), lambda bi: (bi, 0, 0))
    return pl.pallas_call(
        _nsa_cmp_kernel,
        out_shape=(out, out),
        grid=(b,),
        in_specs=[blk("b_kc"), blk("b_vc"), const((CMP_LEN, LANES)), const((CMP_LEN, LANES)),
                  const((CMP_LEN, LANES, LANES)), const((CMP_LEN, LANES, LANES)),
                  const((LANES, LANES)), const((LANES, LANES)), const((1, LANES)), ospec, ospec],
        out_specs=(ospec, ospec),
        scratch_shapes=[pltpu.VMEM((s + CMP_STRIDE, LANES), F32)],
        compiler_params=_cp("parallel"),
        name="nsa_cmp",
    )(u, u, p["pe_k"], p["pe_v"], p["w1k"], p["w1v"], p["w2k"], p["w2v"], p["kc_g"], cosc, sinc)


def _nsa_attn_kernel(q_ref, kc_ref, vc_ref, ks_ref, vs_ref, kw_ref, vw_ref, g_ref, e_ref, ovl_ref, o_ref,
                     *, n_blk, n_sel, tks, ww):
    tq = q_ref.shape[2]
    n_pad = kc_ref.shape[1]
    i = pl.program_id(1)
    t = i * tq + lax.broadcasted_iota(I32, (1, tq), 1)
    lane = _lane((tq, LANES))
    gates_t = _sigmoid(g_ref[0]).T
    rep = B_HEADS // B_KV_HEADS
    heads = lambda x: jnp.concatenate([x] * rep, axis=1)
    ncmp = lax.broadcasted_iota(I32, (n_pad, 1), 0)
    valid_c = heads((ncmp * CMP_STRIDE + (CMP_LEN - 1)) <= t)
    kc = kc_ref[0]
    vc = vc_ref[0]
    blk = lax.broadcasted_iota(I32, (LANES, 1), 0)
    cur = t >> (SEL_LEN.bit_length() - 1)
    forced = jnp.where(blk == cur, 3e4, jnp.where(blk == cur - 1, 2e4, jnp.where(blk == 0, 1e4, 0.0)))
    adm = (blk * SEL_LEN <= t) & (blk < n_blk)
    n_rv = n_blk // 8
    sub = lax.broadcasted_iota(I32, (8, tq), 0)
    w0 = pl.multiple_of(jnp.maximum(i * tq + tq - ww, 0), tq)
    kp_w = w0 + lax.broadcasted_iota(I32, (ww, 1), 0)
    win_bias = heads(jnp.where((kp_w <= t) & (kp_w > t - WINDOW), 0.0, NEG))
    qs = jnp.concatenate([q_ref[0, h] for h in range(B_HEADS)], axis=0)
    gate = lambda c: jnp.concatenate([gates_t[h * 3 + c:h * 3 + c + 1, :] for h in range(B_HEADS)], axis=1)
    groups = lambda xs: jnp.concatenate([heads(x) for x in xs], axis=1)
    valid_all = groups([valid_c[:, 0:tq]] * B_KV_HEADS)
    s = jnp.where(valid_all, _nt(kc, qs), NEG)
    ex = jnp.exp2(s - jnp.max(s, axis=0, keepdims=True))
    pc = jnp.where(valid_all, ex / jnp.sum(ex, axis=0, keepdims=True), 0.0)
    out = gate(0) * _tn(vc, pc.astype(BF16))
    sel_ts = []
    for g in range(B_KV_HEADS):
        psum = pc[:, rep * g * tq:(rep * g + 1) * tq]
        for r in range(1, rep):
            psum = psum + pc[:, (rep * g + r) * tq:(rep * g + r + 1) * tq]
        p_hi = psum.astype(BF16)
        p_lo = (psum - p_hi.astype(F32)).astype(BF16)
        imp = _mm(ovl_ref[...], p_hi) + _mm(ovl_ref[...], p_lo)
        score_t = jnp.where(adm, imp + forced, NEG)
        sc = [score_t[8 * v:8 * v + 8] for v in range(n_rv)]
        rank = [jnp.zeros((8, tq), F32) for _ in range(n_rv)]
        for jp in range(n_blk):
            col = score_t[jp:jp + 1]
            for v in range(n_rv):
                if v > jp // 8:
                    beats = col >= sc[v]
                elif v < jp // 8:
                    beats = col > sc[v]
                else:
                    beats = (col > sc[v]) | ((col == sc[v]) & (sub > jp % 8))
                rank[v] = rank[v] + jnp.where(beats, 1.0, 0.0)
        sel_ts.append(jnp.where(jnp.concatenate(rank, axis=0) < n_sel, 1.0, 0.0).astype(BF16))

    tile = lambda j: pl.ds(pl.multiple_of(j * tks, tks), tks)

    def sel_body(j, carry):
        kp = j * tks + lax.broadcasted_iota(I32, (tks, 1), 0)
        bias = [jnp.where((_mm(e_ref[tile(j), :], sel_t) > 0.5) & (kp <= t), 0.0, NEG) for sel_t in sel_ts]
        return _softmax_step_t(qs, ks_ref[0, tile(j), :], vs_ref[0, tile(j), :], groups(bias), carry)

    n_tile = ((i + 1) * tq + tks - 1) // tks
    _, l_s, acc_s = lax.fori_loop(0, n_tile, sel_body, _flash_init_t(B_HEADS * tq))
    out = out + gate(1) * (acc_s / l_s)
    s_w = _nt(kw_ref[0, pl.ds(w0, ww), :], qs) + groups([win_bias[:, 0:tq]] * B_KV_HEADS)
    p_w = jnp.exp2(s_w - jnp.max(s_w, axis=0, keepdims=True))
    o_w = _tn(vw_ref[0, pl.ds(w0, ww), :], p_w.astype(BF16)) / jnp.sum(p_w, axis=0, keepdims=True)
    out = out + gate(2) * o_w
    for g in range(B_KV_HEADS):
        outs = [out[:, (rep * g + r) * tq:(rep * g + r + 1) * tq].T for r in range(rep)]
        for pp in range(rep // 2):
            a, bb = outs[2 * pp], outs[2 * pp + 1]
            if g == 0:
                bb = pltpu.roll(bb, B_HD, 1)
            else:
                a = pltpu.roll(a, B_HD, 1)
            c0 = (rep // 2 * g + pp) * LANES
            o_ref[0, :, c0:c0 + LANES] = jnp.where(lane < B_HD, a, bb).astype(BF16)


def _nsa_attn(u, u32, cols, qx, kcmp, vcmp, ks, kw):
    b, h, s, _ = qx.shape
    tq = min(256, s)
    tks = min(512, s)
    n_pad = kcmp.shape[1]
    n_blk = s // SEL_LEN
    n_sel = min(N_SEL, n_blk)
    n_cmp = (s - CMP_LEN) // CMP_STRIDE + 1
    ww = min(WINDOW + tq, s)
    expand = np.zeros((s, n_blk), np.float32)
    expand[np.arange(s), np.arange(s) // SEL_LEN] = 1.0
    nn = np.arange(n_pad)[None, :]
    jj = np.arange(LANES)[:, None]
    ovl = ((nn * CMP_STRIDE <= jj * SEL_LEN + SEL_LEN - 1) & (nn * CMP_STRIDE + CMP_LEN - 1 >= jj * SEL_LEN)
           & (jj < n_blk) & (nn < n_cmp)).astype(np.float32)
    full = lambda shape: pl.BlockSpec(shape, lambda bi, i: (bi,) + (0,) * (len(shape) - 1))
    ucol = lambda name: pl.BlockSpec((1, s, LANES), lambda bi, i, c=cols[name] // LANES: (bi, 0, c))
    const = lambda shape: pl.BlockSpec(shape, lambda bi, i: (0,) * len(shape))
    return pl.pallas_call(
        functools.partial(_nsa_attn_kernel, n_blk=n_blk, n_sel=n_sel, tks=tks, ww=ww),
        out_shape=jax.ShapeDtypeStruct((b, s, h * B_HD), BF16),
        grid=(b, s // tq),
        in_specs=[pl.BlockSpec((1, h, tq, LANES), lambda bi, i: (bi, 0, i, 0)),
                  full((1, n_pad, LANES)), full((1, n_pad, LANES)),
                  full((1, s, LANES)), ucol("b_vs"), full((1, s, LANES)), ucol("b_vw"),
                  pl.BlockSpec((1, tq, LANES), lambda bi, i: (bi, i, 0)),
                  const((s, n_blk)), const((LANES, n_pad))],
        out_specs=pl.BlockSpec((1, tq, h * B_HD), lambda bi, i: (bi, i, 0)),
        compiler_params=_cp("parallel", "parallel"),
        name="nsa_attn",
    )(qx, kcmp, vcmp, ks, u, kw, u, u32, jnp.asarray(expand, BF16), jnp.asarray(ovl, BF16))


def _dsa_prep_kernel(q_in, iq_in, kv_in, sm_in, cos64_ref, sin64_ref, cos32_ref, sin32_ref, qg_ref, kg_ref,
                     q_ref, kv_ref, iq_ref, ik_ref):
    tm = q_in.shape[1]
    lane = _lane((tm, LANES))
    c64, s64, c32, s32 = cos64_ref[0], sin64_ref[0], cos32_ref[0], sin32_ref[0]
    for p in range(C_HEADS // 2):
        y = _rope(_group_rms(q_in[0, :, p * LANES:(p + 1) * LANES].astype(F32), C_HD) * qg_ref[...], c64, s64, C_HD // 2)
        y = y * (C_HD ** -0.5 * LOG2E)
        q_ref[0, 2 * p] = jnp.where(lane < C_HD, y, 0.0).astype(BF16)
        q_ref[0, 2 * p + 1] = jnp.where(lane < C_HD, pltpu.roll(y, C_HD, 1), 0.0).astype(BF16)
    kv = kv_in[0].astype(F32)
    kn = _rope(_group_rms(kv, C_HD) * kg_ref[...], c64, s64, C_HD // 2)
    kv_ref[0] = jnp.where(lane < C_HD, kn, kv).astype(BF16)
    for c0 in range(0, IDX_HEADS * IDX_HD, LANES):
        iq_ref[0, :, c0:c0 + LANES] = _rope(iq_in[0, :, c0:c0 + LANES].astype(F32), c32, s32, IDX_HD // 2).astype(BF16)
    ik = jnp.where(lane < IDX_HD, _rope(sm_in[0], c32, s32, IDX_HD // 2), 0.0)
    ik = ik + pltpu.roll(ik, IDX_HD, 1)
    ik = ik + pltpu.roll(ik, 2 * IDX_HD, 1)
    ik_ref[0] = ik.astype(BF16)


def _dsa_prep(u, u32, cols, tabs, p):
    b, s, _ = u.shape
    tm = min(256, s)
    row = pl.BlockSpec((1, tm, LANES), lambda bi, i: (bi, i, 0))
    const = pl.BlockSpec((1, LANES), lambda bi, i: (0, 0))
    ublk = lambda name, w: pl.BlockSpec((1, tm, w), lambda bi, i, c=cols[name] // w: (bi, i, c))
    dense = jax.ShapeDtypeStruct((b, s, LANES), BF16)
    return pl.pallas_call(
        _dsa_prep_kernel,
        out_shape=(jax.ShapeDtypeStruct((b, C_HEADS, s, LANES), BF16), dense,
                   jax.ShapeDtypeStruct((b, s, IDX_HEADS * IDX_HD), BF16), dense),
        grid=(b, s // tm),
        in_specs=[ublk("c_q", 512), ublk("c_iq", 256), ublk("c_kv", LANES), row,
                  row, row, row, row, const, const],
        out_specs=(pl.BlockSpec((1, C_HEADS, tm, LANES), lambda bi, i: (bi, 0, i, 0)), row,
                   pl.BlockSpec((1, tm, IDX_HEADS * IDX_HD), lambda bi, i: (bi, i, 0)), row),
        compiler_params=_cp("parallel", "parallel"),
        name="dsa_prep",
    )(u, u, u, u32, tabs["cos64"], tabs["sin64"], tabs["cos32"], tabs["sin32"], p["q_g"], p["k_g"])


def _dsa_attn_kernel(q_ref, iq_ref, sm_ref, ik_ref, kv_ref, tri_ref, o_ref, key_scr, bias_scr, *, tk, topk):
    tq = q_ref.shape[2]
    i = pl.program_id(1)
    t = i * tq + lax.broadcasted_iota(I32, (1, tq), 1)
    lane = _lane((tq, LANES))
    n_tile = ((i + 1) * tq + tk - 1) // tk
    tile = lambda j: pl.ds(pl.multiple_of(j * tk, tk), tk)
    kpos = lambda j: j * tk + lax.broadcasted_iota(I32, (tk, 1), 0)

    iw_t = (sm_ref[0] * (IDX_HEADS ** -0.5)).T
    parts = []
    for h in range(IDX_HEADS):
        blk = iq_ref[0, :, (h // 4) * LANES:(h // 4 + 1) * LANES]
        lo = (h % 4) * IDX_HD
        parts.append(jnp.where((lane >= lo) & (lane < lo + IDX_HD), blk, jnp.zeros_like(blk)))
    iqs = jnp.concatenate(parts, axis=0)

    def idx_body(j, carry):
        lg = _nt(ik_ref[0, tile(j), :], iqs)
        acc = jnp.zeros((tk, tq), F32)
        for h in range(IDX_HEADS):
            acc = acc + iw_t[IDX_HD + h:IDX_HD + h + 1, :] * jnp.maximum(lg[:, h * tq:(h + 1) * tq], 0.0)
        sc = jnp.where(kpos(j) <= t, acc, NEG)
        bits = pltpu.bitcast(sc, I32)
        key = jnp.where(bits < 0, bits ^ 0x7FFFFFFF, bits)
        key_scr[tile(j), :] = jnp.where(sc == 0.0, 0, key)
        return carry

    lax.fori_loop(0, n_tile, idx_body, 0)

    tr = min(tk, 2 * tq)
    n_sub = ((i + 1) * tq + tr - 1) // tr

    def count_ge(thr_key):
        def body(j, cs):
            key = key_scr[pl.ds(pl.multiple_of(j * tr, tr), tr), :]
            cs = list(cs)
            for n, r0 in enumerate(range(0, tr, 8)):
                c = cs[n % len(cs)]
                cs[n % len(cs)] = jnp.where(key[r0:r0 + 8] >= thr_key, c + 1, c)
            return tuple(cs)

        cs = lax.fori_loop(0, n_sub, body, tuple(jnp.zeros((8, tq), I32) for _ in range(4)))
        return jnp.sum(cs[0] + cs[1] + cs[2] + cs[3], axis=0, keepdims=True)

    def bit_body(bi, ucand):
        utrial = ucand | jnp.left_shift(jnp.int32(1), 31 - bi)
        return jnp.where(count_ge(utrial ^ INT_MIN) >= topk, utrial, ucand)

    thr = lax.fori_loop(0, 32, bit_body, jnp.zeros((1, tq), I32)) ^ INT_MIN
    n_ge = count_ge(thr)
    n_gt = count_ge(thr + 1)
    need = topk - n_gt
    row_ok = (n_ge - n_gt == need) | (thr == NEG_KEY) | (n_ge < topk)
    simple = jnp.min(jnp.where(row_ok, 1.0, 0.0)) > 0.5

    def fast_bias():
        def body(j, carry):
            keep = (key_scr[tile(j), :] >= thr) & (kpos(j) <= t)
            bias_scr[tile(j), :] = jnp.where(keep, 0.0, NEG)
            return carry

        lax.fori_loop(0, n_tile, body, 0)

    def tie_bias():
        need_f = need.astype(F32)

        def body(j, run):
            key = key_scr[tile(j), :]
            kp = kpos(j)
            for c0 in range(0, tk, LANES):
                kc = key[c0:c0 + LANES]
                eq = kc == thr
                eq_f = jnp.where(eq, 1.0, 0.0)
                pref = _mm(tri_ref[...], eq_f.astype(BF16)) + run
                keep = ((kc > thr) | (eq & (pref <= need_f))) & (kp[c0:c0 + LANES] <= t)
                bias_scr[pl.ds(pl.multiple_of(j * tk + c0, LANES), LANES), :] = jnp.where(keep, 0.0, NEG)
                run = run + jnp.sum(eq_f, axis=0, keepdims=True)
            return run

        lax.fori_loop(0, n_tile, body, jnp.zeros((1, tq), F32))

    lax.cond(simple, fast_bias, tie_bias)

    qs = jnp.concatenate([q_ref[0, h] for h in range(C_HEADS)], axis=0)

    def att_body(j, carry):
        kv = kv_ref[0, tile(j), :]
        bias_t = jnp.concatenate([bias_scr[tile(j), :]] * C_HEADS, axis=1)
        return _softmax_step_t(qs, kv, kv, bias_t, carry)

    _, l, acc = lax.fori_loop(0, n_tile, att_body, _flash_init_t(C_HEADS * tq))
    o_t = acc / l
    for p in range(C_HEADS // 2):
        a = pltpu.roll(o_t[:, (2 * p) * tq:(2 * p + 1) * tq].T, C_HD, 1)
        bb = o_t[:, (2 * p + 1) * tq:(2 * p + 2) * tq].T
        o_ref[0, :, p * LANES:(p + 1) * LANES] = jnp.where(lane < C_HD, a, bb).astype(BF16)


def _dsa_attn(u32, qc, iq, ik, kv):
    b, h, s, _ = qc.shape
    tq = min(256, s)
    tk = min(1024, s)
    topk = min(TOPK_MAX, s // 4)
    tri = np.tril(np.ones((LANES, LANES), np.float32))
    full = lambda shape: pl.BlockSpec(shape, lambda bi, i: (bi,) + (0,) * (len(shape) - 1))
    return pl.pallas_call(
        functools.partial(_dsa_attn_kernel, tk=tk, topk=topk),
        out_shape=jax.ShapeDtypeStruct((b, s, h * C_HD), BF16),
        grid=(b, s // tq),
        in_specs=[pl.BlockSpec((1, h, tq, LANES), lambda bi, i: (bi, 0, i, 0)),
                  pl.BlockSpec((1, tq, IDX_HEADS * IDX_HD), lambda bi, i: (bi, i, 0)),
                  pl.BlockSpec((1, tq, LANES), lambda bi, i: (bi, i, 0)),
                  full((1, s, LANES)), full((1, s, LANES)),
                  pl.BlockSpec((LANES, LANES), lambda bi, i: (0, 0))],
        out_specs=pl.BlockSpec((1, tq, h * C_HD), lambda bi, i: (bi, i, 0)),
        scratch_shapes=[pltpu.VMEM((s, tq), I32), pltpu.VMEM((s, tq), F32)],
        compiler_params=_cp("parallel", "parallel"),
        name="dsa_attn",
    )(qc, iq, u32, ik, kv, jnp.asarray(tri, BF16))


def _mlstm_conv_kernel(x_ref, halo_ref, w_ref, b_ref, o_ref):
    tm = x_ref.shape[1]
    i = pl.program_id(1)
    lane = _lane((tm, x_ref.shape[2]))
    halo = jnp.where(i > 0, halo_ref[0].astype(F32), 0.0)
    x = x_ref[0].astype(F32)
    xc = jnp.concatenate([halo, x], axis=0)
    off = halo.shape[0] - (CONV_W - 1)
    y = b_ref[...] + jnp.zeros_like(x)
    for j in range(CONV_W):
        y = y + w_ref[j:j + 1, :] * xc[off + j:off + j + tm, :]
    y = y * _sigmoid(y)
    o_ref[0] = jnp.where(lane >= D_HEADS * D_QK, y * (D_QK ** -0.5), y)


def _mlstm_conv(u, qk_block, w, bias):
    b, s, _ = u.shape
    c = w.shape[1]
    tm = min(512, s)
    hb = 16
    return pl.pallas_call(
        _mlstm_conv_kernel,
        out_shape=jax.ShapeDtypeStruct((b, s, c), F32),
        grid=(b, s // tm),
        in_specs=[pl.BlockSpec((1, tm, c), lambda bi, i: (bi, i, qk_block)),
                  pl.BlockSpec((1, hb, c), lambda bi, i: (bi, jnp.maximum(i * (tm // hb) - 1, 0), qk_block)),
                  pl.BlockSpec((CONV_W, c), lambda bi, i: (0, 0)),
                  pl.BlockSpec((1, c), lambda bi, i: (0, 0))],
        out_specs=pl.BlockSpec((1, tm, c), lambda bi, i: (bi, i, 0)),
        compiler_params=_cp("parallel", "parallel"),
        name="mlstm_conv",
    )(u, u, w, bias)


def _mlstm_scan_kernel(qk_ref, v_ref, op_ref, sm_ref, gt_ref, bcol_ref, brow_ref, hg_ref, o_ref,
                       c_scr, n_scr, m_scr):
    tc = qk_ref.shape[1]
    L = CHUNK
    nqk = D_HEADS * D_QK

    @pl.when(pl.program_id(1) == 0)
    def _():
        c_scr[...] = jnp.zeros_like(c_scr)
        n_scr[...] = jnp.zeros_like(n_scr)
        m_scr[...] = jnp.zeros_like(m_scr)

    r = lax.broadcasted_iota(I32, (L, L), 0)
    c = lax.broadcasted_iota(I32, (L, L), 1)
    tril = c <= r
    tril_f = jnp.where(tril, 1.0, 0.0).astype(BF16)
    triu_f = jnp.where(r <= c, 1.0, 0.0).astype(BF16)
    lane = _lane((L, LANES))
    row128 = lax.broadcasted_iota(I32, (LANES, LANES), 0)
    i_lane, f_lane = IDX_HD + IDX_HEADS, IDX_HD + IDX_HEADS + D_HEADS

    def one(bb, ci):
        rows = pl.ds(pl.multiple_of(ci * L, L), L)
        qk = qk_ref[bb, rows, :]
        sm = sm_ref[bb, rows, :] + bcol_ref[...]
        bcol_all = sum(_mm(tril_f, part) for part in _split3(_log_sigmoid(sm)))
        gt = gt_ref[bb, ci] + brow_ref[...]
        brow_all = sum(_mm(part, triu_f) for part in _split3(_log_sigmoid(gt)))
        for p in range(D_HEADS // 2):
            qpair = qk[:, p * LANES:(p + 1) * LANES]
            kpair = qk[:, nqk + p * LANES:nqk + (p + 1) * LANES]
            kpair_b = kpair.astype(BF16)
            sp = bb * (D_HEADS // 2) + p
            c_prev = c_scr[sp]
            n_prev = n_scr[sp, 0:1, :]
            c_prev_b = c_prev.astype(BF16)
            upd = []
            for e in range(2):
                h = 2 * p + e
                in_half = (lane >= D_QK * e) & (lane < D_QK * (e + 1))
                qm = jnp.where(in_half, qpair, 0.0)
                qm_b = qm.astype(BF16)
                vh = v_ref[bb, rows, h * D_VD:(h + 1) * D_VD].astype(F32)
                b_col = jnp.broadcast_to(bcol_all[:, f_lane + h:f_lane + h + 1], (L, LANES))
                i_col = jnp.broadcast_to(sm[:, i_lane + h:i_lane + h + 1], (L, LANES))
                b_row = brow_all[D_HEADS + h:D_HEADS + h + 1, :]
                i_row = gt[h:h + 1, :]
                m_prev = m_scr[bb * D_HEADS + h, 0:1, :]
                dmat = jnp.where(tril, b_col[:, :L] - b_row + i_row, NEG)
                inter = b_col + m_prev
                m_t = jnp.maximum(inter, jnp.max(dmat, axis=-1, keepdims=True))
                a = jnp.exp(inter - m_t)
                w = _nt(qm_b, kpair_b) * jnp.exp(dmat - m_t[:, :L])
                num = a * _mm(qm_b, c_prev_b) + _mm(w.astype(BF16), vh.astype(BF16))
                den = a * jnp.sum(qm * n_prev, axis=-1, keepdims=True) + jnp.sum(w, axis=-1, keepdims=True)
                hout = num / jnp.maximum(jnp.abs(den), jnp.exp(-m_t))
                b_last = b_col[L - 1:L, :]
                g_col = b_last - b_col + i_col
                m_new = jnp.maximum(b_last + m_prev, jnp.max(g_col, axis=0, keepdims=True))
                ws = jnp.exp(g_col - m_new)
                decay = jnp.exp(b_last + m_prev - m_new)
                u_mat = _tn(kpair_b, (ws * vh).astype(BF16))
                k_sum = jnp.sum(ws * kpair, axis=0, keepdims=True)
                upd.append((decay * c_prev + u_mat, decay * n_prev + k_sum))
                m_scr[bb * D_HEADS + h] = jnp.broadcast_to(m_new, m_scr.shape[1:])
                hn = _row_rms(hout) * hg_ref[...]
                y = _sigmoid(op_ref[bb, rows, h * D_VD:(h + 1) * D_VD].astype(F32)) * hn
                o_ref[bb, rows, h * D_VD:(h + 1) * D_VD] = y.astype(BF16)
            c_scr[sp] = jnp.where(row128 < D_QK, upd[0][0], upd[1][0])
            n_new = jnp.where(lane[0:1] < D_QK, upd[0][1], upd[1][1])
            n_scr[sp] = jnp.broadcast_to(n_new, n_scr.shape[1:])

    def chunk(ci, carry):
        for bb in range(qk_ref.shape[0]):
            one(bb, ci)
        return carry

    lax.fori_loop(0, tc // L, chunk, 0)


def _mlstm_scan(u, u32, cols, qk, gt, p):
    b, s, _ = u.shape
    tc = min(256, s)
    nb = next(n for n in (8, 4, 2, 1) if b % n == 0)
    nv = D_HEADS * D_VD
    ublk = lambda name, w: pl.BlockSpec((nb, tc, w), lambda bi, i, c=cols[name] // w: (bi, i, c))
    const = lambda shape: pl.BlockSpec(shape, lambda bi, i: (0,) * len(shape))
    return pl.pallas_call(
        _mlstm_scan_kernel,
        out_shape=jax.ShapeDtypeStruct((b, s, nv), BF16),
        grid=(b // nb, s // tc),
        in_specs=[pl.BlockSpec((nb, tc, qk.shape[-1]), lambda bi, i: (bi, i, 0)),
                  ublk("d_v", nv), ublk("d_o", nv), pl.BlockSpec((nb, tc, LANES), lambda bi, i: (bi, i, 0)),
                  pl.BlockSpec((nb, tc // CHUNK, 8, CHUNK), lambda bi, i: (bi, i, 0, 0)),
                  const((1, LANES)), const((8, 1)), const((1, D_VD))],
        out_specs=pl.BlockSpec((nb, tc, nv), lambda bi, i: (bi, i, 0)),
        scratch_shapes=[pltpu.VMEM((nb * D_HEADS // 2, LANES, LANES), F32),
                        pltpu.VMEM((nb * D_HEADS // 2, 8, LANES), F32),
                        pltpu.VMEM((nb * D_HEADS, 8, LANES), F32)],
        compiler_params=_cp("parallel", "arbitrary"),
        name="mlstm_scan",
    )(qk, u, u, u32, gt, p["bias_col"], p["bias_row"], p["h_g"])


EVEN_SRC = dict(a_ql=(0, 256), a_kvl=(256, 128), a_kr=(384, 32), a_gate=(416, 512), b_q=(928, 512),
                b_kc=(1440, 128), b_vc=(1568, 128), b_ks=(1696, 128), b_vs=(1824, 128), b_kw=(1952, 128),
                b_vw=(2080, 128), b_g=(2208, 24), b_gate=(2232, 512), m_q=(2744, 256), m_gate=(3000, 256))
EVEN_DST = dict(a_ql=0, a_kvl=256, a_kr=384, b_q=512, a_gate=1024, b_gate=1536, m_q=2048, m_gate=2304,
                b_g=2560, b_kc=2688, b_vc=2816, b_ks=2944, b_vs=3072, b_kw=3200, b_vw=3328)
EVEN_COLS_PAD = 3456

ODD_SRC = dict(c_q=(0, 512), c_k=(512, 64), c_v=(576, 64), c_iq=(640, 256), c_ik=(896, 32), c_iw=(928, 8),
               c_gate=(936, 512), d_q=(1448, 256), d_k=(1704, 256), d_v=(1960, 512), d_i=(2472, 4),
               d_f=(2476, 4), d_o=(2480, 512), d_gate=(2992, 512), m_q=(3504, 256), m_gate=(3760, 256))
ODD_DST = dict(c_q=0, c_gate=512, d_gate=1024, d_v=1536, d_o=2048, d_q=2560, d_k=2816, m_q=3072, m_gate=3328,
               c_iq=3584, c_k=3840, c_v=3904, c_ik=3968, c_iw=4000, d_i=4008, d_f=4012)
ODD_COLS_PAD = 4096


def _permute_cols(w, src, dst, total):
    pieces, pos = [], 0
    for name in sorted(src, key=lambda n: dst[n]):
        start, width = src[name]
        if dst[name] > pos:
            pieces.append(jnp.zeros((w.shape[0], dst[name] - pos), BF16))
        pieces.append(w[:, start:start + width].astype(BF16))
        pos = dst[name] + width
    if total > pos:
        pieces.append(jnp.zeros((w.shape[0], total - pos), BF16))
    return jnp.concatenate(pieces, axis=1)


def _heads_split(w, heads, first):
    w3 = w.reshape(w.shape[0], heads, -1)
    return jnp.concatenate([w3[:, :, :first].reshape(w.shape[0], -1),
                            w3[:, :, first:].reshape(w.shape[0], -1)], axis=1).astype(BF16)


def _tile_lanes(v, reps):
    return jnp.tile(v.astype(F32).reshape(1, -1), (1, reps))


def _rope_tables(positions, d2):
    inv = ROPE_THETA ** (-jnp.arange(d2, dtype=F32) / d2)
    ang = positions.astype(F32)[..., None] * inv
    c, s = jnp.cos(ang), jnp.sin(ang)
    reps = LANES // (2 * d2)
    return (jnp.tile(jnp.concatenate([c, c], axis=-1), (1, 1, reps)),
            jnp.tile(jnp.concatenate([-s, s], axis=-1), (1, 1, reps)))


def _block_diag2(w):
    z = jnp.zeros_like(w)
    return jnp.concatenate([jnp.concatenate([w, z], axis=-1), jnp.concatenate([z, w], axis=-1)], axis=-2)


def kernel(x, mem, positions, ln_g, mem_norm_g, mem_w_kv, mem_q_norm_g, mem_k_norm_g, w_out, even_w_in, mla_q_lat_g, mla_kv_lat_g, mla_w_uq, mla_w_ukv, mla_q_norm_g, mla_k_norm_g, nsa_q_norm_g, nsa_k_norm_g, nsa_cmp_pos, nsa_cmp_w1, nsa_cmp_w2, odd_w_in, dsa_q_norm_g, dsa_k_norm_g, mlstm_conv_w, mlstm_conv_b, mlstm_i_bias, mlstm_f_bias, mlstm_h_norm_g):
    b, s, _ = x.shape
    depth = ln_g.shape[0]
    cos64, sin64 = _rope_tables(positions, 32)
    cos32, sin32 = _rope_tables(positions, 16)
    tabs = dict(cos64=cos64, sin64=sin64, cos32=cos32, sin32=sin32)
    n_pad = s // CMP_STRIDE
    cmp_pos = jnp.pad(positions[:, CMP_LEN - 1::CMP_STRIDE], ((0, 0), (0, 0)))[:, :n_pad]
    cmp_pos = jnp.pad(cmp_pos, ((0, 0), (0, n_pad - cmp_pos.shape[1])))
    cosc, sinc = _rope_tables(cmp_pos, 32)

    mem_k, mem_v = _mem_kv(mem, mem_norm_g.reshape(depth, 1, -1), mem_w_kv.astype(BF16),
                           jnp.tile(mem_k_norm_g, (1, 2)).reshape(depth, 1, LANES))

    for layer in range(depth):
        li = layer // 2
        g_ln = ln_g[layer].reshape(1, -1)
        mq_g = _tile_lanes(mem_q_norm_g[layer], 2)
        if layer % 2 == 0:
            cols = EVEN_DST
            u, u32 = _in_proj(x, g_ln, _permute_cols(even_w_in[li], EVEN_SRC, EVEN_DST, EVEN_COLS_PAD), cols["b_g"])
            pa = dict(q_lat_g=mla_q_lat_g[li].reshape(1, -1), kv_lat_g=mla_kv_lat_g[li].reshape(1, -1),
                      w_uq=_heads_split(mla_w_uq[li], A_HEADS, A_NOPE), w_ukv=_heads_split(mla_w_ukv[li], A_HEADS, A_NOPE),
                      qn_g=_tile_lanes(mla_q_norm_g[li, :A_NOPE], 2), qr_g=_tile_lanes(mla_q_norm_g[li, A_NOPE:], 4),
                      kn_g=_tile_lanes(mla_k_norm_g[li, :A_NOPE], 2), kr_g=_tile_lanes(mla_k_norm_g[li, A_NOPE:], 4))
            qa, ka, va = _mla_prep(u, cos32, sin32, pa)
            y1 = _mla_attn(qa, ka, va)
            pb = dict(q_g=_tile_lanes(nsa_q_norm_g[li], 2), ks_g=_tile_lanes(nsa_k_norm_g[li, 1], 2),
                      kw_g=_tile_lanes(nsa_k_norm_g[li, 2], 2), kc_g=_tile_lanes(nsa_k_norm_g[li, 0], 2),
                      pe_k=jnp.tile(nsa_cmp_pos[li, 0], (1, 2)), pe_v=jnp.tile(nsa_cmp_pos[li, 1], (1, 2)),
                      w1k=_block_diag2(nsa_cmp_w1[li, 0].reshape(CMP_LEN, B_HD, B_HD)).astype(BF16),
                      w1v=_block_diag2(nsa_cmp_w1[li, 1].reshape(CMP_LEN, B_HD, B_HD)).astype(BF16),
                      w2k=_block_diag2(nsa_cmp_w2[li, 0]).astype(BF16),
                      w2v=_block_diag2(nsa_cmp_w2[li, 1]).astype(BF16))
            qb, ks, kw = _nsa_prep(u, cols, cos64, sin64, pb)
            kcmp, vcmp = _nsa_cmp(u, cols, cosc, sinc, pb)
            y2 = _nsa_attn(u, u32, cols, qb, kcmp, vcmp, ks, kw)
            gate_blocks = (cols["a_gate"] // 512, cols["b_gate"] // 512, cols["m_gate"] // 256)
        else:
            cols = dict(ODD_DST, c_kv=ODD_DST["c_k"], small=ODD_DST["c_ik"])
            u, u32 = _in_proj(x, g_ln, _permute_cols(odd_w_in[li], ODD_SRC, ODD_DST, ODD_COLS_PAD), cols["small"])
            pc = dict(q_g=_tile_lanes(dsa_q_norm_g[li], 2), k_g=_tile_lanes(dsa_k_norm_g[li], 2))
            qc, kvc, iq, ik = _dsa_prep(u, u32, cols, tabs, pc)
            y1 = _dsa_attn(u32, qc, iq, ik, kvc)
            qk = _mlstm_conv(u, cols["d_q"] // 512, mlstm_conv_w[li], mlstm_conv_b[li].reshape(1, -1))
            gates = u32[:, :, cols["d_i"] - cols["small"]:cols["d_i"] - cols["small"] + 2 * D_HEADS]
            gt = gates.reshape(b, s // CHUNK, CHUNK, 2 * D_HEADS).transpose(0, 1, 3, 2)
            bias8 = jnp.concatenate([mlstm_i_bias[li], mlstm_f_bias[li]]).astype(F32)
            bias_col = jnp.zeros((1, LANES), F32).at[0, cols["d_i"] - cols["small"]:cols["d_i"] - cols["small"] + 8].set(bias8)
            pd = dict(bias_col=bias_col, bias_row=bias8.reshape(8, 1), h_g=mlstm_h_norm_g[li].reshape(1, -1))
            y2 = _mlstm_scan(u, u32, cols, qk, gt, pd)
            gate_blocks = (cols["c_gate"] // 512, cols["d_gate"] // 512, cols["m_gate"] // 256)
        ym = _mem_attn(u, cols["m_q"] // 256, mem_k, mem_v, layer, mq_g)
        x = _out_proj(x, y1, y2, ym, u, gate_blocks, w_out[layer].astype(BF16))
    return x
```

```python
import functools

import numpy as np
import jax
import jax.numpy as jnp
from jax import lax
from jax.experimental import pallas as pl
from jax.experimental.pallas import tpu as pltpu

F32, BF16, I32 = jnp.float32, jnp.bfloat16, jnp.int32
NEG = -1e30
EPS = 1e-6
ROPE_THETA = 10000.0
LANES = 128
VMEM_LIMIT_BYTES = 48 * 1024 * 1024

D_MODEL = 1024
DEPTH = 4
A_HEADS, A_NOPE, A_ROPE, A_VD, A_QLAT, A_KVLAT = 8, 64, 32, 64, 256, 128
B_HEADS, B_KV_HEADS, B_HD = 8, 2, 64
CMP_LEN, CMP_STRIDE, SEL_LEN, N_SEL, WINDOW = 32, 16, 64, 16, 512
C_HEADS, C_HD, IDX_HEADS, IDX_HD, TOPK_MAX = 8, 64, 8, 32, 256
D_HEADS, D_QK, D_VD, CONV_W, CHUNK = 4, 64, 128, 4, 64
M_HEADS, M_HD = 4, 64

INT_MIN = np.int32(-2 ** 31)
NEG_KEY = int(np.float32(NEG).view(np.int32) ^ np.int32(0x7FFFFFFF))
LOG2E = float(np.log2(np.e))


def _cp(*sem):
    return pltpu.CompilerParams(dimension_semantics=sem, vmem_limit_bytes=VMEM_LIMIT_BYTES)


def _nt(a, b):
    return lax.dot_general(a, b, (((1,), (1,)), ((), ())), preferred_element_type=F32)


def _tn(a, b):
    return lax.dot_general(a, b, (((0,), (0,)), ((), ())), preferred_element_type=F32)


def _mm(a, b):
    return jnp.dot(a, b, preferred_element_type=F32)


def _sigmoid(x):
    return 1.0 / (1.0 + jnp.exp(-x))


def _log_sigmoid(x):
    return jnp.minimum(x, 0.0) - jnp.log1p(jnp.exp(-jnp.abs(x)))


def _lane(shape):
    return lax.broadcasted_iota(I32, shape, len(shape) - 1)


def _group_mat(gs):
    r = lax.broadcasted_iota(I32, (LANES, LANES), 0)
    c = lax.broadcasted_iota(I32, (LANES, LANES), 1)
    sh = gs.bit_length() - 1
    return jnp.where((r >> sh) == (c >> sh), 1.0, 0.0).astype(BF16)


def _split3(x):
    hi = x.astype(BF16)
    r1 = x - hi.astype(F32)
    mid = r1.astype(BF16)
    return hi, mid, (r1 - mid.astype(F32)).astype(BF16)


def _mm_split(x, w01):
    hi = x.astype(BF16)
    lo = (x - hi.astype(F32)).astype(BF16)
    return _mm(hi, w01) + _mm(lo, w01)


def _group_rms(x, gs):
    ss = _mm_split(x * x, _group_mat(gs))
    return x * lax.rsqrt(ss * (1.0 / gs) + EPS)


def _rope(x, cosp, sinp, half):
    lane = _lane(x.shape)
    rot = jnp.where((lane & (2 * half - 1)) < half,
                    pltpu.roll(x, LANES - half, 1), pltpu.roll(x, half, 1))
    return x * cosp + rot * sinp


def _row_rms(x):
    return x * lax.rsqrt(jnp.mean(x * x, axis=-1, keepdims=True) + EPS)


def _in_proj_kernel(x_ref, g_ref, w_ref, o_ref, o32_ref, *, c32):
    h = (_row_rms(x_ref[0]) * g_ref[...]).astype(BF16)
    ncol = o_ref.shape[-1]
    for c0 in range(0, ncol, 1024):
        c1 = min(ncol, c0 + 1024)
        o_ref[0, :, c0:c1] = _mm(h, w_ref[:, c0:c1]).astype(BF16)
    o32_ref[0] = _mm(h, w_ref[:, c32:c32 + LANES])


def _in_proj(x, g, w, c32):
    b, s, d = x.shape
    c = w.shape[1]
    tm = min(1024, s)
    return pl.pallas_call(
        functools.partial(_in_proj_kernel, c32=c32),
        out_shape=(jax.ShapeDtypeStruct((b, s, c), BF16), jax.ShapeDtypeStruct((b, s, LANES), F32)),
        grid=(b, s // tm),
        in_specs=[pl.BlockSpec((1, tm, d), lambda bi, i: (bi, i, 0)),
                  pl.BlockSpec((1, d), lambda bi, i: (0, 0)),
                  pl.BlockSpec((d, c), lambda bi, i: (0, 0))],
        out_specs=(pl.BlockSpec((1, tm, c), lambda bi, i: (bi, i, 0)),
                   pl.BlockSpec((1, tm, LANES), lambda bi, i: (bi, i, 0))),
        compiler_params=_cp("parallel", "parallel"),
        name="in_proj",
    )(x, g, w)


def _out_proj_kernel(x_ref, y1_ref, y2_ref, ym_ref, g1_ref, g2_ref, gm_ref, w_ref, o_ref):
    def gated(y_ref, g_ref):
        g = g_ref[0].astype(F32)
        return (y_ref[0].astype(F32) * (g * _sigmoid(g))).astype(BF16)

    n1 = y1_ref.shape[-1]
    n2 = y2_ref.shape[-1]
    acc = x_ref[0]
    acc = acc + _mm(gated(y1_ref, g1_ref), w_ref[0:n1, :])
    acc = acc + _mm(gated(y2_ref, g2_ref), w_ref[n1:n1 + n2, :])
    acc = acc + _mm(gated(ym_ref, gm_ref), w_ref[n1 + n2:, :])
    o_ref[0] = acc


def _out_proj(x, y1, y2, ym, u, gate_blocks, w):
    b, s, d = x.shape
    tm = min(512, s)
    i1, i2, im = gate_blocks
    n1, n2, nm = y1.shape[-1], y2.shape[-1], ym.shape[-1]
    row = lambda bi, i: (bi, i, 0)
    return pl.pallas_call(
        _out_proj_kernel,
        out_shape=jax.ShapeDtypeStruct((b, s, d), F32),
        grid=(b, s // tm),
        in_specs=[pl.BlockSpec((1, tm, d), row),
                  pl.BlockSpec((1, tm, n1), row),
                  pl.BlockSpec((1, tm, n2), row),
                  pl.BlockSpec((1, tm, nm), row),
                  pl.BlockSpec((1, tm, n1), lambda bi, i: (bi, i, i1)),
                  pl.BlockSpec((1, tm, n2), lambda bi, i: (bi, i, i2)),
                  pl.BlockSpec((1, tm, nm), lambda bi, i: (bi, i, im)),
                  pl.BlockSpec(w.shape, lambda bi, i: (0, 0))],
        out_specs=pl.BlockSpec((1, tm, d), row),
        compiler_params=_cp("parallel", "parallel"),
        name="out_proj",
    )(x, y1, y2, ym, u, u, u, w)


def _mem_kv_kernel(mem_ref, g_ref, w_ref, kg_ref, k_ref, v_ref):
    h = (_row_rms(mem_ref[0]) * g_ref[0]).astype(BF16)
    kv = _mm(h, w_ref[0])
    nk = k_ref.shape[-1]
    for c0 in range(0, nk, LANES):
        k_ref[0, 0, :, c0:c0 + LANES] = (_group_rms(kv[:, c0:c0 + LANES], M_HD) * kg_ref[0]).astype(BF16)
    v_ref[0, 0] = kv[:, nk:].astype(BF16)


def _mem_kv(mem, g, w, kg):
    b, m, d = mem.shape
    depth = w.shape[0]
    nk = M_HEADS * M_HD
    out = jax.ShapeDtypeStruct((depth, b, m, nk), BF16)
    return pl.pallas_call(
        _mem_kv_kernel,
        out_shape=(out, out),
        grid=(depth, b),
        in_specs=[pl.BlockSpec((1, m, d), lambda l, bi: (bi, 0, 0)),
                  pl.BlockSpec((1, 1, d), lambda l, bi: (l, 0, 0)),
                  pl.BlockSpec((1, d, 2 * nk), lambda l, bi: (l, 0, 0)),
                  pl.BlockSpec((1, 1, LANES), lambda l, bi: (l, 0, 0))],
        out_specs=(pl.BlockSpec((1, 1, m, nk), lambda l, bi: (l, bi, 0, 0)),
                   pl.BlockSpec((1, 1, m, nk), lambda l, bi: (l, bi, 0, 0))),
        compiler_params=_cp("parallel", "parallel"),
        name="mem_kv",
    )(mem, g, w, kg)


def _mem_attn_kernel(q_ref, k_ref, v_ref, qg_ref, o_ref):
    tq = q_ref.shape[1]
    lane = _lane((tq, LANES))
    row = lax.broadcasted_iota(I32, (LANES, tq), 0)
    parts = []
    for p in range(M_HEADS // 2):
        sl = slice(p * LANES, (p + 1) * LANES)
        q = (_group_rms(q_ref[0, :, sl].astype(F32), M_HD) * qg_ref[...] * (M_HD ** -0.5 * LOG2E)).astype(BF16)
        for e in range(2):
            in_half = (lane >= M_HD * e) & (lane < M_HD * (e + 1))
            parts.append(_nt(k_ref[0, 0, :, sl], jnp.where(in_half, q, jnp.zeros_like(q))))
    s = jnp.concatenate(parts, axis=1)
    ex = jnp.exp2(s - jnp.max(s, axis=0, keepdims=True))
    inv = 1.0 / jnp.sum(ex, axis=0, keepdims=True)
    for p in range(M_HEADS // 2):
        sl = slice(p * LANES, (p + 1) * LANES)
        pv = _tn(v_ref[0, 0, :, sl], ex[:, 2 * p * tq:(2 * p + 2) * tq].astype(BF16)) * inv[:, 2 * p * tq:(2 * p + 2) * tq]
        o_ref[0, :, sl] = jnp.where(row < M_HD, pv[:, :tq], pv[:, tq:]).T.astype(BF16)


def _mem_attn(u, q_block, k, v, layer, qg):
    b, s, _ = u.shape
    m, nk = k.shape[2], k.shape[3]
    tq = min(512, s)
    return pl.pallas_call(
        _mem_attn_kernel,
        out_shape=jax.ShapeDtypeStruct((b, s, nk), BF16),
        grid=(b, s // tq),
        in_specs=[pl.BlockSpec((1, tq, nk), lambda bi, i: (bi, i, q_block)),
                  pl.BlockSpec((1, 1, m, nk), lambda bi, i: (layer, bi, 0, 0)),
                  pl.BlockSpec((1, 1, m, nk), lambda bi, i: (layer, bi, 0, 0)),
                  pl.BlockSpec((1, LANES), lambda bi, i: (0, 0))],
        out_specs=pl.BlockSpec((1, tq, nk), lambda bi, i: (bi, i, 0)),
        compiler_params=_cp("parallel", "parallel"),
        name="mem_attn",
    )(u, k, v, qg)


def _flash_init_t(cols):
    return (jnp.full((1, cols), NEG, F32), jnp.zeros((1, cols), F32), jnp.zeros((LANES, cols), F32))


def _softmax_step_t(q, k, v, bias_t, carry):
    s = _nt(k, q)
    if bias_t is not None:
        s = s + bias_t
    m, l, acc = carry
    m_new = jnp.maximum(m, jnp.max(s, axis=0, keepdims=True))
    alpha = jnp.exp2(m - m_new)
    p = jnp.exp2(s - m_new)
    l = alpha * l + jnp.sum(p, axis=0, keepdims=True)
    acc = alpha * acc + _tn(v, p.astype(BF16))
    return m_new, l, acc


def _mla_prep_kernel(u_ref, cos_ref, sin_ref, qlg_ref, kvlg_ref, wuq_ref, wukv_ref,
                     qng_ref, qrg_ref, kng_ref, krg_ref, q_ref, k_ref, v_ref):
    tm = u_ref.shape[1]
    lane = _lane((tm, LANES))
    cosp, sinp = cos_ref[0], sin_ref[0]
    half = A_ROPE // 2
    scale = (A_NOPE + A_ROPE) ** -0.5 * LOG2E
    ql = (_row_rms(u_ref[0, :, 0:A_QLAT].astype(F32)) * qlg_ref[...]).astype(BF16)
    kvl = (_row_rms(u_ref[0, :, A_QLAT:A_QLAT + A_KVLAT].astype(F32)) * kvlg_ref[...]).astype(BF16)
    q = _mm(ql, wuq_ref[...])
    kv = _mm(kvl, wukv_ref[...])
    n_nope = A_HEADS * A_NOPE
    v_ref[0] = kv[:, n_nope:].astype(BF16)
    kr = u_ref[0, :, A_QLAT + A_KVLAT:A_QLAT + A_KVLAT + LANES].astype(F32)
    kpe = _rope(_group_rms(kr, A_ROPE) * krg_ref[...], cosp, sinp, half)
    kpe = pltpu.roll(kpe, A_NOPE, 1)
    qn = [_group_rms(q[:, c:c + LANES], A_NOPE) * qng_ref[...] for c in range(0, n_nope, LANES)]
    kn = [_group_rms(kv[:, c:c + LANES], A_NOPE) * kng_ref[...] for c in range(0, n_nope, LANES)]
    qr = [_rope(_group_rms(q[:, n_nope + c:n_nope + c + LANES], A_ROPE) * qrg_ref[...], cosp, sinp, half)
          for c in range(0, A_HEADS * A_ROPE, LANES)]
    for h in range(A_HEADS):
        qn_h = qn[h // 2] if h % 2 == 0 else pltpu.roll(qn[h // 2], A_NOPE, 1)
        kn_h = kn[h // 2] if h % 2 == 0 else pltpu.roll(kn[h // 2], A_NOPE, 1)
        shift = (A_NOPE - (h % 4) * A_ROPE) % LANES
        qr_h = qr[h // 4] if shift == 0 else pltpu.roll(qr[h // 4], shift, 1)
        qf = jnp.where(lane < A_NOPE, qn_h, jnp.where(lane < A_NOPE + A_ROPE, qr_h, 0.0))
        kf = jnp.where(lane < A_NOPE, kn_h, jnp.where(lane < A_NOPE + A_ROPE, kpe, 0.0))
        q_ref[0, h] = (qf * scale).astype(BF16)
        k_ref[0, h] = kf.astype(BF16)


def _mla_prep(u, cos32, sin32, p):
    b, s, _ = u.shape
    tm = min(256, s)
    hd = jax.ShapeDtypeStruct((b, A_HEADS, s, LANES), BF16)
    const = lambda shape: pl.BlockSpec(shape, lambda bi, i: (0,) * len(shape))
    return pl.pallas_call(
        _mla_prep_kernel,
        out_shape=(hd, hd, jax.ShapeDtypeStruct((b, s, A_HEADS * A_VD), BF16)),
        grid=(b, s // tm),
        in_specs=[pl.BlockSpec((1, tm, 512), lambda bi, i: (bi, i, 0)),
                  pl.BlockSpec((1, tm, LANES), lambda bi, i: (bi, i, 0)),
                  pl.BlockSpec((1, tm, LANES), lambda bi, i: (bi, i, 0)),
                  const((1, A_QLAT)), const((1, A_KVLAT)),
                  const(p["w_uq"].shape), const(p["w_ukv"].shape),
                  const((1, LANES)), const((1, LANES)), const((1, LANES)), const((1, LANES))],
        out_specs=(pl.BlockSpec((1, A_HEADS, tm, LANES), lambda bi, i: (bi, 0, i, 0)),
                   pl.BlockSpec((1, A_HEADS, tm, LANES), lambda bi, i: (bi, 0, i, 0)),
                   pl.BlockSpec((1, tm, A_HEADS * A_VD), lambda bi, i: (bi, i, 0))),
        compiler_params=_cp("parallel", "parallel"),
        name="mla_prep",
    )(u, cos32, sin32, p["q_lat_g"], p["kv_lat_g"], p["w_uq"], p["w_ukv"],
      p["qn_g"], p["qr_g"], p["kn_g"], p["kr_g"])


def _mla_attn_kernel(q_ref, k_ref, v_ref, o_ref, *, tk):
    tq = q_ref.shape[2]
    per_q = tq // tk
    i = pl.program_id(2)
    lane = _lane((tq, LANES))
    t = i * tq + lax.broadcasted_iota(I32, (1, tq), 1)
    nh = q_ref.shape[1]
    qs = [q_ref[0, e] for e in range(nh)]
    tile = lambda j: pl.ds(pl.multiple_of(j * tk, tk), tk)
    v_at = lambda j, e: v_ref[0, tile(j), (e // 2) * LANES:(e // 2 + 1) * LANES]

    def step(j, q0, bias_t, c):
        m, l, acc = c
        w = tq - q0
        s = jnp.concatenate([_nt(k_ref[0, e, tile(j), :], qs[e][q0:]) for e in range(nh)], axis=1)
        if bias_t is not None:
            s = s + jnp.concatenate([bias_t] * nh, axis=1)
        m_new = jnp.maximum(m, jnp.max(s, axis=0, keepdims=True))
        alpha = jnp.exp2(m - m_new)
        p = jnp.exp2(s - m_new)
        l = alpha * l + jnp.sum(p, axis=0, keepdims=True)
        pv = jnp.concatenate([_tn(v_at(j, 2 * pp), p[:, 2 * pp * w:(2 * pp + 2) * w].astype(BF16))
                              for pp in range(nh // 2)], axis=1)
        return m_new, l, alpha * acc + pv

    carry = lax.fori_loop(0, i * per_q, lambda j, c: step(j, 0, None, c), _flash_init_t(nh * tq))
    for d in range(per_q):
        j = i * per_q + d
        q0 = d * tk
        bias_t = jnp.where(j * tk + lax.broadcasted_iota(I32, (tk, 1), 0) <= t[:, q0:], 0.0, NEG)
        take = lambda x: jnp.concatenate([x[:, e * tq + q0:(e + 1) * tq] for e in range(nh)], axis=1)
        upd = step(j, q0, bias_t, tuple(take(x) for x in carry))
        if d:
            w = tq - q0
            upd = tuple(jnp.concatenate(
                [part for e in range(nh) for part in (old[:, e * tq:e * tq + q0], new[:, e * w:(e + 1) * w])], axis=1)
                for old, new in zip(carry, upd))
        carry = upd
    res = [(carry[2][:, e * tq:(e + 1) * tq] / carry[1][:, e * tq:(e + 1) * tq]).T for e in range(nh)]
    for p in range(nh // 2):
        o_ref[0, :, p * LANES:(p + 1) * LANES] = jnp.where(lane < A_VD, res[2 * p], res[2 * p + 1]).astype(BF16)


def _mla_attn(q, k, v):
    b, h, s, _ = q.shape
    tq = min(1024, s)
    tk = min(512, s)
    nh = 4
    return pl.pallas_call(
        functools.partial(_mla_attn_kernel, tk=tk),
        out_shape=jax.ShapeDtypeStruct((b, s, h * A_VD), BF16),
        grid=(b, h // nh, s // tq),
        in_specs=[pl.BlockSpec((1, nh, tq, LANES), lambda bi, p, i: (bi, p, i, 0)),
                  pl.BlockSpec((1, nh, s, LANES), lambda bi, p, i: (bi, p, 0, 0)),
                  pl.BlockSpec((1, s, nh * A_VD), lambda bi, p, i: (bi, 0, p))],
        out_specs=pl.BlockSpec((1, tq, nh * A_VD), lambda bi, p, i: (bi, i, p)),
        compiler_params=_cp("parallel", "parallel", "parallel"),
        name="mla_attn",
    )(q, k, v)


def _nsa_prep_kernel(q_in, ks_in, kw_in, cos_ref, sin_ref, qg_ref, ksg_ref, kwg_ref, q_ref, ks_ref, kw_ref):
    tm = q_in.shape[1]
    lane = _lane((tm, LANES))
    cosp, sinp = cos_ref[0], sin_ref[0]
    half = B_HD // 2
    rep = B_HEADS // B_KV_HEADS
    for p in range(B_HEADS // 2):
        y = _rope(_group_rms(q_in[0, :, p * LANES:(p + 1) * LANES].astype(F32), B_HD) * qg_ref[...], cosp, sinp, half)
        y = y * (B_HD ** -0.5 * LOG2E)
        y_sw = pltpu.roll(y, B_HD, 1)
        for e in range(2):
            h = 2 * p + e
            g = h // rep
            src = y if e == g else y_sw
            in_grp = (lane >= B_HD * g) & (lane < B_HD * (g + 1))
            q_ref[0, h] = jnp.where(in_grp, src, 0.0).astype(BF16)
    ks_ref[0] = _rope(_group_rms(ks_in[0].astype(F32), B_HD) * ksg_ref[...], cosp, sinp, half).astype(BF16)
    kw_ref[0] = _rope(_group_rms(kw_in[0].astype(F32), B_HD) * kwg_ref[...], cosp, sinp, half).astype(BF16)


def _nsa_prep(u, cols, cos64, sin64, p):
    b, s, _ = u.shape
    tm = min(256, s)
    blk = lambda name: pl.BlockSpec((1, tm, LANES), lambda bi, i, c=cols[name] // LANES: (bi, i, c))
    row = pl.BlockSpec((1, tm, LANES), lambda bi, i: (bi, i, 0))
    const = pl.BlockSpec((1, LANES), lambda bi, i: (0, 0))
    kvs = jax.ShapeDtypeStruct((b, s, LANES), BF16)
    return pl.pallas_call(
        _nsa_prep_kernel,
        out_shape=(jax.ShapeDtypeStruct((b, B_HEADS, s, LANES), BF16), kvs, kvs),
        grid=(b, s // tm),
        in_specs=[pl.BlockSpec((1, tm, 512), lambda bi, i, c=cols["b_q"] // 512: (bi, i, c)),
                  blk("b_ks"), blk("b_kw"), row, row, const, const, const],
        out_specs=(pl.BlockSpec((1, B_HEADS, tm, LANES), lambda bi, i: (bi, 0, i, 0)), row, row),
        compiler_params=_cp("parallel", "parallel"),
        name="nsa_prep",
    )(u, u, u, cos64, sin64, p["q_g"], p["ks_g"], p["kw_g"])


def _nsa_cmp_kernel(kc_in, vc_in, pek_ref, pev_ref, w1k_ref, w1v_ref, w2k_ref, w2v_ref, kg_ref,
                    cos_ref, sin_ref, ko_ref, vo_ref, pad_ref):
    s = kc_in.shape[1]
    n_pad = ko_ref.shape[1]

    def compress(x_in, pe_ref, w1_ref, w2_ref):
        pad_ref[0:s, :] = x_in[0].astype(F32)
        pad_ref[s:s + CMP_STRIDE, :] = jnp.zeros((CMP_STRIDE, LANES), F32)
        acc = jnp.zeros((n_pad, LANES), F32)
        for l in range(CMP_LEN):
            xl = pad_ref[pl.ds(l, n_pad, stride=CMP_STRIDE), :] + pe_ref[l:l + 1, :]
            acc = acc + _mm(xl.astype(BF16), w1_ref[l])
        mid = acc * _sigmoid(acc)
        return _mm(mid.astype(BF16), w2_ref[...])

    kc = compress(kc_in, pek_ref, w1k_ref, w2k_ref)
    ko_ref[0] = _rope(_group_rms(kc, B_HD) * kg_ref[...], cos_ref[0], sin_ref[0], B_HD // 2).astype(BF16)
    vo_ref[0] = compress(vc_in, pev_ref, w1v_ref, w2v_ref).astype(BF16)


def _nsa_cmp(u, cols, cosc, sinc, p):
    b, s, _ = u.shape
    n_pad = s // CMP_STRIDE
    blk = lambda name: pl.BlockSpec((1, s, LANES), lambda bi, c=cols[name] // LANES: (bi, 0, c))
    const = lambda shape: pl.BlockSpec(shape, lambda bi: (0,) * len(shape))
    out = jax.ShapeDtypeStruct((b, n_pad, LANES), BF16)
    ospec = pl.BlockSpec((1, n_pad, LANES), lambda bi: (bi, 0, 0))
    return pl.pallas_call(
        _nsa_cmp_kernel,
        out_shape=(out, out),
        grid=(b,),
        in_specs=[blk("b_kc"), blk("b_vc"), const((CMP_LEN, LANES)), const((CMP_LEN, LANES)),
                  const((CMP_LEN, LANES, LANES)), const((CMP_LEN, LANES, LANES)),
                  const((LANES, LANES)), const((LANES, LANES)), const((1, LANES)), ospec, ospec],
        out_specs=(ospec, ospec),
        scratch_shapes=[pltpu.VMEM((s + CMP_STRIDE, LANES), F32)],
        compiler_params=_cp("parallel"),
        name="nsa_cmp",
    )(u, u, p["pe_k"], p["pe_v"], p["w1k"], p["w1v"], p["w2k"], p["w2v"], p["kc_g"], cosc, sinc)


def _nsa_attn_kernel(q_ref, kc_ref, vc_ref, ks_ref, vs_ref, kw_ref, vw_ref, g_ref, e_ref, ovl_ref, o_ref,
                     *, n_blk, n_sel, tks, ww):
    tq = q_ref.shape[2]
    n_pad = kc_ref.shape[1]
    i = pl.program_id(1)
    t = i * tq + lax.broadcasted_iota(I32, (1, tq), 1)
    lane = _lane((tq, LANES))
    gates_t = _sigmoid(g_ref[0]).T
    rep = B_HEADS // B_KV_HEADS
    heads = lambda x: jnp.concatenate([x] * rep, axis=1)
    ncmp = lax.broadcasted_iota(I32, (n_pad, 1), 0)
    valid_c = heads((ncmp * CMP_STRIDE + (CMP_LEN - 1)) <= t)
    kc = kc_ref[0]
    vc = vc_ref[0]
    blk = lax.broadcasted_iota(I32, (LANES, 1), 0)
    cur = t >> (SEL_LEN.bit_length() - 1)
    forced = jnp.where(blk == cur, 3e4, jnp.where(blk == cur - 1, 2e4, jnp.where(blk == 0, 1e4, 0.0)))
    adm = (blk * SEL_LEN <= t) & (blk < n_blk)
    n_rv = n_blk // 8
    sub = lax.broadcasted_iota(I32, (8, tq), 0)
    w0 = pl.multiple_of(jnp.maximum(i * tq + tq - ww, 0), tq)
    kp_w = w0 + lax.broadcasted_iota(I32, (ww, 1), 0)
    win_bias = heads(jnp.where((kp_w <= t) & (kp_w > t - WINDOW), 0.0, NEG))
    qs = jnp.concatenate([q_ref[0, h] for h in range(B_HEADS)], axis=0)
    gate = lambda c: jnp.concatenate([gates_t[h * 3 + c:h * 3 + c + 1, :] for h in range(B_HEADS)], axis=1)
    groups = lambda xs: jnp.concatenate([heads(x) for x in xs], axis=1)
    valid_all = groups([valid_c[:, 0:tq]] * B_KV_HEADS)
    s = jnp.where(valid_all, _nt(kc, qs), NEG)
    ex = jnp.exp2(s - jnp.max(s, axis=0, keepdims=True))
    pc = jnp.where(valid_all, ex / jnp.sum(ex, axis=0, keepdims=True), 0.0)
    out = gate(0) * _tn(vc, pc.astype(BF16))
    sel_ts = []
    for g in range(B_KV_HEADS):
        psum = pc[:, rep * g * tq:(rep * g + 1) * tq]
        for r in range(1, rep):
            psum = psum + pc[:, (rep * g + r) * tq:(rep * g + r + 1) * tq]
        p_hi = psum.astype(BF16)
        p_lo = (psum - p_hi.astype(F32)).astype(BF16)
        imp = _mm(ovl_ref[...], p_hi) + _mm(ovl_ref[...], p_lo)
        score_t = jnp.where(adm, imp + forced, NEG)
        sc = [score_t[8 * v:8 * v + 8] for v in range(n_rv)]
        rank = [jnp.zeros((8, tq), F32) for _ in range(n_rv)]
        for jp in range(n_blk):
            col = score_t[jp:jp + 1]
            for v in range(n_rv):
                if v > jp // 8:
                    beats = col >= sc[v]
                elif v < jp // 8:
                    beats = col > sc[v]
                else:
                    beats = (col > sc[v]) | ((col == sc[v]) & (sub > jp % 8))
                rank[v] = rank[v] + jnp.where(beats, 1.0, 0.0)
        sel_ts.append(jnp.where(jnp.concatenate(rank, axis=0) < n_sel, 1.0, 0.0).astype(BF16))

    tile = lambda j: pl.ds(pl.multiple_of(j * tks, tks), tks)

    def sel_body(j, carry):
        kp = j * tks + lax.broadcasted_iota(I32, (tks, 1), 0)
        bias = [jnp.where((_mm(e_ref[tile(j), :], sel_t) > 0.5) & (kp <= t), 0.0, NEG) for sel_t in sel_ts]
        return _softmax_step_t(qs, ks_ref[0, tile(j), :], vs_ref[0, tile(j), :], groups(bias), carry)

    n_tile = ((i + 1) * tq + tks - 1) // tks
    _, l_s, acc_s = lax.fori_loop(0, n_tile, sel_body, _flash_init_t(B_HEADS * tq))
    out = out + gate(1) * (acc_s / l_s)
    s_w = _nt(kw_ref[0, pl.ds(w0, ww), :], qs) + groups([win_bias[:, 0:tq]] * B_KV_HEADS)
    p_w = jnp.exp2(s_w - jnp.max(s_w, axis=0, keepdims=True))
    o_w = _tn(vw_ref[0, pl.ds(w0, ww), :], p_w.astype(BF16)) / jnp.sum(p_w, axis=0, keepdims=True)
    out = out + gate(2) * o_w
    for g in range(B_KV_HEADS):
        outs = [out[:, (rep * g + r) * tq:(rep * g + r + 1) * tq].T for r in range(rep)]
        for pp in range(rep // 2):
            a, bb = outs[2 * pp], outs[2 * pp + 1]
            if g == 0:
                bb = pltpu.roll(bb, B_HD, 1)
            else:
                a = pltpu.roll(a, B_HD, 1)
            c0 = (rep // 2 * g + pp) * LANES
            o_ref[0, :, c0:c0 + LANES] = jnp.where(lane < B_HD, a, bb).astype(BF16)


def _nsa_attn(u, u32, cols, qx, kcmp, vcmp, ks, kw):
    b, h, s, _ = qx.shape
    tq = min(256, s)
    tks = min(512, s)
    n_pad = kcmp.shape[1]
    n_blk = s // SEL_LEN
    n_sel = min(N_SEL, n_blk)
    n_cmp = (s - CMP_LEN) // CMP_STRIDE + 1
    ww = min(WINDOW + tq, s)
    expand = np.zeros((s, n_blk), np.float32)
    expand[np.arange(s), np.arange(s) // SEL_LEN] = 1.0
    nn = np.arange(n_pad)[None, :]
    jj = np.arange(LANES)[:, None]
    ovl = ((nn * CMP_STRIDE <= jj * SEL_LEN + SEL_LEN - 1) & (nn * CMP_STRIDE + CMP_LEN - 1 >= jj * SEL_LEN)
           & (jj < n_blk) & (nn < n_cmp)).astype(np.float32)
    full = lambda shape: pl.BlockSpec(shape, lambda bi, i: (bi,) + (0,) * (len(shape) - 1))
    ucol = lambda name: pl.BlockSpec((1, s, LANES), lambda bi, i, c=cols[name] // LANES: (bi, 0, c))
    const = lambda shape: pl.BlockSpec(shape, lambda bi, i: (0,) * len(shape))
    return pl.pallas_call(
        functools.partial(_nsa_attn_kernel, n_blk=n_blk, n_sel=n_sel, tks=tks, ww=ww),
        out_shape=jax.ShapeDtypeStruct((b, s, h * B_HD), BF16),
        grid=(b, s // tq),
        in_specs=[pl.BlockSpec((1, h, tq, LANES), lambda bi, i: (bi, 0, i, 0)),
                  full((1, n_pad, LANES)), full((1, n_pad, LANES)),
                  full((1, s, LANES)), ucol("b_vs"), full((1, s, LANES)), ucol("b_vw"),
                  pl.BlockSpec((1, tq, LANES), lambda bi, i: (bi, i, 0)),
                  const((s, n_blk)), const((LANES, n_pad))],
        out_specs=pl.BlockSpec((1, tq, h * B_HD), lambda bi, i: (bi, i, 0)),
        compiler_params=_cp("parallel", "parallel"),
        name="nsa_attn",
    )(qx, kcmp, vcmp, ks, u, kw, u, u32, jnp.asarray(expand, BF16), jnp.asarray(ovl, BF16))


def _dsa_prep_kernel(q_in, iq_in, kv_in, sm_in, cos64_ref, sin64_ref, cos32_ref, sin32_ref, qg_ref, kg_ref,
                     q_ref, kv_ref, iq_ref, ik_ref):
    tm = q_in.shape[1]
    lane = _lane((tm, LANES))
    c64, s64, c32, s32 = cos64_ref[0], sin64_ref[0], cos32_ref[0], sin32_ref[0]
    for p in range(C_HEADS // 2):
        y = _rope(_group_rms(q_in[0, :, p * LANES:(p + 1) * LANES].astype(F32), C_HD) * qg_ref[...], c64, s64, C_HD // 2)
        y = y * (C_HD ** -0.5 * LOG2E)
        q_ref[0, 2 * p] = jnp.where(lane < C_HD, y, 0.0).astype(BF16)
        q_ref[0, 2 * p + 1] = jnp.where(lane < C_HD, pltpu.roll(y, C_HD, 1), 0.0).astype(BF16)
    kv = kv_in[0].astype(F32)
    kn = _rope(_group_rms(kv, C_HD) * kg_ref[...], c64, s64, C_HD // 2)
    kv_ref[0] = jnp.where(lane < C_HD, kn, kv).astype(BF16)
    for c0 in range(0, IDX_HEADS * IDX_HD, LANES):
        iq_ref[0, :, c0:c0 + LANES] = _rope(iq_in[0, :, c0:c0 + LANES].astype(F32), c32, s32, IDX_HD // 2).astype(BF16)
    ik = jnp.where(lane < IDX_HD, _rope(sm_in[0], c32, s32, IDX_HD // 2), 0.0)
    ik = ik + pltpu.roll(ik, IDX_HD, 1)
    ik = ik + pltpu.roll(ik, 2 * IDX_HD, 1)
    ik_ref[0] = ik.astype(BF16)


def _dsa_prep(u, u32, cols, tabs, p):
    b, s, _ = u.shape
    tm = min(256, s)
    row = pl.BlockSpec((1, tm, LANES), lambda bi, i: (bi, i, 0))
    const = pl.BlockSpec((1, LANES), lambda bi, i: (0, 0))
    ublk = lambda name, w: pl.BlockSpec((1, tm, w), lambda bi, i, c=cols[name] // w: (bi, i, c))
    dense = jax.ShapeDtypeStruct((b, s, LANES), BF16)
    return pl.pallas_call(
        _dsa_prep_kernel,
        out_shape=(jax.ShapeDtypeStruct((b, C_HEADS, s, LANES), BF16), dense,
                   jax.ShapeDtypeStruct((b, s, IDX_HEADS * IDX_HD), BF16), dense),
        grid=(b, s // tm),
        in_specs=[ublk("c_q", 512), ublk("c_iq", 256), ublk("c_kv", LANES), row,
                  row, row, row, row, const, const],
        out_specs=(pl.BlockSpec((1, C_HEADS, tm, LANES), lambda bi, i: (bi, 0, i, 0)), row,
                   pl.BlockSpec((1, tm, IDX_HEADS * IDX_HD), lambda bi, i: (bi, i, 0)), row),
        compiler_params=_cp("parallel", "parallel"),
        name="dsa_prep",
    )(u, u, u, u32, tabs["cos64"], tabs["sin64"], tabs["cos32"], tabs["sin32"], p["q_g"], p["k_g"])


def _dsa_attn_kernel(q_ref, iq_ref, sm_ref, ik_ref, kv_ref, tri_ref, o_ref, key_scr, bias_scr, *, tk, topk):
    tq = q_ref.shape[2]
    i = pl.program_id(1)
    t = i * tq + lax.broadcasted_iota(I32, (1, tq), 1)
    lane = _lane((tq, LANES))
    n_tile = ((i + 1) * tq + tk - 1) // tk
    tile = lambda j: pl.ds(pl.multiple_of(j * tk, tk), tk)
    kpos = lambda j: j * tk + lax.broadcasted_iota(I32, (tk, 1), 0)

    iw_t = (sm_ref[0] * (IDX_HEADS ** -0.5)).T
    parts = []
    for h in range(IDX_HEADS):
        blk = iq_ref[0, :, (h // 4) * LANES:(h // 4 + 1) * LANES]
        lo = (h % 4) * IDX_HD
        parts.append(jnp.where((lane >= lo) & (lane < lo + IDX_HD), blk, jnp.zeros_like(blk)))
    iqs = jnp.concatenate(parts, axis=0)

    def idx_body(j, carry):
        lg = _nt(ik_ref[0, tile(j), :], iqs)
        acc = jnp.zeros((tk, tq), F32)
        for h in range(IDX_HEADS):
            acc = acc + iw_t[IDX_HD + h:IDX_HD + h + 1, :] * jnp.maximum(lg[:, h * tq:(h + 1) * tq], 0.0)
        sc = jnp.where(kpos(j) <= t, acc, NEG)
        bits = pltpu.bitcast(sc, I32)
        key = jnp.where(bits < 0, bits ^ 0x7FFFFFFF, bits)
        key_scr[tile(j), :] = jnp.where(sc == 0.0, 0, key)
        return carry

    lax.fori_loop(0, n_tile, idx_body, 0)

    tr = min(tk, 2 * tq)
    n_sub = ((i + 1) * tq + tr - 1) // tr

    def count_ge(thr_key):
        def body(j, cs):
            key = key_scr[pl.ds(pl.multiple_of(j * tr, tr), tr), :]
            cs = list(cs)
            for n, r0 in enumerate(range(0, tr, 8)):
                c = cs[n % len(cs)]
                cs[n % len(cs)] = jnp.where(key[r0:r0 + 8] >= thr_key, c + 1, c)
            return tuple(cs)

        cs = lax.fori_loop(0, n_sub, body, tuple(jnp.zeros((8, tq), I32) for _ in range(4)))
        return jnp.sum(cs[0] + cs[1] + cs[2] + cs[3], axis=0, keepdims=True)

    def bit_body(bi, ucand):
        utrial = ucand | jnp.left_shift(jnp.int32(1), 31 - bi)
        return jnp.where(count_ge(utrial ^ INT_MIN) >= topk, utrial, ucand)

    thr = lax.fori_loop(0, 32, bit_body, jnp.zeros((1, tq), I32)) ^ INT_MIN
    n_ge = count_ge(thr)
    n_gt = count_ge(thr + 1)
    need = topk - n_gt
    row_ok = (n_ge - n_gt == need) | (thr == NEG_KEY) | (n_ge < topk)
    simple = jnp.min(jnp.where(row_ok, 1.0, 0.0)) > 0.5

    def fast_bias():
        def body(j, carry):
            keep = (key_scr[tile(j), :] >= thr) & (kpos(j) <= t)
            bias_scr[tile(j), :] = jnp.where(keep, 0.0, NEG)
            return carry

        lax.fori_loop(0, n_tile, body, 0)

    def tie_bias():
        need_f = need.astype(F32)

        def body(j, run):
            key = key_scr[tile(j), :]
            kp = kpos(j)
            for c0 in range(0, tk, LANES):
                kc = key[c0:c0 + LANES]
                eq = kc == thr
                eq_f = jnp.where(eq, 1.0, 0.0)
                pref = _mm(tri_ref[...], eq_f.astype(BF16)) + run
                keep = ((kc > thr) | (eq & (pref <= need_f))) & (kp[c0:c0 + LANES] <= t)
                bias_scr[pl.ds(pl.multiple_of(j * tk + c0, LANES), LANES), :] = jnp.where(keep, 0.0, NEG)
                run = run + jnp.sum(eq_f, axis=0, keepdims=True)
            return run

        lax.fori_loop(0, n_tile, body, jnp.zeros((1, tq), F32))

    lax.cond(simple, fast_bias, tie_bias)

    qs = jnp.concatenate([q_ref[0, h] for h in range(C_HEADS)], axis=0)

    def att_body(j, carry):
        kv = kv_ref[0, tile(j), :]
        bias_t = jnp.concatenate([bias_scr[tile(j), :]] * C_HEADS, axis=1)
        return _softmax_step_t(qs, kv, kv, bias_t, carry)

    _, l, acc = lax.fori_loop(0, n_tile, att_body, _flash_init_t(C_HEADS * tq))
    o_t = acc / l
    for p in range(C_HEADS // 2):
        a = pltpu.roll(o_t[:, (2 * p) * tq:(2 * p + 1) * tq].T, C_HD, 1)
        bb = o_t[:, (2 * p + 1) * tq:(2 * p + 2) * tq].T
        o_ref[0, :, p * LANES:(p + 1) * LANES] = jnp.where(lane < C_HD, a, bb).astype(BF16)


def _dsa_attn(u32, qc, iq, ik, kv):
    b, h, s, _ = qc.shape
    tq = min(256, s)
    tk = min(1024, s)
    topk = min(TOPK_MAX, s // 4)
    tri = np.tril(np.ones((LANES, LANES), np.float32))
    full = lambda shape: pl.BlockSpec(shape, lambda bi, i: (bi,) + (0,) * (len(shape) - 1))
    return pl.pallas_call(
        functools.partial(_dsa_attn_kernel, tk=tk, topk=topk),
        out_shape=jax.ShapeDtypeStruct((b, s, h * C_HD), BF16),
        grid=(b, s // tq),
        in_specs=[pl.BlockSpec((1, h, tq, LANES), lambda bi, i: (bi, 0, i, 0)),
                  pl.BlockSpec((1, tq, IDX_HEADS * IDX_HD), lambda bi, i: (bi, i, 0)),
                  pl.BlockSpec((1, tq, LANES), lambda bi, i: (bi, i, 0)),
                  full((1, s, LANES)), full((1, s, LANES)),
                  pl.BlockSpec((LANES, LANES), lambda bi, i: (0, 0))],
        out_specs=pl.BlockSpec((1, tq, h * C_HD), lambda bi, i: (bi, i, 0)),
        scratch_shapes=[pltpu.VMEM((s, tq), I32), pltpu.VMEM((s, tq), F32)],
        compiler_params=_cp("parallel", "parallel"),
        name="dsa_attn",
    )(qc, iq, u32, ik, kv, jnp.asarray(tri, BF16))


def _mlstm_conv_kernel(x_ref, halo_ref, w_ref, b_ref, o_ref):
    tm = x_ref.shape[1]
    i = pl.program_id(1)
    lane = _lane((tm, x_ref.shape[2]))
    halo = jnp.where(i > 0, halo_ref[0].astype(F32), 0.0)
    x = x_ref[0].astype(F32)
    xc = jnp.concatenate([halo, x], axis=0)
    off = halo.shape[0] - (CONV_W - 1)
    y = b_ref[...] + jnp.zeros_like(x)
    for j in range(CONV_W):
        y = y + w_ref[j:j + 1, :] * xc[off + j:off + j + tm, :]
    y = y * _sigmoid(y)
    o_ref[0] = jnp.where(lane >= D_HEADS * D_QK, y * (D_QK ** -0.5), y)


def _mlstm_conv(u, qk_block, w, bias):
    b, s, _ = u.shape
    c = w.shape[1]
    tm = min(512, s)
    hb = 16
    return pl.pallas_call(
        _mlstm_conv_kernel,
        out_shape=jax.ShapeDtypeStruct((b, s, c), F32),
        grid=(b, s // tm),
        in_specs=[pl.BlockSpec((1, tm, c), lambda bi, i: (bi, i, qk_block)),
                  pl.BlockSpec((1, hb, c), lambda bi, i: (bi, jnp.maximum(i * (tm // hb) - 1, 0), qk_block)),
                  pl.BlockSpec((CONV_W, c), lambda bi, i: (0, 0)),
                  pl.BlockSpec((1, c), lambda bi, i: (0, 0))],
        out_specs=pl.BlockSpec((1, tm, c), lambda bi, i: (bi, i, 0)),
        compiler_params=_cp("parallel", "parallel"),
        name="mlstm_conv",
    )(u, u, w, bias)


def _mlstm_scan_kernel(qk_ref, v_ref, op_ref, sm_ref, gt_ref, bcol_ref, brow_ref, hg_ref, o_ref,
                       c_scr, m_scr):
    tc = qk_ref.shape[1]
    L = CHUNK
    nqk = D_HEADS * D_QK

    @pl.when(pl.program_id(1) == 0)
    def _():
        c_scr[...] = jnp.zeros_like(c_scr)
        m_scr[...] = jnp.zeros_like(m_scr)

    r = lax.broadcasted_iota(I32, (L, L), 0)
    c = lax.broadcasted_iota(I32, (L, L), 1)
    tril = c <= r
    tril_f = jnp.where(tril, 1.0, 0.0).astype(BF16)
    triu_f = jnp.where(r <= c, 1.0, 0.0).astype(BF16)
    lane = _lane((L, LANES))
    row128 = lax.broadcasted_iota(I32, (LANES, 2 * LANES), 0)
    ones = jnp.ones((L, LANES), F32)
    i_lane, f_lane = IDX_HD + IDX_HEADS, IDX_HD + IDX_HEADS + D_HEADS

    def one(bb, ci):
        rows = pl.ds(pl.multiple_of(ci * L, L), L)
        qk = qk_ref[bb, rows, :]
        sm = sm_ref[bb, rows, :] + bcol_ref[...]
        bcol_all = sum(_mm(tril_f, part) for part in _split3(_log_sigmoid(sm)))
        gt = gt_ref[bb, ci] + brow_ref[...]
        brow_all = sum(_mm(part, triu_f) for part in _split3(_log_sigmoid(gt)))
        for p in range(D_HEADS // 2):
            qpair = qk[:, p * LANES:(p + 1) * LANES]
            kpair = qk[:, nqk + p * LANES:nqk + (p + 1) * LANES]
            kpair_b = kpair.astype(BF16)
            sp = bb * (D_HEADS // 2) + p
            c_prev = c_scr[sp]
            c_prev_b = c_prev.astype(BF16)
            upd = []
            for e in range(2):
                h = 2 * p + e
                in_half = (lane >= D_QK * e) & (lane < D_QK * (e + 1))
                qm = jnp.where(in_half, qpair, 0.0)
                qm_b = qm.astype(BF16)
                vh = v_ref[bb, rows, h * D_VD:(h + 1) * D_VD].astype(F32)
                b_col = jnp.broadcast_to(bcol_all[:, f_lane + h:f_lane + h + 1], (L, LANES))
                i_col = jnp.broadcast_to(sm[:, i_lane + h:i_lane + h + 1], (L, LANES))
                b_row = brow_all[D_HEADS + h:D_HEADS + h + 1, :]
                i_row = gt[h:h + 1, :]
                m_prev = m_scr[bb * D_HEADS + h, 0:1, :]
                dmat = jnp.where(tril, b_col[:, :L] - b_row + i_row, NEG)
                inter = b_col + m_prev
                m_t = jnp.maximum(inter, jnp.max(dmat, axis=-1, keepdims=True))
                a = jnp.exp(inter - m_t)
                w = _nt(qm_b, kpair_b) * jnp.exp(dmat - m_t[:, :L])
                qc = _mm(qm_b, c_prev_b)
                wv = _mm(w.astype(BF16), jnp.concatenate([vh, ones], axis=1).astype(BF16))
                num = a * qc[:, :LANES] + wv[:, :LANES]
                den = a * qc[:, LANES:] + wv[:, LANES:]
                hout = num / jnp.maximum(jnp.abs(den), jnp.exp(-m_t))
                b_last = b_col[L - 1:L, :]
                g_col = b_last - b_col + i_col
                m_new = jnp.maximum(b_last + m_prev, jnp.max(g_col, axis=0, keepdims=True))
                ws = jnp.exp(g_col - m_new)
                decay = jnp.exp(b_last + m_prev - m_new)
                u_mat = _tn(kpair_b, jnp.concatenate([ws * vh, ws], axis=1).astype(BF16))
                upd.append(jnp.concatenate([decay, decay], axis=1) * c_prev + u_mat)
                m_scr[bb * D_HEADS + h] = jnp.broadcast_to(m_new, m_scr.shape[1:])
                hn = _row_rms(hout) * hg_ref[...]
                y = _sigmoid(op_ref[bb, rows, h * D_VD:(h + 1) * D_VD].astype(F32)) * hn
                o_ref[bb, rows, h * D_VD:(h + 1) * D_VD] = y.astype(BF16)
            c_scr[sp] = jnp.where(row128 < D_QK, upd[0], upd[1])

    def chunk(ci, carry):
        for bb in range(qk_ref.shape[0]):
            one(bb, ci)
        return carry

    lax.fori_loop(0, tc // L, chunk, 0)


def _mlstm_scan(u, u32, cols, qk, gt, p):
    b, s, _ = u.shape
    tc = min(256, s)
    nb = next(n for n in (8, 4, 2, 1) if b % n == 0)
    nv = D_HEADS * D_VD
    ublk = lambda name, w: pl.BlockSpec((nb, tc, w), lambda bi, i, c=cols[name] // w: (bi, i, c))
    const = lambda shape: pl.BlockSpec(shape, lambda bi, i: (0,) * len(shape))
    return pl.pallas_call(
        _mlstm_scan_kernel,
        out_shape=jax.ShapeDtypeStruct((b, s, nv), BF16),
        grid=(b // nb, s // tc),
        in_specs=[pl.BlockSpec((nb, tc, qk.shape[-1]), lambda bi, i: (bi, i, 0)),
                  ublk("d_v", nv), ublk("d_o", nv), pl.BlockSpec((nb, tc, LANES), lambda bi, i: (bi, i, 0)),
                  pl.BlockSpec((nb, tc // CHUNK, 8, CHUNK), lambda bi, i: (bi, i, 0, 0)),
                  const((1, LANES)), const((8, 1)), const((1, D_VD))],
        out_specs=pl.BlockSpec((nb, tc, nv), lambda bi, i: (bi, i, 0)),
        scratch_shapes=[pltpu.VMEM((nb * D_HEADS // 2, LANES, 2 * LANES), F32),
                        pltpu.VMEM((nb * D_HEADS, 8, LANES), F32)],
        compiler_params=_cp("parallel", "arbitrary"),
        name="mlstm_scan",
    )(qk, u, u, u32, gt, p["bias_col"], p["bias_row"], p["h_g"])


EVEN_SRC = dict(a_ql=(0, 256), a_kvl=(256, 128), a_kr=(384, 32), a_gate=(416, 512), b_q=(928, 512),
                b_kc=(1440, 128), b_vc=(1568, 128), b_ks=(1696, 128), b_vs=(1824, 128), b_kw=(1952, 128),
                b_vw=(2080, 128), b_g=(2208, 24), b_gate=(2232, 512), m_q=(2744, 256), m_gate=(3000, 256))
EVEN_DST = dict(a_ql=0, a_kvl=256, a_kr=384, b_q=512, a_gate=1024, b_gate=1536, m_q=2048, m_gate=2304,
                b_g=2560, b_kc=2688, b_vc=2816, b_ks=2944, b_vs=3072, b_kw=3200, b_vw=3328)
EVEN_COLS_PAD = 3456

ODD_SRC = dict(c_q=(0, 512), c_k=(512, 64), c_v=(576, 64), c_iq=(640, 256), c_ik=(896, 32), c_iw=(928, 8),
               c_gate=(936, 512), d_q=(1448, 256), d_k=(1704, 256), d_v=(1960, 512), d_i=(2472, 4),
               d_f=(2476, 4), d_o=(2480, 512), d_gate=(2992, 512), m_q=(3504, 256), m_gate=(3760, 256))
ODD_DST = dict(c_q=0, c_gate=512, d_gate=1024, d_v=1536, d_o=2048, d_q=2560, d_k=2816, m_q=3072, m_gate=3328,
               c_iq=3584, c_k=3840, c_v=3904, c_ik=3968, c_iw=4000, d_i=4008, d_f=4012)
ODD_COLS_PAD = 4096


def _permute_cols(w, src, dst, total):
    pieces, pos = [], 0
    for name in sorted(src, key=lambda n: dst[n]):
        start, width = src[name]
        if dst[name] > pos:
            pieces.append(jnp.zeros((w.shape[0], dst[name] - pos), BF16))
        pieces.append(w[:, start:start + width].astype(BF16))
        pos = dst[name] + width
    if total > pos:
        pieces.append(jnp.zeros((w.shape[0], total - pos), BF16))
    return jnp.concatenate(pieces, axis=1)


def _heads_split(w, heads, first):
    w3 = w.reshape(w.shape[0], heads, -1)
    return jnp.concatenate([w3[:, :, :first].reshape(w.shape[0], -1),
                            w3[:, :, first:].reshape(w.shape[0], -1)], axis=1).astype(BF16)


def _tile_lanes(v, reps):
    return jnp.tile(v.astype(F32).reshape(1, -1), (1, reps))


def _rope_tables(positions, d2):
    inv = ROPE_THETA ** (-jnp.arange(d2, dtype=F32) / d2)
    ang = positions.astype(F32)[..., None] * inv
    c, s = jnp.cos(ang), jnp.sin(ang)
    reps = LANES // (2 * d2)
    return (jnp.tile(jnp.concatenate([c, c], axis=-1), (1, 1, reps)),
            jnp.tile(jnp.concatenate([-s, s], axis=-1), (1, 1, reps)))


def _block_diag2(w):
    z = jnp.zeros_like(w)
    return jnp.concatenate([jnp.concatenate([w, z], axis=-1), jnp.concatenate([z, w], axis=-1)], axis=-2)


def kernel(x, mem, positions, ln_g, mem_norm_g, mem_w_kv, mem_q_norm_g, mem_k_norm_g, w_out, even_w_in, mla_q_lat_g, mla_kv_lat_g, mla_w_uq, mla_w_ukv, mla_q_norm_g, mla_k_norm_g, nsa_q_norm_g, nsa_k_norm_g, nsa_cmp_pos, nsa_cmp_w1, nsa_cmp_w2, odd_w_in, dsa_q_norm_g, dsa_k_norm_g, mlstm_conv_w, mlstm_conv_b, mlstm_i_bias, mlstm_f_bias, mlstm_h_norm_g):
    b, s, _ = x.shape
    depth = ln_g.shape[0]
    cos64, sin64 = _rope_tables(positions, 32)
    cos32, sin32 = _rope_tables(positions, 16)
    tabs = dict(cos64=cos64, sin64=sin64, cos32=cos32, sin32=sin32)
    n_pad = s // CMP_STRIDE
    cmp_pos = jnp.pad(positions[:, CMP_LEN - 1::CMP_STRIDE], ((0, 0), (0, 0)))[:, :n_pad]
    cmp_pos = jnp.pad(cmp_pos, ((0, 0), (0, n_pad - cmp_pos.shape[1])))
    cosc, sinc = _rope_tables(cmp_pos, 32)

    mem_k, mem_v = _mem_kv(mem, mem_norm_g.reshape(depth, 1, -1), mem_w_kv.astype(BF16),
                           jnp.tile(mem_k_norm_g, (1, 2)).reshape(depth, 1, LANES))

    for layer in range(depth):
        li = layer // 2
        g_ln = ln_g[layer].reshape(1, -1)
        mq_g = _tile_lanes(mem_q_norm_g[layer], 2)
        if layer % 2 == 0:
            cols = EVEN_DST
            u, u32 = _in_proj(x, g_ln, _permute_cols(even_w_in[li], EVEN_SRC, EVEN_DST, EVEN_COLS_PAD), cols["b_g"])
            pa = dict(q_lat_g=mla_q_lat_g[li].reshape(1, -1), kv_lat_g=mla_kv_lat_g[li].reshape(1, -1),
                      w_uq=_heads_split(mla_w_uq[li], A_HEADS, A_NOPE), w_ukv=_heads_split(mla_w_ukv[li], A_HEADS, A_NOPE),
                      qn_g=_tile_lanes(mla_q_norm_g[li, :A_NOPE], 2), qr_g=_tile_lanes(mla_q_norm_g[li, A_NOPE:], 4),
                      kn_g=_tile_lanes(mla_k_norm_g[li, :A_NOPE], 2), kr_g=_tile_lanes(mla_k_norm_g[li, A_NOPE:], 4))
            qa, ka, va = _mla_prep(u, cos32, sin32, pa)
            y1 = _mla_attn(qa, ka, va)
            pb = dict(q_g=_tile_lanes(nsa_q_norm_g[li], 2), ks_g=_tile_lanes(nsa_k_norm_g[li, 1], 2),
                      kw_g=_tile_lanes(nsa_k_norm_g[li, 2], 2), kc_g=_tile_lanes(nsa_k_norm_g[li, 0], 2),
                      pe_k=jnp.tile(nsa_cmp_pos[li, 0], (1, 2)), pe_v=jnp.tile(nsa_cmp_pos[li, 1], (1, 2)),
                      w1k=_block_diag2(nsa_cmp_w1[li, 0].reshape(CMP_LEN, B_HD, B_HD)).astype(BF16),
                      w1v=_block_diag2(nsa_cmp_w1[li, 1].reshape(CMP_LEN, B_HD, B_HD)).astype(BF16),
                      w2k=_block_diag2(nsa_cmp_w2[li, 0]).astype(BF16),
                      w2v=_block_diag2(nsa_cmp_w2[li, 1]).astype(BF16))
            qb, ks, kw = _nsa_prep(u, cols, cos64, sin64, pb)
            kcmp, vcmp = _nsa_cmp(u, cols, cosc, sinc, pb)
            y2 = _nsa_attn(u, u32, cols, qb, kcmp, vcmp, ks, kw)
            gate_blocks = (cols["a_gate"] // 512, cols["b_gate"] // 512, cols["m_gate"] // 256)
        else:
            cols = dict(ODD_DST, c_kv=ODD_DST["c_k"], small=ODD_DST["c_ik"])
            u, u32 = _in_proj(x, g_ln, _permute_cols(odd_w_in[li], ODD_SRC, ODD_DST, ODD_COLS_PAD), cols["small"])
            pc = dict(q_g=_tile_lanes(dsa_q_norm_g[li], 2), k_g=_tile_lanes(dsa_k_norm_g[li], 2))
            qc, kvc, iq, ik = _dsa_prep(u, u32, cols, tabs, pc)
            y1 = _dsa_attn(u32, qc, iq, ik, kvc)
            qk = _mlstm_conv(u, cols["d_q"] // 512, mlstm_conv_w[li], mlstm_conv_b[li].reshape(1, -1))
            gates = u32[:, :, cols["d_i"] - cols["small"]:cols["d_i"] - cols["small"] + 2 * D_HEADS]
            gt = gates.reshape(b, s // CHUNK, CHUNK, 2 * D_HEADS).transpose(0, 1, 3, 2)
            bias8 = jnp.concatenate([mlstm_i_bias[li], mlstm_f_bias[li]]).astype(F32)
            bias_col = jnp.zeros((1, LANES), F32).at[0, cols["d_i"] - cols["small"]:cols["d_i"] - cols["small"] + 8].set(bias8)
            pd = dict(bias_col=bias_col, bias_row=bias8.reshape(8, 1), h_g=mlstm_h_norm_g[li].reshape(1, -1))
            y2 = _mlstm_scan(u, u32, cols, qk, gt, pd)
            gate_blocks = (cols["c_gate"] // 512, cols["d_gate"] // 512, cols["m_gate"] // 256)
        ym = _mem_attn(u, cols["m_q"] // 256, mem_k, mem_v, layer, mq_g)
        x = _out_proj(x, y1, y2, ym, u, gate_blocks, w_out[layer].astype(BF16))
    return x
```

```python
import functools

import numpy as np
import jax
import jax.numpy as jnp
from jax import lax
from jax.experimental import pallas as pl
from jax.experimental.pallas import tpu as pltpu

F32, BF16, I32 = jnp.float32, jnp.bfloat16, jnp.int32
NEG = -1e30
EPS = 1e-6
ROPE_THETA = 10000.0
LANES = 128
VMEM_LIMIT_BYTES = 48 * 1024 * 1024

D_MODEL = 1024
DEPTH = 4
A_HEADS, A_NOPE, A_ROPE, A_VD, A_QLAT, A_KVLAT = 8, 64, 32, 64, 256, 128
B_HEADS, B_KV_HEADS, B_HD = 8, 2, 64
CMP_LEN, CMP_STRIDE, SEL_LEN, N_SEL, WINDOW = 32, 16, 64, 16, 512
C_HEADS, C_HD, IDX_HEADS, IDX_HD, TOPK_MAX = 8, 64, 8, 32, 256
D_HEADS, D_QK, D_VD, CONV_W, CHUNK = 4, 64, 128, 4, 64
M_HEADS, M_HD = 4, 64

INT_MIN = np.int32(-2 ** 31)
NEG_KEY = int(np.float32(NEG).view(np.int32) ^ np.int32(0x7FFFFFFF))
LOG2E = float(np.log2(np.e))


def _cp(*sem):
    return pltpu.CompilerParams(dimension_semantics=sem, vmem_limit_bytes=VMEM_LIMIT_BYTES)


def _nt(a, b):
    return lax.dot_general(a, b, (((1,), (1,)), ((), ())), preferred_element_type=F32)


def _tn(a, b):
    return lax.dot_general(a, b, (((0,), (0,)), ((), ())), preferred_element_type=F32)


def _mm(a, b):
    return jnp.dot(a, b, preferred_element_type=F32)


def _sigmoid(x):
    return 1.0 / (1.0 + jnp.exp(-x))


def _log_sigmoid(x):
    return jnp.minimum(x, 0.0) - jnp.log1p(jnp.exp(-jnp.abs(x)))


def _lane(shape):
    return lax.broadcasted_iota(I32, shape, len(shape) - 1)


def _group_mat(gs):
    r = lax.broadcasted_iota(I32, (LANES, LANES), 0)
    c = lax.broadcasted_iota(I32, (LANES, LANES), 1)
    sh = gs.bit_length() - 1
    return jnp.where((r >> sh) == (c >> sh), 1.0, 0.0).astype(BF16)


def _split3(x):
    hi = x.astype(BF16)
    r1 = x - hi.astype(F32)
    mid = r1.astype(BF16)
    return hi, mid, (r1 - mid.astype(F32)).astype(BF16)


def _mm_split(x, w01):
    hi = x.astype(BF16)
    lo = (x - hi.astype(F32)).astype(BF16)
    return _mm(hi, w01) + _mm(lo, w01)


def _group_rms(x, gs):
    ss = _mm_split(x * x, _group_mat(gs))
    return x * lax.rsqrt(ss * (1.0 / gs) + EPS)


def _rope(x, cosp, sinp, half):
    lane = _lane(x.shape)
    rot = jnp.where((lane & (2 * half - 1)) < half,
                    pltpu.roll(x, LANES - half, 1), pltpu.roll(x, half, 1))
    return x * cosp + rot * sinp


def _row_rms(x):
    return x * lax.rsqrt(jnp.mean(x * x, axis=-1, keepdims=True) + EPS)


def _in_proj_kernel(x_ref, g_ref, w_ref, o_ref, o32_ref, *, c32):
    h = (_row_rms(x_ref[0]) * g_ref[...]).astype(BF16)
    ncol = o_ref.shape[-1]
    for c0 in range(0, ncol, 1024):
        c1 = min(ncol, c0 + 1024)
        o_ref[0, :, c0:c1] = _mm(h, w_ref[:, c0:c1]).astype(BF16)
    o32_ref[0] = _mm(h, w_ref[:, c32:c32 + LANES])


def _in_proj(x, g, w, c32):
    b, s, d = x.shape
    c = w.shape[1]
    tm = min(1024, s)
    return pl.pallas_call(
        functools.partial(_in_proj_kernel, c32=c32),
        out_shape=(jax.ShapeDtypeStruct((b, s, c), BF16), jax.ShapeDtypeStruct((b, s, LANES), F32)),
        grid=(b, s // tm),
        in_specs=[pl.BlockSpec((1, tm, d), lambda bi, i: (bi, i, 0)),
                  pl.BlockSpec((1, d), lambda bi, i: (0, 0)),
                  pl.BlockSpec((d, c), lambda bi, i: (0, 0))],
        out_specs=(pl.BlockSpec((1, tm, c), lambda bi, i: (bi, i, 0)),
                   pl.BlockSpec((1, tm, LANES), lambda bi, i: (bi, i, 0))),
        compiler_params=_cp("parallel", "parallel"),
        name="in_proj",
    )(x, g, w)


def _out_proj_kernel(x_ref, y1_ref, y2_ref, ym_ref, g1_ref, g2_ref, gm_ref, w_ref, o_ref):
    def gated(y_ref, g_ref):
        g = g_ref[0].astype(F32)
        return (y_ref[0].astype(F32) * (g * _sigmoid(g))).astype(BF16)

    n1 = y1_ref.shape[-1]
    n2 = y2_ref.shape[-1]
    acc = x_ref[0]
    acc = acc + _mm(gated(y1_ref, g1_ref), w_ref[0:n1, :])
    acc = acc + _mm(gated(y2_ref, g2_ref), w_ref[n1:n1 + n2, :])
    acc = acc + _mm(gated(ym_ref, gm_ref), w_ref[n1 + n2:, :])
    o_ref[0] = acc


def _out_proj(x, y1, y2, ym, u, gate_blocks, w):
    b, s, d = x.shape
    tm = min(512, s)
    i1, i2, im = gate_blocks
    n1, n2, nm = y1.shape[-1], y2.shape[-1], ym.shape[-1]
    row = lambda bi, i: (bi, i, 0)
    return pl.pallas_call(
        _out_proj_kernel,
        out_shape=jax.ShapeDtypeStruct((b, s, d), F32),
        grid=(b, s // tm),
        in_specs=[pl.BlockSpec((1, tm, d), row),
                  pl.BlockSpec((1, tm, n1), row),
                  pl.BlockSpec((1, tm, n2), row),
                  pl.BlockSpec((1, tm, nm), row),
                  pl.BlockSpec((1, tm, n1), lambda bi, i: (bi, i, i1)),
                  pl.BlockSpec((1, tm, n2), lambda bi, i: (bi, i, i2)),
                  pl.BlockSpec((1, tm, nm), lambda bi, i: (bi, i, im)),
                  pl.BlockSpec(w.shape, lambda bi, i: (0, 0))],
        out_specs=pl.BlockSpec((1, tm, d), row),
        compiler_params=_cp("parallel", "parallel"),
        name="out_proj",
    )(x, y1, y2, ym, u, u, u, w)


def _mem_kv_kernel(mem_ref, g_ref, w_ref, kg_ref, k_ref, v_ref):
    h = (_row_rms(mem_ref[0]) * g_ref[0]).astype(BF16)
    kv = _mm(h, w_ref[0])
    nk = k_ref.shape[-1]
    for c0 in range(0, nk, LANES):
        k_ref[0, 0, :, c0:c0 + LANES] = (_group_rms(kv[:, c0:c0 + LANES], M_HD) * kg_ref[0]).astype(BF16)
    v_ref[0, 0] = kv[:, nk:].astype(BF16)


def _mem_kv(mem, g, w, kg):
    b, m, d = mem.shape
    depth = w.shape[0]
    nk = M_HEADS * M_HD
    out = jax.ShapeDtypeStruct((depth, b, m, nk), BF16)
    return pl.pallas_call(
        _mem_kv_kernel,
        out_shape=(out, out),
        grid=(depth, b),
        in_specs=[pl.BlockSpec((1, m, d), lambda l, bi: (bi, 0, 0)),
                  pl.BlockSpec((1, 1, d), lambda l, bi: (l, 0, 0)),
                  pl.BlockSpec((1, d, 2 * nk), lambda l, bi: (l, 0, 0)),
                  pl.BlockSpec((1, 1, LANES), lambda l, bi: (l, 0, 0))],
        out_specs=(pl.BlockSpec((1, 1, m, nk), lambda l, bi: (l, bi, 0, 0)),
                   pl.BlockSpec((1, 1, m, nk), lambda l, bi: (l, bi, 0, 0))),
        compiler_params=_cp("parallel", "parallel"),
        name="mem_kv",
    )(mem, g, w, kg)


def _mem_attn_kernel(q_ref, k_ref, v_ref, qg_ref, o_ref):
    tq = q_ref.shape[1]
    lane = _lane((tq, LANES))
    row = lax.broadcasted_iota(I32, (LANES, tq), 0)
    parts = []
    for p in range(M_HEADS // 2):
        sl = slice(p * LANES, (p + 1) * LANES)
        q = (_group_rms(q_ref[0, :, sl].astype(F32), M_HD) * qg_ref[...] * (M_HD ** -0.5 * LOG2E)).astype(BF16)
        for e in range(2):
            in_half = (lane >= M_HD * e) & (lane < M_HD * (e + 1))
            parts.append(_nt(k_ref[0, 0, :, sl], jnp.where(in_half, q, jnp.zeros_like(q))))
    s = jnp.concatenate(parts, axis=1)
    ex = jnp.exp2(s - jnp.max(s, axis=0, keepdims=True))
    inv = 1.0 / jnp.sum(ex, axis=0, keepdims=True)
    for p in range(M_HEADS // 2):
        sl = slice(p * LANES, (p + 1) * LANES)
        pv = _tn(v_ref[0, 0, :, sl], ex[:, 2 * p * tq:(2 * p + 2) * tq].astype(BF16)) * inv[:, 2 * p * tq:(2 * p + 2) * tq]
        o_ref[0, :, sl] = jnp.where(row < M_HD, pv[:, :tq], pv[:, tq:]).T.astype(BF16)


def _mem_attn(u, q_block, k, v, layer, qg):
    b, s, _ = u.shape
    m, nk = k.shape[2], k.shape[3]
    tq = min(512, s)
    return pl.pallas_call(
        _mem_attn_kernel,
        out_shape=jax.ShapeDtypeStruct((b, s, nk), BF16),
        grid=(b, s // tq),
        in_specs=[pl.BlockSpec((1, tq, nk), lambda bi, i: (bi, i, q_block)),
                  pl.BlockSpec((1, 1, m, nk), lambda bi, i: (layer, bi, 0, 0)),
                  pl.BlockSpec((1, 1, m, nk), lambda bi, i: (layer, bi, 0, 0)),
                  pl.BlockSpec((1, LANES), lambda bi, i: (0, 0))],
        out_specs=pl.BlockSpec((1, tq, nk), lambda bi, i: (bi, i, 0)),
        compiler_params=_cp("parallel", "parallel"),
        name="mem_attn",
    )(u, k, v, qg)


def _flash_init_t(cols):
    return (jnp.full((1, cols), NEG, F32), jnp.zeros((1, cols), F32), jnp.zeros((LANES, cols), F32))


def _softmax_step_t(q, k, v, bias_t, carry):
    s = _nt(k, q)
    if bias_t is not None:
        s = s + bias_t
    m, l, acc = carry
    m_new = jnp.maximum(m, jnp.max(s, axis=0, keepdims=True))
    alpha = jnp.exp2(m - m_new)
    p = jnp.exp2(s - m_new)
    l = alpha * l + jnp.sum(p, axis=0, keepdims=True)
    acc = alpha * acc + _tn(v, p.astype(BF16))
    return m_new, l, acc


def _mla_prep_kernel(u_ref, cos_ref, sin_ref, qlg_ref, kvlg_ref, wuq_ref, wukv_ref,
                     qng_ref, qrg_ref, kng_ref, krg_ref, q_ref, k_ref, v_ref):
    tm = u_ref.shape[1]
    lane = _lane((tm, LANES))
    cosp, sinp = cos_ref[0], sin_ref[0]
    half = A_ROPE // 2
    scale = (A_NOPE + A_ROPE) ** -0.5 * LOG2E
    ql = (_row_rms(u_ref[0, :, 0:A_QLAT].astype(F32)) * qlg_ref[...]).astype(BF16)
    kvl = (_row_rms(u_ref[0, :, A_QLAT:A_QLAT + A_KVLAT].astype(F32)) * kvlg_ref[...]).astype(BF16)
    q = _mm(ql, wuq_ref[...])
    kv = _mm(kvl, wukv_ref[...])
    n_nope = A_HEADS * A_NOPE
    v_ref[0] = kv[:, n_nope:].astype(BF16)
    kr = u_ref[0, :, A_QLAT + A_KVLAT:A_QLAT + A_KVLAT + LANES].astype(F32)
    kpe = _rope(_group_rms(kr, A_ROPE) * krg_ref[...], cosp, sinp, half)
    kpe = pltpu.roll(kpe, A_NOPE, 1)
    qn = [_group_rms(q[:, c:c + LANES], A_NOPE) * qng_ref[...] for c in range(0, n_nope, LANES)]
    kn = [_group_rms(kv[:, c:c + LANES], A_NOPE) * kng_ref[...] for c in range(0, n_nope, LANES)]
    qr = [_rope(_group_rms(q[:, n_nope + c:n_nope + c + LANES], A_ROPE) * qrg_ref[...], cosp, sinp, half)
          for c in range(0, A_HEADS * A_ROPE, LANES)]
    for h in range(A_HEADS):
        qn_h = qn[h // 2] if h % 2 == 0 else pltpu.roll(qn[h // 2], A_NOPE, 1)
        kn_h = kn[h // 2] if h % 2 == 0 else pltpu.roll(kn[h // 2], A_NOPE, 1)
        shift = (A_NOPE - (h % 4) * A_ROPE) % LANES
        qr_h = qr[h // 4] if shift == 0 else pltpu.roll(qr[h // 4], shift, 1)
        qf = jnp.where(lane < A_NOPE, qn_h, jnp.where(lane < A_NOPE + A_ROPE, qr_h, 0.0))
        kf = jnp.where(lane < A_NOPE, kn_h, jnp.where(lane < A_NOPE + A_ROPE, kpe, 0.0))
        q_ref[0, h] = (qf * scale).astype(BF16)
        k_ref[0, h] = kf.astype(BF16)


def _mla_prep(u, cos32, sin32, p):
    b, s, _ = u.shape
    tm = min(512, s)
    hd = jax.ShapeDtypeStruct((b, A_HEADS, s, LANES), BF16)
    const = lambda shape: pl.BlockSpec(shape, lambda bi, i: (0,) * len(shape))
    return pl.pallas_call(
        _mla_prep_kernel,
        out_shape=(hd, hd, jax.ShapeDtypeStruct((b, s, A_HEADS * A_VD), BF16)),
        grid=(b, s // tm),
        in_specs=[pl.BlockSpec((1, tm, 512), lambda bi, i: (bi, i, 0)),
                  pl.BlockSpec((1, tm, LANES), lambda bi, i: (bi, i, 0)),
                  pl.BlockSpec((1, tm, LANES), lambda bi, i: (bi, i, 0)),
                  const((1, A_QLAT)), const((1, A_KVLAT)),
                  const(p["w_uq"].shape), const(p["w_ukv"].shape),
                  const((1, LANES)), const((1, LANES)), const((1, LANES)), const((1, LANES))],
        out_specs=(pl.BlockSpec((1, A_HEADS, tm, LANES), lambda bi, i: (bi, 0, i, 0)),
                   pl.BlockSpec((1, A_HEADS, tm, LANES), lambda bi, i: (bi, 0, i, 0)),
                   pl.BlockSpec((1, tm, A_HEADS * A_VD), lambda bi, i: (bi, i, 0))),
        compiler_params=_cp("parallel", "parallel"),
        name="mla_prep",
    )(u, cos32, sin32, p["q_lat_g"], p["kv_lat_g"], p["w_uq"], p["w_ukv"],
      p["qn_g"], p["qr_g"], p["kn_g"], p["kr_g"])


def _mla_attn_kernel(q_ref, k_ref, v_ref, o_ref, *, tk):
    tq = q_ref.shape[2]
    per_q = tq // tk
    i = pl.program_id(2)
    lane = _lane((tq, LANES))
    t = i * tq + lax.broadcasted_iota(I32, (1, tq), 1)
    nh = q_ref.shape[1]
    qs = [q_ref[0, e] for e in range(nh)]
    tile = lambda j: pl.ds(pl.multiple_of(j * tk, tk), tk)
    v_at = lambda j, e: v_ref[0, tile(j), (e // 2) * LANES:(e // 2 + 1) * LANES]

    def step(j, q0, bias_t, c):
        m, l, acc = c
        w = tq - q0
        s = jnp.concatenate([_nt(k_ref[0, e, tile(j), :], qs[e][q0:]) for e in range(nh)], axis=1)
        if bias_t is not None:
            s = s + jnp.concatenate([bias_t] * nh, axis=1)
        m_new = jnp.maximum(m, jnp.max(s, axis=0, keepdims=True))
        alpha = jnp.exp2(m - m_new)
        p = jnp.exp2(s - m_new)
        l = alpha * l + jnp.sum(p, axis=0, keepdims=True)
        pv = jnp.concatenate([_tn(v_at(j, 2 * pp), p[:, 2 * pp * w:(2 * pp + 2) * w].astype(BF16))
                              for pp in range(nh // 2)], axis=1)
        return m_new, l, alpha * acc + pv

    carry = lax.fori_loop(0, i * per_q, lambda j, c: step(j, 0, None, c), _flash_init_t(nh * tq))
    for d in range(per_q):
        j = i * per_q + d
        q0 = d * tk
        bias_t = jnp.where(j * tk + lax.broadcasted_iota(I32, (tk, 1), 0) <= t[:, q0:], 0.0, NEG)
        take = lambda x: jnp.concatenate([x[:, e * tq + q0:(e + 1) * tq] for e in range(nh)], axis=1)
        upd = step(j, q0, bias_t, tuple(take(x) for x in carry))
        if d:
            w = tq - q0
            upd = tuple(jnp.concatenate(
                [part for e in range(nh) for part in (old[:, e * tq:e * tq + q0], new[:, e * w:(e + 1) * w])], axis=1)
                for old, new in zip(carry, upd))
        carry = upd
    res = [(carry[2][:, e * tq:(e + 1) * tq] / carry[1][:, e * tq:(e + 1) * tq]).T for e in range(nh)]
    for p in range(nh // 2):
        o_ref[0, :, p * LANES:(p + 1) * LANES] = jnp.where(lane < A_VD, res[2 * p], res[2 * p + 1]).astype(BF16)


def _mla_attn(q, k, v):
    b, h, s, _ = q.shape
    tq = min(1024, s)
    tk = min(512, s)
    nh = 4
    return pl.pallas_call(
        functools.partial(_mla_attn_kernel, tk=tk),
        out_shape=jax.ShapeDtypeStruct((b, s, h * A_VD), BF16),
        grid=(b, h // nh, s // tq),
        in_specs=[pl.BlockSpec((1, nh, tq, LANES), lambda bi, p, i: (bi, p, i, 0)),
                  pl.BlockSpec((1, nh, s, LANES), lambda bi, p, i: (bi, p, 0, 0)),
                  pl.BlockSpec((1, s, nh * A_VD), lambda bi, p, i: (bi, 0, p))],
        out_specs=pl.BlockSpec((1, tq, nh * A_VD), lambda bi, p, i: (bi, i, p)),
        compiler_params=_cp("parallel", "parallel", "parallel"),
        name="mla_attn",
    )(q, k, v)


def _nsa_prep_kernel(q_in, ks_in, kw_in, cos_ref, sin_ref, qg_ref, ksg_ref, kwg_ref, q_ref, ks_ref, kw_ref):
    tm = q_in.shape[1]
    lane = _lane((tm, LANES))
    cosp, sinp = cos_ref[0], sin_ref[0]
    half = B_HD // 2
    rep = B_HEADS // B_KV_HEADS
    for p in range(B_HEADS // 2):
        y = _rope(_group_rms(q_in[0, :, p * LANES:(p + 1) * LANES].astype(F32), B_HD) * qg_ref[...], cosp, sinp, half)
        y = y * (B_HD ** -0.5 * LOG2E)
        y_sw = pltpu.roll(y, B_HD, 1)
        for e in range(2):
            h = 2 * p + e
            g = h // rep
            src = y if e == g else y_sw
            in_grp = (lane >= B_HD * g) & (lane < B_HD * (g + 1))
            q_ref[0, h] = jnp.where(in_grp, src, 0.0).astype(BF16)
    ks_ref[0] = _rope(_group_rms(ks_in[0].astype(F32), B_HD) * ksg_ref[...], cosp, sinp, half).astype(BF16)
    kw_ref[0] = _rope(_group_rms(kw_in[0].astype(F32), B_HD) * kwg_ref[...], cosp, sinp, half).astype(BF16)


def _nsa_prep(u, cols, cos64, sin64, p):
    b, s, _ = u.shape
    tm = min(512, s)
    blk = lambda name: pl.BlockSpec((1, tm, LANES), lambda bi, i, c=cols[name] // LANES: (bi, i, c))
    row = pl.BlockSpec((1, tm, LANES), lambda bi, i: (bi, i, 0))
    const = pl.BlockSpec((1, LANES), lambda bi, i: (0, 0))
    kvs = jax.ShapeDtypeStruct((b, s, LANES), BF16)
    return pl.pallas_call(
        _nsa_prep_kernel,
        out_shape=(jax.ShapeDtypeStruct((b, B_HEADS, s, LANES), BF16), kvs, kvs),
        grid=(b, s // tm),
        in_specs=[pl.BlockSpec((1, tm, 512), lambda bi, i, c=cols["b_q"] // 512: (bi, i, c)),
                  blk("b_ks"), blk("b_kw"), row, row, const, const, const],
        out_specs=(pl.BlockSpec((1, B_HEADS, tm, LANES), lambda bi, i: (bi, 0, i, 0)), row, row),
        compiler_params=_cp("parallel", "parallel"),
        name="nsa_prep",
    )(u, u, u, cos64, sin64, p["q_g"], p["ks_g"], p["kw_g"])


def _nsa_cmp_kernel(kc_in, vc_in, pek_ref, pev_ref, w1k_ref, w1v_ref, w2k_ref, w2v_ref, kg_ref,
                    cos_ref, sin_ref, ko_ref, vo_ref, pad_ref):
    s = kc_in.shape[1]
    n_pad = ko_ref.shape[1]

    def compress(x_in, pe_ref, w1_ref, w2_ref):
        pad_ref[0:s, :] = x_in[0].astype(F32)
        pad_ref[s:s + CMP_STRIDE, :] = jnp.zeros((CMP_STRIDE, LANES), F32)
        acc = jnp.zeros((n_pad, LANES), F32)
        for l in range(CMP_LEN):
            xl = pad_ref[pl.ds(l, n_pad, stride=CMP_STRIDE), :] + pe_ref[l:l + 1, :]
            acc = acc + _mm(xl.astype(BF16), w1_ref[l])
        mid = acc * _sigmoid(acc)
        return _mm(mid.astype(BF16), w2_ref[...])

    kc = compress(kc_in, pek_ref, w1k_ref, w2k_ref)
    ko_ref[0] = _rope(_group_rms(kc, B_HD) * kg_ref[...], cos_ref[0], sin_ref[0], B_HD // 2).astype(BF16)
    vo_ref[0] = compress(vc_in, pev_ref, w1v_ref, w2v_ref).astype(BF16)


def _nsa_cmp(u, cols, cosc, sinc, p):
    b, s, _ = u.shape
    n_pad = s // CMP_STRIDE
    blk = lambda name: pl.BlockSpec((1, s, LANES), lambda bi, c=cols[name] // LANES: (bi, 0, c))
    const = lambda shape: pl.BlockSpec(shape, lambda bi: (0,) * len(shape))
    out = jax.ShapeDtypeStruct((b, n_pad, LANES), BF16)
    ospec = pl.BlockSpec((1, n_pad, LANES), lambda bi: (bi, 0, 0))
    return pl.pallas_call(
        _nsa_cmp_kernel,
        out_shape=(out, out),
        grid=(b,),
        in_specs=[blk("b_kc"), blk("b_vc"), const((CMP_LEN, LANES)), const((CMP_LEN, LANES)),
                  const((CMP_LEN, LANES, LANES)), const((CMP_LEN, LANES, LANES)),
                  const((LANES, LANES)), const((LANES, LANES)), const((1, LANES)), ospec, ospec],
        out_specs=(ospec, ospec),
        scratch_shapes=[pltpu.VMEM((s + CMP_STRIDE, LANES), F32)],
        compiler_params=_cp("parallel"),
        name="nsa_cmp",
    )(u, u, p["pe_k"], p["pe_v"], p["w1k"], p["w1v"], p["w2k"], p["w2v"], p["kc_g"], cosc, sinc)


def _nsa_attn_kernel(q_ref, kc_ref, vc_ref, ks_ref, vs_ref, kw_ref, vw_ref, g_ref, e_ref, ovl_ref, o_ref,
                     *, n_blk, n_sel, tks, ww):
    tq = q_ref.shape[2]
    n_pad = kc_ref.shape[1]
    i = pl.program_id(1)
    t = i * tq + lax.broadcasted_iota(I32, (1, tq), 1)
    lane = _lane((tq, LANES))
    gates_t = _sigmoid(g_ref[0]).T
    rep = B_HEADS // B_KV_HEADS
    heads = lambda x: jnp.concatenate([x] * rep, axis=1)
    ncmp = lax.broadcasted_iota(I32, (n_pad, 1), 0)
    valid_c = heads((ncmp * CMP_STRIDE + (CMP_LEN - 1)) <= t)
    kc = kc_ref[0]
    vc = vc_ref[0]
    blk = lax.broadcasted_iota(I32, (LANES, 1), 0)
    cur = t >> (SEL_LEN.bit_length() - 1)
    forced = jnp.where(blk == cur, 3e4, jnp.where(blk == cur - 1, 2e4, jnp.where(blk == 0, 1e4, 0.0)))
    adm = (blk * SEL_LEN <= t) & (blk < n_blk)
    n_rv = n_blk // 8
    sub = lax.broadcasted_iota(I32, (8, tq), 0)
    w0 = pl.multiple_of(jnp.maximum(i * tq + tq - ww, 0), tq)
    kp_w = w0 + lax.broadcasted_iota(I32, (ww, 1), 0)
    win_bias = heads(jnp.where((kp_w <= t) & (kp_w > t - WINDOW), 0.0, NEG))
    qs = jnp.concatenate([q_ref[0, h] for h in range(B_HEADS)], axis=0)
    gate = lambda c: jnp.concatenate([gates_t[h * 3 + c:h * 3 + c + 1, :] for h in range(B_HEADS)], axis=1)
    groups = lambda xs: jnp.concatenate([heads(x) for x in xs], axis=1)
    valid_all = groups([valid_c[:, 0:tq]] * B_KV_HEADS)
    s = jnp.where(valid_all, _nt(kc, qs), NEG)
    ex = jnp.exp2(s - jnp.max(s, axis=0, keepdims=True))
    pc = jnp.where(valid_all, ex / jnp.sum(ex, axis=0, keepdims=True), 0.0)
    out = gate(0) * _tn(vc, pc.astype(BF16))
    sel_ts = []
    for g in range(B_KV_HEADS):
        psum = pc[:, rep * g * tq:(rep * g + 1) * tq]
        for r in range(1, rep):
            psum = psum + pc[:, (rep * g + r) * tq:(rep * g + r + 1) * tq]
        p_hi = psum.astype(BF16)
        p_lo = (psum - p_hi.astype(F32)).astype(BF16)
        imp = _mm(ovl_ref[...], p_hi) + _mm(ovl_ref[...], p_lo)
        score_t = jnp.where(adm, imp + forced, NEG)
        sc = [score_t[8 * v:8 * v + 8] for v in range(n_rv)]
        rank = [jnp.zeros((8, tq), F32) for _ in range(n_rv)]
        for jp in range(n_blk):
            col = score_t[jp:jp + 1]
            for v in range(n_rv):
                if v > jp // 8:
                    beats = col >= sc[v]
                elif v < jp // 8:
                    beats = col > sc[v]
                else:
                    beats = (col > sc[v]) | ((col == sc[v]) & (sub > jp % 8))
                rank[v] = rank[v] + jnp.where(beats, 1.0, 0.0)
        sel_ts.append(jnp.where(jnp.concatenate(rank, axis=0) < n_sel, 1.0, 0.0).astype(BF16))

    tile = lambda j: pl.ds(pl.multiple_of(j * tks, tks), tks)

    def sel_body(j, carry):
        kp = j * tks + lax.broadcasted_iota(I32, (tks, 1), 0)
        bias = [jnp.where((_mm(e_ref[tile(j), :], sel_t) > 0.5) & (kp <= t), 0.0, NEG) for sel_t in sel_ts]
        return _softmax_step_t(qs, ks_ref[0, tile(j), :], vs_ref[0, tile(j), :], groups(bias), carry)

    n_tile = ((i + 1) * tq + tks - 1) // tks
    _, l_s, acc_s = lax.fori_loop(0, n_tile, sel_body, _flash_init_t(B_HEADS * tq))
    out = out + gate(1) * (acc_s / l_s)
    s_w = _nt(kw_ref[0, pl.ds(w0, ww), :], qs) + groups([win_bias[:, 0:tq]] * B_KV_HEADS)
    p_w = jnp.exp2(s_w - jnp.max(s_w, axis=0, keepdims=True))
    o_w = _tn(vw_ref[0, pl.ds(w0, ww), :], p_w.astype(BF16)) / jnp.sum(p_w, axis=0, keepdims=True)
    out = out + gate(2) * o_w
    for g in range(B_KV_HEADS):
        outs = [out[:, (rep * g + r) * tq:(rep * g + r + 1) * tq].T for r in range(rep)]
        for pp in range(rep // 2):
            a, bb = outs[2 * pp], outs[2 * pp + 1]
            if g == 0:
                bb = pltpu.roll(bb, B_HD, 1)
            else:
                a = pltpu.roll(a, B_HD, 1)
            c0 = (rep // 2 * g + pp) * LANES
            o_ref[0, :, c0:c0 + LANES] = jnp.where(lane < B_HD, a, bb).astype(BF16)


def _nsa_attn(u, u32, cols, qx, kcmp, vcmp, ks, kw):
    b, h, s, _ = qx.shape
    tq = min(256, s)
    tks = min(512, s)
    n_pad = kcmp.shape[1]
    n_blk = s // SEL_LEN
    n_sel = min(N_SEL, n_blk)
    n_cmp = (s - CMP_LEN) // CMP_STRIDE + 1
    ww = min(WINDOW + tq, s)
    expand = np.zeros((s, n_blk), np.float32)
    expand[np.arange(s), np.arange(s) // SEL_LEN] = 1.0
    nn = np.arange(n_pad)[None, :]
    jj = np.arange(LANES)[:, None]
    ovl = ((nn * CMP_STRIDE <= jj * SEL_LEN + SEL_LEN - 1) & (nn * CMP_STRIDE + CMP_LEN - 1 >= jj * SEL_LEN)
           & (jj < n_blk) & (nn < n_cmp)).astype(np.float32)
    full = lambda shape: pl.BlockSpec(shape, lambda bi, i: (bi,) + (0,) * (len(shape) - 1))
    ucol = lambda name: pl.BlockSpec((1, s, LANES), lambda bi, i, c=cols[name] // LANES: (bi, 0, c))
    const = lambda shape: pl.BlockSpec(shape, lambda bi, i: (0,) * len(shape))
    return pl.pallas_call(
        functools.partial(_nsa_attn_kernel, n_blk=n_blk, n_sel=n_sel, tks=tks, ww=ww),
        out_shape=jax.ShapeDtypeStruct((b, s, h * B_HD), BF16),
        grid=(b, s // tq),
        in_specs=[pl.BlockSpec((1, h, tq, LANES), lambda bi, i: (bi, 0, i, 0)),
                  full((1, n_pad, LANES)), full((1, n_pad, LANES)),
                  full((1, s, LANES)), ucol("b_vs"), full((1, s, LANES)), ucol("b_vw"),
                  pl.BlockSpec((1, tq, LANES), lambda bi, i: (bi, i, 0)),
                  const((s, n_blk)), const((LANES, n_pad))],
        out_specs=pl.BlockSpec((1, tq, h * B_HD), lambda bi, i: (bi, i, 0)),
        compiler_params=_cp("parallel", "parallel"),
        name="nsa_attn",
    )(qx, kcmp, vcmp, ks, u, kw, u, u32, jnp.asarray(expand, BF16), jnp.asarray(ovl, BF16))


def _dsa_prep_kernel(q_in, iq_in, kv_in, sm_in, cos64_ref, sin64_ref, cos32_ref, sin32_ref, qg_ref, kg_ref,
                     q_ref, kv_ref, iq_ref, ik_ref):
    tm = q_in.shape[1]
    lane = _lane((tm, LANES))
    c64, s64, c32, s32 = cos64_ref[0], sin64_ref[0], cos32_ref[0], sin32_ref[0]
    for p in range(C_HEADS // 2):
        y = _rope(_group_rms(q_in[0, :, p * LANES:(p + 1) * LANES].astype(F32), C_HD) * qg_ref[...], c64, s64, C_HD // 2)
        y = y * (C_HD ** -0.5 * LOG2E)
        q_ref[0, 2 * p] = jnp.where(lane < C_HD, y, 0.0).astype(BF16)
        q_ref[0, 2 * p + 1] = jnp.where(lane < C_HD, pltpu.roll(y, C_HD, 1), 0.0).astype(BF16)
    kv = kv_in[0].astype(F32)
    kn = _rope(_group_rms(kv, C_HD) * kg_ref[...], c64, s64, C_HD // 2)
    kv_ref[0] = jnp.where(lane < C_HD, kn, kv).astype(BF16)
    for c0 in range(0, IDX_HEADS * IDX_HD, LANES):
        iq_ref[0, :, c0:c0 + LANES] = _rope(iq_in[0, :, c0:c0 + LANES].astype(F32), c32, s32, IDX_HD // 2).astype(BF16)
    ik = jnp.where(lane < IDX_HD, _rope(sm_in[0], c32, s32, IDX_HD // 2), 0.0)
    ik = ik + pltpu.roll(ik, IDX_HD, 1)
    ik = ik + pltpu.roll(ik, 2 * IDX_HD, 1)
    ik_ref[0] = ik.astype(BF16)


def _dsa_prep(u, u32, cols, tabs, p):
    b, s, _ = u.shape
    tm = min(512, s)
    row = pl.BlockSpec((1, tm, LANES), lambda bi, i: (bi, i, 0))
    const = pl.BlockSpec((1, LANES), lambda bi, i: (0, 0))
    ublk = lambda name, w: pl.BlockSpec((1, tm, w), lambda bi, i, c=cols[name] // w: (bi, i, c))
    dense = jax.ShapeDtypeStruct((b, s, LANES), BF16)
    return pl.pallas_call(
        _dsa_prep_kernel,
        out_shape=(jax.ShapeDtypeStruct((b, C_HEADS, s, LANES), BF16), dense,
                   jax.ShapeDtypeStruct((b, s, IDX_HEADS * IDX_HD), BF16), dense),
        grid=(b, s // tm),
        in_specs=[ublk("c_q", 512), ublk("c_iq", 256), ublk("c_kv", LANES), row,
                  row, row, row, row, const, const],
        out_specs=(pl.BlockSpec((1, C_HEADS, tm, LANES), lambda bi, i: (bi, 0, i, 0)), row,
                   pl.BlockSpec((1, tm, IDX_HEADS * IDX_HD), lambda bi, i: (bi, i, 0)), row),
        compiler_params=_cp("parallel", "parallel"),
        name="dsa_prep",
    )(u, u, u, u32, tabs["cos64"], tabs["sin64"], tabs["cos32"], tabs["sin32"], p["q_g"], p["k_g"])


def _dsa_attn_kernel(q_ref, iq_ref, sm_ref, ik_ref, kv_ref, tri_ref, o_ref, key_scr, bias_scr, *, tk, topk):
    tq = q_ref.shape[2]
    i = pl.program_id(1)
    t = i * tq + lax.broadcasted_iota(I32, (1, tq), 1)
    lane = _lane((tq, LANES))
    n_tile = ((i + 1) * tq + tk - 1) // tk
    tile = lambda j: pl.ds(pl.multiple_of(j * tk, tk), tk)
    kpos = lambda j: j * tk + lax.broadcasted_iota(I32, (tk, 1), 0)

    iw_t = (sm_ref[0] * (IDX_HEADS ** -0.5)).T
    parts = []
    for h in range(IDX_HEADS):
        blk = iq_ref[0, :, (h // 4) * LANES:(h // 4 + 1) * LANES]
        lo = (h % 4) * IDX_HD
        parts.append(jnp.where((lane >= lo) & (lane < lo + IDX_HD), blk, jnp.zeros_like(blk)))
    iqs = jnp.concatenate(parts, axis=0)

    def idx_body(j, carry):
        lg = _nt(ik_ref[0, tile(j), :], iqs)
        acc = jnp.zeros((tk, tq), F32)
        for h in range(IDX_HEADS):
            acc = acc + iw_t[IDX_HD + h:IDX_HD + h + 1, :] * jnp.maximum(lg[:, h * tq:(h + 1) * tq], 0.0)
        sc = jnp.where(kpos(j) <= t, acc, NEG)
        bits = pltpu.bitcast(sc, I32)
        key = jnp.where(bits < 0, bits ^ 0x7FFFFFFF, bits)
        key_scr[tile(j), :] = jnp.where(sc == 0.0, 0, key)
        return carry

    lax.fori_loop(0, n_tile, idx_body, 0)

    tr = min(tk, 2 * tq)
    n_sub = ((i + 1) * tq + tr - 1) // tr

    def count_ge(thr_key):
        def body(j, cs):
            key = key_scr[pl.ds(pl.multiple_of(j * tr, tr), tr), :]
            cs = list(cs)
            for n, r0 in enumerate(range(0, tr, 8)):
                c = cs[n % len(cs)]
                cs[n % len(cs)] = jnp.where(key[r0:r0 + 8] >= thr_key, c + 1, c)
            return tuple(cs)

        cs = lax.fori_loop(0, n_sub, body, tuple(jnp.zeros((8, tq), I32) for _ in range(4)))
        return jnp.sum(cs[0] + cs[1] + cs[2] + cs[3], axis=0, keepdims=True)

    def bit_body(bi, ucand):
        utrial = ucand | jnp.left_shift(jnp.int32(1), 31 - bi)
        return jnp.where(count_ge(utrial ^ INT_MIN) >= topk, utrial, ucand)

    thr = lax.fori_loop(0, 32, bit_body, jnp.zeros((1, tq), I32)) ^ INT_MIN
    n_ge = count_ge(thr)
    n_gt = count_ge(thr + 1)
    need = topk - n_gt
    row_ok = (n_ge - n_gt == need) | (thr == NEG_KEY) | (n_ge < topk)
    simple = jnp.min(jnp.where(row_ok, 1.0, 0.0)) > 0.5

    def fast_bias():
        def body(j, carry):
            keep = (key_scr[tile(j), :] >= thr) & (kpos(j) <= t)
            bias_scr[tile(j), :] = jnp.where(keep, 0.0, NEG)
            return carry

        lax.fori_loop(0, n_tile, body, 0)

    def tie_bias():
        need_f = need.astype(F32)

        def body(j, run):
            key = key_scr[tile(j), :]
            kp = kpos(j)
            for c0 in range(0, tk, LANES):
                kc = key[c0:c0 + LANES]
                eq = kc == thr
                eq_f = jnp.where(eq, 1.0, 0.0)
                pref = _mm(tri_ref[...], eq_f.astype(BF16)) + run
                keep = ((kc > thr) | (eq & (pref <= need_f))) & (kp[c0:c0 + LANES] <= t)
                bias_scr[pl.ds(pl.multiple_of(j * tk + c0, LANES), LANES), :] = jnp.where(keep, 0.0, NEG)
                run = run + jnp.sum(eq_f, axis=0, keepdims=True)
            return run

        lax.fori_loop(0, n_tile, body, jnp.zeros((1, tq), F32))

    lax.cond(simple, fast_bias, tie_bias)

    qs = jnp.concatenate([q_ref[0, h] for h in range(C_HEADS)], axis=0)

    def att_body(j, carry):
        kv = kv_ref[0, tile(j), :]
        bias_t = jnp.concatenate([bias_scr[tile(j), :]] * C_HEADS, axis=1)
        return _softmax_step_t(qs, kv, kv, bias_t, carry)

    _, l, acc = lax.fori_loop(0, n_tile, att_body, _flash_init_t(C_HEADS * tq))
    o_t = acc / l
    for p in range(C_HEADS // 2):
        a = pltpu.roll(o_t[:, (2 * p) * tq:(2 * p + 1) * tq].T, C_HD, 1)
        bb = o_t[:, (2 * p + 1) * tq:(2 * p + 2) * tq].T
        o_ref[0, :, p * LANES:(p + 1) * LANES] = jnp.where(lane < C_HD, a, bb).astype(BF16)


def _dsa_attn(u32, qc, iq, ik, kv):
    b, h, s, _ = qc.shape
    tq = min(256, s)
    tk = min(1024, s)
    topk = min(TOPK_MAX, s // 4)
    tri = np.tril(np.ones((LANES, LANES), np.float32))
    full = lambda shape: pl.BlockSpec(shape, lambda bi, i: (bi,) + (0,) * (len(shape) - 1))
    return pl.pallas_call(
        functools.partial(_dsa_attn_kernel, tk=tk, topk=topk),
        out_shape=jax.ShapeDtypeStruct((b, s, h * C_HD), BF16),
        grid=(b, s // tq),
        in_specs=[pl.BlockSpec((1, h, tq, LANES), lambda bi, i: (bi, 0, i, 0)),
                  pl.BlockSpec((1, tq, IDX_HEADS * IDX_HD), lambda bi, i: (bi, i, 0)),
                  pl.BlockSpec((1, tq, LANES), lambda bi, i: (bi, i, 0)),
                  full((1, s, LANES)), full((1, s, LANES)),
                  pl.BlockSpec((LANES, LANES), lambda bi, i: (0, 0))],
        out_specs=pl.BlockSpec((1, tq, h * C_HD), lambda bi, i: (bi, i, 0)),
        scratch_shapes=[pltpu.VMEM((s, tq), I32), pltpu.VMEM((s, tq), F32)],
        compiler_params=_cp("parallel", "parallel"),
        name="dsa_attn",
    )(qc, iq, u32, ik, kv, jnp.asarray(tri, BF16))


def _mlstm_conv_kernel(x_ref, halo_ref, w_ref, b_ref, o_ref):
    tm = x_ref.shape[1]
    i = pl.program_id(1)
    lane = _lane((tm, x_ref.shape[2]))
    halo = jnp.where(i > 0, halo_ref[0].astype(F32), 0.0)
    x = x_ref[0].astype(F32)
    xc = jnp.concatenate([halo, x], axis=0)
    off = halo.shape[0] - (CONV_W - 1)
    y = b_ref[...] + jnp.zeros_like(x)
    for j in range(CONV_W):
        y = y + w_ref[j:j + 1, :] * xc[off + j:off + j + tm, :]
    y = y * _sigmoid(y)
    o_ref[0] = jnp.where(lane >= D_HEADS * D_QK, y * (D_QK ** -0.5), y)


def _mlstm_conv(u, qk_block, w, bias):
    b, s, _ = u.shape
    c = w.shape[1]
    tm = min(512, s)
    hb = 16
    return pl.pallas_call(
        _mlstm_conv_kernel,
        out_shape=jax.ShapeDtypeStruct((b, s, c), F32),
        grid=(b, s // tm),
        in_specs=[pl.BlockSpec((1, tm, c), lambda bi, i: (bi, i, qk_block)),
                  pl.BlockSpec((1, hb, c), lambda bi, i: (bi, jnp.maximum(i * (tm // hb) - 1, 0), qk_block)),
                  pl.BlockSpec((CONV_W, c), lambda bi, i: (0, 0)),
                  pl.BlockSpec((1, c), lambda bi, i: (0, 0))],
        out_specs=pl.BlockSpec((1, tm, c), lambda bi, i: (bi, i, 0)),
        compiler_params=_cp("parallel", "parallel"),
        name="mlstm_conv",
    )(u, u, w, bias)


def _mlstm_scan_kernel(qk_ref, v_ref, op_ref, sm_ref, gt_ref, bcol_ref, brow_ref, hg_ref, o_ref,
                       c_scr, m_scr):
    tc = qk_ref.shape[1]
    L = CHUNK
    nqk = D_HEADS * D_QK

    @pl.when(pl.program_id(1) == 0)
    def _():
        c_scr[...] = jnp.zeros_like(c_scr)
        m_scr[...] = jnp.zeros_like(m_scr)

    r = lax.broadcasted_iota(I32, (L, L), 0)
    c = lax.broadcasted_iota(I32, (L, L), 1)
    tril = c <= r
    tril_f = jnp.where(tril, 1.0, 0.0).astype(BF16)
    triu_f = jnp.where(r <= c, 1.0, 0.0).astype(BF16)
    lane = _lane((L, LANES))
    row128 = lax.broadcasted_iota(I32, (LANES, 2 * LANES), 0)
    ones = jnp.ones((L, LANES), F32)
    i_lane, f_lane = IDX_HD + IDX_HEADS, IDX_HD + IDX_HEADS + D_HEADS

    def one(bb, ci):
        rows = pl.ds(pl.multiple_of(ci * L, L), L)
        qk = qk_ref[bb, rows, :]
        sm = sm_ref[bb, rows, :] + bcol_ref[...]
        bcol_all = sum(_mm(tril_f, part) for part in _split3(_log_sigmoid(sm)))
        gt = gt_ref[bb, ci] + brow_ref[...]
        brow_all = sum(_mm(part, triu_f) for part in _split3(_log_sigmoid(gt)))
        for p in range(D_HEADS // 2):
            qpair = qk[:, p * LANES:(p + 1) * LANES]
            kpair = qk[:, nqk + p * LANES:nqk + (p + 1) * LANES]
            kpair_b = kpair.astype(BF16)
            sp = bb * (D_HEADS // 2) + p
            c_prev = c_scr[sp]
            c_prev_b = c_prev.astype(BF16)
            upd = []
            for e in range(2):
                h = 2 * p + e
                in_half = (lane >= D_QK * e) & (lane < D_QK * (e + 1))
                qm = jnp.where(in_half, qpair, 0.0)
                qm_b = qm.astype(BF16)
                vh = v_ref[bb, rows, h * D_VD:(h + 1) * D_VD].astype(F32)
                b_col = jnp.broadcast_to(bcol_all[:, f_lane + h:f_lane + h + 1], (L, LANES))
                i_col = jnp.broadcast_to(sm[:, i_lane + h:i_lane + h + 1], (L, LANES))
                b_row = brow_all[D_HEADS + h:D_HEADS + h + 1, :]
                i_row = gt[h:h + 1, :]
                m_prev = m_scr[bb * D_HEADS + h, 0:1, :]
                dmat = jnp.where(tril, b_col[:, :L] - b_row + i_row, NEG)
                inter = b_col + m_prev
                m_t = jnp.maximum(inter, jnp.max(dmat, axis=-1, keepdims=True))
                a = jnp.exp(inter - m_t)
                w = _nt(qm_b, kpair_b) * jnp.exp(dmat - m_t[:, :L])
                qc = _mm(qm_b, c_prev_b)
                wv = _mm(w.astype(BF16), jnp.concatenate([vh, ones], axis=1).astype(BF16))
                num = a * qc[:, :LANES] + wv[:, :LANES]
                den = a * qc[:, LANES:] + wv[:, LANES:]
                hout = num / jnp.maximum(jnp.abs(den), jnp.exp(-m_t))
                b_last = b_col[L - 1:L, :]
                g_col = b_last - b_col + i_col
                m_new = jnp.maximum(b_last + m_prev, jnp.max(g_col, axis=0, keepdims=True))
                ws = jnp.exp(g_col - m_new)
                decay = jnp.exp(b_last + m_prev - m_new)
                u_mat = _tn(kpair_b, jnp.concatenate([ws * vh, ws], axis=1).astype(BF16))
                upd.append(jnp.concatenate([decay, decay], axis=1) * c_prev + u_mat)
                m_scr[bb * D_HEADS + h] = jnp.broadcast_to(m_new, m_scr.shape[1:])
                hn = _row_rms(hout) * hg_ref[...]
                y = _sigmoid(op_ref[bb, rows, h * D_VD:(h + 1) * D_VD].astype(F32)) * hn
                o_ref[bb, rows, h * D_VD:(h + 1) * D_VD] = y.astype(BF16)
            c_scr[sp] = jnp.where(row128 < D_QK, upd[0], upd[1])

    def chunk(ci, carry):
        for bb in range(qk_ref.shape[0]):
            one(bb, ci)
        return carry

    lax.fori_loop(0, tc // L, chunk, 0)


def _mlstm_scan(u, u32, cols, qk, gt, p):
    b, s, _ = u.shape
    tc = min(256, s)
    nb = next(n for n in (8, 4, 2, 1) if b % n == 0)
    nv = D_HEADS * D_VD
    ublk = lambda name, w: pl.BlockSpec((nb, tc, w), lambda bi, i, c=cols[name] // w: (bi, i, c))
    const = lambda shape: pl.BlockSpec(shape, lambda bi, i: (0,) * len(shape))
    return pl.pallas_call(
        _mlstm_scan_kernel,
        out_shape=jax.ShapeDtypeStruct((b, s, nv), BF16),
        grid=(b // nb, s // tc),
        in_specs=[pl.BlockSpec((nb, tc, qk.shape[-1]), lambda bi, i: (bi, i, 0)),
                  ublk("d_v", nv), ublk("d_o", nv), pl.BlockSpec((nb, tc, LANES), lambda bi, i: (bi, i, 0)),
                  pl.BlockSpec((nb, tc // CHUNK, 8, CHUNK), lambda bi, i: (bi, i, 0, 0)),
                  const((1, LANES)), const((8, 1)), const((1, D_VD))],
        out_specs=pl.BlockSpec((nb, tc, nv), lambda bi, i: (bi, i, 0)),
        scratch_shapes=[pltpu.VMEM((nb * D_HEADS // 2, LANES, 2 * LANES), F32),
                        pltpu.VMEM((nb * D_HEADS, 8, LANES), F32)],
        compiler_params=_cp("parallel", "arbitrary"),
        name="mlstm_scan",
    )(qk, u, u, u32, gt, p["bias_col"], p["bias_row"], p["h_g"])


EVEN_SRC = dict(a_ql=(0, 256), a_kvl=(256, 128), a_kr=(384, 32), a_gate=(416, 512), b_q=(928, 512),
                b_kc=(1440, 128), b_vc=(1568, 128), b_ks=(1696, 128), b_vs=(1824, 128), b_kw=(1952, 128),
                b_vw=(2080, 128), b_g=(2208, 24), b_gate=(2232, 512), m_q=(2744, 256), m_gate=(3000, 256))
EVEN_DST = dict(a_ql=0, a_kvl=256, a_kr=384, b_q=512, a_gate=1024, b_gate=1536, m_q=2048, m_gate=2304,
                b_g=2560, b_kc=2688, b_vc=2816, b_ks=2944, b_vs=3072, b_kw=3200, b_vw=3328)
EVEN_COLS_PAD = 3456

ODD_SRC = dict(c_q=(0, 512), c_k=(512, 64), c_v=(576, 64), c_iq=(640, 256), c_ik=(896, 32), c_iw=(928, 8),
               c_gate=(936, 512), d_q=(1448, 256), d_k=(1704, 256), d_v=(1960, 512), d_i=(2472, 4),
               d_f=(2476, 4), d_o=(2480, 512), d_gate=(2992, 512), m_q=(3504, 256), m_gate=(3760, 256))
ODD_DST = dict(c_q=0, c_gate=512, d_gate=1024, d_v=1536, d_o=2048, d_q=2560, d_k=2816, m_q=3072, m_gate=3328,
               c_iq=3584, c_k=3840, c_v=3904, c_ik=3968, c_iw=4000, d_i=4008, d_f=4012)
ODD_COLS_PAD = 4096


def _permute_cols(w, src, dst, total):
    pieces, pos = [], 0
    for name in sorted(src, key=lambda n: dst[n]):
        start, width = src[name]
        if dst[name] > pos:
            pieces.append(jnp.zeros((w.shape[0], dst[name] - pos), BF16))
        pieces.append(w[:, start:start + width].astype(BF16))
        pos = dst[name] + width
    if total > pos:
        pieces.append(jnp.zeros((w.shape[0], total - pos), BF16))
    return jnp.concatenate(pieces, axis=1)


def _heads_split(w, heads, first):
    w3 = w.reshape(w.shape[0], heads, -1)
    return jnp.concatenate([w3[:, :, :first].reshape(w.shape[0], -1),
                            w3[:, :, first:].reshape(w.shape[0], -1)], axis=1).astype(BF16)


def _tile_lanes(v, reps):
    return jnp.tile(v.astype(F32).reshape(1, -1), (1, reps))


def _rope_tables(positions, d2):
    inv = ROPE_THETA ** (-jnp.arange(d2, dtype=F32) / d2)
    ang = positions.astype(F32)[..., None] * inv
    c, s = jnp.cos(ang), jnp.sin(ang)
    reps = LANES // (2 * d2)
    return (jnp.tile(jnp.concatenate([c, c], axis=-1), (1, 1, reps)),
            jnp.tile(jnp.concatenate([-s, s], axis=-1), (1, 1, reps)))


def _block_diag2(w):
    z = jnp.zeros_like(w)
    return jnp.concatenate([jnp.concatenate([w, z], axis=-1), jnp.concatenate([z, w], axis=-1)], axis=-2)


def kernel(x, mem, positions, ln_g, mem_norm_g, mem_w_kv, mem_q_norm_g, mem_k_norm_g, w_out, even_w_in, mla_q_lat_g, mla_kv_lat_g, mla_w_uq, mla_w_ukv, mla_q_norm_g, mla_k_norm_g, nsa_q_norm_g, nsa_k_norm_g, nsa_cmp_pos, nsa_cmp_w1, nsa_cmp_w2, odd_w_in, dsa_q_norm_g, dsa_k_norm_g, mlstm_conv_w, mlstm_conv_b, mlstm_i_bias, mlstm_f_bias, mlstm_h_norm_g):
    b, s, _ = x.shape
    depth = ln_g.shape[0]
    cos64, sin64 = _rope_tables(positions, 32)
    cos32, sin32 = _rope_tables(positions, 16)
    tabs = dict(cos64=cos64, sin64=sin64, cos32=cos32, sin32=sin32)
    n_pad = s // CMP_STRIDE
    cmp_pos = jnp.pad(positions[:, CMP_LEN - 1::CMP_STRIDE], ((0, 0), (0, 0)))[:, :n_pad]
    cmp_pos = jnp.pad(cmp_pos, ((0, 0), (0, n_pad - cmp_pos.shape[1])))
    cosc, sinc = _rope_tables(cmp_pos, 32)

    mem_k, mem_v = _mem_kv(mem, mem_norm_g.reshape(depth, 1, -1), mem_w_kv.astype(BF16),
                           jnp.tile(mem_k_norm_g, (1, 2)).reshape(depth, 1, LANES))

    for layer in range(depth):
        li = layer // 2
        g_ln = ln_g[layer].reshape(1, -1)
        mq_g = _tile_lanes(mem_q_norm_g[layer], 2)
        if layer % 2 == 0:
            cols = EVEN_DST
            u, u32 = _in_proj(x, g_ln, _permute_cols(even_w_in[li], EVEN_SRC, EVEN_DST, EVEN_COLS_PAD), cols["b_g"])
            pa = dict(q_lat_g=mla_q_lat_g[li].reshape(1, -1), kv_lat_g=mla_kv_lat_g[li].reshape(1, -1),
                      w_uq=_heads_split(mla_w_uq[li], A_HEADS, A_NOPE), w_ukv=_heads_split(mla_w_ukv[li], A_HEADS, A_NOPE),
                      qn_g=_tile_lanes(mla_q_norm_g[li, :A_NOPE], 2), qr_g=_tile_lanes(mla_q_norm_g[li, A_NOPE:], 4),
                      kn_g=_tile_lanes(mla_k_norm_g[li, :A_NOPE], 2), kr_g=_tile_lanes(mla_k_norm_g[li, A_NOPE:], 4))
            qa, ka, va = _mla_prep(u, cos32, sin32, pa)
            y1 = _mla_attn(qa, ka, va)
            pb = dict(q_g=_tile_lanes(nsa_q_norm_g[li], 2), ks_g=_tile_lanes(nsa_k_norm_g[li, 1], 2),
                      kw_g=_tile_lanes(nsa_k_norm_g[li, 2], 2), kc_g=_tile_lanes(nsa_k_norm_g[li, 0], 2),
                      pe_k=jnp.tile(nsa_cmp_pos[li, 0], (1, 2)), pe_v=jnp.tile(nsa_cmp_pos[li, 1], (1, 2)),
                      w1k=_block_diag2(nsa_cmp_w1[li, 0].reshape(CMP_LEN, B_HD, B_HD)).astype(BF16),
                      w1v=_block_diag2(nsa_cmp_w1[li, 1].reshape(CMP_LEN, B_HD, B_HD)).astype(BF16),
                      w2k=_block_diag2(nsa_cmp_w2[li, 0]).astype(BF16),
                      w2v=_block_diag2(nsa_cmp_w2[li, 1]).astype(BF16))
            qb, ks, kw = _nsa_prep(u, cols, cos64, sin64, pb)
            kcmp, vcmp = _nsa_cmp(u, cols, cosc, sinc, pb)
            y2 = _nsa_attn(u, u32, cols, qb, kcmp, vcmp, ks, kw)
            gate_blocks = (cols["a_gate"] // 512, cols["b_gate"] // 512, cols["m_gate"] // 256)
        else:
            cols = dict(ODD_DST, c_kv=ODD_DST["c_k"], small=ODD_DST["c_ik"])
            u, u32 = _in_proj(x, g_ln, _permute_cols(odd_w_in[li], ODD_SRC, ODD_DST, ODD_COLS_PAD), cols["small"])
            pc = dict(q_g=_tile_lanes(dsa_q_norm_g[li], 2), k_g=_tile_lanes(dsa_k_norm_g[li], 2))
            qc, kvc, iq, ik = _dsa_prep(u, u32, cols, tabs, pc)
            y1 = _dsa_attn(u32, qc, iq, ik, kvc)
            qk = _mlstm_conv(u, cols["d_q"] // 512, mlstm_conv_w[li], mlstm_conv_b[li].reshape(1, -1))
            gates = u32[:, :, cols["d_i"] - cols["small"]:cols["d_i"] - cols["small"] + 2 * D_HEADS]
            gt = gates.reshape(b, s // CHUNK, CHUNK, 2 * D_HEADS).transpose(0, 1, 3, 2)
            bias8 = jnp.concatenate([mlstm_i_bias[li], mlstm_f_bias[li]]).astype(F32)
            bias_col = jnp.zeros((1, LANES), F32).at[0, cols["d_i"] - cols["small"]:cols["d_i"] - cols["small"] + 8].set(bias8)
            pd = dict(bias_col=bias_col, bias_row=bias8.reshape(8, 1), h_g=mlstm_h_norm_g[li].reshape(1, -1))
            y2 = _mlstm_scan(u, u32, cols, qk, gt, pd)
            gate_blocks = (cols["c_gate"] // 512, cols["d_gate"] // 512, cols["m_gate"] // 256)
        ym = _mem_attn(u, cols["m_q"] // 256, mem_k, mem_v, layer, mq_g)
        x = _out_proj(x, y1, y2, ym, u, gate_blocks, w_out[layer].astype(BF16))
    return x
```

```python
import functools

import numpy as np
import jax
import jax.numpy as jnp
from jax import lax
from jax.experimental import pallas as pl
from jax.experimental.pallas import tpu as pltpu

F32, BF16, I32 = jnp.float32, jnp.bfloat16, jnp.int32
NEG = -1e30
EPS = 1e-6
ROPE_THETA = 10000.0
LANES = 128
VMEM_LIMIT_BYTES = 48 * 1024 * 1024

D_MODEL = 1024
DEPTH = 4
A_HEADS, A_NOPE, A_ROPE, A_VD, A_QLAT, A_KVLAT = 8, 64, 32, 64, 256, 128
B_HEADS, B_KV_HEADS, B_HD = 8, 2, 64
CMP_LEN, CMP_STRIDE, SEL_LEN, N_SEL, WINDOW = 32, 16, 64, 16, 512
C_HEADS, C_HD, IDX_HEADS, IDX_HD, TOPK_MAX = 8, 64, 8, 32, 256
D_HEADS, D_QK, D_VD, CONV_W, CHUNK = 4, 64, 128, 4, 64
M_HEADS, M_HD = 4, 64

INT_MIN = np.int32(-2 ** 31)
NEG_KEY = int(np.float32(NEG).view(np.int32) ^ np.int32(0x7FFFFFFF))
LOG2E = float(np.log2(np.e))


def _cp(*sem):
    return pltpu.CompilerParams(dimension_semantics=sem, vmem_limit_bytes=VMEM_LIMIT_BYTES)


def _nt(a, b):
    return lax.dot_general(a, b, (((1,), (1,)), ((), ())), preferred_element_type=F32)


def _tn(a, b):
    return lax.dot_general(a, b, (((0,), (0,)), ((), ())), preferred_element_type=F32)


def _mm(a, b):
    return jnp.dot(a, b, preferred_element_type=F32)


def _sigmoid(x):
    return 1.0 / (1.0 + jnp.exp(-x))


def _log_sigmoid(x):
    return jnp.minimum(x, 0.0) - jnp.log1p(jnp.exp(-jnp.abs(x)))


def _lane(shape):
    return lax.broadcasted_iota(I32, shape, len(shape) - 1)


def _group_mat(gs):
    r = lax.broadcasted_iota(I32, (LANES, LANES), 0)
    c = lax.broadcasted_iota(I32, (LANES, LANES), 1)
    sh = gs.bit_length() - 1
    return jnp.where((r >> sh) == (c >> sh), 1.0, 0.0).astype(BF16)


def _split3(x):
    hi = x.astype(BF16)
    r1 = x - hi.astype(F32)
    mid = r1.astype(BF16)
    return hi, mid, (r1 - mid.astype(F32)).astype(BF16)


def _mm_split(x, w01):
    hi = x.astype(BF16)
    lo = (x - hi.astype(F32)).astype(BF16)
    return _mm(hi, w01) + _mm(lo, w01)


def _group_rms(x, gs):
    ss = _mm_split(x * x, _group_mat(gs))
    return x * lax.rsqrt(ss * (1.0 / gs) + EPS)


def _rope(x, cosp, sinp, half):
    lane = _lane(x.shape)
    rot = jnp.where((lane & (2 * half - 1)) < half,
                    pltpu.roll(x, LANES - half, 1), pltpu.roll(x, half, 1))
    return x * cosp + rot * sinp


def _row_rms(x):
    return x * lax.rsqrt(jnp.mean(x * x, axis=-1, keepdims=True) + EPS)


def _in_proj_kernel(x_ref, g_ref, w_ref, o_ref, o32_ref, *, c32):
    h = (_row_rms(x_ref[0]) * g_ref[...]).astype(BF16)
    ncol = o_ref.shape[-1]
    for c0 in range(0, ncol, 1024):
        c1 = min(ncol, c0 + 1024)
        o_ref[0, :, c0:c1] = _mm(h, w_ref[:, c0:c1]).astype(BF16)
    o32_ref[0] = _mm(h, w_ref[:, c32:c32 + LANES])


def _in_proj(x, g, w, c32):
    b, s, d = x.shape
    c = w.shape[1]
    tm = min(1024, s)
    return pl.pallas_call(
        functools.partial(_in_proj_kernel, c32=c32),
        out_shape=(jax.ShapeDtypeStruct((b, s, c), BF16), jax.ShapeDtypeStruct((b, s, LANES), F32)),
        grid=(b, s // tm),
        in_specs=[pl.BlockSpec((1, tm, d), lambda bi, i: (bi, i, 0)),
                  pl.BlockSpec((1, d), lambda bi, i: (0, 0)),
                  pl.BlockSpec((d, c), lambda bi, i: (0, 0))],
        out_specs=(pl.BlockSpec((1, tm, c), lambda bi, i: (bi, i, 0)),
                   pl.BlockSpec((1, tm, LANES), lambda bi, i: (bi, i, 0))),
        compiler_params=_cp("parallel", "parallel"),
        name="in_proj",
    )(x, g, w)


def _out_proj_kernel(x_ref, y1_ref, y2_ref, ym_ref, g1_ref, g2_ref, gm_ref, w_ref, o_ref):
    def gated(y_ref, g_ref):
        g = g_ref[0].astype(F32)
        return (y_ref[0].astype(F32) * (g * _sigmoid(g))).astype(BF16)

    n1 = y1_ref.shape[-1]
    n2 = y2_ref.shape[-1]
    acc = x_ref[0]
    acc = acc + _mm(gated(y1_ref, g1_ref), w_ref[0:n1, :])
    acc = acc + _mm(gated(y2_ref, g2_ref), w_ref[n1:n1 + n2, :])
    acc = acc + _mm(gated(ym_ref, gm_ref), w_ref[n1 + n2:, :])
    o_ref[0] = acc


def _out_proj(x, y1, y2, ym, u, gate_blocks, w):
    b, s, d = x.shape
    tm = min(512, s)
    i1, i2, im = gate_blocks
    n1, n2, nm = y1.shape[-1], y2.shape[-1], ym.shape[-1]
    row = lambda bi, i: (bi, i, 0)
    return pl.pallas_call(
        _out_proj_kernel,
        out_shape=jax.ShapeDtypeStruct((b, s, d), F32),
        grid=(b, s // tm),
        in_specs=[pl.BlockSpec((1, tm, d), row),
                  pl.BlockSpec((1, tm, n1), row),
                  pl.BlockSpec((1, tm, n2), row),
                  pl.BlockSpec((1, tm, nm), row),
                  pl.BlockSpec((1, tm, n1), lambda bi, i: (bi, i, i1)),
                  pl.BlockSpec((1, tm, n2), lambda bi, i: (bi, i, i2)),
                  pl.BlockSpec((1, tm, nm), lambda bi, i: (bi, i, im)),
                  pl.BlockSpec(w.shape, lambda bi, i: (0, 0))],
        out_specs=pl.BlockSpec((1, tm, d), row),
        compiler_params=_cp("parallel", "parallel"),
        name="out_proj",
    )(x, y1, y2, ym, u, u, u, w)


def _mem_kv_kernel(mem_ref, g_ref, w_ref, kg_ref, k_ref, v_ref):
    h = (_row_rms(mem_ref[0]) * g_ref[0]).astype(BF16)
    kv = _mm(h, w_ref[0])
    nk = k_ref.shape[-1]
    for c0 in range(0, nk, LANES):
        k_ref[0, 0, :, c0:c0 + LANES] = (_group_rms(kv[:, c0:c0 + LANES], M_HD) * kg_ref[0]).astype(BF16)
    v_ref[0, 0] = kv[:, nk:].astype(BF16)


def _mem_kv(mem, g, w, kg):
    b, m, d = mem.shape
    depth = w.shape[0]
    nk = M_HEADS * M_HD
    out = jax.ShapeDtypeStruct((depth, b, m, nk), BF16)
    return pl.pallas_call(
        _mem_kv_kernel,
        out_shape=(out, out),
        grid=(depth, b),
        in_specs=[pl.BlockSpec((1, m, d), lambda l, bi: (bi, 0, 0)),
                  pl.BlockSpec((1, 1, d), lambda l, bi: (l, 0, 0)),
                  pl.BlockSpec((1, d, 2 * nk), lambda l, bi: (l, 0, 0)),
                  pl.BlockSpec((1, 1, LANES), lambda l, bi: (l, 0, 0))],
        out_specs=(pl.BlockSpec((1, 1, m, nk), lambda l, bi: (l, bi, 0, 0)),
                   pl.BlockSpec((1, 1, m, nk), lambda l, bi: (l, bi, 0, 0))),
        compiler_params=_cp("parallel", "parallel"),
        name="mem_kv",
    )(mem, g, w, kg)


def _mem_attn_kernel(q_ref, k_ref, v_ref, qg_ref, o_ref):
    tq = q_ref.shape[1]
    lane = _lane((tq, LANES))
    row = lax.broadcasted_iota(I32, (LANES, tq), 0)
    parts = []
    for p in range(M_HEADS // 2):
        sl = slice(p * LANES, (p + 1) * LANES)
        q = (_group_rms(q_ref[0, :, sl].astype(F32), M_HD) * qg_ref[...] * (M_HD ** -0.5 * LOG2E)).astype(BF16)
        for e in range(2):
            in_half = (lane >= M_HD * e) & (lane < M_HD * (e + 1))
            parts.append(_nt(k_ref[0, 0, :, sl], jnp.where(in_half, q, jnp.zeros_like(q))))
    s = jnp.concatenate(parts, axis=1)
    ex = jnp.exp2(s - jnp.max(s, axis=0, keepdims=True))
    inv = 1.0 / jnp.sum(ex, axis=0, keepdims=True)
    for p in range(M_HEADS // 2):
        sl = slice(p * LANES, (p + 1) * LANES)
        pv = _tn(v_ref[0, 0, :, sl], ex[:, 2 * p * tq:(2 * p + 2) * tq].astype(BF16)) * inv[:, 2 * p * tq:(2 * p + 2) * tq]
        o_ref[0, :, sl] = jnp.where(row < M_HD, pv[:, :tq], pv[:, tq:]).T.astype(BF16)


def _mem_attn(u, q_block, k, v, layer, qg):
    b, s, _ = u.shape
    m, nk = k.shape[2], k.shape[3]
    tq = min(1024, s)
    return pl.pallas_call(
        _mem_attn_kernel,
        out_shape=jax.ShapeDtypeStruct((b, s, nk), BF16),
        grid=(b, s // tq),
        in_specs=[pl.BlockSpec((1, tq, nk), lambda bi, i: (bi, i, q_block)),
                  pl.BlockSpec((1, 1, m, nk), lambda bi, i: (layer, bi, 0, 0)),
                  pl.BlockSpec((1, 1, m, nk), lambda bi, i: (layer, bi, 0, 0)),
                  pl.BlockSpec((1, LANES), lambda bi, i: (0, 0))],
        out_specs=pl.BlockSpec((1, tq, nk), lambda bi, i: (bi, i, 0)),
        compiler_params=_cp("parallel", "parallel"),
        name="mem_attn",
    )(u, k, v, qg)


def _flash_init_t(cols):
    return (jnp.full((1, cols), NEG, F32), jnp.zeros((1, cols), F32), jnp.zeros((LANES, cols), F32))


def _softmax_step_t(q, k, v, bias_t, carry):
    s = _nt(k, q)
    if bias_t is not None:
        s = s + bias_t
    m, l, acc = carry
    m_new = jnp.maximum(m, jnp.max(s, axis=0, keepdims=True))
    alpha = jnp.exp2(m - m_new)
    p = jnp.exp2(s - m_new)
    l = alpha * l + jnp.sum(p, axis=0, keepdims=True)
    acc = alpha * acc + _tn(v, p.astype(BF16))
    return m_new, l, acc


def _mla_prep_kernel(u_ref, cos_ref, sin_ref, qlg_ref, kvlg_ref, wuq_ref, wukv_ref,
                     qng_ref, qrg_ref, kng_ref, krg_ref, q_ref, k_ref, v_ref):
    tm = u_ref.shape[1]
    lane = _lane((tm, LANES))
    cosp, sinp = cos_ref[0], sin_ref[0]
    half = A_ROPE // 2
    scale = (A_NOPE + A_ROPE) ** -0.5 * LOG2E
    ql = (_row_rms(u_ref[0, :, 0:A_QLAT].astype(F32)) * qlg_ref[...]).astype(BF16)
    kvl = (_row_rms(u_ref[0, :, A_QLAT:A_QLAT + A_KVLAT].astype(F32)) * kvlg_ref[...]).astype(BF16)
    q = _mm(ql, wuq_ref[...])
    kv = _mm(kvl, wukv_ref[...])
    n_nope = A_HEADS * A_NOPE
    v_ref[0] = kv[:, n_nope:].astype(BF16)
    kr = u_ref[0, :, A_QLAT + A_KVLAT:A_QLAT + A_KVLAT + LANES].astype(F32)
    kpe = _rope(_group_rms(kr, A_ROPE) * krg_ref[...], cosp, sinp, half)
    kpe = pltpu.roll(kpe, A_NOPE, 1)
    qn = [_group_rms(q[:, c:c + LANES], A_NOPE) * qng_ref[...] for c in range(0, n_nope, LANES)]
    kn = [_group_rms(kv[:, c:c + LANES], A_NOPE) * kng_ref[...] for c in range(0, n_nope, LANES)]
    qr = [_rope(_group_rms(q[:, n_nope + c:n_nope + c + LANES], A_ROPE) * qrg_ref[...], cosp, sinp, half)
          for c in range(0, A_HEADS * A_ROPE, LANES)]
    for h in range(A_HEADS):
        qn_h = qn[h // 2] if h % 2 == 0 else pltpu.roll(qn[h // 2], A_NOPE, 1)
        kn_h = kn[h // 2] if h % 2 == 0 else pltpu.roll(kn[h // 2], A_NOPE, 1)
        shift = (A_NOPE - (h % 4) * A_ROPE) % LANES
        qr_h = qr[h // 4] if shift == 0 else pltpu.roll(qr[h // 4], shift, 1)
        qf = jnp.where(lane < A_NOPE, qn_h, jnp.where(lane < A_NOPE + A_ROPE, qr_h, 0.0))
        kf = jnp.where(lane < A_NOPE, kn_h, jnp.where(lane < A_NOPE + A_ROPE, kpe, 0.0))
        q_ref[0, h] = (qf * scale).astype(BF16)
        k_ref[0, h] = kf.astype(BF16)


def _mla_prep(u, cos32, sin32, p):
    b, s, _ = u.shape
    tm = min(512, s)
    hd = jax.ShapeDtypeStruct((b, A_HEADS, s, LANES), BF16)
    const = lambda shape: pl.BlockSpec(shape, lambda bi, i: (0,) * len(shape))
    return pl.pallas_call(
        _mla_prep_kernel,
        out_shape=(hd, hd, jax.ShapeDtypeStruct((b, s, A_HEADS * A_VD), BF16)),
        grid=(b, s // tm),
        in_specs=[pl.BlockSpec((1, tm, 512), lambda bi, i: (bi, i, 0)),
                  pl.BlockSpec((1, tm, LANES), lambda bi, i: (bi, i, 0)),
                  pl.BlockSpec((1, tm, LANES), lambda bi, i: (bi, i, 0)),
                  const((1, A_QLAT)), const((1, A_KVLAT)),
                  const(p["w_uq"].shape), const(p["w_ukv"].shape),
                  const((1, LANES)), const((1, LANES)), const((1, LANES)), const((1, LANES))],
        out_specs=(pl.BlockSpec((1, A_HEADS, tm, LANES), lambda bi, i: (bi, 0, i, 0)),
                   pl.BlockSpec((1, A_HEADS, tm, LANES), lambda bi, i: (bi, 0, i, 0)),
                   pl.BlockSpec((1, tm, A_HEADS * A_VD), lambda bi, i: (bi, i, 0))),
        compiler_params=_cp("parallel", "parallel"),
        name="mla_prep",
    )(u, cos32, sin32, p["q_lat_g"], p["kv_lat_g"], p["w_uq"], p["w_ukv"],
      p["qn_g"], p["qr_g"], p["kn_g"], p["kr_g"])


def _mla_attn_kernel(q_ref, k_ref, v_ref, o_ref, *, tk):
    tq = q_ref.shape[2]
    per_q = tq // tk
    i = pl.program_id(2)
    lane = _lane((tq, LANES))
    t = i * tq + lax.broadcasted_iota(I32, (1, tq), 1)
    nh = q_ref.shape[1]
    qs = [q_ref[0, e] for e in range(nh)]
    tile = lambda j: pl.ds(pl.multiple_of(j * tk, tk), tk)
    v_at = lambda j, e: v_ref[0, tile(j), (e // 2) * LANES:(e // 2 + 1) * LANES]

    def step(j, q0, bias_t, c):
        m, l, acc = c
        w = tq - q0
        s = jnp.concatenate([_nt(k_ref[0, e, tile(j), :], qs[e][q0:]) for e in range(nh)], axis=1)
        if bias_t is not None:
            s = s + jnp.concatenate([bias_t] * nh, axis=1)
        m_new = jnp.maximum(m, jnp.max(s, axis=0, keepdims=True))
        alpha = jnp.exp2(m - m_new)
        p = jnp.exp2(s - m_new)
        l = alpha * l + jnp.sum(p, axis=0, keepdims=True)
        pv = jnp.concatenate([_tn(v_at(j, 2 * pp), p[:, 2 * pp * w:(2 * pp + 2) * w].astype(BF16))
                              for pp in range(nh // 2)], axis=1)
        return m_new, l, alpha * acc + pv

    carry = lax.fori_loop(0, i * per_q, lambda j, c: step(j, 0, None, c), _flash_init_t(nh * tq))
    for d in range(per_q):
        j = i * per_q + d
        q0 = d * tk
        bias_t = jnp.where(j * tk + lax.broadcasted_iota(I32, (tk, 1), 0) <= t[:, q0:], 0.0, NEG)
        take = lambda x: jnp.concatenate([x[:, e * tq + q0:(e + 1) * tq] for e in range(nh)], axis=1)
        upd = step(j, q0, bias_t, tuple(take(x) for x in carry))
        if d:
            w = tq - q0
            upd = tuple(jnp.concatenate(
                [part for e in range(nh) for part in (old[:, e * tq:e * tq + q0], new[:, e * w:(e + 1) * w])], axis=1)
                for old, new in zip(carry, upd))
        carry = upd
    res = [(carry[2][:, e * tq:(e + 1) * tq] / carry[1][:, e * tq:(e + 1) * tq]).T for e in range(nh)]
    for p in range(nh // 2):
        o_ref[0, :, p * LANES:(p + 1) * LANES] = jnp.where(lane < A_VD, res[2 * p], res[2 * p + 1]).astype(BF16)


def _mla_attn(q, k, v):
    b, h, s, _ = q.shape
    tq = min(1024, s)
    tk = min(512, s)
    nh = 4
    return pl.pallas_call(
        functools.partial(_mla_attn_kernel, tk=tk),
        out_shape=jax.ShapeDtypeStruct((b, s, h * A_VD), BF16),
        grid=(b, h // nh, s // tq),
        in_specs=[pl.BlockSpec((1, nh, tq, LANES), lambda bi, p, i: (bi, p, i, 0)),
                  pl.BlockSpec((1, nh, s, LANES), lambda bi, p, i: (bi, p, 0, 0)),
                  pl.BlockSpec((1, s, nh * A_VD), lambda bi, p, i: (bi, 0, p))],
        out_specs=pl.BlockSpec((1, tq, nh * A_VD), lambda bi, p, i: (bi, i, p)),
        compiler_params=_cp("parallel", "parallel", "parallel"),
        name="mla_attn",
    )(q, k, v)


def _nsa_prep_kernel(q_in, ks_in, kw_in, cos_ref, sin_ref, qg_ref, ksg_ref, kwg_ref, q_ref, ks_ref, kw_ref):
    tm = q_in.shape[1]
    lane = _lane((tm, LANES))
    cosp, sinp = cos_ref[0], sin_ref[0]
    half = B_HD // 2
    rep = B_HEADS // B_KV_HEADS
    for p in range(B_HEADS // 2):
        y = _rope(_group_rms(q_in[0, :, p * LANES:(p + 1) * LANES].astype(F32), B_HD) * qg_ref[...], cosp, sinp, half)
        y = y * (B_HD ** -0.5 * LOG2E)
        y_sw = pltpu.roll(y, B_HD, 1)
        for e in range(2):
            h = 2 * p + e
            g = h // rep
            src = y if e == g else y_sw
            in_grp = (lane >= B_HD * g) & (lane < B_HD * (g + 1))
            q_ref[0, h] = jnp.where(in_grp, src, 0.0).astype(BF16)
    ks_ref[0] = _rope(_group_rms(ks_in[0].astype(F32), B_HD) * ksg_ref[...], cosp, sinp, half).astype(BF16)
    kw_ref[0] = _rope(_group_rms(kw_in[0].astype(F32), B_HD) * kwg_ref[...], cosp, sinp, half).astype(BF16)


def _nsa_prep(u, cols, cos64, sin64, p):
    b, s, _ = u.shape
    tm = min(512, s)
    blk = lambda name: pl.BlockSpec((1, tm, LANES), lambda bi, i, c=cols[name] // LANES: (bi, i, c))
    row = pl.BlockSpec((1, tm, LANES), lambda bi, i: (bi, i, 0))
    const = pl.BlockSpec((1, LANES), lambda bi, i: (0, 0))
    kvs = jax.ShapeDtypeStruct((b, s, LANES), BF16)
    return pl.pallas_call(
        _nsa_prep_kernel,
        out_shape=(jax.ShapeDtypeStruct((b, B_HEADS, s, LANES), BF16), kvs, kvs),
        grid=(b, s // tm),
        in_specs=[pl.BlockSpec((1, tm, 512), lambda bi, i, c=cols["b_q"] // 512: (bi, i, c)),
                  blk("b_ks"), blk("b_kw"), row, row, const, const, const],
        out_specs=(pl.BlockSpec((1, B_HEADS, tm, LANES), lambda bi, i: (bi, 0, i, 0)), row, row),
        compiler_params=_cp("parallel", "parallel"),
        name="nsa_prep",
    )(u, u, u, cos64, sin64, p["q_g"], p["ks_g"], p["kw_g"])


def _nsa_cmp_kernel(kc_in, vc_in, pek_ref, pev_ref, w1k_ref, w1v_ref, w2k_ref, w2v_ref, kg_ref,
                    cos_ref, sin_ref, ko_ref, vo_ref, pad_ref):
    s = kc_in.shape[1]
    n_pad = ko_ref.shape[1]

    def compress(x_in, pe_ref, w1_ref, w2_ref):
        pad_ref[0:s, :] = x_in[0].astype(F32)
        pad_ref[s:s + CMP_STRIDE, :] = jnp.zeros((CMP_STRIDE, LANES), F32)
        acc = jnp.zeros((n_pad, LANES), F32)
        for l in range(CMP_LEN):
            xl = pad_ref[pl.ds(l, n_pad, stride=CMP_STRIDE), :] + pe_ref[l:l + 1, :]
            acc = acc + _mm(xl.astype(BF16), w1_ref[l])
        mid = acc * _sigmoid(acc)
        return _mm(mid.astype(BF16), w2_ref[...])

    kc = compress(kc_in, pek_ref, w1k_ref, w2k_ref)
    ko_ref[0] = _rope(_group_rms(kc, B_HD) * kg_ref[...], cos_ref[0], sin_ref[0], B_HD // 2).astype(BF16)
    vo_ref[0] = compress(vc_in, pev_ref, w1v_ref, w2v_ref).astype(BF16)


def _nsa_cmp(u, cols, cosc, sinc, p):
    b, s, _ = u.shape
    n_pad = s // CMP_STRIDE
    blk = lambda name: pl.BlockSpec((1, s, LANES), lambda bi, c=cols[name] // LANES: (bi, 0, c))
    const = lambda shape: pl.BlockSpec(shape, lambda bi: (0,) * len(shape))
    out = jax.ShapeDtypeStruct((b, n_pad, LANES), BF16)
    ospec = pl.BlockSpec((1, n_pad, LANES), lambda bi: (bi, 0, 0))
    return pl.pallas_call(
        _nsa_cmp_kernel,
        out_shape=(out, out),
        grid=(b,),
        in_specs=[blk("b_kc"), blk("b_vc"), const((CMP_LEN, LANES)), const((CMP_LEN, LANES)),
                  const((CMP_LEN, LANES, LANES)), const((CMP_LEN, LANES, LANES)),
                  const((LANES, LANES)), const((LANES, LANES)), const((1, LANES)), ospec, ospec],
        out_specs=(ospec, ospec),
        scratch_shapes=[pltpu.VMEM((s + CMP_STRIDE, LANES), F32)],
        compiler_params=_cp("parallel"),
        name="nsa_cmp",
    )(u, u, p["pe_k"], p["pe_v"], p["w1k"], p["w1v"], p["w2k"], p["w2v"], p["kc_g"], cosc, sinc)


def _nsa_attn_kernel(q_ref, kc_ref, vc_ref, ks_ref, vs_ref, kw_ref, vw_ref, g_ref, e_ref, ovl_ref, o_ref,
                     *, n_blk, n_sel, tks, ww):
    tq = q_ref.shape[2]
    n_pad = kc_ref.shape[1]
    i = pl.program_id(1)
    t = i * tq + lax.broadcasted_iota(I32, (1, tq), 1)
    lane = _lane((tq, LANES))
    gates_t = _sigmoid(g_ref[0]).T
    rep = B_HEADS // B_KV_HEADS
    heads = lambda x: jnp.concatenate([x] * rep, axis=1)
    ncmp = lax.broadcasted_iota(I32, (n_pad, 1), 0)
    valid_c = heads((ncmp * CMP_STRIDE + (CMP_LEN - 1)) <= t)
    kc = kc_ref[0]
    vc = vc_ref[0]
    blk = lax.broadcasted_iota(I32, (LANES, 1), 0)
    cur = t >> (SEL_LEN.bit_length() - 1)
    forced = jnp.where(blk == cur, 3e4, jnp.where(blk == cur - 1, 2e4, jnp.where(blk == 0, 1e4, 0.0)))
    adm = (blk * SEL_LEN <= t) & (blk < n_blk)
    n_rv = n_blk // 8
    sub = lax.broadcasted_iota(I32, (8, tq), 0)
    w0 = pl.multiple_of(jnp.maximum(i * tq + tq - ww, 0), tq)
    kp_w = w0 + lax.broadcasted_iota(I32, (ww, 1), 0)
    win_bias = heads(jnp.where((kp_w <= t) & (kp_w > t - WINDOW), 0.0, NEG))
    qs = jnp.concatenate([q_ref[0, h] for h in range(B_HEADS)], axis=0)
    gate = lambda c: jnp.concatenate([gates_t[h * 3 + c:h * 3 + c + 1, :] for h in range(B_HEADS)], axis=1)
    groups = lambda xs: jnp.concatenate([heads(x) for x in xs], axis=1)
    valid_all = groups([valid_c[:, 0:tq]] * B_KV_HEADS)
    s = jnp.where(valid_all, _nt(kc, qs), NEG)
    ex = jnp.exp2(s - jnp.max(s, axis=0, keepdims=True))
    pc = jnp.where(valid_all, ex / jnp.sum(ex, axis=0, keepdims=True), 0.0)
    out = gate(0) * _tn(vc, pc.astype(BF16))
    sel_ts = []
    for g in range(B_KV_HEADS):
        psum = pc[:, rep * g * tq:(rep * g + 1) * tq]
        for r in range(1, rep):
            psum = psum + pc[:, (rep * g + r) * tq:(rep * g + r + 1) * tq]
        p_hi = psum.astype(BF16)
        p_lo = (psum - p_hi.astype(F32)).astype(BF16)
        imp = _mm(ovl_ref[...], p_hi) + _mm(ovl_ref[...], p_lo)
        score_t = jnp.where(adm, imp + forced, NEG)
        sc = [score_t[8 * v:8 * v + 8] for v in range(n_rv)]
        rank = [jnp.zeros((8, tq), F32) for _ in range(n_rv)]
        for jp in range(n_blk):
            col = score_t[jp:jp + 1]
            for v in range(n_rv):
                if v > jp // 8:
                    beats = col >= sc[v]
                elif v < jp // 8:
                    beats = col > sc[v]
                else:
                    beats = (col > sc[v]) | ((col == sc[v]) & (sub > jp % 8))
                rank[v] = rank[v] + jnp.where(beats, 1.0, 0.0)
        sel_ts.append(jnp.where(jnp.concatenate(rank, axis=0) < n_sel, 1.0, 0.0).astype(BF16))

    tile = lambda j: pl.ds(pl.multiple_of(j * tks, tks), tks)

    def sel_body(j, carry):
        kp = j * tks + lax.broadcasted_iota(I32, (tks, 1), 0)
        bias = [jnp.where((_mm(e_ref[tile(j), :], sel_t) > 0.5) & (kp <= t), 0.0, NEG) for sel_t in sel_ts]
        return _softmax_step_t(qs, ks_ref[0, tile(j), :], vs_ref[0, tile(j), :], groups(bias), carry)

    n_tile = ((i + 1) * tq + tks - 1) // tks
    _, l_s, acc_s = lax.fori_loop(0, n_tile, sel_body, _flash_init_t(B_HEADS * tq))
    out = out + gate(1) * (acc_s / l_s)
    s_w = _nt(kw_ref[0, pl.ds(w0, ww), :], qs) + groups([win_bias[:, 0:tq]] * B_KV_HEADS)
    p_w = jnp.exp2(s_w - jnp.max(s_w, axis=0, keepdims=True))
    o_w = _tn(vw_ref[0, pl.ds(w0, ww), :], p_w.astype(BF16)) / jnp.sum(p_w, axis=0, keepdims=True)
    out = out + gate(2) * o_w
    for g in range(B_KV_HEADS):
        outs = [out[:, (rep * g + r) * tq:(rep * g + r + 1) * tq].T for r in range(rep)]
        for pp in range(rep // 2):
            a, bb = outs[2 * pp], outs[2 * pp + 1]
            if g == 0:
                bb = pltpu.roll(bb, B_HD, 1)
            else:
                a = pltpu.roll(a, B_HD, 1)
            c0 = (rep // 2 * g + pp) * LANES
            o_ref[0, :, c0:c0 + LANES] = jnp.where(lane < B_HD, a, bb).astype(BF16)


def _nsa_attn(u, u32, cols, qx, kcmp, vcmp, ks, kw):
    b, h, s, _ = qx.shape
    tq = min(256, s)
    tks = min(512, s)
    n_pad = kcmp.shape[1]
    n_blk = s // SEL_LEN
    n_sel = min(N_SEL, n_blk)
    n_cmp = (s - CMP_LEN) // CMP_STRIDE + 1
    ww = min(WINDOW + tq, s)
    expand = np.zeros((s, n_blk), np.float32)
    expand[np.arange(s), np.arange(s) // SEL_LEN] = 1.0
    nn = np.arange(n_pad)[None, :]
    jj = np.arange(LANES)[:, None]
    ovl = ((nn * CMP_STRIDE <= jj * SEL_LEN + SEL_LEN - 1) & (nn * CMP_STRIDE + CMP_LEN - 1 >= jj * SEL_LEN)
           & (jj < n_blk) & (nn < n_cmp)).astype(np.float32)
    full = lambda shape: pl.BlockSpec(shape, lambda bi, i: (bi,) + (0,) * (len(shape) - 1))
    ucol = lambda name: pl.BlockSpec((1, s, LANES), lambda bi, i, c=cols[name] // LANES: (bi, 0, c))
    const = lambda shape: pl.BlockSpec(shape, lambda bi, i: (0,) * len(shape))
    return pl.pallas_call(
        functools.partial(_nsa_attn_kernel, n_blk=n_blk, n_sel=n_sel, tks=tks, ww=ww),
        out_shape=jax.ShapeDtypeStruct((b, s, h * B_HD), BF16),
        grid=(b, s // tq),
        in_specs=[pl.BlockSpec((1, h, tq, LANES), lambda bi, i: (bi, 0, i, 0)),
                  full((1, n_pad, LANES)), full((1, n_pad, LANES)),
                  full((1, s, LANES)), ucol("b_vs"), full((1, s, LANES)), ucol("b_vw"),
                  pl.BlockSpec((1, tq, LANES), lambda bi, i: (bi, i, 0)),
                  const((s, n_blk)), const((LANES, n_pad))],
        out_specs=pl.BlockSpec((1, tq, h * B_HD), lambda bi, i: (bi, i, 0)),
        compiler_params=_cp("parallel", "parallel"),
        name="nsa_attn",
    )(qx, kcmp, vcmp, ks, u, kw, u, u32, jnp.asarray(expand, BF16), jnp.asarray(ovl, BF16))


def _dsa_prep_kernel(q_in, iq_in, kv_in, sm_in, cos64_ref, sin64_ref, cos32_ref, sin32_ref, qg_ref, kg_ref,
                     q_ref, kv_ref, iq_ref, ik_ref):
    tm = q_in.shape[1]
    lane = _lane((tm, LANES))
    c64, s64, c32, s32 = cos64_ref[0], sin64_ref[0], cos32_ref[0], sin32_ref[0]
    for p in range(C_HEADS // 2):
        y = _rope(_group_rms(q_in[0, :, p * LANES:(p + 1) * LANES].astype(F32), C_HD) * qg_ref[...], c64, s64, C_HD // 2)
        y = y * (C_HD ** -0.5 * LOG2E)
        q_ref[0, 2 * p] = jnp.where(lane < C_HD, y, 0.0).astype(BF16)
        q_ref[0, 2 * p + 1] = jnp.where(lane < C_HD, pltpu.roll(y, C_HD, 1), 0.0).astype(BF16)
    kv = kv_in[0].astype(F32)
    kn = _rope(_group_rms(kv, C_HD) * kg_ref[...], c64, s64, C_HD // 2)
    kv_ref[0] = jnp.where(lane < C_HD, kn, kv).astype(BF16)
    for c0 in range(0, IDX_HEADS * IDX_HD, LANES):
        iq_ref[0, :, c0:c0 + LANES] = _rope(iq_in[0, :, c0:c0 + LANES].astype(F32), c32, s32, IDX_HD // 2).astype(BF16)
    ik = jnp.where(lane < IDX_HD, _rope(sm_in[0], c32, s32, IDX_HD // 2), 0.0)
    ik = ik + pltpu.roll(ik, IDX_HD, 1)
    ik = ik + pltpu.roll(ik, 2 * IDX_HD, 1)
    ik_ref[0] = ik.astype(BF16)


def _dsa_prep(u, u32, cols, tabs, p):
    b, s, _ = u.shape
    tm = min(512, s)
    row = pl.BlockSpec((1, tm, LANES), lambda bi, i: (bi, i, 0))
    const = pl.BlockSpec((1, LANES), lambda bi, i: (0, 0))
    ublk = lambda name, w: pl.BlockSpec((1, tm, w), lambda bi, i, c=cols[name] // w: (bi, i, c))
    dense = jax.ShapeDtypeStruct((b, s, LANES), BF16)
    return pl.pallas_call(
        _dsa_prep_kernel,
        out_shape=(jax.ShapeDtypeStruct((b, C_HEADS, s, LANES), BF16), dense,
                   jax.ShapeDtypeStruct((b, s, IDX_HEADS * IDX_HD), BF16), dense),
        grid=(b, s // tm),
        in_specs=[ublk("c_q", 512), ublk("c_iq", 256), ublk("c_kv", LANES), row,
                  row, row, row, row, const, const],
        out_specs=(pl.BlockSpec((1, C_HEADS, tm, LANES), lambda bi, i: (bi, 0, i, 0)), row,
                   pl.BlockSpec((1, tm, IDX_HEADS * IDX_HD), lambda bi, i: (bi, i, 0)), row),
        compiler_params=_cp("parallel", "parallel"),
        name="dsa_prep",
    )(u, u, u, u32, tabs["cos64"], tabs["sin64"], tabs["cos32"], tabs["sin32"], p["q_g"], p["k_g"])


def _dsa_attn_kernel(q_ref, iq_ref, sm_ref, ik_ref, kv_ref, tri_ref, o_ref, key_scr, bias_scr, *, tk, topk):
    tq = q_ref.shape[2]
    i = pl.program_id(1)
    t = i * tq + lax.broadcasted_iota(I32, (1, tq), 1)
    lane = _lane((tq, LANES))
    n_tile = ((i + 1) * tq + tk - 1) // tk
    tile = lambda j: pl.ds(pl.multiple_of(j * tk, tk), tk)
    kpos = lambda j: j * tk + lax.broadcasted_iota(I32, (tk, 1), 0)

    iw_t = (sm_ref[0] * (IDX_HEADS ** -0.5)).T
    parts = []
    for h in range(IDX_HEADS):
        blk = iq_ref[0, :, (h // 4) * LANES:(h // 4 + 1) * LANES]
        lo = (h % 4) * IDX_HD
        parts.append(jnp.where((lane >= lo) & (lane < lo + IDX_HD), blk, jnp.zeros_like(blk)))
    iqs = jnp.concatenate(parts, axis=0)

    def idx_body(j, carry):
        lg = _nt(ik_ref[0, tile(j), :], iqs)
        acc = jnp.zeros((tk, tq), F32)
        for h in range(IDX_HEADS):
            acc = acc + iw_t[IDX_HD + h:IDX_HD + h + 1, :] * jnp.maximum(lg[:, h * tq:(h + 1) * tq], 0.0)
        sc = jnp.where(kpos(j) <= t, acc, NEG)
        bits = pltpu.bitcast(sc, I32)
        key = jnp.where(bits < 0, bits ^ 0x7FFFFFFF, bits)
        key_scr[tile(j), :] = jnp.where(sc == 0.0, 0, key)
        return carry

    lax.fori_loop(0, n_tile, idx_body, 0)

    tr = min(tk, 2 * tq)
    n_sub = ((i + 1) * tq + tr - 1) // tr

    def count_ge(thr_key):
        def body(j, cs):
            key = key_scr[pl.ds(pl.multiple_of(j * tr, tr), tr), :]
            cs = list(cs)
            for n, r0 in enumerate(range(0, tr, 8)):
                c = cs[n % len(cs)]
                cs[n % len(cs)] = jnp.where(key[r0:r0 + 8] >= thr_key, c + 1, c)
            return tuple(cs)

        cs = lax.fori_loop(0, n_sub, body, tuple(jnp.zeros((8, tq), I32) for _ in range(4)))
        return jnp.sum(cs[0] + cs[1] + cs[2] + cs[3], axis=0, keepdims=True)

    def bit_body(bi, ucand):
        utrial = ucand | jnp.left_shift(jnp.int32(1), 31 - bi)
        return jnp.where(count_ge(utrial ^ INT_MIN) >= topk, utrial, ucand)

    thr = lax.fori_loop(0, 32, bit_body, jnp.zeros((1, tq), I32)) ^ INT_MIN
    n_ge = count_ge(thr)
    n_gt = count_ge(thr + 1)
    need = topk - n_gt
    row_ok = (n_ge - n_gt == need) | (thr == NEG_KEY) | (n_ge < topk)
    simple = jnp.min(jnp.where(row_ok, 1.0, 0.0)) > 0.5

    def fast_bias():
        def body(j, carry):
            keep = (key_scr[tile(j), :] >= thr) & (kpos(j) <= t)
            bias_scr[tile(j), :] = jnp.where(keep, 0.0, NEG)
            return carry

        lax.fori_loop(0, n_tile, body, 0)

    def tie_bias():
        need_f = need.astype(F32)

        def body(j, run):
            key = key_scr[tile(j), :]
            kp = kpos(j)
            for c0 in range(0, tk, LANES):
                kc = key[c0:c0 + LANES]
                eq = kc == thr
                eq_f = jnp.where(eq, 1.0, 0.0)
                pref = _mm(tri_ref[...], eq_f.astype(BF16)) + run
                keep = ((kc > thr) | (eq & (pref <= need_f))) & (kp[c0:c0 + LANES] <= t)
                bias_scr[pl.ds(pl.multiple_of(j * tk + c0, LANES), LANES), :] = jnp.where(keep, 0.0, NEG)
                run = run + jnp.sum(eq_f, axis=0, keepdims=True)
            return run

        lax.fori_loop(0, n_tile, body, jnp.zeros((1, tq), F32))

    lax.cond(simple, fast_bias, tie_bias)

    qs = jnp.concatenate([q_ref[0, h] for h in range(C_HEADS)], axis=0)

    def att_body(j, carry):
        kv = kv_ref[0, tile(j), :]
        bias_t = jnp.concatenate([bias_scr[tile(j), :]] * C_HEADS, axis=1)
        return _softmax_step_t(qs, kv, kv, bias_t, carry)

    _, l, acc = lax.fori_loop(0, n_tile, att_body, _flash_init_t(C_HEADS * tq))
    o_t = acc / l
    for p in range(C_HEADS // 2):
        a = pltpu.roll(o_t[:, (2 * p) * tq:(2 * p + 1) * tq].T, C_HD, 1)
        bb = o_t[:, (2 * p + 1) * tq:(2 * p + 2) * tq].T
        o_ref[0, :, p * LANES:(p + 1) * LANES] = jnp.where(lane < C_HD, a, bb).astype(BF16)


def _dsa_attn(u32, qc, iq, ik, kv):
    b, h, s, _ = qc.shape
    tq = min(256, s)
    tk = min(1024, s)
    topk = min(TOPK_MAX, s // 4)
    tri = np.tril(np.ones((LANES, LANES), np.float32))
    full = lambda shape: pl.BlockSpec(shape, lambda bi, i: (bi,) + (0,) * (len(shape) - 1))
    return pl.pallas_call(
        functools.partial(_dsa_attn_kernel, tk=tk, topk=topk),
        out_shape=jax.ShapeDtypeStruct((b, s, h * C_HD), BF16),
        grid=(b, s // tq),
        in_specs=[pl.BlockSpec((1, h, tq, LANES), lambda bi, i: (bi, 0, i, 0)),
                  pl.BlockSpec((1, tq, IDX_HEADS * IDX_HD), lambda bi, i: (bi, i, 0)),
                  pl.BlockSpec((1, tq, LANES), lambda bi, i: (bi, i, 0)),
                  full((1, s, LANES)), full((1, s, LANES)),
                  pl.BlockSpec((LANES, LANES), lambda bi, i: (0, 0))],
        out_specs=pl.BlockSpec((1, tq, h * C_HD), lambda bi, i: (bi, i, 0)),
        scratch_shapes=[pltpu.VMEM((s, tq), I32), pltpu.VMEM((s, tq), F32)],
        compiler_params=_cp("parallel", "parallel"),
        name="dsa_attn",
    )(qc, iq, u32, ik, kv, jnp.asarray(tri, BF16))


def _mlstm_conv_kernel(x_ref, halo_ref, w_ref, b_ref, o_ref):
    tm = x_ref.shape[1]
    i = pl.program_id(1)
    lane = _lane((tm, x_ref.shape[2]))
    halo = jnp.where(i > 0, halo_ref[0].astype(F32), 0.0)
    x = x_ref[0].astype(F32)
    xc = jnp.concatenate([halo, x], axis=0)
    off = halo.shape[0] - (CONV_W - 1)
    y = b_ref[...] + jnp.zeros_like(x)
    for j in range(CONV_W):
        y = y + w_ref[j:j + 1, :] * xc[off + j:off + j + tm, :]
    y = y * _sigmoid(y)
    o_ref[0] = jnp.where(lane >= D_HEADS * D_QK, y * (D_QK ** -0.5), y)


def _mlstm_conv(u, qk_block, w, bias):
    b, s, _ = u.shape
    c = w.shape[1]
    tm = min(512, s)
    hb = 16
    return pl.pallas_call(
        _mlstm_conv_kernel,
        out_shape=jax.ShapeDtypeStruct((b, s, c), F32),
        grid=(b, s // tm),
        in_specs=[pl.BlockSpec((1, tm, c), lambda bi, i: (bi, i, qk_block)),
                  pl.BlockSpec((1, hb, c), lambda bi, i: (bi, jnp.maximum(i * (tm // hb) - 1, 0), qk_block)),
                  pl.BlockSpec((CONV_W, c), lambda bi, i: (0, 0)),
                  pl.BlockSpec((1, c), lambda bi, i: (0, 0))],
        out_specs=pl.BlockSpec((1, tm, c), lambda bi, i: (bi, i, 0)),
        compiler_params=_cp("parallel", "parallel"),
        name="mlstm_conv",
    )(u, u, w, bias)


def _mlstm_scan_kernel(qk_ref, v_ref, op_ref, sm_ref, gt_ref, bcol_ref, brow_ref, hg_ref, o_ref,
                       c_scr, m_scr):
    tc = qk_ref.shape[1]
    L = CHUNK
    nqk = D_HEADS * D_QK

    @pl.when(pl.program_id(1) == 0)
    def _():
        c_scr[...] = jnp.zeros_like(c_scr)
        m_scr[...] = jnp.zeros_like(m_scr)

    r = lax.broadcasted_iota(I32, (L, L), 0)
    c = lax.broadcasted_iota(I32, (L, L), 1)
    tril = c <= r
    tril_f = jnp.where(tril, 1.0, 0.0).astype(BF16)
    triu_f = jnp.where(r <= c, 1.0, 0.0).astype(BF16)
    lane = _lane((L, LANES))
    row128 = lax.broadcasted_iota(I32, (LANES, 2 * LANES), 0)
    ones = jnp.ones((L, LANES), F32)
    i_lane, f_lane = IDX_HD + IDX_HEADS, IDX_HD + IDX_HEADS + D_HEADS

    def one(bb, ci):
        rows = pl.ds(pl.multiple_of(ci * L, L), L)
        qk = qk_ref[bb, rows, :]
        sm = sm_ref[bb, rows, :] + bcol_ref[...]
        bcol_all = sum(_mm(tril_f, part) for part in _split3(_log_sigmoid(sm)))
        gt = gt_ref[bb, ci] + brow_ref[...]
        brow_all = sum(_mm(part, triu_f) for part in _split3(_log_sigmoid(gt)))
        for p in range(D_HEADS // 2):
            qpair = qk[:, p * LANES:(p + 1) * LANES]
            kpair = qk[:, nqk + p * LANES:nqk + (p + 1) * LANES]
            kpair_b = kpair.astype(BF16)
            sp = bb * (D_HEADS // 2) + p
            c_prev = c_scr[sp]
            c_prev_b = c_prev.astype(BF16)
            upd = []
            for e in range(2):
                h = 2 * p + e
                in_half = (lane >= D_QK * e) & (lane < D_QK * (e + 1))
                qm = jnp.where(in_half, qpair, 0.0)
                qm_b = qm.astype(BF16)
                vh = v_ref[bb, rows, h * D_VD:(h + 1) * D_VD].astype(F32)
                b_col = jnp.broadcast_to(bcol_all[:, f_lane + h:f_lane + h + 1], (L, LANES))
                i_col = jnp.broadcast_to(sm[:, i_lane + h:i_lane + h + 1], (L, LANES))
                b_row = brow_all[D_HEADS + h:D_HEADS + h + 1, :]
                i_row = gt[h:h + 1, :]
                m_prev = m_scr[bb * D_HEADS + h, 0:1, :]
                dmat = jnp.where(tril, b_col[:, :L] - b_row + i_row, NEG)
                inter = b_col + m_prev
                m_t = jnp.maximum(inter, jnp.max(dmat, axis=-1, keepdims=True))
                a = jnp.exp(inter - m_t)
                w = _nt(qm_b, kpair_b) * jnp.exp(dmat - m_t[:, :L])
                qc = _mm(qm_b, c_prev_b)
                wv = _mm(w.astype(BF16), jnp.concatenate([vh, ones], axis=1).astype(BF16))
                num = a * qc[:, :LANES] + wv[:, :LANES]
                den = a * qc[:, LANES:] + wv[:, LANES:]
                hout = num / jnp.maximum(jnp.abs(den), jnp.exp(-m_t))
                b_last = b_col[L - 1:L, :]
                g_col = b_last - b_col + i_col
                m_new = jnp.maximum(b_last + m_prev, jnp.max(g_col, axis=0, keepdims=True))
                ws = jnp.exp(g_col - m_new)
                decay = jnp.exp(b_last + m_prev - m_new)
                u_mat = _tn(kpair_b, jnp.concatenate([ws * vh, ws], axis=1).astype(BF16))
                upd.append(jnp.concatenate([decay, decay], axis=1) * c_prev + u_mat)
                m_scr[bb * D_HEADS + h] = jnp.broadcast_to(m_new, m_scr.shape[1:])
                hn = _row_rms(hout) * hg_ref[...]
                y = _sigmoid(op_ref[bb, rows, h * D_VD:(h + 1) * D_VD].astype(F32)) * hn
                o_ref[bb, rows, h * D_VD:(h + 1) * D_VD] = y.astype(BF16)
            c_scr[sp] = jnp.where(row128 < D_QK, upd[0], upd[1])

    def chunk(ci, carry):
        for bb in range(qk_ref.shape[0]):
            one(bb, ci)
        return carry

    lax.fori_loop(0, tc // L, chunk, 0)


def _mlstm_scan(u, u32, cols, qk, gt, p):
    b, s, _ = u.shape
    tc = min(256, s)
    nb = next(n for n in (8, 4, 2, 1) if b % n == 0)
    nv = D_HEADS * D_VD
    ublk = lambda name, w: pl.BlockSpec((nb, tc, w), lambda bi, i, c=cols[name] // w: (bi, i, c))
    const = lambda shape: pl.BlockSpec(shape, lambda bi, i: (0,) * len(shape))
    return pl.pallas_call(
        _mlstm_scan_kernel,
        out_shape=jax.ShapeDtypeStruct((b, s, nv), BF16),
        grid=(b // nb, s // tc),
        in_specs=[pl.BlockSpec((nb, tc, qk.shape[-1]), lambda bi, i: (bi, i, 0)),
                  ublk("d_v", nv), ublk("d_o", nv), pl.BlockSpec((nb, tc, LANES), lambda bi, i: (bi, i, 0)),
                  pl.BlockSpec((nb, tc // CHUNK, 8, CHUNK), lambda bi, i: (bi, i, 0, 0)),
                  const((1, LANES)), const((8, 1)), const((1, D_VD))],
        out_specs=pl.BlockSpec((nb, tc, nv), lambda bi, i: (bi, i, 0)),
        scratch_shapes=[pltpu.VMEM((nb * D_HEADS // 2, LANES, 2 * LANES), F32),
                        pltpu.VMEM((nb * D_HEADS, 8, LANES), F32)],
        compiler_params=_cp("parallel", "arbitrary"),
        name="mlstm_scan",
    )(qk, u, u, u32, gt, p["bias_col"], p["bias_row"], p["h_g"])


EVEN_SRC = dict(a_ql=(0, 256), a_kvl=(256, 128), a_kr=(384, 32), a_gate=(416, 512), b_q=(928, 512),
                b_kc=(1440, 128), b_vc=(1568, 128), b_ks=(1696, 128), b_vs=(1824, 128), b_kw=(1952, 128),
                b_vw=(2080, 128), b_g=(2208, 24), b_gate=(2232, 512), m_q=(2744, 256), m_gate=(3000, 256))
EVEN_DST = dict(a_ql=0, a_kvl=256, a_kr=384, b_q=512, a_gate=1024, b_gate=1536, m_q=2048, m_gate=2304,
                b_g=2560, b_kc=2688, b_vc=2816, b_ks=2944, b_vs=3072, b_kw=3200, b_vw=3328)
EVEN_COLS_PAD = 3456

ODD_SRC = dict(c_q=(0, 512), c_k=(512, 64), c_v=(576, 64), c_iq=(640, 256), c_ik=(896, 32), c_iw=(928, 8),
               c_gate=(936, 512), d_q=(1448, 256), d_k=(1704, 256), d_v=(1960, 512), d_i=(2472, 4),
               d_f=(2476, 4), d_o=(2480, 512), d_gate=(2992, 512), m_q=(3504, 256), m_gate=(3760, 256))
ODD_DST = dict(c_q=0, c_gate=512, d_gate=1024, d_v=1536, d_o=2048, d_q=2560, d_k=2816, m_q=3072, m_gate=3328,
               c_iq=3584, c_k=3840, c_v=3904, c_ik=3968, c_iw=4000, d_i=4008, d_f=4012)
ODD_COLS_PAD = 4096


def _permute_cols(w, src, dst, total):
    pieces, pos = [], 0
    for name in sorted(src, key=lambda n: dst[n]):
        start, width = src[name]
        if dst[name] > pos:
            pieces.append(jnp.zeros((w.shape[0], dst[name] - pos), BF16))
        pieces.append(w[:, start:start + width].astype(BF16))
        pos = dst[name] + width
    if total > pos:
        pieces.append(jnp.zeros((w.shape[0], total - pos), BF16))
    return jnp.concatenate(pieces, axis=1)


def _heads_split(w, heads, first):
    w3 = w.reshape(w.shape[0], heads, -1)
    return jnp.concatenate([w3[:, :, :first].reshape(w.shape[0], -1),
                            w3[:, :, first:].reshape(w.shape[0], -1)], axis=1).astype(BF16)


def _tile_lanes(v, reps):
    return jnp.tile(v.astype(F32).reshape(1, -1), (1, reps))


def _rope_tables(positions, d2):
    inv = ROPE_THETA ** (-jnp.arange(d2, dtype=F32) / d2)
    ang = positions.astype(F32)[..., None] * inv
    c, s = jnp.cos(ang), jnp.sin(ang)
    reps = LANES // (2 * d2)
    return (jnp.tile(jnp.concatenate([c, c], axis=-1), (1, 1, reps)),
            jnp.tile(jnp.concatenate([-s, s], axis=-1), (1, 1, reps)))


def _block_diag2(w):
    z = jnp.zeros_like(w)
    return jnp.concatenate([jnp.concatenate([w, z], axis=-1), jnp.concatenate([z, w], axis=-1)], axis=-2)


def kernel(x, mem, positions, ln_g, mem_norm_g, mem_w_kv, mem_q_norm_g, mem_k_norm_g, w_out, even_w_in, mla_q_lat_g, mla_kv_lat_g, mla_w_uq, mla_w_ukv, mla_q_norm_g, mla_k_norm_g, nsa_q_norm_g, nsa_k_norm_g, nsa_cmp_pos, nsa_cmp_w1, nsa_cmp_w2, odd_w_in, dsa_q_norm_g, dsa_k_norm_g, mlstm_conv_w, mlstm_conv_b, mlstm_i_bias, mlstm_f_bias, mlstm_h_norm_g):
    b, s, _ = x.shape
    depth = ln_g.shape[0]
    cos64, sin64 = _rope_tables(positions, 32)
    cos32, sin32 = _rope_tables(positions, 16)
    tabs = dict(cos64=cos64, sin64=sin64, cos32=cos32, sin32=sin32)
    n_pad = s // CMP_STRIDE
    cmp_pos = jnp.pad(positions[:, CMP_LEN - 1::CMP_STRIDE], ((0, 0), (0, 0)))[:, :n_pad]
    cmp_pos = jnp.pad(cmp_pos, ((0, 0), (0, n_pad - cmp_pos.shape[1])))
    cosc, sinc = _rope_tables(cmp_pos, 32)

    mem_k, mem_v = _mem_kv(mem, mem_norm_g.reshape(depth, 1, -1), mem_w_kv.astype(BF16),
                           jnp.tile(mem_k_norm_g, (1, 2)).reshape(depth, 1, LANES))

    for layer in range(depth):
        li = layer // 2
        g_ln = ln_g[layer].reshape(1, -1)
        mq_g = _tile_lanes(mem_q_norm_g[layer], 2)
        if layer % 2 == 0:
            cols = EVEN_DST
            u, u32 = _in_proj(x, g_ln, _permute_cols(even_w_in[li], EVEN_SRC, EVEN_DST, EVEN_COLS_PAD), cols["b_g"])
            pa = dict(q_lat_g=mla_q_lat_g[li].reshape(1, -1), kv_lat_g=mla_kv_lat_g[li].reshape(1, -1),
                      w_uq=_heads_split(mla_w_uq[li], A_HEADS, A_NOPE), w_ukv=_heads_split(mla_w_ukv[li], A_HEADS, A_NOPE),
                      qn_g=_tile_lanes(mla_q_norm_g[li, :A_NOPE], 2), qr_g=_tile_lanes(mla_q_norm_g[li, A_NOPE:], 4),
                      kn_g=_tile_lanes(mla_k_norm_g[li, :A_NOPE], 2), kr_g=_tile_lanes(mla_k_norm_g[li, A_NOPE:], 4))
            qa, ka, va = _mla_prep(u, cos32, sin32, pa)
            y1 = _mla_attn(qa, ka, va)
            pb = dict(q_g=_tile_lanes(nsa_q_norm_g[li], 2), ks_g=_tile_lanes(nsa_k_norm_g[li, 1], 2),
                      kw_g=_tile_lanes(nsa_k_norm_g[li, 2], 2), kc_g=_tile_lanes(nsa_k_norm_g[li, 0], 2),
                      pe_k=jnp.tile(nsa_cmp_pos[li, 0], (1, 2)), pe_v=jnp.tile(nsa_cmp_pos[li, 1], (1, 2)),
                      w1k=_block_diag2(nsa_cmp_w1[li, 0].reshape(CMP_LEN, B_HD, B_HD)).astype(BF16),
                      w1v=_block_diag2(nsa_cmp_w1[li, 1].reshape(CMP_LEN, B_HD, B_HD)).astype(BF16),
                      w2k=_block_diag2(nsa_cmp_w2[li, 0]).astype(BF16),
                      w2v=_block_diag2(nsa_cmp_w2[li, 1]).astype(BF16))
            qb, ks, kw = _nsa_prep(u, cols, cos64, sin64, pb)
            kcmp, vcmp = _nsa_cmp(u, cols, cosc, sinc, pb)
            y2 = _nsa_attn(u, u32, cols, qb, kcmp, vcmp, ks, kw)
            gate_blocks = (cols["a_gate"] // 512, cols["b_gate"] // 512, cols["m_gate"] // 256)
        else:
            cols = dict(ODD_DST, c_kv=ODD_DST["c_k"], small=ODD_DST["c_ik"])
            u, u32 = _in_proj(x, g_ln, _permute_cols(odd_w_in[li], ODD_SRC, ODD_DST, ODD_COLS_PAD), cols["small"])
            pc = dict(q_g=_tile_lanes(dsa_q_norm_g[li], 2), k_g=_tile_lanes(dsa_k_norm_g[li], 2))
            qc, kvc, iq, ik = _dsa_prep(u, u32, cols, tabs, pc)
            y1 = _dsa_attn(u32, qc, iq, ik, kvc)
            qk = _mlstm_conv(u, cols["d_q"] // 512, mlstm_conv_w[li], mlstm_conv_b[li].reshape(1, -1))
            gates = u32[:, :, cols["d_i"] - cols["small"]:cols["d_i"] - cols["small"] + 2 * D_HEADS]
            gt = gates.reshape(b, s // CHUNK, CHUNK, 2 * D_HEADS).transpose(0, 1, 3, 2)
            bias8 = jnp.concatenate([mlstm_i_bias[li], mlstm_f_bias[li]]).astype(F32)
            bias_col = jnp.zeros((1, LANES), F32).at[0, cols["d_i"] - cols["small"]:cols["d_i"] - cols["small"] + 8].set(bias8)
            pd = dict(bias_col=bias_col, bias_row=bias8.reshape(8, 1), h_g=mlstm_h_norm_g[li].reshape(1, -1))
            y2 = _mlstm_scan(u, u32, cols, qk, gt, pd)
            gate_blocks = (cols["c_gate"] // 512, cols["d_gate"] // 512, cols["m_gate"] // 256)
        ym = _mem_attn(u, cols["m_q"] // 256, mem_k, mem_v, layer, mq_g)
        x = _out_proj(x, y1, y2, ym, u, gate_blocks, w_out[layer].astype(BF16))
    return x
```

```python
import functools

import numpy as np
import jax
import jax.numpy as jnp
from jax import lax
from jax.experimental import pallas as pl
from jax.experimental.pallas import tpu as pltpu

F32, BF16, I32 = jnp.float32, jnp.bfloat16, jnp.int32
NEG = -1e30
EPS = 1e-6
ROPE_THETA = 10000.0
LANES = 128
VMEM_LIMIT_BYTES = 48 * 1024 * 1024

D_MODEL = 1024
DEPTH = 4
A_HEADS, A_NOPE, A_ROPE, A_VD, A_QLAT, A_KVLAT = 8, 64, 32, 64, 256, 128
B_HEADS, B_KV_HEADS, B_HD = 8, 2, 64
CMP_LEN, CMP_STRIDE, SEL_LEN, N_SEL, WINDOW = 32, 16, 64, 16, 512
C_HEADS, C_HD, IDX_HEADS, IDX_HD, TOPK_MAX = 8, 64, 8, 32, 256
D_HEADS, D_QK, D_VD, CONV_W, CHUNK = 4, 64, 128, 4, 64
M_HEADS, M_HD = 4, 64

INT_MIN = np.int32(-2 ** 31)
NEG_KEY = int(np.float32(NEG).view(np.int32) ^ np.int32(0x7FFFFFFF))
LOG2E = float(np.log2(np.e))


def _cp(*sem):
    return pltpu.CompilerParams(dimension_semantics=sem, vmem_limit_bytes=VMEM_LIMIT_BYTES)


def _nt(a, b):
    return lax.dot_general(a, b, (((1,), (1,)), ((), ())), preferred_element_type=F32)


def _tn(a, b):
    return lax.dot_general(a, b, (((0,), (0,)), ((), ())), preferred_element_type=F32)


def _mm(a, b):
    return jnp.dot(a, b, preferred_element_type=F32)


def _sigmoid(x):
    return 1.0 / (1.0 + jnp.exp(-x))


def _log_sigmoid(x):
    return jnp.minimum(x, 0.0) - jnp.log1p(jnp.exp(-jnp.abs(x)))


def _lane(shape):
    return lax.broadcasted_iota(I32, shape, len(shape) - 1)


def _group_mat(gs):
    r = lax.broadcasted_iota(I32, (LANES, LANES), 0)
    c = lax.broadcasted_iota(I32, (LANES, LANES), 1)
    sh = gs.bit_length() - 1
    return jnp.where((r >> sh) == (c >> sh), 1.0, 0.0).astype(BF16)


def _split3(x):
    hi = x.astype(BF16)
    r1 = x - hi.astype(F32)
    mid = r1.astype(BF16)
    return hi, mid, (r1 - mid.astype(F32)).astype(BF16)


def _mm_split(x, w01):
    hi = x.astype(BF16)
    lo = (x - hi.astype(F32)).astype(BF16)
    return _mm(hi, w01) + _mm(lo, w01)


def _group_rms(x, gs):
    ss = _mm_split(x * x, _group_mat(gs))
    return x * lax.rsqrt(ss * (1.0 / gs) + EPS)


def _rope(x, cosp, sinp, half):
    lane = _lane(x.shape)
    rot = jnp.where((lane & (2 * half - 1)) < half,
                    pltpu.roll(x, LANES - half, 1), pltpu.roll(x, half, 1))
    return x * cosp + rot * sinp


def _row_rms(x):
    return x * lax.rsqrt(jnp.mean(x * x, axis=-1, keepdims=True) + EPS)


def _in_proj_kernel(x_ref, g_ref, w_ref, o_ref, o32_ref, *, c32):
    h = (_row_rms(x_ref[0]) * g_ref[...]).astype(BF16)
    ncol = o_ref.shape[-1]
    for c0 in range(0, ncol, 1024):
        c1 = min(ncol, c0 + 1024)
        o_ref[0, :, c0:c1] = _mm(h, w_ref[:, c0:c1]).astype(BF16)
    o32_ref[0] = _mm(h, w_ref[:, c32:c32 + LANES])


def _in_proj(x, g, w, c32):
    b, s, d = x.shape
    c = w.shape[1]
    tm = min(1024, s)
    return pl.pallas_call(
        functools.partial(_in_proj_kernel, c32=c32),
        out_shape=(jax.ShapeDtypeStruct((b, s, c), BF16), jax.ShapeDtypeStruct((b, s, LANES), F32)),
        grid=(b, s // tm),
        in_specs=[pl.BlockSpec((1, tm, d), lambda bi, i: (bi, i, 0)),
                  pl.BlockSpec((1, d), lambda bi, i: (0, 0)),
                  pl.BlockSpec((d, c), lambda bi, i: (0, 0))],
        out_specs=(pl.BlockSpec((1, tm, c), lambda bi, i: (bi, i, 0)),
                   pl.BlockSpec((1, tm, LANES), lambda bi, i: (bi, i, 0))),
        compiler_params=_cp("parallel", "parallel"),
        name="in_proj",
    )(x, g, w)


def _out_proj_kernel(x_ref, y1_ref, y2_ref, ym_ref, g1_ref, g2_ref, gm_ref, w_ref, o_ref):
    def gated(y_ref, g_ref):
        g = g_ref[0].astype(F32)
        return (y_ref[0].astype(F32) * (g * _sigmoid(g))).astype(BF16)

    n1 = y1_ref.shape[-1]
    n2 = y2_ref.shape[-1]
    acc = x_ref[0]
    acc = acc + _mm(gated(y1_ref, g1_ref), w_ref[0:n1, :])
    acc = acc + _mm(gated(y2_ref, g2_ref), w_ref[n1:n1 + n2, :])
    acc = acc + _mm(gated(ym_ref, gm_ref), w_ref[n1 + n2:, :])
    o_ref[0] = acc


def _out_proj(x, y1, y2, ym, u, gate_blocks, w):
    b, s, d = x.shape
    tm = min(512, s)
    i1, i2, im = gate_blocks
    n1, n2, nm = y1.shape[-1], y2.shape[-1], ym.shape[-1]
    row = lambda bi, i: (bi, i, 0)
    return pl.pallas_call(
        _out_proj_kernel,
        out_shape=jax.ShapeDtypeStruct((b, s, d), F32),
        grid=(b, s // tm),
        in_specs=[pl.BlockSpec((1, tm, d), row),
                  pl.BlockSpec((1, tm, n1), row),
                  pl.BlockSpec((1, tm, n2), row),
                  pl.BlockSpec((1, tm, nm), row),
                  pl.BlockSpec((1, tm, n1), lambda bi, i: (bi, i, i1)),
                  pl.BlockSpec((1, tm, n2), lambda bi, i: (bi, i, i2)),
                  pl.BlockSpec((1, tm, nm), lambda bi, i: (bi, i, im)),
                  pl.BlockSpec(w.shape, lambda bi, i: (0, 0))],
        out_specs=pl.BlockSpec((1, tm, d), row),
        compiler_params=_cp("parallel", "parallel"),
        name="out_proj",
    )(x, y1, y2, ym, u, u, u, w)


def _mem_kv_kernel(mem_ref, g_ref, w_ref, kg_ref, k_ref, v_ref):
    h = (_row_rms(mem_ref[0]) * g_ref[0]).astype(BF16)
    kv = _mm(h, w_ref[0])
    nk = k_ref.shape[-1]
    for c0 in range(0, nk, LANES):
        k_ref[0, 0, :, c0:c0 + LANES] = (_group_rms(kv[:, c0:c0 + LANES], M_HD) * kg_ref[0]).astype(BF16)
    v_ref[0, 0] = kv[:, nk:].astype(BF16)


def _mem_kv(mem, g, w, kg):
    b, m, d = mem.shape
    depth = w.shape[0]
    nk = M_HEADS * M_HD
    out = jax.ShapeDtypeStruct((depth, b, m, nk), BF16)
    return pl.pallas_call(
        _mem_kv_kernel,
        out_shape=(out, out),
        grid=(depth, b),
        in_specs=[pl.BlockSpec((1, m, d), lambda l, bi: (bi, 0, 0)),
                  pl.BlockSpec((1, 1, d), lambda l, bi: (l, 0, 0)),
                  pl.BlockSpec((1, d, 2 * nk), lambda l, bi: (l, 0, 0)),
                  pl.BlockSpec((1, 1, LANES), lambda l, bi: (l, 0, 0))],
        out_specs=(pl.BlockSpec((1, 1, m, nk), lambda l, bi: (l, bi, 0, 0)),
                   pl.BlockSpec((1, 1, m, nk), lambda l, bi: (l, bi, 0, 0))),
        compiler_params=_cp("parallel", "parallel"),
        name="mem_kv",
    )(mem, g, w, kg)


def _mem_attn_kernel(q_ref, k_ref, v_ref, qg_ref, o_ref):
    tq = q_ref.shape[1]
    lane = _lane((tq, LANES))
    row = lax.broadcasted_iota(I32, (LANES, tq), 0)
    parts = []
    for p in range(M_HEADS // 2):
        sl = slice(p * LANES, (p + 1) * LANES)
        q = (_group_rms(q_ref[0, :, sl].astype(F32), M_HD) * qg_ref[...] * (M_HD ** -0.5 * LOG2E)).astype(BF16)
        for e in range(2):
            in_half = (lane >= M_HD * e) & (lane < M_HD * (e + 1))
            parts.append(_nt(k_ref[0, 0, :, sl], jnp.where(in_half, q, jnp.zeros_like(q))))
    s = jnp.concatenate(parts, axis=1)
    ex = jnp.exp2(s - jnp.max(s, axis=0, keepdims=True))
    inv = 1.0 / jnp.sum(ex, axis=0, keepdims=True)
    for p in range(M_HEADS // 2):
        sl = slice(p * LANES, (p + 1) * LANES)
        pv = _tn(v_ref[0, 0, :, sl], ex[:, 2 * p * tq:(2 * p + 2) * tq].astype(BF16)) * inv[:, 2 * p * tq:(2 * p + 2) * tq]
        o_ref[0, :, sl] = jnp.where(row < M_HD, pv[:, :tq], pv[:, tq:]).T.astype(BF16)


def _mem_attn(u, q_block, k, v, layer, qg):
    b, s, _ = u.shape
    m, nk = k.shape[2], k.shape[3]
    tq = min(1024, s)
    return pl.pallas_call(
        _mem_attn_kernel,
        out_shape=jax.ShapeDtypeStruct((b, s, nk), BF16),
        grid=(b, s // tq),
        in_specs=[pl.BlockSpec((1, tq, nk), lambda bi, i: (bi, i, q_block)),
                  pl.BlockSpec((1, 1, m, nk), lambda bi, i: (layer, bi, 0, 0)),
                  pl.BlockSpec((1, 1, m, nk), lambda bi, i: (layer, bi, 0, 0)),
                  pl.BlockSpec((1, LANES), lambda bi, i: (0, 0))],
        out_specs=pl.BlockSpec((1, tq, nk), lambda bi, i: (bi, i, 0)),
        compiler_params=_cp("parallel", "parallel"),
        name="mem_attn",
    )(u, k, v, qg)


def _flash_init_t(cols):
    return (jnp.full((1, cols), NEG, F32), jnp.zeros((1, cols), F32), jnp.zeros((LANES, cols), F32))


def _softmax_step_t(q, k, v, bias_t, carry):
    s = _nt(k, q)
    if bias_t is not None:
        s = s + bias_t
    m, l, acc = carry
    m_new = jnp.maximum(m, jnp.max(s, axis=0, keepdims=True))
    alpha = jnp.exp2(m - m_new)
    p = jnp.exp2(s - m_new)
    l = alpha * l + jnp.sum(p, axis=0, keepdims=True)
    acc = alpha * acc + _tn(v, p.astype(BF16))
    return m_new, l, acc


def _mla_prep_kernel(u_ref, cos_ref, sin_ref, qlg_ref, kvlg_ref, wuq_ref, wukv_ref,
                     qng_ref, qrg_ref, kng_ref, krg_ref, q_ref, k_ref, v_ref):
    tm = u_ref.shape[1]
    lane = _lane((tm, LANES))
    cosp, sinp = cos_ref[0], sin_ref[0]
    half = A_ROPE // 2
    scale = (A_NOPE + A_ROPE) ** -0.5 * LOG2E
    ql = (_row_rms(u_ref[0, :, 0:A_QLAT].astype(F32)) * qlg_ref[...]).astype(BF16)
    kvl = (_row_rms(u_ref[0, :, A_QLAT:A_QLAT + A_KVLAT].astype(F32)) * kvlg_ref[...]).astype(BF16)
    q = _mm(ql, wuq_ref[...])
    kv = _mm(kvl, wukv_ref[...])
    n_nope = A_HEADS * A_NOPE
    v_ref[0] = kv[:, n_nope:].astype(BF16)
    kr = u_ref[0, :, A_QLAT + A_KVLAT:A_QLAT + A_KVLAT + LANES].astype(F32)
    kpe = _rope(_group_rms(kr, A_ROPE) * krg_ref[...], cosp, sinp, half)
    kpe = pltpu.roll(kpe, A_NOPE, 1)
    qn = [_group_rms(q[:, c:c + LANES], A_NOPE) * qng_ref[...] for c in range(0, n_nope, LANES)]
    kn = [_group_rms(kv[:, c:c + LANES], A_NOPE) * kng_ref[...] for c in range(0, n_nope, LANES)]
    qr = [_rope(_group_rms(q[:, n_nope + c:n_nope + c + LANES], A_ROPE) * qrg_ref[...], cosp, sinp, half)
          for c in range(0, A_HEADS * A_ROPE, LANES)]
    for h in range(A_HEADS):
        qn_h = qn[h // 2] if h % 2 == 0 else pltpu.roll(qn[h // 2], A_NOPE, 1)
        kn_h = kn[h // 2] if h % 2 == 0 else pltpu.roll(kn[h // 2], A_NOPE, 1)
        shift = (A_NOPE - (h % 4) * A_ROPE) % LANES
        qr_h = qr[h // 4] if shift == 0 else pltpu.roll(qr[h // 4], shift, 1)
        qf = jnp.where(lane < A_NOPE, qn_h, jnp.where(lane < A_NOPE + A_ROPE, qr_h, 0.0))
        kf = jnp.where(lane < A_NOPE, kn_h, jnp.where(lane < A_NOPE + A_ROPE, kpe, 0.0))
        q_ref[0, h] = (qf * scale).astype(BF16)
        k_ref[0, h] = kf.astype(BF16)


def _mla_prep(u, cos32, sin32, p):
    b, s, _ = u.shape
    tm = min(512, s)
    hd = jax.ShapeDtypeStruct((b, A_HEADS, s, LANES), BF16)
    const = lambda shape: pl.BlockSpec(shape, lambda bi, i: (0,) * len(shape))
    return pl.pallas_call(
        _mla_prep_kernel,
        out_shape=(hd, hd, jax.ShapeDtypeStruct((b, s, A_HEADS * A_VD), BF16)),
        grid=(b, s // tm),
        in_specs=[pl.BlockSpec((1, tm, 512), lambda bi, i: (bi, i, 0)),
                  pl.BlockSpec((1, tm, LANES), lambda bi, i: (bi, i, 0)),
                  pl.BlockSpec((1, tm, LANES), lambda bi, i: (bi, i, 0)),
                  const((1, A_QLAT)), const((1, A_KVLAT)),
                  const(p["w_uq"].shape), const(p["w_ukv"].shape),
                  const((1, LANES)), const((1, LANES)), const((1, LANES)), const((1, LANES))],
        out_specs=(pl.BlockSpec((1, A_HEADS, tm, LANES), lambda bi, i: (bi, 0, i, 0)),
                   pl.BlockSpec((1, A_HEADS, tm, LANES), lambda bi, i: (bi, 0, i, 0)),
                   pl.BlockSpec((1, tm, A_HEADS * A_VD), lambda bi, i: (bi, i, 0))),
        compiler_params=_cp("parallel", "parallel"),
        name="mla_prep",
    )(u, cos32, sin32, p["q_lat_g"], p["kv_lat_g"], p["w_uq"], p["w_ukv"],
      p["qn_g"], p["qr_g"], p["kn_g"], p["kr_g"])


def _mla_attn_kernel(q_ref, k_ref, v_ref, o_ref, *, tk):
    tq = q_ref.shape[2]
    per_q = tq // tk
    i = pl.program_id(2)
    lane = _lane((tq, LANES))
    t = i * tq + lax.broadcasted_iota(I32, (1, tq), 1)
    nh = q_ref.shape[1]
    qs = [q_ref[0, e] for e in range(nh)]
    tile = lambda j: pl.ds(pl.multiple_of(j * tk, tk), tk)
    v_at = lambda j, e: v_ref[0, tile(j), (e // 2) * LANES:(e // 2 + 1) * LANES]

    def step(j, q0, bias_t, c):
        m, l, acc = c
        w = tq - q0
        s = jnp.concatenate([_nt(k_ref[0, e, tile(j), :], qs[e][q0:]) for e in range(nh)], axis=1)
        if bias_t is not None:
            s = s + jnp.concatenate([bias_t] * nh, axis=1)
        m_new = jnp.maximum(m, jnp.max(s, axis=0, keepdims=True))
        alpha = jnp.exp2(m - m_new)
        p = jnp.exp2(s - m_new)
        l = alpha * l + jnp.sum(p, axis=0, keepdims=True)
        pv = jnp.concatenate([_tn(v_at(j, 2 * pp), p[:, 2 * pp * w:(2 * pp + 2) * w].astype(BF16))
                              for pp in range(nh // 2)], axis=1)
        return m_new, l, alpha * acc + pv

    carry = lax.fori_loop(0, i * per_q, lambda j, c: step(j, 0, None, c), _flash_init_t(nh * tq))
    for d in range(per_q):
        j = i * per_q + d
        q0 = d * tk
        bias_t = jnp.where(j * tk + lax.broadcasted_iota(I32, (tk, 1), 0) <= t[:, q0:], 0.0, NEG)
        take = lambda x: jnp.concatenate([x[:, e * tq + q0:(e + 1) * tq] for e in range(nh)], axis=1)
        upd = step(j, q0, bias_t, tuple(take(x) for x in carry))
        if d:
            w = tq - q0
            upd = tuple(jnp.concatenate(
                [part for e in range(nh) for part in (old[:, e * tq:e * tq + q0], new[:, e * w:(e + 1) * w])], axis=1)
                for old, new in zip(carry, upd))
        carry = upd
    res = [(carry[2][:, e * tq:(e + 1) * tq] / carry[1][:, e * tq:(e + 1) * tq]).T for e in range(nh)]
    for p in range(nh // 2):
        o_ref[0, :, p * LANES:(p + 1) * LANES] = jnp.where(lane < A_VD, res[2 * p], res[2 * p + 1]).astype(BF16)


def _mla_attn(q, k, v):
    b, h, s, _ = q.shape
    tq = min(1024, s)
    tk = min(512, s)
    nh = 4
    return pl.pallas_call(
        functools.partial(_mla_attn_kernel, tk=tk),
        out_shape=jax.ShapeDtypeStruct((b, s, h * A_VD), BF16),
        grid=(b, h // nh, s // tq),
        in_specs=[pl.BlockSpec((1, nh, tq, LANES), lambda bi, p, i: (bi, p, i, 0)),
                  pl.BlockSpec((1, nh, s, LANES), lambda bi, p, i: (bi, p, 0, 0)),
                  pl.BlockSpec((1, s, nh * A_VD), lambda bi, p, i: (bi, 0, p))],
        out_specs=pl.BlockSpec((1, tq, nh * A_VD), lambda bi, p, i: (bi, i, p)),
        compiler_params=_cp("parallel", "parallel", "parallel"),
        name="mla_attn",
    )(q, k, v)


def _nsa_prep_kernel(q_in, ks_in, kw_in, cos_ref, sin_ref, qg_ref, ksg_ref, kwg_ref, q_ref, ks_ref, kw_ref):
    tm = q_in.shape[1]
    lane = _lane((tm, LANES))
    cosp, sinp = cos_ref[0], sin_ref[0]
    half = B_HD // 2
    rep = B_HEADS // B_KV_HEADS
    for p in range(B_HEADS // 2):
        y = _rope(_group_rms(q_in[0, :, p * LANES:(p + 1) * LANES].astype(F32), B_HD) * qg_ref[...], cosp, sinp, half)
        y = y * (B_HD ** -0.5 * LOG2E)
        y_sw = pltpu.roll(y, B_HD, 1)
        for e in range(2):
            h = 2 * p + e
            g = h // rep
            src = y if e == g else y_sw
            in_grp = (lane >= B_HD * g) & (lane < B_HD * (g + 1))
            q_ref[0, h] = jnp.where(in_grp, src, 0.0).astype(BF16)
    ks_ref[0] = _rope(_group_rms(ks_in[0].astype(F32), B_HD) * ksg_ref[...], cosp, sinp, half).astype(BF16)
    kw_ref[0] = _rope(_group_rms(kw_in[0].astype(F32), B_HD) * kwg_ref[...], cosp, sinp, half).astype(BF16)


def _nsa_prep(u, cols, cos64, sin64, p):
    b, s, _ = u.shape
    tm = min(512, s)
    blk = lambda name: pl.BlockSpec((1, tm, LANES), lambda bi, i, c=cols[name] // LANES: (bi, i, c))
    row = pl.BlockSpec((1, tm, LANES), lambda bi, i: (bi, i, 0))
    const = pl.BlockSpec((1, LANES), lambda bi, i: (0, 0))
    kvs = jax.ShapeDtypeStruct((b, s, LANES), BF16)
    return pl.pallas_call(
        _nsa_prep_kernel,
        out_shape=(jax.ShapeDtypeStruct((b, B_HEADS, s, LANES), BF16), kvs, kvs),
        grid=(b, s // tm),
        in_specs=[pl.BlockSpec((1, tm, 512), lambda bi, i, c=cols["b_q"] // 512: (bi, i, c)),
                  blk("b_ks"), blk("b_kw"), row, row, const, const, const],
        out_specs=(pl.BlockSpec((1, B_HEADS, tm, LANES), lambda bi, i: (bi, 0, i, 0)), row, row),
        compiler_params=_cp("parallel", "parallel"),
        name="nsa_prep",
    )(u, u, u, cos64, sin64, p["q_g"], p["ks_g"], p["kw_g"])


def _nsa_cmp_kernel(kc_in, vc_in, pek_ref, pev_ref, w1k_ref, w1v_ref, w2k_ref, w2v_ref, kg_ref,
                    cos_ref, sin_ref, ko_ref, vo_ref, pad_ref):
    s = kc_in.shape[1]
    n_pad = ko_ref.shape[1]

    def compress(x_in, pe_ref, w1_ref, w2_ref):
        pad_ref[0:s, :] = x_in[0].astype(F32)
        pad_ref[s:s + CMP_STRIDE, :] = jnp.zeros((CMP_STRIDE, LANES), F32)
        acc = jnp.zeros((n_pad, LANES), F32)
        for l in range(CMP_LEN):
            xl = pad_ref[pl.ds(l, n_pad, stride=CMP_STRIDE), :] + pe_ref[l:l + 1, :]
            acc = acc + _mm(xl.astype(BF16), w1_ref[l])
        mid = acc * _sigmoid(acc)
        return _mm(mid.astype(BF16), w2_ref[...])

    kc = compress(kc_in, pek_ref, w1k_ref, w2k_ref)
    ko_ref[0] = _rope(_group_rms(kc, B_HD) * kg_ref[...], cos_ref[0], sin_ref[0], B_HD // 2).astype(BF16)
    vo_ref[0] = compress(vc_in, pev_ref, w1v_ref, w2v_ref).astype(BF16)


def _nsa_cmp(u, cols, cosc, sinc, p):
    b, s, _ = u.shape
    n_pad = s // CMP_STRIDE
    blk = lambda name: pl.BlockSpec((1, s, LANES), lambda bi, c=cols[name] // LANES: (bi, 0, c))
    const = lambda shape: pl.BlockSpec(shape, lambda bi: (0,) * len(shape))
    out = jax.ShapeDtypeStruct((b, n_pad, LANES), BF16)
    ospec = pl.BlockSpec((1, n_pad, LANES), lambda bi: (bi, 0, 0))
    return pl.pallas_call(
        _nsa_cmp_kernel,
        out_shape=(out, out),
        grid=(b,),
        in_specs=[blk("b_kc"), blk("b_vc"), const((CMP_LEN, LANES)), const((CMP_LEN, LANES)),
                  const((CMP_LEN, LANES, LANES)), const((CMP_LEN, LANES, LANES)),
                  const((LANES, LANES)), const((LANES, LANES)), const((1, LANES)), ospec, ospec],
        out_specs=(ospec, ospec),
        scratch_shapes=[pltpu.VMEM((s + CMP_STRIDE, LANES), F32)],
        compiler_params=_cp("parallel"),
        name="nsa_cmp",
    )(u, u, p["pe_k"], p["pe_v"], p["w1k"], p["w1v"], p["w2k"], p["w2v"], p["kc_g"], cosc, sinc)


def _nsa_attn_kernel(q_ref, kc_ref, vc_ref, ks_ref, vs_ref, kw_ref, vw_ref, g_ref, e_ref, ovl_ref, o_ref,
                     *, n_blk, n_sel, tks, ww):
    tq = q_ref.shape[2]
    n_pad = kc_ref.shape[1]
    i = pl.program_id(1)
    t = i * tq + lax.broadcasted_iota(I32, (1, tq), 1)
    lane = _lane((tq, LANES))
    gates_t = _sigmoid(g_ref[0]).T
    rep = B_HEADS // B_KV_HEADS
    heads = lambda x: jnp.concatenate([x] * rep, axis=1)
    ncmp = lax.broadcasted_iota(I32, (n_pad, 1), 0)
    valid_c = heads((ncmp * CMP_STRIDE + (CMP_LEN - 1)) <= t)
    kc = kc_ref[0]
    vc = vc_ref[0]
    blk = lax.broadcasted_iota(I32, (LANES, 1), 0)
    cur = t >> (SEL_LEN.bit_length() - 1)
    forced = jnp.where(blk == cur, 3e4, jnp.where(blk == cur - 1, 2e4, jnp.where(blk == 0, 1e4, 0.0)))
    adm = (blk * SEL_LEN <= t) & (blk < n_blk)
    n_rv = n_blk // 8
    sub = lax.broadcasted_iota(I32, (8, tq), 0)
    w0 = pl.multiple_of(jnp.maximum(i * tq + tq - ww, 0), tq)
    kp_w = w0 + lax.broadcasted_iota(I32, (ww, 1), 0)
    win_bias = heads(jnp.where((kp_w <= t) & (kp_w > t - WINDOW), 0.0, NEG))
    qs = jnp.concatenate([q_ref[0, h] for h in range(B_HEADS)], axis=0)
    gate = lambda c: jnp.concatenate([gates_t[h * 3 + c:h * 3 + c + 1, :] for h in range(B_HEADS)], axis=1)
    groups = lambda xs: jnp.concatenate([heads(x) for x in xs], axis=1)
    valid_all = groups([valid_c[:, 0:tq]] * B_KV_HEADS)
    s = jnp.where(valid_all, _nt(kc, qs), NEG)
    ex = jnp.exp2(s - jnp.max(s, axis=0, keepdims=True))
    pc = jnp.where(valid_all, ex / jnp.sum(ex, axis=0, keepdims=True), 0.0)
    out = gate(0) * _tn(vc, pc.astype(BF16))
    sel_ts = []
    for g in range(B_KV_HEADS):
        psum = pc[:, rep * g * tq:(rep * g + 1) * tq]
        for r in range(1, rep):
            psum = psum + pc[:, (rep * g + r) * tq:(rep * g + r + 1) * tq]
        p_hi = psum.astype(BF16)
        p_lo = (psum - p_hi.astype(F32)).astype(BF16)
        imp = _mm(ovl_ref[...], p_hi) + _mm(ovl_ref[...], p_lo)
        score_t = jnp.where(adm, imp + forced, NEG)
        sc = [score_t[8 * v:8 * v + 8] for v in range(n_rv)]
        rank = [jnp.zeros((8, tq), F32) for _ in range(n_rv)]
        for jp in range(n_blk):
            col = score_t[jp:jp + 1]
            for v in range(n_rv):
                if v > jp // 8:
                    beats = col >= sc[v]
                elif v < jp // 8:
                    beats = col > sc[v]
                else:
                    beats = (col > sc[v]) | ((col == sc[v]) & (sub > jp % 8))
                rank[v] = rank[v] + jnp.where(beats, 1.0, 0.0)
        sel_ts.append(jnp.where(jnp.concatenate(rank, axis=0) < n_sel, 1.0, 0.0).astype(BF16))

    tile = lambda j: pl.ds(pl.multiple_of(j * tks, tks), tks)

    def sel_body(j, carry):
        kp = j * tks + lax.broadcasted_iota(I32, (tks, 1), 0)
        bias = [jnp.where((_mm(e_ref[tile(j), :], sel_t) > 0.5) & (kp <= t), 0.0, NEG) for sel_t in sel_ts]
        return _softmax_step_t(qs, ks_ref[0, tile(j), :], vs_ref[0, tile(j), :], groups(bias), carry)

    n_tile = ((i + 1) * tq + tks - 1) // tks
    _, l_s, acc_s = lax.fori_loop(0, n_tile, sel_body, _flash_init_t(B_HEADS * tq))
    out = out + gate(1) * (acc_s / l_s)
    s_w = _nt(kw_ref[0, pl.ds(w0, ww), :], qs) + groups([win_bias[:, 0:tq]] * B_KV_HEADS)
    p_w = jnp.exp2(s_w - jnp.max(s_w, axis=0, keepdims=True))
    o_w = _tn(vw_ref[0, pl.ds(w0, ww), :], p_w.astype(BF16)) / jnp.sum(p_w, axis=0, keepdims=True)
    out = out + gate(2) * o_w
    for g in range(B_KV_HEADS):
        outs = [out[:, (rep * g + r) * tq:(rep * g + r + 1) * tq].T for r in range(rep)]
        for pp in range(rep // 2):
            a, bb = outs[2 * pp], outs[2 * pp + 1]
            if g == 0:
                bb = pltpu.roll(bb, B_HD, 1)
            else:
                a = pltpu.roll(a, B_HD, 1)
            c0 = (rep // 2 * g + pp) * LANES
            o_ref[0, :, c0:c0 + LANES] = jnp.where(lane < B_HD, a, bb).astype(BF16)


def _nsa_attn(u, u32, cols, qx, kcmp, vcmp, ks, kw):
    b, h, s, _ = qx.shape
    tq = min(256, s)
    tks = min(512, s)
    n_pad = kcmp.shape[1]
    n_blk = s // SEL_LEN
    n_sel = min(N_SEL, n_blk)
    n_cmp = (s - CMP_LEN) // CMP_STRIDE + 1
    ww = min(WINDOW + tq, s)
    expand = np.zeros((s, n_blk), np.float32)
    expand[np.arange(s), np.arange(s) // SEL_LEN] = 1.0
    nn = np.arange(n_pad)[None, :]
    jj = np.arange(LANES)[:, None]
    ovl = ((nn * CMP_STRIDE <= jj * SEL_LEN + SEL_LEN - 1) & (nn * CMP_STRIDE + CMP_LEN - 1 >= jj * SEL_LEN)
           & (jj < n_blk) & (nn < n_cmp)).astype(np.float32)
    full = lambda shape: pl.BlockSpec(shape, lambda bi, i: (bi,) + (0,) * (len(shape) - 1))
    ucol = lambda name: pl.BlockSpec((1, s, LANES), lambda bi, i, c=cols[name] // LANES: (bi, 0, c))
    const = lambda shape: pl.BlockSpec(shape, lambda bi, i: (0,) * len(shape))
    return pl.pallas_call(
        functools.partial(_nsa_attn_kernel, n_blk=n_blk, n_sel=n_sel, tks=tks, ww=ww),
        out_shape=jax.ShapeDtypeStruct((b, s, h * B_HD), BF16),
        grid=(b, s // tq),
        in_specs=[pl.BlockSpec((1, h, tq, LANES), lambda bi, i: (bi, 0, i, 0)),
                  full((1, n_pad, LANES)), full((1, n_pad, LANES)),
                  full((1, s, LANES)), ucol("b_vs"), full((1, s, LANES)), ucol("b_vw"),
                  pl.BlockSpec((1, tq, LANES), lambda bi, i: (bi, i, 0)),
                  const((s, n_blk)), const((LANES, n_pad))],
        out_specs=pl.BlockSpec((1, tq, h * B_HD), lambda bi, i: (bi, i, 0)),
        compiler_params=_cp("parallel", "parallel"),
        name="nsa_attn",
    )(qx, kcmp, vcmp, ks, u, kw, u, u32, jnp.asarray(expand, BF16), jnp.asarray(ovl, BF16))


def _dsa_prep_kernel(q_in, iq_in, kv_in, sm_in, cos64_ref, sin64_ref, cos32_ref, sin32_ref, qg_ref, kg_ref,
                     q_ref, kv_ref, iq_ref, ik_ref):
    tm = q_in.shape[1]
    lane = _lane((tm, LANES))
    c64, s64, c32, s32 = cos64_ref[0], sin64_ref[0], cos32_ref[0], sin32_ref[0]
    for p in range(C_HEADS // 2):
        y = _rope(_group_rms(q_in[0, :, p * LANES:(p + 1) * LANES].astype(F32), C_HD) * qg_ref[...], c64, s64, C_HD // 2)
        y = y * (C_HD ** -0.5 * LOG2E)
        q_ref[0, 2 * p] = jnp.where(lane < C_HD, y, 0.0).astype(BF16)
        q_ref[0, 2 * p + 1] = jnp.where(lane < C_HD, pltpu.roll(y, C_HD, 1), 0.0).astype(BF16)
    kv = kv_in[0].astype(F32)
    kn = _rope(_group_rms(kv, C_HD) * kg_ref[...], c64, s64, C_HD // 2)
    kv_ref[0] = jnp.where(lane < C_HD, kn, kv).astype(BF16)
    for c0 in range(0, IDX_HEADS * IDX_HD, LANES):
        iq_ref[0, :, c0:c0 + LANES] = _rope(iq_in[0, :, c0:c0 + LANES].astype(F32), c32, s32, IDX_HD // 2).astype(BF16)
    ik = jnp.where(lane < IDX_HD, _rope(sm_in[0], c32, s32, IDX_HD // 2), 0.0)
    ik = ik + pltpu.roll(ik, IDX_HD, 1)
    ik = ik + pltpu.roll(ik, 2 * IDX_HD, 1)
    ik_ref[0] = ik.astype(BF16)


def _dsa_prep(u, u32, cols, tabs, p):
    b, s, _ = u.shape
    tm = min(512, s)
    row = pl.BlockSpec((1, tm, LANES), lambda bi, i: (bi, i, 0))
    const = pl.BlockSpec((1, LANES), lambda bi, i: (0, 0))
    ublk = lambda name, w: pl.BlockSpec((1, tm, w), lambda bi, i, c=cols[name] // w: (bi, i, c))
    dense = jax.ShapeDtypeStruct((b, s, LANES), BF16)
    return pl.pallas_call(
        _dsa_prep_kernel,
        out_shape=(jax.ShapeDtypeStruct((b, C_HEADS, s, LANES), BF16), dense,
                   jax.ShapeDtypeStruct((b, s, IDX_HEADS * IDX_HD), BF16), dense),
        grid=(b, s // tm),
        in_specs=[ublk("c_q", 512), ublk("c_iq", 256), ublk("c_kv", LANES), row,
                  row, row, row, row, const, const],
        out_specs=(pl.BlockSpec((1, C_HEADS, tm, LANES), lambda bi, i: (bi, 0, i, 0)), row,
                   pl.BlockSpec((1, tm, IDX_HEADS * IDX_HD), lambda bi, i: (bi, i, 0)), row),
        compiler_params=_cp("parallel", "parallel"),
        name="dsa_prep",
    )(u, u, u, u32, tabs["cos64"], tabs["sin64"], tabs["cos32"], tabs["sin32"], p["q_g"], p["k_g"])


def _dsa_attn_kernel(q_ref, iq_ref, sm_ref, ik_ref, kv_ref, tri_ref, o_ref, key_scr, bias_scr, *, tk, topk):
    tq = q_ref.shape[2]
    i = pl.program_id(1)
    t = i * tq + lax.broadcasted_iota(I32, (1, tq), 1)
    lane = _lane((tq, LANES))
    n_tile = ((i + 1) * tq + tk - 1) // tk
    tile = lambda j: pl.ds(pl.multiple_of(j * tk, tk), tk)
    kpos = lambda j: j * tk + lax.broadcasted_iota(I32, (tk, 1), 0)

    iw_t = (sm_ref[0] * (IDX_HEADS ** -0.5)).T
    parts = []
    for h in range(IDX_HEADS):
        blk = iq_ref[0, :, (h // 4) * LANES:(h // 4 + 1) * LANES]
        lo = (h % 4) * IDX_HD
        parts.append(jnp.where((lane >= lo) & (lane < lo + IDX_HD), blk, jnp.zeros_like(blk)))
    iqs = jnp.concatenate(parts, axis=0)

    def idx_body(j, carry, causal_edge):
        lg = _nt(ik_ref[0, tile(j), :], iqs)
        acc = jnp.zeros((tk, tq), F32)
        for h in range(IDX_HEADS):
            acc = acc + iw_t[IDX_HD + h:IDX_HD + h + 1, :] * jnp.maximum(lg[:, h * tq:(h + 1) * tq], 0.0)
        sc = jnp.where(kpos(j) <= t, acc, NEG) if causal_edge else acc
        bits = pltpu.bitcast(sc, I32)
        key = jnp.where(bits < 0, bits ^ 0x7FFFFFFF, bits)
        key_scr[tile(j), :] = jnp.where(sc == 0.0, 0, key)
        return carry

    n_full = (i * tq) // tk
    lax.fori_loop(0, n_full, functools.partial(idx_body, causal_edge=False), 0)
    lax.fori_loop(n_full, n_tile, functools.partial(idx_body, causal_edge=True), 0)

    tr = min(tk, 2 * tq)
    n_sub = ((i + 1) * tq + tr - 1) // tr

    def count_ge(thr_key):
        def body(j, cs):
            key = key_scr[pl.ds(pl.multiple_of(j * tr, tr), tr), :]
            cs = list(cs)
            for n, r0 in enumerate(range(0, tr, 8)):
                c = cs[n % len(cs)]
                cs[n % len(cs)] = jnp.where(key[r0:r0 + 8] >= thr_key, c + 1, c)
            return tuple(cs)

        cs = lax.fori_loop(0, n_sub, body, tuple(jnp.zeros((8, tq), I32) for _ in range(4)))
        return jnp.sum(cs[0] + cs[1] + cs[2] + cs[3], axis=0, keepdims=True)

    def bit_body(bi, ucand):
        utrial = ucand | jnp.left_shift(jnp.int32(1), 31 - bi)
        return jnp.where(count_ge(utrial ^ INT_MIN) >= topk, utrial, ucand)

    thr = lax.fori_loop(0, 32, bit_body, jnp.zeros((1, tq), I32)) ^ INT_MIN
    n_ge = count_ge(thr)
    n_gt = count_ge(thr + 1)
    need = topk - n_gt
    row_ok = (n_ge - n_gt == need) | (thr == NEG_KEY) | (n_ge < topk)
    simple = jnp.min(jnp.where(row_ok, 1.0, 0.0)) > 0.5

    def fast_bias():
        def body(j, carry, causal_edge):
            keep = key_scr[tile(j), :] >= thr
            if causal_edge:
                keep = keep & (kpos(j) <= t)
            bias_scr[tile(j), :] = jnp.where(keep, 0.0, NEG)
            return carry

        lax.fori_loop(0, n_full, functools.partial(body, causal_edge=False), 0)
        lax.fori_loop(n_full, n_tile, functools.partial(body, causal_edge=True), 0)

    def tie_bias():
        need_f = need.astype(F32)

        def body(j, run):
            key = key_scr[tile(j), :]
            kp = kpos(j)
            for c0 in range(0, tk, LANES):
                kc = key[c0:c0 + LANES]
                eq = kc == thr
                eq_f = jnp.where(eq, 1.0, 0.0)
                pref = _mm(tri_ref[...], eq_f.astype(BF16)) + run
                keep = ((kc > thr) | (eq & (pref <= need_f))) & (kp[c0:c0 + LANES] <= t)
                bias_scr[pl.ds(pl.multiple_of(j * tk + c0, LANES), LANES), :] = jnp.where(keep, 0.0, NEG)
                run = run + jnp.sum(eq_f, axis=0, keepdims=True)
            return run

        lax.fori_loop(0, n_tile, body, jnp.zeros((1, tq), F32))

    lax.cond(simple, fast_bias, tie_bias)

    qs = jnp.concatenate([q_ref[0, h] for h in range(C_HEADS)], axis=0)

    def att_body(j, carry):
        kv = kv_ref[0, tile(j), :]
        bias_t = jnp.concatenate([bias_scr[tile(j), :]] * C_HEADS, axis=1)
        return _softmax_step_t(qs, kv, kv, bias_t, carry)

    _, l, acc = lax.fori_loop(0, n_tile, att_body, _flash_init_t(C_HEADS * tq))
    o_t = acc / l
    for p in range(C_HEADS // 2):
        a = pltpu.roll(o_t[:, (2 * p) * tq:(2 * p + 1) * tq].T, C_HD, 1)
        bb = o_t[:, (2 * p + 1) * tq:(2 * p + 2) * tq].T
        o_ref[0, :, p * LANES:(p + 1) * LANES] = jnp.where(lane < C_HD, a, bb).astype(BF16)


def _dsa_attn(u32, qc, iq, ik, kv):
    b, h, s, _ = qc.shape
    tq = min(256, s)
    tk = min(1024, s)
    topk = min(TOPK_MAX, s // 4)
    tri = np.tril(np.ones((LANES, LANES), np.float32))
    full = lambda shape: pl.BlockSpec(shape, lambda bi, i: (bi,) + (0,) * (len(shape) - 1))
    return pl.pallas_call(
        functools.partial(_dsa_attn_kernel, tk=tk, topk=topk),
        out_shape=jax.ShapeDtypeStruct((b, s, h * C_HD), BF16),
        grid=(b, s // tq),
        in_specs=[pl.BlockSpec((1, h, tq, LANES), lambda bi, i: (bi, 0, i, 0)),
                  pl.BlockSpec((1, tq, IDX_HEADS * IDX_HD), lambda bi, i: (bi, i, 0)),
                  pl.BlockSpec((1, tq, LANES), lambda bi, i: (bi, i, 0)),
                  full((1, s, LANES)), full((1, s, LANES)),
                  pl.BlockSpec((LANES, LANES), lambda bi, i: (0, 0))],
        out_specs=pl.BlockSpec((1, tq, h * C_HD), lambda bi, i: (bi, i, 0)),
        scratch_shapes=[pltpu.VMEM((s, tq), I32), pltpu.VMEM((s, tq), F32)],
        compiler_params=_cp("parallel", "parallel"),
        name="dsa_attn",
    )(qc, iq, u32, ik, kv, jnp.asarray(tri, BF16))


def _mlstm_conv_kernel(x_ref, halo_ref, w_ref, b_ref, o_ref):
    tm = x_ref.shape[1]
    i = pl.program_id(1)
    lane = _lane((tm, x_ref.shape[2]))
    halo = jnp.where(i > 0, halo_ref[0].astype(F32), 0.0)
    x = x_ref[0].astype(F32)
    xc = jnp.concatenate([halo, x], axis=0)
    off = halo.shape[0] - (CONV_W - 1)
    y = b_ref[...] + jnp.zeros_like(x)
    for j in range(CONV_W):
        y = y + w_ref[j:j + 1, :] * xc[off + j:off + j + tm, :]
    y = y * _sigmoid(y)
    o_ref[0] = jnp.where(lane >= D_HEADS * D_QK, y * (D_QK ** -0.5), y)


def _mlstm_conv(u, qk_block, w, bias):
    b, s, _ = u.shape
    c = w.shape[1]
    tm = min(512, s)
    hb = 16
    return pl.pallas_call(
        _mlstm_conv_kernel,
        out_shape=jax.ShapeDtypeStruct((b, s, c), F32),
        grid=(b, s // tm),
        in_specs=[pl.BlockSpec((1, tm, c), lambda bi, i: (bi, i, qk_block)),
                  pl.BlockSpec((1, hb, c), lambda bi, i: (bi, jnp.maximum(i * (tm // hb) - 1, 0), qk_block)),
                  pl.BlockSpec((CONV_W, c), lambda bi, i: (0, 0)),
                  pl.BlockSpec((1, c), lambda bi, i: (0, 0))],
        out_specs=pl.BlockSpec((1, tm, c), lambda bi, i: (bi, i, 0)),
        compiler_params=_cp("parallel", "parallel"),
        name="mlstm_conv",
    )(u, u, w, bias)


def _mlstm_scan_kernel(qk_ref, v_ref, op_ref, sm_ref, gt_ref, bcol_ref, brow_ref, hg_ref, o_ref,
                       c_scr, m_scr):
    tc = qk_ref.shape[1]
    L = CHUNK
    nqk = D_HEADS * D_QK

    @pl.when(pl.program_id(1) == 0)
    def _():
        c_scr[...] = jnp.zeros_like(c_scr)
        m_scr[...] = jnp.zeros_like(m_scr)

    r = lax.broadcasted_iota(I32, (L, L), 0)
    c = lax.broadcasted_iota(I32, (L, L), 1)
    tril = c <= r
    tril_f = jnp.where(tril, 1.0, 0.0).astype(BF16)
    triu_f = jnp.where(r <= c, 1.0, 0.0).astype(BF16)
    lane = _lane((L, LANES))
    row128 = lax.broadcasted_iota(I32, (LANES, 2 * LANES), 0)
    ones = jnp.ones((L, LANES), F32)
    i_lane, f_lane = IDX_HD + IDX_HEADS, IDX_HD + IDX_HEADS + D_HEADS

    def one(bb, ci):
        rows = pl.ds(pl.multiple_of(ci * L, L), L)
        qk = qk_ref[bb, rows, :]
        sm = sm_ref[bb, rows, :] + bcol_ref[...]
        bcol_all = sum(_mm(tril_f, part) for part in _split3(_log_sigmoid(sm)))
        gt = gt_ref[bb, ci] + brow_ref[...]
        brow_all = sum(_mm(part, triu_f) for part in _split3(_log_sigmoid(gt)))
        for p in range(D_HEADS // 2):
            qpair = qk[:, p * LANES:(p + 1) * LANES]
            kpair = qk[:, nqk + p * LANES:nqk + (p + 1) * LANES]
            kpair_b = kpair.astype(BF16)
            sp = bb * (D_HEADS // 2) + p
            c_prev = c_scr[sp]
            c_prev_b = c_prev.astype(BF16)
            upd = []
            for e in range(2):
                h = 2 * p + e
                in_half = (lane >= D_QK * e) & (lane < D_QK * (e + 1))
                qm = jnp.where(in_half, qpair, 0.0)
                qm_b = qm.astype(BF16)
                vh = v_ref[bb, rows, h * D_VD:(h + 1) * D_VD].astype(F32)
                b_col = jnp.broadcast_to(bcol_all[:, f_lane + h:f_lane + h + 1], (L, LANES))
                i_col = jnp.broadcast_to(sm[:, i_lane + h:i_lane + h + 1], (L, LANES))
                b_row = brow_all[D_HEADS + h:D_HEADS + h + 1, :]
                i_row = gt[h:h + 1, :]
                m_prev = m_scr[bb * D_HEADS + h, 0:1, :]
                dmat = jnp.where(tril, b_col[:, :L] - b_row + i_row, NEG)
                inter = b_col + m_prev
                m_t = jnp.maximum(inter, jnp.max(dmat, axis=-1, keepdims=True))
                a = jnp.exp(inter - m_t)
                w = _nt(qm_b, kpair_b) * jnp.exp(dmat - m_t[:, :L])
                qc = _mm(qm_b, c_prev_b)
                wv = _mm(w.astype(BF16), jnp.concatenate([vh, ones], axis=1).astype(BF16))
                num = a * qc[:, :LANES] + wv[:, :LANES]
                den = a * qc[:, LANES:] + wv[:, LANES:]
                hout = num / jnp.maximum(jnp.abs(den), jnp.exp(-m_t))
                b_last = b_col[L - 1:L, :]
                g_col = b_last - b_col + i_col
                m_new = jnp.maximum(b_last + m_prev, jnp.max(g_col, axis=0, keepdims=True))
                ws = jnp.exp(g_col - m_new)
                decay = jnp.exp(b_last + m_prev - m_new)
                u_mat = _tn(kpair_b, jnp.concatenate([ws * vh, ws], axis=1).astype(BF16))
                upd.append(jnp.concatenate([decay, decay], axis=1) * c_prev + u_mat)
                m_scr[bb * D_HEADS + h] = jnp.broadcast_to(m_new, m_scr.shape[1:])
                hn = _row_rms(hout) * hg_ref[...]
                y = _sigmoid(op_ref[bb, rows, h * D_VD:(h + 1) * D_VD].astype(F32)) * hn
                o_ref[bb, rows, h * D_VD:(h + 1) * D_VD] = y.astype(BF16)
            c_scr[sp] = jnp.where(row128 < D_QK, upd[0], upd[1])

    def chunk(ci, carry):
        for bb in range(qk_ref.shape[0]):
            one(bb, ci)
        return carry

    lax.fori_loop(0, tc // L, chunk, 0)


def _mlstm_scan(u, u32, cols, qk, gt, p):
    b, s, _ = u.shape
    tc = min(256, s)
    nb = next(n for n in (8, 4, 2, 1) if b % n == 0)
    nv = D_HEADS * D_VD
    ublk = lambda name, w: pl.BlockSpec((nb, tc, w), lambda bi, i, c=cols[name] // w: (bi, i, c))
    const = lambda shape: pl.BlockSpec(shape, lambda bi, i: (0,) * len(shape))
    return pl.pallas_call(
        _mlstm_scan_kernel,
        out_shape=jax.ShapeDtypeStruct((b, s, nv), BF16),
        grid=(b // nb, s // tc),
        in_specs=[pl.BlockSpec((nb, tc, qk.shape[-1]), lambda bi, i: (bi, i, 0)),
                  ublk("d_v", nv), ublk("d_o", nv), pl.BlockSpec((nb, tc, LANES), lambda bi, i: (bi, i, 0)),
                  pl.BlockSpec((nb, tc // CHUNK, 8, CHUNK), lambda bi, i: (bi, i, 0, 0)),
                  const((1, LANES)), const((8, 1)), const((1, D_VD))],
        out_specs=pl.BlockSpec((nb, tc, nv), lambda bi, i: (bi, i, 0)),
        scratch_shapes=[pltpu.VMEM((nb * D_HEADS // 2, LANES, 2 * LANES), F32),
                        pltpu.VMEM((nb * D_HEADS, 8, LANES), F32)],
        compiler_params=_cp("parallel", "arbitrary"),
        name="mlstm_scan",
    )(qk, u, u, u32, gt, p["bias_col"], p["bias_row"], p["h_g"])


EVEN_SRC = dict(a_ql=(0, 256), a_kvl=(256, 128), a_kr=(384, 32), a_gate=(416, 512), b_q=(928, 512),
                b_kc=(1440, 128), b_vc=(1568, 128), b_ks=(1696, 128), b_vs=(1824, 128), b_kw=(1952, 128),
                b_vw=(2080, 128), b_g=(2208, 24), b_gate=(2232, 512), m_q=(2744, 256), m_gate=(3000, 256))
EVEN_DST = dict(a_ql=0, a_kvl=256, a_kr=384, b_q=512, a_gate=1024, b_gate=1536, m_q=2048, m_gate=2304,
                b_g=2560, b_kc=2688, b_vc=2816, b_ks=2944, b_vs=3072, b_kw=3200, b_vw=3328)
EVEN_COLS_PAD = 3456

ODD_SRC = dict(c_q=(0, 512), c_k=(512, 64), c_v=(576, 64), c_iq=(640, 256), c_ik=(896, 32), c_iw=(928, 8),
               c_gate=(936, 512), d_q=(1448, 256), d_k=(1704, 256), d_v=(1960, 512), d_i=(2472, 4),
               d_f=(2476, 4), d_o=(2480, 512), d_gate=(2992, 512), m_q=(3504, 256), m_gate=(3760, 256))
ODD_DST = dict(c_q=0, c_gate=512, d_gate=1024, d_v=1536, d_o=2048, d_q=2560, d_k=2816, m_q=3072, m_gate=3328,
               c_iq=3584, c_k=3840, c_v=3904, c_ik=3968, c_iw=4000, d_i=4008, d_f=4012)
ODD_COLS_PAD = 4096


def _permute_cols(w, src, dst, total):
    pieces, pos = [], 0
    for name in sorted(src, key=lambda n: dst[n]):
        start, width = src[name]
        if dst[name] > pos:
            pieces.append(jnp.zeros((w.shape[0], dst[name] - pos), BF16))
        pieces.append(w[:, start:start + width].astype(BF16))
        pos = dst[name] + width
    if total > pos:
        pieces.append(jnp.zeros((w.shape[0], total - pos), BF16))
    return jnp.concatenate(pieces, axis=1)


def _heads_split(w, heads, first):
    w3 = w.reshape(w.shape[0], heads, -1)
    return jnp.concatenate([w3[:, :, :first].reshape(w.shape[0], -1),
                            w3[:, :, first:].reshape(w.shape[0], -1)], axis=1).astype(BF16)


def _tile_lanes(v, reps):
    return jnp.tile(v.astype(F32).reshape(1, -1), (1, reps))


def _rope_tables(positions, d2):
    inv = ROPE_THETA ** (-jnp.arange(d2, dtype=F32) / d2)
    ang = positions.astype(F32)[..., None] * inv
    c, s = jnp.cos(ang), jnp.sin(ang)
    reps = LANES // (2 * d2)
    return (jnp.tile(jnp.concatenate([c, c], axis=-1), (1, 1, reps)),
            jnp.tile(jnp.concatenate([-s, s], axis=-1), (1, 1, reps)))


def _block_diag2(w):
    z = jnp.zeros_like(w)
    return jnp.concatenate([jnp.concatenate([w, z], axis=-1), jnp.concatenate([z, w], axis=-1)], axis=-2)


def kernel(x, mem, positions, ln_g, mem_norm_g, mem_w_kv, mem_q_norm_g, mem_k_norm_g, w_out, even_w_in, mla_q_lat_g, mla_kv_lat_g, mla_w_uq, mla_w_ukv, mla_q_norm_g, mla_k_norm_g, nsa_q_norm_g, nsa_k_norm_g, nsa_cmp_pos, nsa_cmp_w1, nsa_cmp_w2, odd_w_in, dsa_q_norm_g, dsa_k_norm_g, mlstm_conv_w, mlstm_conv_b, mlstm_i_bias, mlstm_f_bias, mlstm_h_norm_g):
    b, s, _ = x.shape
    depth = ln_g.shape[0]
    cos64, sin64 = _rope_tables(positions, 32)
    cos32, sin32 = _rope_tables(positions, 16)
    tabs = dict(cos64=cos64, sin64=sin64, cos32=cos32, sin32=sin32)
    n_pad = s // CMP_STRIDE
    cmp_pos = jnp.pad(positions[:, CMP_LEN - 1::CMP_STRIDE], ((0, 0), (0, 0)))[:, :n_pad]
    cmp_pos = jnp.pad(cmp_pos, ((0, 0), (0, n_pad - cmp_pos.shape[1])))
    cosc, sinc = _rope_tables(cmp_pos, 32)

    mem_k, mem_v = _mem_kv(mem, mem_norm_g.reshape(depth, 1, -1), mem_w_kv.astype(BF16),
                           jnp.tile(mem_k_norm_g, (1, 2)).reshape(depth, 1, LANES))

    for layer in range(depth):
        li = layer // 2
        g_ln = ln_g[layer].reshape(1, -1)
        mq_g = _tile_lanes(mem_q_norm_g[layer], 2)
        if layer % 2 == 0:
            cols = EVEN_DST
            u, u32 = _in_proj(x, g_ln, _permute_cols(even_w_in[li], EVEN_SRC, EVEN_DST, EVEN_COLS_PAD), cols["b_g"])
            pa = dict(q_lat_g=mla_q_lat_g[li].reshape(1, -1), kv_lat_g=mla_kv_lat_g[li].reshape(1, -1),
                      w_uq=_heads_split(mla_w_uq[li], A_HEADS, A_NOPE), w_ukv=_heads_split(mla_w_ukv[li], A_HEADS, A_NOPE),
                      qn_g=_tile_lanes(mla_q_norm_g[li, :A_NOPE], 2), qr_g=_tile_lanes(mla_q_norm_g[li, A_NOPE:], 4),
                      kn_g=_tile_lanes(mla_k_norm_g[li, :A_NOPE], 2), kr_g=_tile_lanes(mla_k_norm_g[li, A_NOPE:], 4))
            qa, ka, va = _mla_prep(u, cos32, sin32, pa)
            y1 = _mla_attn(qa, ka, va)
            pb = dict(q_g=_tile_lanes(nsa_q_norm_g[li], 2), ks_g=_tile_lanes(nsa_k_norm_g[li, 1], 2),
                      kw_g=_tile_lanes(nsa_k_norm_g[li, 2], 2), kc_g=_tile_lanes(nsa_k_norm_g[li, 0], 2),
                      pe_k=jnp.tile(nsa_cmp_pos[li, 0], (1, 2)), pe_v=jnp.tile(nsa_cmp_pos[li, 1], (1, 2)),
                      w1k=_block_diag2(nsa_cmp_w1[li, 0].reshape(CMP_LEN, B_HD, B_HD)).astype(BF16),
                      w1v=_block_diag2(nsa_cmp_w1[li, 1].reshape(CMP_LEN, B_HD, B_HD)).astype(BF16),
                      w2k=_block_diag2(nsa_cmp_w2[li, 0]).astype(BF16),
                      w2v=_block_diag2(nsa_cmp_w2[li, 1]).astype(BF16))
            qb, ks, kw = _nsa_prep(u, cols, cos64, sin64, pb)
            kcmp, vcmp = _nsa_cmp(u, cols, cosc, sinc, pb)
            y2 = _nsa_attn(u, u32, cols, qb, kcmp, vcmp, ks, kw)
            gate_blocks = (cols["a_gate"] // 512, cols["b_gate"] // 512, cols["m_gate"] // 256)
        else:
            cols = dict(ODD_DST, c_kv=ODD_DST["c_k"], small=ODD_DST["c_ik"])
            u, u32 = _in_proj(x, g_ln, _permute_cols(odd_w_in[li], ODD_SRC, ODD_DST, ODD_COLS_PAD), cols["small"])
            pc = dict(q_g=_tile_lanes(dsa_q_norm_g[li], 2), k_g=_tile_lanes(dsa_k_norm_g[li], 2))
            qc, kvc, iq, ik = _dsa_prep(u, u32, cols, tabs, pc)
            y1 = _dsa_attn(u32, qc, iq, ik, kvc)
            qk = _mlstm_conv(u, cols["d_q"] // 512, mlstm_conv_w[li], mlstm_conv_b[li].reshape(1, -1))
            gates = u32[:, :, cols["d_i"] - cols["small"]:cols["d_i"] - cols["small"] + 2 * D_HEADS]
            gt = gates.reshape(b, s // CHUNK, CHUNK, 2 * D_HEADS).transpose(0, 1, 3, 2)
            bias8 = jnp.concatenate([mlstm_i_bias[li], mlstm_f_bias[li]]).astype(F32)
            bias_col = jnp.zeros((1, LANES), F32).at[0, cols["d_i"] - cols["small"]:cols["d_i"] - cols["small"] + 8].set(bias8)
            pd = dict(bias_col=bias_col, bias_row=bias8.reshape(8, 1), h_g=mlstm_h_norm_g[li].reshape(1, -1))
            y2 = _mlstm_scan(u, u32, cols, qk, gt, pd)
            gate_blocks = (cols["c_gate"] // 512, cols["d_gate"] // 512, cols["m_gate"] // 256)
        ym = _mem_attn(u, cols["m_q"] // 256, mem_k, mem_v, layer, mq_g)
        x = _out_proj(x, y1, y2, ym, u, gate_blocks, w_out[layer].astype(BF16))
    return x
```
